```python
import jax, jax.numpy as jnp
from jax import lax
import numpy as np

D_MODEL = 1024
BATCH = 32
SEQ = 256
DEPTH = 2
DEC_BATCH = 2
DEC_SEQ = 1024
PAST_LEN = 256

GRID_W = 64
MIX_WIDTH = D_MODEL
HEAD_DIM = 64
N_Q_HEADS = 8
N_KV_HEADS = 2
Q_PER_KV = N_Q_HEADS // N_KV_HEADS
ATTN_W = N_Q_HEADS * HEAD_DIM
KV_W = N_KV_HEADS * HEAD_DIM
CONV_CH = MIX_WIDTH // 4
CONV_K = 31
CONV_PAD = CONV_K // 2
N_RET_HEADS = 4
RET_HEAD_DIM = 64
RET_W = N_RET_HEADS * RET_HEAD_DIM
OFF_K = ATTN_W
OFF_V = OFF_K + KV_W
OFF_CONV = OFF_V + KV_W
OFF_RET = OFF_CONV + 2 * CONV_CH
IN_COLS = OFF_RET + 4 * RET_W
ROPE_HALF = HEAD_DIM // 2
ROPE_QUART = HEAD_DIM // 4
ROPE_THETA = 10000.0
Q_BLOCK = 128
RET_CHUNK = 128
N_EXPERTS = 32
TOP_K = 4
D_FF = D_MODEL
SWIGLU_LIMIT = 7.0
SWIGLU_ALPHA = 1.702
MOE_BLOCK = 128
EPS = 1e-6
LN_EPS = 1e-5

kernel_name = 'hybrid_diffusion_prefix_step'


def rmsnorm(x, g):
    xf = x.astype(jnp.float32)
    y = xf * lax.rsqrt(jnp.mean(xf * xf, axis=-1, keepdims=True) + EPS)
    return (y * g.astype(jnp.float32)).astype(x.dtype)


def adaln_params(cvec, w_mod, b_mod):
    m = jax.nn.silu(cvec) @ w_mod + b_mod
    return jnp.split(m[:, None, :], 6, axis=-1)


def axial_rope_tables(L):
    rows = L // GRID_W
    row = jnp.repeat(jnp.arange(rows), GRID_W).astype(jnp.float32)
    col = jnp.tile(jnp.arange(GRID_W), rows).astype(jnp.float32)
    inv = 1.0 / (ROPE_THETA ** (jnp.arange(0, ROPE_HALF, 2, dtype=jnp.float32) / ROPE_HALF))
    ar = row[:, None] * inv[None, :]
    ac = col[:, None] * inv[None, :]
    cos = jnp.concatenate([jnp.cos(ar), jnp.cos(ar), jnp.cos(ac), jnp.cos(ac)], axis=-1)
    sin = jnp.concatenate([jnp.sin(ar), jnp.sin(ar), jnp.sin(ac), jnp.sin(ac)], axis=-1)
    return cos, sin


def apply_rope(x, cos, sin):
    shape = (1, x.shape[1]) + (1,) * (x.ndim - 3) + (HEAD_DIM,)
    c = cos.reshape(shape)
    s = sin.reshape(shape)
    def rot(t):
        return jnp.concatenate([-t[..., ROPE_QUART:], t[..., :ROPE_QUART]], axis=-1)
    xrot = jnp.concatenate([rot(x[..., :ROPE_HALF]), rot(x[..., ROPE_HALF:])], axis=-1)
    return (x.astype(jnp.float32) * c + xrot.astype(jnp.float32) * s).astype(x.dtype)


def attend(q, k, v):
    B, Lq, Hkv, G, Dh = q.shape
    nb = Lq // Q_BLOCK
    qb = q.reshape(B, nb, Q_BLOCK, Hkv, G, Dh).transpose(1, 0, 2, 3, 4, 5)
    kf = k.astype(jnp.float32)
    vf = v.astype(jnp.float32)
    scale = Dh ** -0.5
    def one(qblk):
        s = jnp.einsum('bqhgd,bkhd->bhgqk', qblk.astype(jnp.float32), kf) * scale
        p = jax.nn.softmax(s, axis=-1)
        return jnp.einsum('bhgqk,bkhd->bqhgd', p, vf).astype(q.dtype)
    o = lax.map(one, qb)
    return o.transpose(1, 0, 2, 3, 4, 5).reshape(B, Lq, Hkv * G * Dh)


def conv_module(u, w, b, ln_g, ln_b):
    a, gt = jnp.split(u, 2, axis=-1)
    z = a * jax.nn.sigmoid(gt)
    z = lax.conv_general_dilated(z, w[:, None, :].astype(z.dtype), (1,), [(CONV_PAD, CONV_PAD)],
                                 dimension_numbers=('NWC', 'WIO', 'NWC'),
                                 feature_group_count=CONV_CH) + b
    zf = z.astype(jnp.float32)
    mu = jnp.mean(zf, axis=-1, keepdims=True)
    var = jnp.mean(jnp.square(zf - mu), axis=-1, keepdims=True)
    zf = (zf - mu) * lax.rsqrt(var + LN_EPS) * ln_g.astype(jnp.float32) + ln_b.astype(jnp.float32)
    return jax.nn.silu(zf).astype(u.dtype)


def chunk_retention(q, k, v, log_g, r0):
    B, L, H, Dk = q.shape
    Dv = v.shape[-1]
    nc = L // RET_CHUNK
    def to_chunks(t):
        return t.reshape(B, nc, RET_CHUNK, H, t.shape[-1]).transpose(1, 0, 3, 2, 4)
    idx = jnp.arange(RET_CHUNK, dtype=jnp.float32)
    diff = idx[:, None] - idx[None, :]
    decay_in = jnp.where(diff[None] >= 0,
                         jnp.exp(jnp.maximum(diff, 0.0)[None] * log_g[:, None, None]), 0.0)
    xi = jnp.exp((idx + 1.0)[None, :] * log_g[:, None])[..., None]
    zeta = jnp.exp((RET_CHUNK - 1.0 - idx)[None, :] * log_g[:, None])[..., None]
    g_chunk = jnp.exp(RET_CHUNK * log_g)[:, None, None]
    def step(r, qkv):
        qc, kc, vc = qkv
        a = jnp.einsum('bhnd,bhmd->bhnm', qc, kc) * decay_in
        o = jnp.einsum('bhnm,bhmv->bhnv', a, vc) + jnp.einsum('bhnd,bhdv->bhnv', qc * xi, r)
        r = g_chunk * r + jnp.einsum('bhmd,bhmv->bhdv', kc * zeta, vc)
        return r, o
    r_fin, o = lax.scan(step, r0, (to_chunks(q), to_chunks(k), to_chunks(v)))
    o = o.transpose(1, 0, 3, 2, 4).reshape(B, L, H, Dv)
    return o, r_fin


def bidir_retention(rq, rk, rv, rg, decay_logit, gn_g, init):
    B, L, H, Dh = rq.shape
    f32 = jnp.float32
    q = rq.astype(f32)
    k = rk.astype(f32) * (Dh ** -0.5)
    v = rv.astype(f32)
    log_g = jax.nn.log_sigmoid(decay_logit.astype(f32))
    init = init.astype(f32)
    o_f, s_f = chunk_retention(q, k, v, log_g[0], init[:, 0])
    o_b, s_b = chunk_retention(q[:, ::-1], k[:, ::-1], v[:, ::-1], log_g[1], init[:, 1])
    y = o_f + o_b[:, ::-1]
    mu = jnp.mean(y, axis=-1, keepdims=True)
    var = jnp.mean(jnp.square(y - mu), axis=-1, keepdims=True)
    y = ((y - mu) * lax.rsqrt(var + LN_EPS)).reshape(B, L, H * Dh) * gn_g.astype(f32)
    out = (jax.nn.silu(rg.astype(f32)) * y).astype(rq.dtype)
    return out, jnp.stack([s_f, s_b], axis=1)


def token_mixers(h, lp, kv_ctx, ret_init):
    B, L, _ = h.shape
    p = h @ lp['w_in']
    q = p[..., :OFF_K].reshape(B, L, N_KV_HEADS, Q_PER_KV, HEAD_DIM)
    k = p[..., OFF_K:OFF_V].reshape(B, L, N_KV_HEADS, HEAD_DIM)
    v = p[..., OFF_V:OFF_CONV].reshape(B, L, N_KV_HEADS, HEAD_DIM)
    conv_in = p[..., OFF_CONV:OFF_RET]
    r = p[..., OFF_RET:].reshape(B, L, 4, N_RET_HEADS, RET_HEAD_DIM)
    rq, rk, rv = r[:, :, 0], r[:, :, 1], r[:, :, 2]
    rg = r[:, :, 3].reshape(B, L, RET_W)
    q = rmsnorm(q, lp['q_norm_g'])
    k = rmsnorm(k, lp['k_norm_g'])
    if kv_ctx is None:
        attn = attend(q, k, v)
    else:
        k_ctx, v_ctx = kv_ctx
        cos, sin = axial_rope_tables(L)
        qr = apply_rope(q, cos, sin)
        kr = apply_rope(k, cos, sin)
        k_all = jnp.concatenate([kr, k_ctx.astype(kr.dtype)], axis=1)
        v_all = jnp.concatenate([v, v_ctx.astype(v.dtype)], axis=1)
        attn = attend(qr, k_all, v_all)
    conv = conv_module(conv_in, lp['conv_w'], lp['conv_b'], lp['conv_ln_g'], lp['conv_ln_b'])
    ret, states = bidir_retention(rq, rk, rv, rg, lp['ret_decay_logit'], lp['ret_gn_g'], ret_init)
    mixed = jnp.concatenate([attn, conv, ret], axis=-1) @ lp['w_out']
    return mixed, k, v, states


def moe(x, router_w, router_b, w_gu, b_gu, w_dn, b_dn):
    N, D = x.shape
    logits = x.astype(jnp.float32) @ router_w.astype(jnp.float32) + router_b.astype(jnp.float32)
    top_v, top_e = lax.top_k(logits, TOP_K)
    gates = jax.nn.softmax(top_v, axis=-1)
    flat_e = top_e.reshape(-1)
    flat_t = jnp.repeat(jnp.arange(N, dtype=jnp.int32), TOP_K)
    flat_g = gates.reshape(-1)
    order = jnp.argsort(flat_e)
    se, st, sg = flat_e[order], flat_t[order], flat_g[order]
    counts = jnp.bincount(flat_e, length=N_EXPERTS)
    padded = (counts + MOE_BLOCK - 1) // MOE_BLOCK * MOE_BLOCK
    start = jnp.cumsum(counts) - counts
    pad_end = jnp.cumsum(padded)
    pad_start = pad_end - padded
    dest = pad_start[se] + jnp.arange(N * TOP_K) - start[se]
    n_blocks = -(-(N * TOP_K) // MOE_BLOCK) + N_EXPERTS
    n_rows = n_blocks * MOE_BLOCK
    row_t = jnp.zeros((n_rows,), jnp.int32).at[dest].set(st)
    row_g = jnp.zeros((n_rows,), jnp.float32).at[dest].set(sg)
    blk_e = jnp.minimum(jnp.searchsorted(pad_end, jnp.arange(n_blocks) * MOE_BLOCK, side='right'),
                        N_EXPERTS - 1)
    xs = x[row_t].reshape(n_blocks, MOE_BLOCK, D)
    def expert(args):
        xb, e = args
        gu = xb @ w_gu[e] + b_gu[e]
        gate = jnp.minimum(gu[:, :D_FF], SWIGLU_LIMIT)
        up = jnp.clip(gu[:, D_FF:], -SWIGLU_LIMIT, SWIGLU_LIMIT)
        hdn = (up + 1.0) * (gate * jax.nn.sigmoid(SWIGLU_ALPHA * gate))
        return hdn @ w_dn[e] + b_dn[e]
    out = lax.map(expert, (xs, blk_e))
    y = jax.ops.segment_sum(out.reshape(n_rows, D).astype(jnp.float32) * row_g[:, None], row_t,
                            num_segments=N)
    return y.astype(x.dtype)


def layer(x, cvec, lp, kv_ctx, ret_init):
    sh1, sc1, g1, sh2, sc2, g2 = adaln_params(cvec, lp['w_mod'], lp['b_mod'])
    h = rmsnorm(x, lp['norm1_g']) * (1.0 + sc1) + sh1
    mixed, k, v, st = token_mixers(h, lp, kv_ctx, ret_init)
    x = x + g1 * mixed
    h = rmsnorm(x, lp['norm2_g']) * (1.0 + sc2) + sh2
    B, L, D = x.shape
    ff = moe(h.reshape(B * L, D), lp['router_w'], lp['router_b'], lp['moe_w_gu'], lp['moe_b_gu'],
             lp['moe_w_dn'], lp['moe_b_dn']).reshape(B, L, D)
    x = x + g2 * ff
    return x, k, v, st


def setup_inputs(seed: int = 0) -> dict:
    key = jax.random.key(seed)
    ks = jax.random.split(key, 28)
    f32 = jnp.float32
    def nrm(k, shape, scale):
        return jax.random.normal(k, shape, f32) * scale
    D = D_MODEL
    base_logit = jnp.log(jnp.exp2((5 + jnp.arange(N_RET_HEADS)).astype(f32)) - 1.0)
    return {
        'x_prompt': nrm(ks[0], (BATCH, SEQ, D), 1.0),
        'x_sample': nrm(ks[1], (DEC_BATCH, DEC_SEQ, D), 1.0),
        'cache_k': nrm(ks[2], (DEC_BATCH, DEPTH, PAST_LEN, N_KV_HEADS, HEAD_DIM), 1.0),
        'cache_v': nrm(ks[3], (DEC_BATCH, DEPTH, PAST_LEN, N_KV_HEADS, HEAD_DIM), 1.0),
        'state_ret': nrm(ks[4], (DEC_BATCH, DEPTH, 2, N_RET_HEADS, RET_HEAD_DIM, RET_HEAD_DIM), 0.5),
        'c': nrm(ks[5], (DEC_BATCH, D), 1.0),
        'c_ctx': nrm(ks[6], (D,), 1.0),
        'w_mod': nrm(ks[7], (DEPTH, D, 6 * D), 0.5 * D ** -0.5),
        'b_mod': nrm(ks[8], (DEPTH, 6 * D), 0.02),
        'norm1_g': 1.0 + nrm(ks[9], (DEPTH, D), 0.01),
        'norm2_g': 1.0 + nrm(ks[10], (DEPTH, D), 0.01),
        'w_in': nrm(ks[11], (DEPTH, D, IN_COLS), D ** -0.5),
        'q_norm_g': 1.0 + nrm(ks[12], (DEPTH, HEAD_DIM), 0.01),
        'k_norm_g': 1.0 + nrm(ks[13], (DEPTH, HEAD_DIM), 0.01),
        'conv_w': nrm(ks[14], (DEPTH, CONV_K, CONV_CH), CONV_K ** -0.5),
        'conv_b': nrm(ks[15], (DEPTH, CONV_CH), 0.01),
        'conv_ln_g': 1.0 + nrm(ks[16], (DEPTH, CONV_CH), 0.01),
        'conv_ln_b': nrm(ks[17], (DEPTH, CONV_CH), 0.01),
        'ret_decay_logit': base_logit[None, None, :] + nrm(ks[18], (DEPTH, 2, N_RET_HEADS), 0.05),
        'ret_gn_g': 1.0 + nrm(ks[19], (DEPTH, RET_W), 0.01),
        'w_out': nrm(ks[20], (DEPTH, MIX_WIDTH, D), MIX_WIDTH ** -0.5),
        'router_w': nrm(ks[21], (DEPTH, D, N_EXPERTS), D ** -0.5),
        'router_b': nrm(ks[22], (DEPTH, N_EXPERTS), 0.01),
        'moe_w_gu': nrm(ks[23], (DEPTH, N_EXPERTS, D, 2 * D_FF), D ** -0.5),
        'moe_b_gu': nrm(ks[24], (DEPTH, N_EXPERTS, 2 * D_FF), 0.01),
        'moe_w_dn': nrm(ks[25], (DEPTH, N_EXPERTS, D_FF, D), D_FF ** -0.5),
        'moe_b_dn': nrm(ks[26], (DEPTH, N_EXPERTS, D), 0.01),
        'final_g': 1.0 + nrm(ks[27], (D,), 0.01),
    }


def reference(x_prompt, x_sample, cache_k, cache_v, state_ret, c, c_ctx, w_mod, b_mod, norm1_g,
              norm2_g, w_in, q_norm_g, k_norm_g, conv_w, conv_b, conv_ln_g, conv_ln_b,
              ret_decay_logit, ret_gn_g, w_out, router_w, router_b, moe_w_gu, moe_b_gu, moe_w_dn,
              moe_b_dn, final_g):
    xp = x_prompt
    xs = x_sample
    ks_out, vs_out, ss_out = [], [], []
    for l in range(DEPTH):
        lp = dict(w_mod=w_mod[l], b_mod=b_mod[l], norm1_g=norm1_g[l], norm2_g=norm2_g[l],
                  w_in=w_in[l], q_norm_g=q_norm_g[l], k_norm_g=k_norm_g[l], conv_w=conv_w[l],
                  conv_b=conv_b[l], conv_ln_g=conv_ln_g[l], conv_ln_b=conv_ln_b[l],
                  ret_decay_logit=ret_decay_logit[l], ret_gn_g=ret_gn_g[l], w_out=w_out[l],
                  router_w=router_w[l], router_b=router_b[l], moe_w_gu=moe_w_gu[l],
                  moe_b_gu=moe_b_gu[l], moe_w_dn=moe_w_dn[l], moe_b_dn=moe_b_dn[l])
        zero_state = jnp.zeros((xp.shape[0], 2, N_RET_HEADS, RET_HEAD_DIM, RET_HEAD_DIM), jnp.float32)
        xp, kp, vp, sp = layer(xp, c_ctx[None, :], lp, None, zero_state)
        ks_out.append(kp)
        vs_out.append(vp)
        ss_out.append(sp)
        xs, _, _, _ = layer(xs, c, lp, (cache_k[:, l], cache_v[:, l]), state_ret[:, l])
    y_prompt = rmsnorm(xp, final_g)
    y_sample = rmsnorm(xs, final_g)
    new_cache_k = jnp.stack(ks_out, axis=1)
    new_cache_v = jnp.stack(vs_out, axis=1)
    new_state_ret = jnp.stack(ss_out, axis=1)
    return (y_prompt, y_sample, new_cache_k, new_cache_v, new_state_ret)
```

```python
import functools

import numpy as np
import jax
import jax.numpy as jnp
from jax import lax
from jax.experimental import pallas as pl
from jax.experimental.pallas import tpu as pltpu

F32 = jnp.float32
BF16 = jnp.bfloat16
I32 = jnp.int32

D_MODEL = 1024
GRID_W = 64
HEAD_DIM = 64
N_Q_HEADS = 8
N_KV_HEADS = 2
ATTN_W = N_Q_HEADS * HEAD_DIM
KV_W = N_KV_HEADS * HEAD_DIM
QK_W = ATTN_W + KV_W
CONV_CH = 256
CONV_K = 31
CONV_PAD = CONV_K // 2
CONV_HALO = 16
N_RET_HEADS = 4
RET_W = 256
OFF_V = QK_W
OFF_CONV = OFF_V + KV_W
OFF_RET = OFF_CONV + 2 * CONV_CH
IN_COLS = OFF_RET + 4 * RET_W
ROPE_HALF = HEAD_DIM // 2
ROPE_THETA = 10000.0
N_EXPERTS = 32
TOP_K = 4
D_FF = D_MODEL
SWIGLU_LIMIT = 7.0
SWIGLU_ALPHA = 1.702
EPS = 1e-6
LN_EPS = 1e-5

LANES = 128
TOK_TILE = 512
ATTN_QB = 256
RET_QB = 256
CONV_ROWS = 64
ROUTE_TILE = 512
MOE_BM = 256
MOVE_TILE = 256
VMEM_LIMIT = 56 * 1024 * 1024

MOD_SH1, MOD_SC1, MOD_G1, MOD_SH2, MOD_SC2, MOD_G2 = range(6)


def _cparams(*sem):
    return pltpu.CompilerParams(dimension_semantics=sem, vmem_limit_bytes=VMEM_LIMIT)


def _dot(a, b, **kw):
    return jnp.dot(a, b, preferred_element_type=F32, **kw)


def _dot_nt(a, b, **kw):
    return lax.dot_general(a, b, (((1,), (1,)), ((), ())), preferred_element_type=F32, **kw)


def _dot_tn(a, b, **kw):
    return lax.dot_general(a, b, (((0,), (0,)), ((), ())), preferred_element_type=F32, **kw)


def _split_bf16(x):
    hi = x.astype(BF16)
    lo = (x - hi.astype(F32)).astype(BF16)
    return hi, lo


MOD_TN = 1536


def _mod_kernel(ct_ref, w_ref, b_ref, o_ref):
    s = ct_ref[...]
    s = s * jax.nn.sigmoid(s)
    w = w_ref[0]
    rows = [jnp.sum(w * s[:, r:r + 1], axis=0, keepdims=True) for r in range(3)]
    rows.append(jnp.zeros((5, w.shape[1]), F32))
    o_ref[0] = jnp.concatenate(rows, axis=0) + b_ref[0]


def _modulation(cvec3, w_mod, b_mod):
    depth, d, cols = w_mod.shape
    ct = jnp.zeros((d, 8), F32).at[:, :3].set(cvec3.T)
    out = pl.pallas_call(
        _mod_kernel,
        grid=(depth, cols // MOD_TN),
        in_specs=[
            pl.BlockSpec((d, 8), lambda l, j: (0, 0)),
            pl.BlockSpec((1, d, MOD_TN), lambda l, j: (l, 0, j)),
            pl.BlockSpec((1, 1, MOD_TN), lambda l, j: (l, 0, j)),
        ],
        out_specs=pl.BlockSpec((1, 8, MOD_TN), lambda l, j: (l, 0, j)),
        out_shape=jax.ShapeDtypeStruct((depth, 8, cols), F32),
        compiler_params=_cparams("arbitrary", "arbitrary"),
    )(ct, w_mod, b_mod.reshape(depth, 1, cols))
    return out[:, :3].reshape(depth, 3 * 6, 1, d)


def _inproj_kernel(x_ref, sh_ref, sc_ref, g_ref, w_ref, gqk_ref, hsum_ref, hbc_ref,
                   cos_ref, sa_ref, sb_ref, p_ref, k_ref, v_ref):
    x = x_ref[...]
    inv = lax.rsqrt(jnp.mean(x * x, axis=-1, keepdims=True) + EPS)
    h = (x * inv * g_ref[...]) * (1.0 + sc_ref[0]) + sh_ref[0]
    acc = _dot(h.astype(BF16), w_ref[...])
    qk = acc[:, :QK_W]
    ss = _dot((qk * qk).astype(BF16), hsum_ref[...])
    r = lax.rsqrt(ss * (1.0 / HEAD_DIM) + EPS)
    r_hi, r_lo = _split_bf16(r)
    rb = _dot(r_hi, hbc_ref[...]) + _dot(r_lo, hbc_ref[...])
    qkn = qk * rb * gqk_ref[...]
    k_ref[...] = qkn[:, ATTN_W:QK_W]
    v_ref[...] = acc[:, OFF_V:OFF_CONV]
    cos = cos_ref[...]
    sa = sa_ref[...]
    sb = sb_ref[...]
    for j in range(QK_W // LANES):
        c = qkn[:, j * LANES:(j + 1) * LANES]
        up = pltpu.roll(c, LANES - ROPE_HALF // 2, 1)
        dn = pltpu.roll(c, ROPE_HALF // 2, 1)
        p_ref[:, j * LANES:(j + 1) * LANES] = (c * cos + up * sa + dn * sb).astype(BF16)
    p_ref[:, QK_W:] = acc[:, QK_W:].astype(BF16)


def _rope_tables(dec_seq):
    rows = dec_seq // GRID_W
    row = jnp.repeat(jnp.arange(rows), GRID_W).astype(F32)
    col = jnp.tile(jnp.arange(GRID_W), rows).astype(F32)
    inv = 1.0 / (ROPE_THETA ** (jnp.arange(0, ROPE_HALF, 2, dtype=F32) / ROPE_HALF))
    ar = row[:, None] * inv[None, :]
    ac = col[:, None] * inv[None, :]
    cos = jnp.concatenate([jnp.cos(ar), jnp.cos(ar), jnp.cos(ac), jnp.cos(ac)], axis=-1)
    sin = jnp.concatenate([jnp.sin(ar), jnp.sin(ar), jnp.sin(ac), jnp.sin(ac)], axis=-1)
    first = (jnp.arange(HEAD_DIM) % ROPE_HALF) < ROPE_HALF // 2
    sa = jnp.where(first[None, :], -sin, 0.0)
    sb = jnp.where(first[None, :], 0.0, sin)
    def table(t, ident):
        t = jnp.concatenate([jnp.full((TOK_TILE, HEAD_DIM), ident, F32), t], axis=0)
        return jnp.tile(t, (1, LANES // HEAD_DIM))
    return table(cos, 1.0), table(sa, 0.0), table(sb, 0.0)


def _group_of_tile(i, tile, n_ctx, dec_seq):
    tok = i * tile
    return jnp.where(tok < n_ctx, 0, 1 + (tok - n_ctx) // dec_seq)


def _inproj(x, mod, norm_g, w_in_bf, gqk, hsum, hbc, rope, n_ctx, dec_seq):
    n, d = x.shape
    t = TOK_TILE
    grp = functools.partial(_group_of_tile, tile=t, n_ctx=n_ctx, dec_seq=dec_seq)

    def mod_spec(which):
        return pl.BlockSpec((1, 1, d), lambda i: (grp(i) * 6 + which, 0, 0))

    def rope_idx(i):
        tok = i * t
        return (jnp.where(tok < n_ctx, 0, 1 + ((tok - n_ctx) % dec_seq) // t), 0)

    rope_spec = pl.BlockSpec((t, LANES), rope_idx)
    const = lambda shape: pl.BlockSpec(shape, lambda i: (0,) * len(shape))
    return pl.pallas_call(
        _inproj_kernel,
        grid=(n // t,),
        in_specs=[
            pl.BlockSpec((t, d), lambda i: (i, 0)),
            mod_spec(MOD_SH1), mod_spec(MOD_SC1),
            const((1, d)),
            const((d, IN_COLS)),
            const((1, QK_W)), const((QK_W, LANES)), const((LANES, QK_W)),
            rope_spec, rope_spec, rope_spec,
        ],
        out_specs=[
            pl.BlockSpec((t, IN_COLS), lambda i: (i, 0)),
            pl.BlockSpec((t, KV_W), lambda i: (i, 0)),
            pl.BlockSpec((t, KV_W), lambda i: (i, 0)),
        ],
        out_shape=[
            jax.ShapeDtypeStruct((n, IN_COLS), BF16),
            jax.ShapeDtypeStruct((n, KV_W), F32),
            jax.ShapeDtypeStruct((n, KV_W), F32),
        ],
        compiler_params=_cparams("arbitrary"),
    )(x, mod, mod, norm_g, w_in_bf, gqk, hsum, hbc, *rope)


def _head_halves(x, hkv, low):
    r = pltpu.roll(x, HEAD_DIM, 1)
    rep = jnp.where(low, x, r) if hkv == 0 else jnp.where(low, r, x)
    return jnp.where(low, rep, 0.0).astype(BF16), jnp.where(low, 0.0, rep).astype(BF16)


def _make_attn_kernel(has_ctx):
    def kern(*refs):
        if has_ctx:
            q_ref, k_ref, v_ref, ck_ref, cv_ref, o_ref = refs
        else:
            q_ref, k_ref, v_ref, o_ref = refs
        low = lax.broadcasted_iota(I32, (1, LANES), 1) < HEAD_DIM
        k = k_ref[...].astype(F32)
        v = v_ref[...].astype(F32)
        if has_ctx:
            ck = ck_ref[0, 0]
            cv = cv_ref[0, 0]
        scale = HEAD_DIM ** -0.5
        for hkv in range(N_KV_HEADS):
            kh = _head_halves(k, hkv, low)
            vh = _head_halves(v, hkv, low)
            if has_ctx:
                ckh = _head_halves(ck, hkv, low)
                cvh = _head_halves(cv, hkv, low)
            for c in range(2):
                col = hkv * 2 * LANES + c * LANES
                qc = q_ref[:, col:col + LANES] * scale
                o_c = None
                for j in range(2):
                    s = _dot_nt(qc, kh[j])
                    m = jnp.max(s, axis=-1, keepdims=True)
                    if has_ctx:
                        s2 = _dot_nt(qc, ckh[j])
                        m = jnp.maximum(m, jnp.max(s2, axis=-1, keepdims=True))
                    p = jnp.exp(s - m)
                    l = jnp.sum(p, axis=-1, keepdims=True)
                    o = _dot(p.astype(BF16), vh[j])
                    if has_ctx:
                        p2 = jnp.exp(s2 - m)
                        l = l + jnp.sum(p2, axis=-1, keepdims=True)
                        o = o + _dot(p2.astype(BF16), cvh[j])
                    o = o / l
                    o_c = o if o_c is None else o_c + o
                o_ref[:, col:col + LANES] = o_c.astype(BF16)
    return kern


def _attention(p, out, n_seq, seq_len, row0, ctx_kv, layer):
    n = p.shape[0]
    qb = ATTN_QB
    nq = seq_len // qb
    qrow0 = row0 // qb
    srow0 = row0 // seq_len
    has_ctx = ctx_kv is not None
    in_specs = [
        pl.BlockSpec((qb, ATTN_W), lambda b, i: (qrow0 + b * nq + i, 0)),
        pl.BlockSpec((seq_len, KV_W), lambda b, i: (srow0 + b, ATTN_W // KV_W)),
        pl.BlockSpec((seq_len, KV_W), lambda b, i: (srow0 + b, OFF_V // KV_W)),
    ]
    args = [p, p, p]
    if has_ctx:
        ck, cv = ctx_kv
        past = ck.shape[2]
        spec = pl.BlockSpec((1, 1, past, KV_W), lambda b, i: (b, layer, 0, 0))
        in_specs += [spec, spec]
        args += [ck, cv]
    in_specs.append(pl.BlockSpec(memory_space=pl.ANY))
    args.append(out)

    body = _make_attn_kernel(has_ctx)

    def kern(*refs):
        body(*refs[:-2], refs[-1])

    return pl.pallas_call(
        kern,
        grid=(n_seq, nq),
        in_specs=in_specs,
        out_specs=pl.BlockSpec((qb, ATTN_W), lambda b, i: (qrow0 + b * nq + i, 0)),
        out_shape=jax.ShapeDtypeStruct((n, ATTN_W), BF16),
        input_output_aliases={len(args) - 1: 0},
        compiler_params=_cparams("arbitrary", "arbitrary"),
    )(*args)


def _make_conv_kernel(seq_len):
    def kern(a_ref, g_ref, w_ref, b_ref, lng_ref, lnb_ref, o_ref, zp_ref):
        zero = jnp.zeros((CONV_HALO, CONV_CH), F32)
        zp_ref[0:CONV_HALO, :] = zero
        zp_ref[CONV_HALO + seq_len:2 * CONV_HALO + seq_len, :] = zero
        zp_ref[CONV_HALO:CONV_HALO + seq_len, :] = (
            a_ref[...].astype(F32) * jax.nn.sigmoid(g_ref[...].astype(F32)))
        w = w_ref[...]
        bias = b_ref[...]
        for c in range(seq_len // CONV_ROWS):
            base = c * CONV_ROWS + CONV_HALO - CONV_PAD
            acc = jnp.zeros((CONV_ROWS, CONV_CH), F32) + bias
            for j in range(CONV_K):
                acc = acc + zp_ref[base + j:base + j + CONV_ROWS, :] * w[j:j + 1, :]
            mu = jnp.mean(acc, axis=-1, keepdims=True)
            dlt = acc - mu
            var = jnp.mean(dlt * dlt, axis=-1, keepdims=True)
            y = dlt * lax.rsqrt(var + LN_EPS) * lng_ref[...] + lnb_ref[...]
            o_ref[c * CONV_ROWS:(c + 1) * CONV_ROWS, :] = (y * jax.nn.sigmoid(y)).astype(BF16)
    return kern


def _conv(p, out, n_seq, seq_len, row0, w, b, lng, lnb):
    n = p.shape[0]
    srow0 = row0 // seq_len
    const = lambda shape: pl.BlockSpec(shape, lambda s: (0,) * len(shape))
    body = _make_conv_kernel(seq_len)

    def kern(a_ref, g_ref, w_ref, b_ref, lng_ref, lnb_ref, _, o_ref, zp_ref):
        body(a_ref, g_ref, w_ref, b_ref, lng_ref, lnb_ref, o_ref, zp_ref)

    return pl.pallas_call(
        kern,
        grid=(n_seq,),
        in_specs=[
            pl.BlockSpec((seq_len, CONV_CH), lambda s: (srow0 + s, OFF_CONV // CONV_CH)),
            pl.BlockSpec((seq_len, CONV_CH), lambda s: (srow0 + s, OFF_CONV // CONV_CH + 1)),
            const((CONV_K, CONV_CH)), const((1, CONV_CH)), const((1, CONV_CH)), const((1, CONV_CH)),
            pl.BlockSpec(memory_space=pl.ANY),
        ],
        out_specs=pl.BlockSpec((seq_len, CONV_CH), lambda s: (srow0 + s, 0)),
        out_shape=jax.ShapeDtypeStruct((n, CONV_CH), BF16),
        scratch_shapes=[pltpu.VMEM((seq_len + 2 * CONV_HALO, CONV_CH), F32)],
        input_output_aliases={6: 0},
        compiler_params=_cparams("arbitrary"),
    )(p, p, w, b, lng, lnb, out)


def _make_ret_kernel(seq_len, has_init):
    qb = min(seq_len, RET_QB)
    nq = seq_len // qb
    scale = HEAD_DIM ** -0.5

    def kern(*refs):
        if has_init:
            lg_ref, q_ref, k_ref, v_ref, g_ref, gn_ref, gm_ref, r0_ref, o_ref, st_ref = refs
        else:
            lg_ref, q_ref, k_ref, v_ref, g_ref, gn_ref, gm_ref, o_ref, st_ref = refs
        low = lax.broadcasted_iota(I32, (1, LANES), 1) < HEAD_DIM
        pos = lax.broadcasted_iota(I32, (seq_len, 1), 0).astype(F32)
        qpos = lax.broadcasted_iota(I32, (qb, 1), 0).astype(F32)
        kpos = lax.broadcasted_iota(I32, (1, seq_len), 1).astype(F32)
        gm = gm_ref[...]
        for c in range(2):
            cs = slice(c * LANES, (c + 1) * LANES)
            lgf = jnp.where(low, lg_ref[0, 2 * c], lg_ref[0, 2 * c + 1])
            lgb = jnp.where(low, lg_ref[1, 2 * c], lg_ref[1, 2 * c + 1])
            qc = q_ref[:, cs]
            kc = k_ref[:, cs]
            vc = v_ref[:, cs]
            kf = kc.astype(F32) * scale
            zeta_f = jnp.exp((seq_len - 1.0 - pos) * lgf)
            zeta_b = jnp.exp(pos * lgb)
            for d, zeta, lgd in ((0, zeta_f, lgf), (1, zeta_b, lgb)):
                st = _dot_tn((kf * zeta).astype(BF16), vc)
                if has_init:
                    st = st + r0_ref[0, d, c] * jnp.exp(seq_len * lgd)
                st_ref[0, d, c] = st
            y_blocks = [None] * nq
            for j in range(2):
                sel = low if j == 0 else jnp.logical_not(low)
                kh = jnp.where(sel, kc, jnp.zeros_like(kc))
                vh = jnp.where(sel, vc, jnp.zeros_like(vc))
                lf = lg_ref[0, 2 * c + j]
                lb = lg_ref[1, 2 * c + j]
                for i in range(nq):
                    s = _dot_nt(qc[i * qb:(i + 1) * qb], kh)
                    diff = (qpos + float(i * qb)) - kpos
                    dec = jnp.exp(jnp.where(diff >= 0, diff * lf, -diff * lb))
                    dec = dec * jnp.where(diff == 0, 2.0 * scale, scale)
                    y = _dot((s * dec).astype(BF16), vh)
                    y_blocks[i] = y if y_blocks[i] is None else y_blocks[i] + y
            y = jnp.concatenate(y_blocks, axis=0) if nq > 1 else y_blocks[0]
            if has_init:
                xi_f = jnp.exp((pos + 1.0) * lgf)
                xi_b = jnp.exp((seq_len - pos) * lgb)
                y = y + _dot(qc, r0_ref[0, 0, c].astype(BF16)) * xi_f
                y = y + _dot(qc, r0_ref[0, 1, c].astype(BF16)) * xi_b
            y_hi, y_lo = _split_bf16(y)
            mu = _dot(y_hi, gm) + _dot(y_lo, gm)
            dlt = y - mu
            var = _dot((dlt * dlt).astype(BF16), gm)
            yn = dlt * lax.rsqrt(var + LN_EPS) * gn_ref[:, cs]
            gate = g_ref[:, cs].astype(F32)
            o_ref[:, cs] = (gate * jax.nn.sigmoid(gate) * yn).astype(BF16)
    return kern


def _retention(p, out, n_seq, seq_len, row0, log_g, gn_g, gmat, r0):
    n = p.shape[0]
    srow0 = row0 // seq_len
    has_init = r0 is not None
    body = _make_ret_kernel(seq_len, has_init)
    nin = 8 if has_init else 7

    def kern(*refs):
        body(*refs[:nin], *refs[nin + 1:])

    col = OFF_RET // RET_W
    in_specs = [pl.BlockSpec(memory_space=pltpu.SMEM)]
    in_specs += [pl.BlockSpec((seq_len, RET_W), functools.partial(lambda s, j: (srow0 + s, col + j), j=j))
                 for j in range(4)]
    in_specs += [pl.BlockSpec((1, RET_W), lambda s: (0, 0)), pl.BlockSpec((LANES, LANES), lambda s: (0, 0))]
    args = [log_g, p, p, p, p, gn_g, gmat]
    st_spec = pl.BlockSpec((1, 2, 2, LANES, LANES), lambda s: (s, 0, 0, 0, 0))
    if has_init:
        in_specs.append(st_spec)
        args.append(r0)
    in_specs.append(pl.BlockSpec(memory_space=pl.ANY))
    args.append(out)
    return pl.pallas_call(
        kern,
        grid=(n_seq,),
        in_specs=in_specs,
        out_specs=[pl.BlockSpec((seq_len, RET_W), lambda s: (srow0 + s, 0)), st_spec],
        out_shape=[jax.ShapeDtypeStruct((n, RET_W), BF16),
                   jax.ShapeDtypeStruct((n_seq, 2, 2, LANES, LANES), F32)],
        input_output_aliases={len(args) - 1: 0},
        compiler_params=_cparams("arbitrary"),
    )(*args)


def _outproj_kernel(a_ref, c_ref, r_ref, x_ref, g1_ref, sc_ref, sh_ref, n2_ref, wo_ref, rwt_ref, rb_ref,
                    x1_ref, h2_ref, te_ref, tg_ref):
    mixed = (_dot(a_ref[...], wo_ref[0:ATTN_W, :])
             + _dot(c_ref[...], wo_ref[ATTN_W:ATTN_W + CONV_CH, :])
             + _dot(r_ref[...], wo_ref[ATTN_W + CONV_CH:, :]))
    x1 = x_ref[...] + g1_ref[0] * mixed
    x1_ref[...] = x1
    inv = lax.rsqrt(jnp.mean(x1 * x1, axis=-1, keepdims=True) + EPS)
    h2 = (x1 * inv * n2_ref[...]) * (1.0 + sc_ref[0]) + sh_ref[0]
    h2_ref[...] = h2
    logits = _dot_nt(rwt_ref[...], h2, precision=lax.Precision.HIGHEST) + rb_ref[...]
    t = logits.shape[1]
    eidx = lax.broadcasted_iota(I32, (N_EXPERTS, t), 0).astype(F32)
    vals = logits
    tops, idxs = [], []
    for _ in range(TOP_K):
        m = jnp.max(vals, axis=0, keepdims=True)
        idx = jnp.min(jnp.where(vals == m, eidx, float(N_EXPERTS)), axis=0, keepdims=True)
        tops.append(m)
        idxs.append(idx)
        vals = jnp.where(eidx == idx, -jnp.inf, vals)
    es = [jnp.exp(m - tops[0]) for m in tops]
    tot = es[0] + es[1] + es[2] + es[3]
    te_ref[...] = jnp.concatenate(idxs + [jnp.zeros((8 - TOP_K, t), F32)], axis=0).astype(I32)
    tg_ref[...] = jnp.concatenate([e / tot for e in es] + [jnp.zeros((8 - TOP_K, t), F32)], axis=0)


def _outproj(attn, conv, ret, x, mod, norm_g, w_out_bf, rwt, rb, n_ctx, dec_seq):
    n, d = x.shape
    t = TOK_TILE
    grp = functools.partial(_group_of_tile, tile=t, n_ctx=n_ctx, dec_seq=dec_seq)

    def mod_spec(which):
        return pl.BlockSpec((1, 1, d), lambda i: (grp(i) * 6 + which, 0, 0))

    const = lambda shape: pl.BlockSpec(shape, lambda i: (0,) * len(shape))
    row = lambda w: pl.BlockSpec((t, w), lambda i: (i, 0))
    lane = lambda: pl.BlockSpec((8, t), lambda i: (0, i))
    return pl.pallas_call(
        _outproj_kernel,
        grid=(n // t,),
        in_specs=[row(ATTN_W), row(CONV_CH), row(RET_W), row(d),
                  mod_spec(MOD_G1), mod_spec(MOD_SC2), mod_spec(MOD_SH2),
                  const((1, d)), const((d, d)), const((N_EXPERTS, d)), const((N_EXPERTS, 1))],
        out_specs=[row(d), row(d), lane(), lane()],
        out_shape=[jax.ShapeDtypeStruct((n, d), F32), jax.ShapeDtypeStruct((n, d), F32),
                   jax.ShapeDtypeStruct((8, n), I32), jax.ShapeDtypeStruct((8, n), F32)],
        compiler_params=_cparams("arbitrary"),
    )(attn, conv, ret, x, mod, mod, mod, norm_g, w_out_bf, rwt, rb)


def _route_kernel(te_ref, u_ref, ltri_ref, dest_ref, cnt_ref, run_ref, start_ref):
    ph = pl.program_id(0)
    i = pl.program_id(1)
    te = te_ref[...]
    t = te.shape[1]
    eidx = lax.broadcasted_iota(I32, (N_EXPERTS, t), 0)
    hits = [eidx == te[k:k + 1, :] for k in range(TOP_K)]
    onehot = sum(h.astype(F32) for h in hits)
    tile_cnt = jnp.sum(onehot, axis=1, keepdims=True)

    @pl.when(jnp.logical_and(ph == 0, i == 0))
    def _():
        run_ref[...] = jnp.zeros_like(run_ref)

    @pl.when(ph == 0)
    def _():
        run_ref[...] = run_ref[...] + tile_cnt

    @pl.when(jnp.logical_and(ph == 1, i == 0))
    def _():
        cnt = run_ref[...]
        cnt_ref[...] = cnt
        padded = jnp.floor((cnt + (MOE_BM - 1.0)) * (1.0 / MOE_BM)) * MOE_BM
        start_ref[...] = _dot(ltri_ref[...], padded, precision=lax.Precision.HIGHEST)
        run_ref[...] = jnp.zeros_like(run_ref)

    @pl.when(ph == 1)
    def _():
        before = _dot(onehot.astype(BF16), u_ref[...])
        base = before + run_ref[:, 0:1] + start_ref[:, 0:1]
        rows = [jnp.sum(jnp.where(h, base, 0.0), axis=0, keepdims=True) for h in hits]
        rows.append(jnp.zeros((8 - TOP_K, t), F32))
        dest_ref[...] = jnp.concatenate(rows, axis=0).astype(I32)
        run_ref[...] = run_ref[...] + tile_cnt


def _route(top_e, upper, ltri):
    n = top_e.shape[1]
    t = ROUTE_TILE
    return pl.pallas_call(
        _route_kernel,
        grid=(2, n // t),
        in_specs=[pl.BlockSpec((8, t), lambda ph, i: (0, i)),
                  pl.BlockSpec((t, t), lambda ph, i: (0, 0)),
                  pl.BlockSpec((N_EXPERTS, N_EXPERTS), lambda ph, i: (0, 0))],
        out_specs=[pl.BlockSpec((8, t), lambda ph, i: (0, i * ph)),
                   pl.BlockSpec((N_EXPERTS, LANES), lambda ph, i: (0, 0))],
        out_shape=[jax.ShapeDtypeStruct((8, n), I32), jax.ShapeDtypeStruct((N_EXPERTS, LANES), F32)],
        scratch_shapes=[pltpu.VMEM((N_EXPERTS, LANES), F32), pltpu.VMEM((N_EXPERTS, LANES), F32)],
        compiler_params=_cparams("arbitrary", "arbitrary"),
    )(top_e, upper, ltri)


def _row_copy(src_ref, src_row, dst_ref, dst_row, sem):
    return pltpu.make_async_copy(src_ref.at[pl.ds(src_row, 1)], dst_ref.at[pl.ds(dst_row, 1)], sem)


def _make_dispatch_kernel(n):
    t = MOVE_TILE

    def kern(dest_ref, h_ref, _, xs_ref, sem):
        base = pl.program_id(0) * t

        def issue(r, carry):
            for k in range(TOP_K):
                _row_copy(h_ref, r, xs_ref, dest_ref[k * n + base + r], sem).start()
            return carry

        lax.fori_loop(0, t, issue, 0, unroll=8)
        for k in range(TOP_K):
            pltpu.make_async_copy(h_ref, xs_ref.at[pl.ds(0, t)], sem).wait()
    return kern


def _dispatch(dest_flat, h2, xs_init):
    n, d = h2.shape
    t = MOVE_TILE
    return pl.pallas_call(
        _make_dispatch_kernel(n),
        grid_spec=pltpu.PrefetchScalarGridSpec(
            num_scalar_prefetch=1,
            grid=(n // t,),
            in_specs=[pl.BlockSpec((t, d), lambda i, dest: (i, 0)),
                      pl.BlockSpec(memory_space=pl.ANY)],
            out_specs=pl.BlockSpec(memory_space=pl.ANY),
            scratch_shapes=[pltpu.SemaphoreType.DMA(())],
        ),
        out_shape=jax.ShapeDtypeStruct(xs_init.shape, xs_init.dtype),
        input_output_aliases={2: 0},
        compiler_params=_cparams("arbitrary"),
    )(dest_flat, h2, xs_init)


def _expert_kernel(be_ref, bs_ref, nv_ref, x_ref, wgu_ref, bgu_ref, wdn_ref, bdn_ref, y_ref, wgu_s, wdn_s):
    i = pl.program_id(0)
    prev = be_ref[jnp.maximum(i - 1, 0)]
    new_expert = jnp.logical_or(i == 0, be_ref[i] != prev)

    @pl.when(new_expert)
    def _():
        wgu_s[...] = wgu_ref[0].astype(BF16)
        wdn_s[...] = wdn_ref[0].astype(BF16)

    @pl.when(i < nv_ref[0])
    def _():
        x = x_ref[...].astype(BF16)
        gu = _dot(x, wgu_s[...]) + bgu_ref[0]
        gate = jnp.minimum(gu[:, :D_FF], SWIGLU_LIMIT)
        up = jnp.clip(gu[:, D_FF:], -SWIGLU_LIMIT, SWIGLU_LIMIT)
        hdn = (up + 1.0) * (gate * jax.nn.sigmoid(SWIGLU_ALPHA * gate))
        y_ref[...] = _dot(hdn.astype(BF16), wdn_s[...]) + bdn_ref[0]

    @pl.when(i >= nv_ref[0])
    def _():
        y_ref[...] = jnp.zeros_like(y_ref)


def _experts(blk_e, blk_src, n_valid, xs, w_gu, b_gu, w_dn, b_dn):
    r, d = xs.shape
    nb = r // MOE_BM
    rows = pl.BlockSpec((MOE_BM, d), lambda i, be, bs, nv: (bs[i], 0))
    return pl.pallas_call(
        _expert_kernel,
        grid_spec=pltpu.PrefetchScalarGridSpec(
            num_scalar_prefetch=3,
            grid=(nb,),
            in_specs=[rows,
                      pl.BlockSpec((1, d, 2 * D_FF), lambda i, be, bs, nv: (be[i], 0, 0)),
                      pl.BlockSpec((1, 1, 2 * D_FF), lambda i, be, bs, nv: (be[i], 0, 0)),
                      pl.BlockSpec((1, D_FF, d), lambda i, be, bs, nv: (be[i], 0, 0)),
                      pl.BlockSpec((1, 1, d), lambda i, be, bs, nv: (be[i], 0, 0))],
            out_specs=pl.BlockSpec((MOE_BM, d), lambda i, be, bs, nv: (i, 0)),
            scratch_shapes=[pltpu.VMEM((d, 2 * D_FF), BF16), pltpu.VMEM((D_FF, d), BF16)],
        ),
        out_shape=jax.ShapeDtypeStruct((r, d), F32),
        compiler_params=_cparams("arbitrary"),
    )(blk_e, blk_src, n_valid, xs, w_gu, b_gu.reshape(N_EXPERTS, 1, -1), w_dn, b_dn.reshape(N_EXPERTS, 1, -1))


def _make_combine_kernel(n, final):
    t = MOVE_TILE

    def kern(dest_ref, ys_ref, x1_ref, gt_ref, g2_ref, fg_ref, o_ref, buf, sem):
        base = pl.program_id(0) * t

        def issue(r, carry):
            for k in range(TOP_K):
                _row_copy(ys_ref, dest_ref[k * n + base + r], buf.at[k], r, sem).start()
            return carry

        lax.fori_loop(0, t, issue, 0, unroll=8)
        for k in range(TOP_K):
            pltpu.make_async_copy(ys_ref.at[pl.ds(0, t)], buf.at[k], sem).wait()
        gt = gt_ref[...]
        y = buf[0] * gt[:, 0:1]
        for k in range(1, TOP_K):
            y = y + buf[k] * gt[:, k:k + 1]
        x2 = x1_ref[...] + g2_ref[0] * y
        if final:
            x2 = x2 * lax.rsqrt(jnp.mean(x2 * x2, axis=-1, keepdims=True) + EPS) * fg_ref[...]
        o_ref[...] = x2
    return kern


def _combine(dest_flat, ys, x1, gates_t, mod, final_g, final, n_ctx, dec_seq):
    n, d = x1.shape
    t = MOVE_TILE
    grp = functools.partial(_group_of_tile, tile=t, n_ctx=n_ctx, dec_seq=dec_seq)
    return pl.pallas_call(
        _make_combine_kernel(n, final),
        grid_spec=pltpu.PrefetchScalarGridSpec(
            num_scalar_prefetch=1,
            grid=(n // t,),
            in_specs=[pl.BlockSpec(memory_space=pl.ANY),
                      pl.BlockSpec((t, d), lambda i, dest: (i, 0)),
                      pl.BlockSpec((t, 8), lambda i, dest: (i, 0)),
                      pl.BlockSpec((1, 1, d), lambda i, dest: (grp(i) * 6 + MOD_G2, 0, 0)),
                      pl.BlockSpec((1, d), lambda i, dest: (0, 0))],
            out_specs=pl.BlockSpec((t, d), lambda i, dest: (i, 0)),
            scratch_shapes=[pltpu.VMEM((TOP_K, t, d), F32), pltpu.SemaphoreType.DMA(())],
        ),
        out_shape=jax.ShapeDtypeStruct((n, d), F32),
        compiler_params=_cparams("arbitrary"),
    )(dest_flat, ys, x1, gates_t, mod, final_g)


def _blockdiag_pairs(s):
    z = jnp.zeros_like(s[..., 0, :, :])
    def pair(a, b):
        return jnp.concatenate([jnp.concatenate([a, z], axis=-1), jnp.concatenate([z, b], axis=-1)], axis=-2)
    return jnp.stack([pair(s[..., 0, :, :], s[..., 1, :, :]), pair(s[..., 2, :, :], s[..., 3, :, :])], axis=-3)


def _diag_blocks(st):
    h = HEAD_DIM
    blocks = [st[:, :, c, j * h:(j + 1) * h, j * h:(j + 1) * h] for c in range(2) for j in range(2)]
    return jnp.stack(blocks, axis=2)


def kernel(x_prompt, x_sample, cache_k, cache_v, state_ret, c, c_ctx, w_mod, b_mod, norm1_g, norm2_g, w_in,
           q_norm_g, k_norm_g, conv_w, conv_b, conv_ln_g, conv_ln_b, ret_decay_logit, ret_gn_g, w_out,
           router_w, router_b, moe_w_gu, moe_b_gu, moe_w_dn, moe_b_dn, final_g):
    batch, seq, d = x_prompt.shape
    dec_batch, dec_seq, _ = x_sample.shape
    depth = w_mod.shape[0]
    past = cache_k.shape[2]
    n_ctx = batch * seq
    n_lat = dec_batch * dec_seq
    n = n_ctx + n_lat
    assert d == D_MODEL and dec_batch == 2
    assert n_ctx % dec_seq == 0 and dec_seq % TOK_TILE == 0 and seq % ATTN_QB == 0 and dec_seq % ATTN_QB == 0

    x = jnp.concatenate([x_prompt.reshape(n_ctx, d), x_sample.reshape(n_lat, d)], axis=0)
    mods = _modulation(jnp.concatenate([c_ctx[None, :], c], axis=0), w_mod, b_mod)
    rope = _rope_tables(dec_seq)

    head_of_col = jnp.arange(QK_W) // HEAD_DIM
    hsum = (head_of_col[:, None] == jnp.arange(LANES)[None, :]).astype(BF16)
    hbc = hsum.T
    lane_head = jnp.arange(LANES) // HEAD_DIM
    gmat = ((lane_head[:, None] == lane_head[None, :]).astype(F32) / HEAD_DIM).astype(BF16)
    tt = jnp.arange(ROUTE_TILE)
    upper = (tt[:, None] < tt[None, :]).astype(BF16)
    ee = jnp.arange(N_EXPERTS)
    ltri = (ee[None, :] < ee[:, None]).astype(F32)

    n_rows = n * TOP_K + N_EXPERTS * MOE_BM
    n_blocks = n_rows // MOE_BM
    cache_k2 = cache_k.reshape(dec_batch, depth, past, KV_W)
    cache_v2 = cache_v.reshape(dec_batch, depth, past, KV_W)

    ks_out, vs_out, ss_out = [], [], []
    for l in range(depth):
        mod = mods[l]
        gqk = jnp.concatenate([jnp.tile(q_norm_g[l], N_Q_HEADS), jnp.tile(k_norm_g[l], N_KV_HEADS)])[None, :]
        p, kn, vv = _inproj(x, mod, norm1_g[l][None, :], w_in[l].astype(BF16), gqk, hsum, hbc, rope,
                            n_ctx, dec_seq)
        ks_out.append(kn[:n_ctx].reshape(batch, seq, N_KV_HEADS, HEAD_DIM))
        vs_out.append(vv[:n_ctx].reshape(batch, seq, N_KV_HEADS, HEAD_DIM))

        attn = jnp.zeros((n, ATTN_W), BF16)
        attn = _attention(p, attn, batch, seq, 0, None, l)
        attn = _attention(p, attn, dec_batch, dec_seq, n_ctx, (cache_k2, cache_v2), l)

        cw, cb = conv_w[l], conv_b[l][None, :]
        clg, clb = conv_ln_g[l][None, :], conv_ln_b[l][None, :]
        conv = jnp.zeros((n, CONV_CH), BF16)
        conv = _conv(p, conv, batch, seq, 0, cw, cb, clg, clb)
        conv = _conv(p, conv, dec_batch, dec_seq, n_ctx, cw, cb, clg, clb)

        log_g = jax.nn.log_sigmoid(ret_decay_logit[l].astype(F32))
        gn = ret_gn_g[l][None, :]
        ret = jnp.zeros((n, RET_W), BF16)
        ret, st_ctx = _retention(p, ret, batch, seq, 0, log_g, gn, gmat, None)
        ret, _ = _retention(p, ret, dec_batch, dec_seq, n_ctx, log_g, gn, gmat,
                            _blockdiag_pairs(state_ret[:, l].astype(F32)))
        ss_out.append(_diag_blocks(st_ctx))

        x1, h2, top_e, top_g = _outproj(attn, conv, ret, x, mod, norm2_g[l][None, :], w_out[l].astype(BF16),
                                        router_w[l].T, router_b[l][:, None], n_ctx, dec_seq)
        dest, cnt = _route(top_e, upper, ltri)
        counts = cnt[:, 0].astype(I32)
        pad_end = jnp.cumsum((counts + MOE_BM - 1) // MOE_BM * MOE_BM)
        n_valid = pad_end[-1] // MOE_BM
        blk_src = jnp.minimum(jnp.arange(n_blocks, dtype=I32), n_valid - 1)
        blk_e = jnp.minimum(jnp.searchsorted(pad_end, blk_src * MOE_BM, side='right'), N_EXPERTS - 1).astype(I32)
        dest_flat = dest[:TOP_K].reshape(-1)

        xs = _dispatch(dest_flat, h2, jnp.zeros((n_rows, d), F32))
        ys = _experts(blk_e, blk_src, n_valid.reshape(1).astype(I32), xs,
                      moe_w_gu[l], moe_b_gu[l], moe_w_dn[l], moe_b_dn[l])
        x = _combine(dest_flat, ys, x1, top_g.T, mod, final_g[None, :], l == depth - 1, n_ctx, dec_seq)

    y_prompt = x[:n_ctx].reshape(batch, seq, d)
    y_sample = x[n_ctx:].reshape(dec_batch, dec_seq, d)
    return (y_prompt, y_sample, jnp.stack(ks_out, axis=1), jnp.stack(vs_out, axis=1),
            jnp.stack(ss_out, axis=1))
```

```python
import functools

import numpy as np
import jax
import jax.numpy as jnp
from jax import lax
from jax.experimental import pallas as pl
from jax.experimental.pallas import tpu as pltpu

F32 = jnp.float32
BF16 = jnp.bfloat16
I32 = jnp.int32

D_MODEL = 1024
GRID_W = 64
HEAD_DIM = 64
N_Q_HEADS = 8
N_KV_HEADS = 2
ATTN_W = N_Q_HEADS * HEAD_DIM
KV_W = N_KV_HEADS * HEAD_DIM
QK_W = ATTN_W + KV_W
CONV_CH = 256
CONV_K = 31
CONV_PAD = CONV_K // 2
CONV_HALO = 16
N_RET_HEADS = 4
RET_W = 256
OFF_V = QK_W
OFF_CONV = OFF_V + KV_W
OFF_RET = OFF_CONV + 2 * CONV_CH
IN_COLS = OFF_RET + 4 * RET_W
ROPE_HALF = HEAD_DIM // 2
ROPE_THETA = 10000.0
N_EXPERTS = 32
TOP_K = 4
D_FF = D_MODEL
SWIGLU_LIMIT = 7.0
SWIGLU_ALPHA = 1.702
EPS = 1e-6
LN_EPS = 1e-5

LANES = 128
TOK_TILE = 512
ATTN_QB = 256
RET_QB = 256
CONV_ROWS = 64
ROUTE_TILE = 512
MOE_BM = 256
MOVE_TILE = 256
VMEM_LIMIT = 56 * 1024 * 1024

MOD_SH1, MOD_SC1, MOD_G1, MOD_SH2, MOD_SC2, MOD_G2 = range(6)


def _cparams(*sem):
    return pltpu.CompilerParams(dimension_semantics=sem, vmem_limit_bytes=VMEM_LIMIT)


def _dot(a, b, **kw):
    return jnp.dot(a, b, preferred_element_type=F32, **kw)


def _dot_nt(a, b, **kw):
    return lax.dot_general(a, b, (((1,), (1,)), ((), ())), preferred_element_type=F32, **kw)


def _dot_tn(a, b, **kw):
    return lax.dot_general(a, b, (((0,), (0,)), ((), ())), preferred_element_type=F32, **kw)


def _split_bf16(x):
    hi = x.astype(BF16)
    lo = (x - hi.astype(F32)).astype(BF16)
    return hi, lo


MOD_TN = 1536


def _mod_kernel(ct_ref, w_ref, b_ref, o_ref):
    s = ct_ref[...]
    s = s * jax.nn.sigmoid(s)
    w = w_ref[0]
    rows = [jnp.sum(w * s[:, r:r + 1], axis=0, keepdims=True) for r in range(3)]
    rows.append(jnp.zeros((5, w.shape[1]), F32))
    o_ref[0] = jnp.concatenate(rows, axis=0) + b_ref[0]


def _modulation(cvec3, w_mod, b_mod):
    depth, d, cols = w_mod.shape
    ct = jnp.zeros((d, 8), F32).at[:, :3].set(cvec3.T)
    out = pl.pallas_call(
        _mod_kernel,
        grid=(depth, cols // MOD_TN),
        in_specs=[
            pl.BlockSpec((d, 8), lambda l, j: (0, 0)),
            pl.BlockSpec((1, d, MOD_TN), lambda l, j: (l, 0, j)),
            pl.BlockSpec((1, 1, MOD_TN), lambda l, j: (l, 0, j)),
        ],
        out_specs=pl.BlockSpec((1, 8, MOD_TN), lambda l, j: (l, 0, j)),
        out_shape=jax.ShapeDtypeStruct((depth, 8, cols), F32),
        compiler_params=_cparams("arbitrary", "arbitrary"),
    )(ct, w_mod, b_mod.reshape(depth, 1, cols))
    return out[:, :3].reshape(depth, 3 * 6, 1, d)


def _inproj_kernel(x_ref, sh_ref, sc_ref, g_ref, w_ref, gqk_ref, hsum_ref, hbc_ref,
                   cos_ref, sa_ref, sb_ref, p_ref, k_ref, v_ref):
    x = x_ref[...]
    inv = lax.rsqrt(jnp.mean(x * x, axis=-1, keepdims=True) + EPS)
    h = (x * inv * g_ref[...]) * (1.0 + sc_ref[0]) + sh_ref[0]
    acc = _dot(h.astype(BF16), w_ref[...])
    qk = acc[:, :QK_W]
    ss = _dot((qk * qk).astype(BF16), hsum_ref[...])
    r = lax.rsqrt(ss * (1.0 / HEAD_DIM) + EPS)
    r_hi, r_lo = _split_bf16(r)
    rb = _dot(r_hi, hbc_ref[...]) + _dot(r_lo, hbc_ref[...])
    qkn = qk * rb * gqk_ref[...]
    k_ref[...] = qkn[:, ATTN_W:QK_W]
    v_ref[...] = acc[:, OFF_V:OFF_CONV]
    cos = cos_ref[...]
    sa = sa_ref[...]
    sb = sb_ref[...]
    for j in range(QK_W // LANES):
        c = qkn[:, j * LANES:(j + 1) * LANES]
        up = pltpu.roll(c, LANES - ROPE_HALF // 2, 1)
        dn = pltpu.roll(c, ROPE_HALF // 2, 1)
        p_ref[:, j * LANES:(j + 1) * LANES] = (c * cos + up * sa + dn * sb).astype(BF16)
    p_ref[:, QK_W:] = acc[:, QK_W:].astype(BF16)


def _rope_tables(dec_seq):
    rows = dec_seq // GRID_W
    row = jnp.repeat(jnp.arange(rows), GRID_W).astype(F32)
    col = jnp.tile(jnp.arange(GRID_W), rows).astype(F32)
    inv = 1.0 / (ROPE_THETA ** (jnp.arange(0, ROPE_HALF, 2, dtype=F32) / ROPE_HALF))
    ar = row[:, None] * inv[None, :]
    ac = col[:, None] * inv[None, :]
    cos = jnp.concatenate([jnp.cos(ar), jnp.cos(ar), jnp.cos(ac), jnp.cos(ac)], axis=-1)
    sin = jnp.concatenate([jnp.sin(ar), jnp.sin(ar), jnp.sin(ac), jnp.sin(ac)], axis=-1)
    first = (jnp.arange(HEAD_DIM) % ROPE_HALF) < ROPE_HALF // 2
    sa = jnp.where(first[None, :], -sin, 0.0)
    sb = jnp.where(first[None, :], 0.0, sin)
    def table(t, ident):
        t = jnp.concatenate([jnp.full((TOK_TILE, HEAD_DIM), ident, F32), t], axis=0)
        return jnp.tile(t, (1, LANES // HEAD_DIM))
    return table(cos, 1.0), table(sa, 0.0), table(sb, 0.0)


def _group_of_tile(i, tile, n_ctx, dec_seq):
    tok = i * tile
    return jnp.where(tok < n_ctx, 0, 1 + (tok - n_ctx) // dec_seq)


def _inproj(x, mod, norm_g, w_in_bf, gqk, hsum, hbc, rope, n_ctx, dec_seq):
    n, d = x.shape
    t = TOK_TILE
    grp = functools.partial(_group_of_tile, tile=t, n_ctx=n_ctx, dec_seq=dec_seq)

    def mod_spec(which):
        return pl.BlockSpec((1, 1, d), lambda i: (grp(i) * 6 + which, 0, 0))

    def rope_idx(i):
        tok = i * t
        return (jnp.where(tok < n_ctx, 0, 1 + ((tok - n_ctx) % dec_seq) // t), 0)

    rope_spec = pl.BlockSpec((t, LANES), rope_idx)
    const = lambda shape: pl.BlockSpec(shape, lambda i: (0,) * len(shape))
    return pl.pallas_call(
        _inproj_kernel,
        grid=(n // t,),
        in_specs=[
            pl.BlockSpec((t, d), lambda i: (i, 0)),
            mod_spec(MOD_SH1), mod_spec(MOD_SC1),
            const((1, d)),
            const((d, IN_COLS)),
            const((1, QK_W)), const((QK_W, LANES)), const((LANES, QK_W)),
            rope_spec, rope_spec, rope_spec,
        ],
        out_specs=[
            pl.BlockSpec((t, IN_COLS), lambda i: (i, 0)),
            pl.BlockSpec((t, KV_W), lambda i: (i, 0)),
            pl.BlockSpec((t, KV_W), lambda i: (i, 0)),
        ],
        out_shape=[
            jax.ShapeDtypeStruct((n, IN_COLS), BF16),
            jax.ShapeDtypeStruct((n, KV_W), F32),
            jax.ShapeDtypeStruct((n, KV_W), F32),
        ],
        compiler_params=_cparams("arbitrary"),
    )(x, mod, mod, norm_g, w_in_bf, gqk, hsum, hbc, *rope)


def _head_halves(x, hkv, low):
    r = pltpu.roll(x, HEAD_DIM, 1)
    rep = jnp.where(low, x, r) if hkv == 0 else jnp.where(low, r, x)
    return jnp.where(low, rep, 0.0).astype(BF16), jnp.where(low, 0.0, rep).astype(BF16)


def _make_attn_kernel(has_ctx):
    def kern(*refs):
        if has_ctx:
            q_ref, k_ref, v_ref, ck_ref, cv_ref, o_ref = refs
        else:
            q_ref, k_ref, v_ref, o_ref = refs
        low = lax.broadcasted_iota(I32, (1, LANES), 1) < HEAD_DIM
        k = k_ref[...].astype(F32)
        v = v_ref[...].astype(F32)
        if has_ctx:
            ck = ck_ref[0, 0]
            cv = cv_ref[0, 0]
        scale = HEAD_DIM ** -0.5
        for hkv in range(N_KV_HEADS):
            kh = _head_halves(k, hkv, low)
            vh = _head_halves(v, hkv, low)
            if has_ctx:
                ckh = _head_halves(ck, hkv, low)
                cvh = _head_halves(cv, hkv, low)
            for c in range(2):
                col = hkv * 2 * LANES + c * LANES
                qc = q_ref[:, col:col + LANES] * scale
                o_c = None
                for j in range(2):
                    s = _dot_nt(qc, kh[j])
                    m = jnp.max(s, axis=-1, keepdims=True)
                    if has_ctx:
                        s2 = _dot_nt(qc, ckh[j])
                        m = jnp.maximum(m, jnp.max(s2, axis=-1, keepdims=True))
                    p = jnp.exp(s - m)
                    l = jnp.sum(p, axis=-1, keepdims=True)
                    o = _dot(p.astype(BF16), vh[j])
                    if has_ctx:
                        p2 = jnp.exp(s2 - m)
                        l = l + jnp.sum(p2, axis=-1, keepdims=True)
                        o = o + _dot(p2.astype(BF16), cvh[j])
                    o = o / l
                    o_c = o if o_c is None else o_c + o
                o_ref[:, col:col + LANES] = o_c.astype(BF16)
    return kern


def _attention(p, out, n_seq, seq_len, row0, ctx_kv, layer):
    n = p.shape[0]
    qb = ATTN_QB
    nq = seq_len // qb
    qrow0 = row0 // qb
    srow0 = row0 // seq_len
    has_ctx = ctx_kv is not None
    in_specs = [
        pl.BlockSpec((qb, ATTN_W), lambda b, i: (qrow0 + b * nq + i, 0)),
        pl.BlockSpec((seq_len, KV_W), lambda b, i: (srow0 + b, ATTN_W // KV_W)),
        pl.BlockSpec((seq_len, KV_W), lambda b, i: (srow0 + b, OFF_V // KV_W)),
    ]
    args = [p, p, p]
    if has_ctx:
        ck, cv = ctx_kv
        past = ck.shape[2]
        spec = pl.BlockSpec((1, 1, past, KV_W), lambda b, i: (b, layer, 0, 0))
        in_specs += [spec, spec]
        args += [ck, cv]
    in_specs.append(pl.BlockSpec(memory_space=pl.ANY))
    args.append(out)

    body = _make_attn_kernel(has_ctx)

    def kern(*refs):
        body(*refs[:-2], refs[-1])

    return pl.pallas_call(
        kern,
        grid=(n_seq, nq),
        in_specs=in_specs,
        out_specs=pl.BlockSpec((qb, ATTN_W), lambda b, i: (qrow0 + b * nq + i, 0)),
        out_shape=jax.ShapeDtypeStruct((n, ATTN_W), BF16),
        input_output_aliases={len(args) - 1: 0},
        compiler_params=_cparams("arbitrary", "arbitrary"),
    )(*args)


def _make_conv_kernel(seq_len):
    def kern(a_ref, g_ref, w_ref, b_ref, lng_ref, lnb_ref, o_ref, zp_ref):
        zero = jnp.zeros((CONV_HALO, CONV_CH), F32)
        zp_ref[0:CONV_HALO, :] = zero
        zp_ref[CONV_HALO + seq_len:2 * CONV_HALO + seq_len, :] = zero
        zp_ref[CONV_HALO:CONV_HALO + seq_len, :] = (
            a_ref[...].astype(F32) * jax.nn.sigmoid(g_ref[...].astype(F32)))
        w = w_ref[...]
        bias = b_ref[...]
        for c in range(seq_len // CONV_ROWS):
            base = c * CONV_ROWS + CONV_HALO - CONV_PAD
            acc = jnp.zeros((CONV_ROWS, CONV_CH), F32) + bias
            for j in range(CONV_K):
                acc = acc + zp_ref[base + j:base + j + CONV_ROWS, :] * w[j:j + 1, :]
            mu = jnp.mean(acc, axis=-1, keepdims=True)
            dlt = acc - mu
            var = jnp.mean(dlt * dlt, axis=-1, keepdims=True)
            y = dlt * lax.rsqrt(var + LN_EPS) * lng_ref[...] + lnb_ref[...]
            o_ref[c * CONV_ROWS:(c + 1) * CONV_ROWS, :] = (y * jax.nn.sigmoid(y)).astype(BF16)
    return kern


def _conv(p, out, n_seq, seq_len, row0, w, b, lng, lnb):
    n = p.shape[0]
    srow0 = row0 // seq_len
    const = lambda shape: pl.BlockSpec(shape, lambda s: (0,) * len(shape))
    body = _make_conv_kernel(seq_len)

    def kern(a_ref, g_ref, w_ref, b_ref, lng_ref, lnb_ref, _, o_ref, zp_ref):
        body(a_ref, g_ref, w_ref, b_ref, lng_ref, lnb_ref, o_ref, zp_ref)

    return pl.pallas_call(
        kern,
        grid=(n_seq,),
        in_specs=[
            pl.BlockSpec((seq_len, CONV_CH), lambda s: (srow0 + s, OFF_CONV // CONV_CH)),
            pl.BlockSpec((seq_len, CONV_CH), lambda s: (srow0 + s, OFF_CONV // CONV_CH + 1)),
            const((CONV_K, CONV_CH)), const((1, CONV_CH)), const((1, CONV_CH)), const((1, CONV_CH)),
            pl.BlockSpec(memory_space=pl.ANY),
        ],
        out_specs=pl.BlockSpec((seq_len, CONV_CH), lambda s: (srow0 + s, 0)),
        out_shape=jax.ShapeDtypeStruct((n, CONV_CH), BF16),
        scratch_shapes=[pltpu.VMEM((seq_len + 2 * CONV_HALO, CONV_CH), F32)],
        input_output_aliases={6: 0},
        compiler_params=_cparams("arbitrary"),
    )(p, p, w, b, lng, lnb, out)


def _make_ret_kernel(seq_len, has_init):
    qb = min(seq_len, RET_QB)
    nq = seq_len // qb
    scale = HEAD_DIM ** -0.5

    def kern(*refs):
        if has_init:
            lg_ref, q_ref, k_ref, v_ref, g_ref, gn_ref, gm_ref, r0_ref, o_ref, st_ref = refs
        else:
            lg_ref, q_ref, k_ref, v_ref, g_ref, gn_ref, gm_ref, o_ref, st_ref = refs
        low = lax.broadcasted_iota(I32, (1, LANES), 1) < HEAD_DIM
        pos = lax.broadcasted_iota(I32, (seq_len, 1), 0).astype(F32)
        qpos = lax.broadcasted_iota(I32, (qb, 1), 0).astype(F32)
        kpos = lax.broadcasted_iota(I32, (1, seq_len), 1).astype(F32)
        gm = gm_ref[...]
        for c in range(2):
            cs = slice(c * LANES, (c + 1) * LANES)
            lgf = jnp.where(low, lg_ref[0, 2 * c], lg_ref[0, 2 * c + 1])
            lgb = jnp.where(low, lg_ref[1, 2 * c], lg_ref[1, 2 * c + 1])
            qc = q_ref[:, cs]
            kc = k_ref[:, cs]
            vc = v_ref[:, cs]
            kf = kc.astype(F32) * scale
            zeta_f = jnp.exp((seq_len - 1.0 - pos) * lgf)
            zeta_b = jnp.exp(pos * lgb)
            for d, zeta, lgd in ((0, zeta_f, lgf), (1, zeta_b, lgb)):
                st = _dot_tn((kf * zeta).astype(BF16), vc)
                if has_init:
                    st = st + r0_ref[0, d, c] * jnp.exp(seq_len * lgd)
                st_ref[0, d, c] = st
            y_blocks = [None] * nq
            for j in range(2):
                sel = low if j == 0 else jnp.logical_not(low)
                kh = jnp.where(sel, kc, jnp.zeros_like(kc))
                vh = jnp.where(sel, vc, jnp.zeros_like(vc))
                lf = lg_ref[0, 2 * c + j]
                lb = lg_ref[1, 2 * c + j]
                for i in range(nq):
                    s = _dot_nt(qc[i * qb:(i + 1) * qb], kh)
                    diff = (qpos + float(i * qb)) - kpos
                    dec = jnp.exp(jnp.where(diff >= 0, diff * lf, -diff * lb))
                    dec = dec * jnp.where(diff == 0, 2.0 * scale, scale)
                    y = _dot((s * dec).astype(BF16), vh)
                    y_blocks[i] = y if y_blocks[i] is None else y_blocks[i] + y
            y = jnp.concatenate(y_blocks, axis=0) if nq > 1 else y_blocks[0]
            if has_init:
                xi_f = jnp.exp((pos + 1.0) * lgf)
                xi_b = jnp.exp((seq_len - pos) * lgb)
                y = y + _dot(qc, r0_ref[0, 0, c].astype(BF16)) * xi_f
                y = y + _dot(qc, r0_ref[0, 1, c].astype(BF16)) * xi_b
            y_hi, y_lo = _split_bf16(y)
            mu = _dot(y_hi, gm) + _dot(y_lo, gm)
            dlt = y - mu
            var = _dot((dlt * dlt).astype(BF16), gm)
            yn = dlt * lax.rsqrt(var + LN_EPS) * gn_ref[:, cs]
            gate = g_ref[:, cs].astype(F32)
            o_ref[:, cs] = (gate * jax.nn.sigmoid(gate) * yn).astype(BF16)
    return kern


def _retention(p, out, n_seq, seq_len, row0, log_g, gn_g, gmat, r0):
    n = p.shape[0]
    srow0 = row0 // seq_len
    has_init = r0 is not None
    body = _make_ret_kernel(seq_len, has_init)
    nin = 8 if has_init else 7

    def kern(*refs):
        body(*refs[:nin], *refs[nin + 1:])

    col = OFF_RET // RET_W
    in_specs = [pl.BlockSpec(memory_space=pltpu.SMEM)]
    in_specs += [pl.BlockSpec((seq_len, RET_W), functools.partial(lambda s, j: (srow0 + s, col + j), j=j))
                 for j in range(4)]
    in_specs += [pl.BlockSpec((1, RET_W), lambda s: (0, 0)), pl.BlockSpec((LANES, LANES), lambda s: (0, 0))]
    args = [log_g, p, p, p, p, gn_g, gmat]
    st_spec = pl.BlockSpec((1, 2, 2, LANES, LANES), lambda s: (s, 0, 0, 0, 0))
    if has_init:
        in_specs.append(st_spec)
        args.append(r0)
    in_specs.append(pl.BlockSpec(memory_space=pl.ANY))
    args.append(out)
    return pl.pallas_call(
        kern,
        grid=(n_seq,),
        in_specs=in_specs,
        out_specs=[pl.BlockSpec((seq_len, RET_W), lambda s: (srow0 + s, 0)), st_spec],
        out_shape=[jax.ShapeDtypeStruct((n, RET_W), BF16),
                   jax.ShapeDtypeStruct((n_seq, 2, 2, LANES, LANES), F32)],
        input_output_aliases={len(args) - 1: 0},
        compiler_params=_cparams("arbitrary"),
    )(*args)


def _outproj_kernel(a_ref, c_ref, r_ref, x_ref, g1_ref, sc_ref, sh_ref, n2_ref, wo_ref, rwt_ref, rb_ref,
                    x1_ref, h2_ref, te_ref, tg_ref):
    mixed = (_dot(a_ref[...], wo_ref[0:ATTN_W, :])
             + _dot(c_ref[...], wo_ref[ATTN_W:ATTN_W + CONV_CH, :])
             + _dot(r_ref[...], wo_ref[ATTN_W + CONV_CH:, :]))
    x1 = x_ref[...] + g1_ref[0] * mixed
    x1_ref[...] = x1
    inv = lax.rsqrt(jnp.mean(x1 * x1, axis=-1, keepdims=True) + EPS)
    h2 = (x1 * inv * n2_ref[...]) * (1.0 + sc_ref[0]) + sh_ref[0]
    h2_ref[...] = h2
    logits = _dot_nt(rwt_ref[...], h2, precision=lax.Precision.HIGHEST) + rb_ref[...]
    t = logits.shape[1]
    eidx = lax.broadcasted_iota(I32, (N_EXPERTS, t), 0).astype(F32)
    vals = logits
    tops, idxs = [], []
    for _ in range(TOP_K):
        m = jnp.max(vals, axis=0, keepdims=True)
        idx = jnp.min(jnp.where(vals == m, eidx, float(N_EXPERTS)), axis=0, keepdims=True)
        tops.append(m)
        idxs.append(idx)
        vals = jnp.where(eidx == idx, -jnp.inf, vals)
    es = [jnp.exp(m - tops[0]) for m in tops]
    tot = es[0] + es[1] + es[2] + es[3]
    te_ref[...] = jnp.concatenate(idxs + [jnp.zeros((8 - TOP_K, t), F32)], axis=0).astype(I32)
    tg_ref[...] = jnp.concatenate([e / tot for e in es] + [jnp.zeros((8 - TOP_K, t), F32)], axis=0)


def _outproj(attn, conv, ret, x, mod, norm_g, w_out_bf, rwt, rb, n_ctx, dec_seq):
    n, d = x.shape
    t = TOK_TILE
    grp = functools.partial(_group_of_tile, tile=t, n_ctx=n_ctx, dec_seq=dec_seq)

    def mod_spec(which):
        return pl.BlockSpec((1, 1, d), lambda i: (grp(i) * 6 + which, 0, 0))

    const = lambda shape: pl.BlockSpec(shape, lambda i: (0,) * len(shape))
    row = lambda w: pl.BlockSpec((t, w), lambda i: (i, 0))
    lane = lambda: pl.BlockSpec((8, t), lambda i: (0, i))
    return pl.pallas_call(
        _outproj_kernel,
        grid=(n // t,),
        in_specs=[row(ATTN_W), row(CONV_CH), row(RET_W), row(d),
                  mod_spec(MOD_G1), mod_spec(MOD_SC2), mod_spec(MOD_SH2),
                  const((1, d)), const((d, d)), const((N_EXPERTS, d)), const((N_EXPERTS, 1))],
        out_specs=[row(d), row(d), lane(), lane()],
        out_shape=[jax.ShapeDtypeStruct((n, d), F32), jax.ShapeDtypeStruct((n, d), F32),
                   jax.ShapeDtypeStruct((8, n), I32), jax.ShapeDtypeStruct((8, n), F32)],
        compiler_params=_cparams("arbitrary"),
    )(attn, conv, ret, x, mod, mod, mod, norm_g, w_out_bf, rwt, rb)


def _route_kernel(te_ref, u_ref, ltri_ref, dest_ref, cnt_ref, run_ref, start_ref):
    ph = pl.program_id(0)
    i = pl.program_id(1)
    te = te_ref[...]
    t = te.shape[1]
    eidx = lax.broadcasted_iota(I32, (N_EXPERTS, t), 0)
    hits = [eidx == te[k:k + 1, :] for k in range(TOP_K)]
    onehot = sum(h.astype(F32) for h in hits)
    tile_cnt = jnp.sum(onehot, axis=1, keepdims=True)

    @pl.when(jnp.logical_and(ph == 0, i == 0))
    def _():
        run_ref[...] = jnp.zeros_like(run_ref)

    @pl.when(ph == 0)
    def _():
        run_ref[...] = run_ref[...] + tile_cnt

    @pl.when(jnp.logical_and(ph == 1, i == 0))
    def _():
        cnt = run_ref[...]
        cnt_ref[...] = cnt
        padded = jnp.floor((cnt + (MOE_BM - 1.0)) * (1.0 / MOE_BM)) * MOE_BM
        start_ref[...] = _dot(ltri_ref[...], padded, precision=lax.Precision.HIGHEST)
        run_ref[...] = jnp.zeros_like(run_ref)

    @pl.when(ph == 1)
    def _():
        before = _dot(onehot.astype(BF16), u_ref[...])
        base = before + run_ref[:, 0:1] + start_ref[:, 0:1]
        rows = [jnp.sum(jnp.where(h, base, 0.0), axis=0, keepdims=True) for h in hits]
        rows.append(jnp.zeros((8 - TOP_K, t), F32))
        dest_ref[...] = jnp.concatenate(rows, axis=0).astype(I32)
        run_ref[...] = run_ref[...] + tile_cnt


def _route(top_e, upper, ltri):
    n = top_e.shape[1]
    t = ROUTE_TILE
    return pl.pallas_call(
        _route_kernel,
        grid=(2, n // t),
        in_specs=[pl.BlockSpec((8, t), lambda ph, i: (0, i)),
                  pl.BlockSpec((t, t), lambda ph, i: (0, 0)),
                  pl.BlockSpec((N_EXPERTS, N_EXPERTS), lambda ph, i: (0, 0))],
        out_specs=[pl.BlockSpec((8, t), lambda ph, i: (0, i * ph)),
                   pl.BlockSpec((N_EXPERTS, LANES), lambda ph, i: (0, 0))],
        out_shape=[jax.ShapeDtypeStruct((8, n), I32), jax.ShapeDtypeStruct((N_EXPERTS, LANES), F32)],
        scratch_shapes=[pltpu.VMEM((N_EXPERTS, LANES), F32), pltpu.VMEM((N_EXPERTS, LANES), F32)],
        compiler_params=_cparams("arbitrary", "arbitrary"),
    )(top_e, upper, ltri)


def _row_copy(src_ref, src_row, dst_ref, dst_row, sem):
    return pltpu.make_async_copy(src_ref.at[pl.ds(src_row, 1)], dst_ref.at[pl.ds(dst_row, 1)], sem)


def _make_dispatch_kernel(n, n_blocks):
    t = MOVE_TILE

    def kern(dest_ref, cnt_ref, start_ref, nv_ref, h_ref, xs_ref, zeros_ref, sem, zsem):
        step = pl.program_id(0)
        base = step * t

        def for_each_pad_chunk(action):
            def per_expert(e, carry):
                c = cnt_ref[e]
                padlen = (-c) & (MOE_BM - 1)
                row = start_ref[e] + c
                head = padlen & 7
                for j in range(7):
                    @pl.when(j < head)
                    def _(j=j):
                        action(pltpu.make_async_copy(zeros_ref.at[pl.ds(0, 1)], xs_ref.at[pl.ds(row + j, 1)], zsem))

                row = row + head
                bit = MOE_BM // 2
                while bit >= 8:
                    hit = (padlen & bit) != 0

                    @pl.when(hit)
                    def _(row=row, bit=bit):
                        dst = xs_ref.at[pl.ds(pl.multiple_of(row, 8), bit)]
                        action(pltpu.make_async_copy(zeros_ref.at[pl.ds(0, bit)], dst, zsem))

                    row = row + jnp.where(hit, bit, 0)
                    bit //= 2
                return carry

            lax.fori_loop(0, N_EXPERTS, per_expert, 0)

            def per_tail_block(b, carry):
                action(pltpu.make_async_copy(zeros_ref, xs_ref.at[pl.ds(b * MOE_BM, MOE_BM)], zsem))
                return carry

            lax.fori_loop(nv_ref[0], n_blocks, per_tail_block, 0)

        @pl.when(step == 0)
        def _():
            zeros_ref[...] = jnp.zeros_like(zeros_ref)
            for_each_pad_chunk(lambda cp: cp.start())
            for_each_pad_chunk(lambda cp: cp.wait())

        def issue(r, carry):
            for k in range(TOP_K):
                _row_copy(h_ref, r, xs_ref, dest_ref[k * n + base + r], sem).start()
            return carry

        lax.fori_loop(0, t, issue, 0, unroll=8)
        for k in range(TOP_K):
            pltpu.make_async_copy(h_ref, xs_ref.at[pl.ds(0, t)], sem).wait()
    return kern


def _dispatch(dest_flat, counts, starts, n_valid, h2, n_rows):
    n, d = h2.shape
    t = MOVE_TILE
    return pl.pallas_call(
        _make_dispatch_kernel(n, n_rows // MOE_BM),
        grid_spec=pltpu.PrefetchScalarGridSpec(
            num_scalar_prefetch=4,
            grid=(n // t,),
            in_specs=[pl.BlockSpec((t, d), lambda i, *_: (i, 0))],
            out_specs=pl.BlockSpec(memory_space=pl.ANY),
            scratch_shapes=[pltpu.VMEM((MOE_BM, d), F32), pltpu.SemaphoreType.DMA(()),
                            pltpu.SemaphoreType.DMA(())],
        ),
        out_shape=jax.ShapeDtypeStruct((n_rows, d), F32),
        compiler_params=_cparams("arbitrary"),
    )(dest_flat, counts, starts, n_valid, h2)


def _expert_kernel(be_ref, bs_ref, nv_ref, x_ref, wgu_ref, bgu_ref, wdn_ref, bdn_ref, y_ref, wgu_s, wdn_s):
    i = pl.program_id(0)
    prev = be_ref[jnp.maximum(i - 1, 0)]
    new_expert = jnp.logical_or(i == 0, be_ref[i] != prev)

    @pl.when(new_expert)
    def _():
        wgu_s[...] = wgu_ref[0, 0].astype(BF16)
        wdn_s[...] = wdn_ref[0, 0].astype(BF16)

    @pl.when(i < nv_ref[0])
    def _():
        x = x_ref[...].astype(BF16)
        gu = _dot(x, wgu_s[...]) + bgu_ref[0, 0]
        gate = jnp.minimum(gu[:, :D_FF], SWIGLU_LIMIT)
        up = jnp.clip(gu[:, D_FF:], -SWIGLU_LIMIT, SWIGLU_LIMIT)
        hdn = (up + 1.0) * (gate * jax.nn.sigmoid(SWIGLU_ALPHA * gate))
        y_ref[...] = _dot(hdn.astype(BF16), wdn_s[...]) + bdn_ref[0, 0]

    @pl.when(i >= nv_ref[0])
    def _():
        y_ref[...] = jnp.zeros_like(y_ref)


def _experts(blk_e, blk_src, n_valid, xs, w_gu, b_gu, w_dn, b_dn, layer):
    r, d = xs.shape
    nb = r // MOE_BM
    depth = w_gu.shape[0]

    def per_expert(*shape):
        return pl.BlockSpec((1, 1) + shape, lambda i, be, bs, nv: (layer, be[i], 0, 0))

    return pl.pallas_call(
        _expert_kernel,
        grid_spec=pltpu.PrefetchScalarGridSpec(
            num_scalar_prefetch=3,
            grid=(nb,),
            in_specs=[pl.BlockSpec((MOE_BM, d), lambda i, be, bs, nv: (bs[i], 0)),
                      per_expert(d, 2 * D_FF), per_expert(1, 2 * D_FF),
                      per_expert(D_FF, d), per_expert(1, d)],
            out_specs=pl.BlockSpec((MOE_BM, d), lambda i, be, bs, nv: (i, 0)),
            scratch_shapes=[pltpu.VMEM((d, 2 * D_FF), BF16), pltpu.VMEM((D_FF, d), BF16)],
        ),
        out_shape=jax.ShapeDtypeStruct((r, d), F32),
        compiler_params=_cparams("arbitrary"),
    )(blk_e, blk_src, n_valid, xs, w_gu, b_gu.reshape(depth, N_EXPERTS, 1, -1), w_dn,
      b_dn.reshape(depth, N_EXPERTS, 1, -1))


def _make_combine_kernel(n, final):
    t = MOVE_TILE

    def kern(dest_ref, ys_ref, x1_ref, gt_ref, g2_ref, fg_ref, o_ref, buf, sem):
        base = pl.program_id(0) * t

        def issue(r, carry):
            for k in range(TOP_K):
                _row_copy(ys_ref, dest_ref[k * n + base + r], buf.at[k], r, sem).start()
            return carry

        lax.fori_loop(0, t, issue, 0, unroll=8)
        for k in range(TOP_K):
            pltpu.make_async_copy(ys_ref.at[pl.ds(0, t)], buf.at[k], sem).wait()
        gt = gt_ref[...]
        y = buf[0] * gt[:, 0:1]
        for k in range(1, TOP_K):
            y = y + buf[k] * gt[:, k:k + 1]
        x2 = x1_ref[...] + g2_ref[0] * y
        if final:
            x2 = x2 * lax.rsqrt(jnp.mean(x2 * x2, axis=-1, keepdims=True) + EPS) * fg_ref[...]
        o_ref[...] = x2
    return kern


def _combine(dest_flat, ys, x1, gates_t, mod, final_g, final, n_ctx, dec_seq):
    n, d = x1.shape
    t = MOVE_TILE
    grp = functools.partial(_group_of_tile, tile=t, n_ctx=n_ctx, dec_seq=dec_seq)
    return pl.pallas_call(
        _make_combine_kernel(n, final),
        grid_spec=pltpu.PrefetchScalarGridSpec(
            num_scalar_prefetch=1,
            grid=(n // t,),
            in_specs=[pl.BlockSpec(memory_space=pl.ANY),
                      pl.BlockSpec((t, d), lambda i, dest: (i, 0)),
                      pl.BlockSpec((t, 8), lambda i, dest: (i, 0)),
                      pl.BlockSpec((1, 1, d), lambda i, dest: (grp(i) * 6 + MOD_G2, 0, 0)),
                      pl.BlockSpec((1, d), lambda i, dest: (0, 0))],
            out_specs=pl.BlockSpec((t, d), lambda i, dest: (i, 0)),
            scratch_shapes=[pltpu.VMEM((TOP_K, t, d), F32), pltpu.SemaphoreType.DMA(())],
        ),
        out_shape=jax.ShapeDtypeStruct((n, d), F32),
        compiler_params=_cparams("arbitrary"),
    )(dest_flat, ys, x1, gates_t, mod, final_g)


def _blockdiag_pairs(s):
    z = jnp.zeros_like(s[..., 0, :, :])
    def pair(a, b):
        return jnp.concatenate([jnp.concatenate([a, z], axis=-1), jnp.concatenate([z, b], axis=-1)], axis=-2)
    return jnp.stack([pair(s[..., 0, :, :], s[..., 1, :, :]), pair(s[..., 2, :, :], s[..., 3, :, :])], axis=-3)


def _diag_blocks(st):
    h = HEAD_DIM
    blocks = [st[:, :, c, j * h:(j + 1) * h, j * h:(j + 1) * h] for c in range(2) for j in range(2)]
    return jnp.stack(blocks, axis=2)


def kernel(x_prompt, x_sample, cache_k, cache_v, state_ret, c, c_ctx, w_mod, b_mod, norm1_g, norm2_g, w_in,
           q_norm_g, k_norm_g, conv_w, conv_b, conv_ln_g, conv_ln_b, ret_decay_logit, ret_gn_g, w_out,
           router_w, router_b, moe_w_gu, moe_b_gu, moe_w_dn, moe_b_dn, final_g):
    batch, seq, d = x_prompt.shape
    dec_batch, dec_seq, _ = x_sample.shape
    depth = w_mod.shape[0]
    past = cache_k.shape[2]
    n_ctx = batch * seq
    n_lat = dec_batch * dec_seq
    n = n_ctx + n_lat
    assert d == D_MODEL and dec_batch == 2
    assert n_ctx % dec_seq == 0 and dec_seq % TOK_TILE == 0 and seq % ATTN_QB == 0 and dec_seq % ATTN_QB == 0

    x = jnp.concatenate([x_prompt.reshape(n_ctx, d), x_sample.reshape(n_lat, d)], axis=0)
    mods = _modulation(jnp.concatenate([c_ctx[None, :], c], axis=0), w_mod, b_mod)
    rope = _rope_tables(dec_seq)

    head_of_col = jnp.arange(QK_W) // HEAD_DIM
    hsum = (head_of_col[:, None] == jnp.arange(LANES)[None, :]).astype(BF16)
    hbc = hsum.T
    lane_head = jnp.arange(LANES) // HEAD_DIM
    gmat = ((lane_head[:, None] == lane_head[None, :]).astype(F32) / HEAD_DIM).astype(BF16)
    tt = jnp.arange(ROUTE_TILE)
    upper = (tt[:, None] < tt[None, :]).astype(BF16)
    ee = jnp.arange(N_EXPERTS)
    ltri = (ee[None, :] < ee[:, None]).astype(F32)

    n_rows = n * TOP_K + N_EXPERTS * MOE_BM
    n_blocks = n_rows // MOE_BM
    cache_k2 = cache_k.reshape(dec_batch, depth, past, KV_W)
    cache_v2 = cache_v.reshape(dec_batch, depth, past, KV_W)

    ks_out, vs_out, ss_out = [], [], []
    for l in range(depth):
        mod = mods[l]
        gqk = jnp.concatenate([jnp.tile(q_norm_g[l], N_Q_HEADS), jnp.tile(k_norm_g[l], N_KV_HEADS)])[None, :]
        p, kn, vv = _inproj(x, mod, norm1_g[l][None, :], w_in[l].astype(BF16), gqk, hsum, hbc, rope,
                            n_ctx, dec_seq)
        ks_out.append(kn[:n_ctx].reshape(batch, seq, N_KV_HEADS, HEAD_DIM))
        vs_out.append(vv[:n_ctx].reshape(batch, seq, N_KV_HEADS, HEAD_DIM))

        attn = jnp.zeros((n, ATTN_W), BF16)
        attn = _attention(p, attn, batch, seq, 0, None, l)
        attn = _attention(p, attn, dec_batch, dec_seq, n_ctx, (cache_k2, cache_v2), l)

        cw, cb = conv_w[l], conv_b[l][None, :]
        clg, clb = conv_ln_g[l][None, :], conv_ln_b[l][None, :]
        conv = jnp.zeros((n, CONV_CH), BF16)
        conv = _conv(p, conv, batch, seq, 0, cw, cb, clg, clb)
        conv = _conv(p, conv, dec_batch, dec_seq, n_ctx, cw, cb, clg, clb)

        log_g = jax.nn.log_sigmoid(ret_decay_logit[l].astype(F32))
        gn = ret_gn_g[l][None, :]
        ret = jnp.zeros((n, RET_W), BF16)
        ret, st_ctx = _retention(p, ret, batch, seq, 0, log_g, gn, gmat, None)
        ret, _ = _retention(p, ret, dec_batch, dec_seq, n_ctx, log_g, gn, gmat,
                            _blockdiag_pairs(state_ret[:, l].astype(F32)))
        ss_out.append(_diag_blocks(st_ctx))

        x1, h2, top_e, top_g = _outproj(attn, conv, ret, x, mod, norm2_g[l][None, :], w_out[l].astype(BF16),
                                        router_w[l].T, router_b[l][:, None], n_ctx, dec_seq)
        dest, cnt = _route(top_e, upper, ltri)
        counts = cnt[:, 0].astype(I32)
        padded = (counts + MOE_BM - 1) // MOE_BM * MOE_BM
        pad_end = jnp.cumsum(padded)
        n_valid = (pad_end[-1] // MOE_BM).reshape(1)
        blk_src = jnp.minimum(jnp.arange(n_blocks, dtype=I32), n_valid - 1)
        blk_e = jnp.minimum(jnp.sum((pad_end[None, :] <= (blk_src * MOE_BM)[:, None]).astype(I32), axis=1),
                            N_EXPERTS - 1)
        dest_flat = dest[:TOP_K].reshape(-1)

        xs = _dispatch(dest_flat, counts, pad_end - padded, n_valid, h2, n_rows)
        ys = _experts(blk_e, blk_src, n_valid, xs, moe_w_gu, moe_b_gu, moe_w_dn, moe_b_dn, l)
        x = _combine(dest_flat, ys, x1, top_g.T, mod, final_g[None, :], l == depth - 1, n_ctx, dec_seq)

    y_prompt = x[:n_ctx].reshape(batch, seq, d)
    y_sample = x[n_ctx:].reshape(dec_batch, dec_seq, d)
    return (y_prompt, y_sample, jnp.stack(ks_out, axis=1), jnp.stack(vs_out, axis=1),
            jnp.stack(ss_out, axis=1))
```

```python
import functools

import numpy as np
import jax
import jax.numpy as jnp
from jax import lax
from jax.experimental import pallas as pl
from jax.experimental.pallas import tpu as pltpu

F32 = jnp.float32
BF16 = jnp.bfloat16
I32 = jnp.int32

D_MODEL = 1024
GRID_W = 64
HEAD_DIM = 64
N_Q_HEADS = 8
N_KV_HEADS = 2
ATTN_W = N_Q_HEADS * HEAD_DIM
KV_W = N_KV_HEADS * HEAD_DIM
QK_W = ATTN_W + KV_W
CONV_CH = 256
CONV_K = 31
CONV_PAD = CONV_K // 2
CONV_HALO = 16
N_RET_HEADS = 4
RET_W = 256
OFF_V = QK_W
OFF_CONV = OFF_V + KV_W
OFF_RET = OFF_CONV + 2 * CONV_CH
IN_COLS = OFF_RET + 4 * RET_W
ROPE_HALF = HEAD_DIM // 2
ROPE_THETA = 10000.0
N_EXPERTS = 32
TOP_K = 4
D_FF = D_MODEL
SWIGLU_LIMIT = 7.0
SWIGLU_ALPHA = 1.702
EPS = 1e-6
LN_EPS = 1e-5

LANES = 128
TOK_TILE = 512
ATTN_QB = 256
RET_QB = 256
CONV_ROWS = 64
MOE_BM = 256
MOVE_TILE = 256
SUBLANES = 8
COMPACT_ROWS = 1280
assert COMPACT_ROWS >= MOVE_TILE * TOP_K + N_EXPERTS * (SUBLANES - 1) and COMPACT_ROWS % MOE_BM == 0
CHUNK_SIZES = tuple(MOVE_TILE >> s for s in range(6))
VMEM_LIMIT = 56 * 1024 * 1024

MOD_SH1, MOD_SC1, MOD_G1, MOD_SH2, MOD_SC2, MOD_G2 = range(6)


def _cparams(*sem):
    return pltpu.CompilerParams(dimension_semantics=sem, vmem_limit_bytes=VMEM_LIMIT)


def _dot(a, b, **kw):
    return jnp.dot(a, b, preferred_element_type=F32, **kw)


def _dot_nt(a, b, **kw):
    return lax.dot_general(a, b, (((1,), (1,)), ((), ())), preferred_element_type=F32, **kw)


def _dot_tn(a, b, **kw):
    return lax.dot_general(a, b, (((0,), (0,)), ((), ())), preferred_element_type=F32, **kw)


def _split_bf16(x):
    hi = x.astype(BF16)
    lo = (x - hi.astype(F32)).astype(BF16)
    return hi, lo


MOD_TN = 1536


def _mod_kernel(ct_ref, w_ref, b_ref, o_ref):
    s = ct_ref[...]
    s = s * jax.nn.sigmoid(s)
    w = w_ref[0]
    rows = [jnp.sum(w * s[:, r:r + 1], axis=0, keepdims=True) for r in range(3)]
    rows.append(jnp.zeros((5, w.shape[1]), F32))
    o_ref[0] = jnp.concatenate(rows, axis=0) + b_ref[0]


def _modulation(cvec3, w_mod, b_mod):
    depth, d, cols = w_mod.shape
    ct = jnp.zeros((d, 8), F32).at[:, :3].set(cvec3.T)
    out = pl.pallas_call(
        _mod_kernel,
        grid=(depth, cols // MOD_TN),
        in_specs=[
            pl.BlockSpec((d, 8), lambda l, j: (0, 0)),
            pl.BlockSpec((1, d, MOD_TN), lambda l, j: (l, 0, j)),
            pl.BlockSpec((1, 1, MOD_TN), lambda l, j: (l, 0, j)),
        ],
        out_specs=pl.BlockSpec((1, 8, MOD_TN), lambda l, j: (l, 0, j)),
        out_shape=jax.ShapeDtypeStruct((depth, 8, cols), F32),
        compiler_params=_cparams("arbitrary", "arbitrary"),
    )(ct, w_mod, b_mod.reshape(depth, 1, cols))
    return out[:, :3].reshape(depth, 3 * 6, 1, d)


def _inproj_kernel(x_ref, sh_ref, sc_ref, g_ref, w_ref, gqk_ref, hsum_ref, hbc_ref,
                   cos_ref, sa_ref, sb_ref, p_ref, k_ref, v_ref):
    x = x_ref[...]
    inv = lax.rsqrt(jnp.mean(x * x, axis=-1, keepdims=True) + EPS)
    h = (x * inv * g_ref[...]) * (1.0 + sc_ref[0]) + sh_ref[0]
    acc = _dot(h.astype(BF16), w_ref[...])
    qk = acc[:, :QK_W]
    ss = _dot((qk * qk).astype(BF16), hsum_ref[...])
    r = lax.rsqrt(ss * (1.0 / HEAD_DIM) + EPS)
    r_hi, r_lo = _split_bf16(r)
    rb = _dot(r_hi, hbc_ref[...]) + _dot(r_lo, hbc_ref[...])
    qkn = qk * rb * gqk_ref[...]
    k_ref[...] = qkn[:, ATTN_W:QK_W]
    v_ref[...] = acc[:, OFF_V:OFF_CONV]
    cos = cos_ref[...]
    sa = sa_ref[...]
    sb = sb_ref[...]
    for j in range(QK_W // LANES):
        c = qkn[:, j * LANES:(j + 1) * LANES]
        up = pltpu.roll(c, LANES - ROPE_HALF // 2, 1)
        dn = pltpu.roll(c, ROPE_HALF // 2, 1)
        p_ref[:, j * LANES:(j + 1) * LANES] = (c * cos + up * sa + dn * sb).astype(BF16)
    p_ref[:, QK_W:] = acc[:, QK_W:].astype(BF16)


def _rope_tables(dec_seq):
    rows = dec_seq // GRID_W
    row = jnp.repeat(jnp.arange(rows), GRID_W).astype(F32)
    col = jnp.tile(jnp.arange(GRID_W), rows).astype(F32)
    inv = 1.0 / (ROPE_THETA ** (jnp.arange(0, ROPE_HALF, 2, dtype=F32) / ROPE_HALF))
    ar = row[:, None] * inv[None, :]
    ac = col[:, None] * inv[None, :]
    cos = jnp.concatenate([jnp.cos(ar), jnp.cos(ar), jnp.cos(ac), jnp.cos(ac)], axis=-1)
    sin = jnp.concatenate([jnp.sin(ar), jnp.sin(ar), jnp.sin(ac), jnp.sin(ac)], axis=-1)
    first = (jnp.arange(HEAD_DIM) % ROPE_HALF) < ROPE_HALF // 2
    sa = jnp.where(first[None, :], -sin, 0.0)
    sb = jnp.where(first[None, :], 0.0, sin)
    def table(t, ident):
        t = jnp.concatenate([jnp.full((TOK_TILE, HEAD_DIM), ident, F32), t], axis=0)
        return jnp.tile(t, (1, LANES // HEAD_DIM))
    return table(cos, 1.0), table(sa, 0.0), table(sb, 0.0)


def _group_of_tile(i, tile, n_ctx, dec_seq):
    tok = i * tile
    return jnp.where(tok < n_ctx, 0, 1 + (tok - n_ctx) // dec_seq)


def _inproj(x, mod, norm_g, w_in_bf, gqk, hsum, hbc, rope, n_ctx, dec_seq):
    n, d = x.shape
    t = TOK_TILE
    grp = functools.partial(_group_of_tile, tile=t, n_ctx=n_ctx, dec_seq=dec_seq)

    def mod_spec(which):
        return pl.BlockSpec((1, 1, d), lambda i: (grp(i) * 6 + which, 0, 0))

    def rope_idx(i):
        tok = i * t
        return (jnp.where(tok < n_ctx, 0, 1 + ((tok - n_ctx) % dec_seq) // t), 0)

    rope_spec = pl.BlockSpec((t, LANES), rope_idx)
    const = lambda shape: pl.BlockSpec(shape, lambda i: (0,) * len(shape))
    return pl.pallas_call(
        _inproj_kernel,
        grid=(n // t,),
        in_specs=[
            pl.BlockSpec((t, d), lambda i: (i, 0)),
            mod_spec(MOD_SH1), mod_spec(MOD_SC1),
            const((1, d)),
            const((d, IN_COLS)),
            const((1, QK_W)), const((QK_W, LANES)), const((LANES, QK_W)),
            rope_spec, rope_spec, rope_spec,
        ],
        out_specs=[
            pl.BlockSpec((t, IN_COLS), lambda i: (i, 0)),
            pl.BlockSpec((t, KV_W), lambda i: (i, 0)),
            pl.BlockSpec((t, KV_W), lambda i: (i, 0)),
        ],
        out_shape=[
            jax.ShapeDtypeStruct((n, IN_COLS), BF16),
            jax.ShapeDtypeStruct((n, KV_W), F32),
            jax.ShapeDtypeStruct((n, KV_W), F32),
        ],
        compiler_params=_cparams("arbitrary"),
    )(x, mod, mod, norm_g, w_in_bf, gqk, hsum, hbc, *rope)


def _head_halves(x, hkv, low):
    r = pltpu.roll(x, HEAD_DIM, 1)
    rep = jnp.where(low, x, r) if hkv == 0 else jnp.where(low, r, x)
    return jnp.where(low, rep, 0.0).astype(BF16), jnp.where(low, 0.0, rep).astype(BF16)


def _make_attn_kernel(has_ctx):
    def kern(*refs):
        if has_ctx:
            q_ref, k_ref, v_ref, ck_ref, cv_ref, o_ref = refs
        else:
            q_ref, k_ref, v_ref, o_ref = refs
        low = lax.broadcasted_iota(I32, (1, LANES), 1) < HEAD_DIM
        k = k_ref[...].astype(F32)
        v = v_ref[...].astype(F32)
        if has_ctx:
            ck = ck_ref[0, 0]
            cv = cv_ref[0, 0]
        scale = HEAD_DIM ** -0.5
        for hkv in range(N_KV_HEADS):
            kh = _head_halves(k, hkv, low)
            vh = _head_halves(v, hkv, low)
            if has_ctx:
                ckh = _head_halves(ck, hkv, low)
                cvh = _head_halves(cv, hkv, low)
            for c in range(2):
                col = hkv * 2 * LANES + c * LANES
                qc = q_ref[:, col:col + LANES] * scale
                o_c = None
                for j in range(2):
                    s = _dot_nt(qc, kh[j])
                    m = jnp.max(s, axis=-1, keepdims=True)
                    if has_ctx:
                        s2 = _dot_nt(qc, ckh[j])
                        m = jnp.maximum(m, jnp.max(s2, axis=-1, keepdims=True))
                    p = jnp.exp(s - m)
                    l = jnp.sum(p, axis=-1, keepdims=True)
                    o = _dot(p.astype(BF16), vh[j])
                    if has_ctx:
                        p2 = jnp.exp(s2 - m)
                        l = l + jnp.sum(p2, axis=-1, keepdims=True)
                        o = o + _dot(p2.astype(BF16), cvh[j])
                    o = o / l
                    o_c = o if o_c is None else o_c + o
                o_ref[:, col:col + LANES] = o_c.astype(BF16)
    return kern


def _attention(p, out, n_seq, seq_len, row0, ctx_kv, layer):
    n = p.shape[0]
    qb = ATTN_QB
    nq = seq_len // qb
    qrow0 = row0 // qb
    srow0 = row0 // seq_len
    has_ctx = ctx_kv is not None
    in_specs = [
        pl.BlockSpec((qb, ATTN_W), lambda b, i: (qrow0 + b * nq + i, 0)),
        pl.BlockSpec((seq_len, KV_W), lambda b, i: (srow0 + b, ATTN_W // KV_W)),
        pl.BlockSpec((seq_len, KV_W), lambda b, i: (srow0 + b, OFF_V // KV_W)),
    ]
    args = [p, p, p]
    if has_ctx:
        ck, cv = ctx_kv
        past = ck.shape[2]
        spec = pl.BlockSpec((1, 1, past, KV_W), lambda b, i: (b, layer, 0, 0))
        in_specs += [spec, spec]
        args += [ck, cv]
    in_specs.append(pl.BlockSpec(memory_space=pl.ANY))
    args.append(out)

    body = _make_attn_kernel(has_ctx)

    def kern(*refs):
        body(*refs[:-2], refs[-1])

    return pl.pallas_call(
        kern,
        grid=(n_seq, nq),
        in_specs=in_specs,
        out_specs=pl.BlockSpec((qb, ATTN_W), lambda b, i: (qrow0 + b * nq + i, 0)),
        out_shape=jax.ShapeDtypeStruct((n, ATTN_W), BF16),
        input_output_aliases={len(args) - 1: 0},
        compiler_params=_cparams("arbitrary", "arbitrary"),
    )(*args)


def _make_conv_kernel(seq_len):
    def kern(a_ref, g_ref, w_ref, b_ref, lng_ref, lnb_ref, o_ref, zp_ref):
        zero = jnp.zeros((CONV_HALO, CONV_CH), F32)
        zp_ref[0:CONV_HALO, :] = zero
        zp_ref[CONV_HALO + seq_len:2 * CONV_HALO + seq_len, :] = zero
        zp_ref[CONV_HALO:CONV_HALO + seq_len, :] = (
            a_ref[...].astype(F32) * jax.nn.sigmoid(g_ref[...].astype(F32)))
        w = w_ref[...]
        bias = b_ref[...]
        for c in range(seq_len // CONV_ROWS):
            base = c * CONV_ROWS + CONV_HALO - CONV_PAD
            acc = jnp.zeros((CONV_ROWS, CONV_CH), F32) + bias
            for j in range(CONV_K):
                acc = acc + zp_ref[base + j:base + j + CONV_ROWS, :] * w[j:j + 1, :]
            mu = jnp.mean(acc, axis=-1, keepdims=True)
            dlt = acc - mu
            var = jnp.mean(dlt * dlt, axis=-1, keepdims=True)
            y = dlt * lax.rsqrt(var + LN_EPS) * lng_ref[...] + lnb_ref[...]
            o_ref[c * CONV_ROWS:(c + 1) * CONV_ROWS, :] = (y * jax.nn.sigmoid(y)).astype(BF16)
    return kern


def _conv(p, out, n_seq, seq_len, row0, w, b, lng, lnb):
    n = p.shape[0]
    srow0 = row0 // seq_len
    const = lambda shape: pl.BlockSpec(shape, lambda s: (0,) * len(shape))
    body = _make_conv_kernel(seq_len)

    def kern(a_ref, g_ref, w_ref, b_ref, lng_ref, lnb_ref, _, o_ref, zp_ref):
        body(a_ref, g_ref, w_ref, b_ref, lng_ref, lnb_ref, o_ref, zp_ref)

    return pl.pallas_call(
        kern,
        grid=(n_seq,),
        in_specs=[
            pl.BlockSpec((seq_len, CONV_CH), lambda s: (srow0 + s, OFF_CONV // CONV_CH)),
            pl.BlockSpec((seq_len, CONV_CH), lambda s: (srow0 + s, OFF_CONV // CONV_CH + 1)),
            const((CONV_K, CONV_CH)), const((1, CONV_CH)), const((1, CONV_CH)), const((1, CONV_CH)),
            pl.BlockSpec(memory_space=pl.ANY),
        ],
        out_specs=pl.BlockSpec((seq_len, CONV_CH), lambda s: (srow0 + s, 0)),
        out_shape=jax.ShapeDtypeStruct((n, CONV_CH), BF16),
        scratch_shapes=[pltpu.VMEM((seq_len + 2 * CONV_HALO, CONV_CH), F32)],
        input_output_aliases={6: 0},
        compiler_params=_cparams("arbitrary"),
    )(p, p, w, b, lng, lnb, out)


def _make_ret_kernel(seq_len, has_init):
    qb = min(seq_len, RET_QB)
    nq = seq_len // qb
    scale = HEAD_DIM ** -0.5

    def kern(*refs):
        if has_init:
            lg_ref, q_ref, k_ref, v_ref, g_ref, gn_ref, gm_ref, r0_ref, o_ref, st_ref = refs
        else:
            lg_ref, q_ref, k_ref, v_ref, g_ref, gn_ref, gm_ref, o_ref, st_ref = refs
        low = lax.broadcasted_iota(I32, (1, LANES), 1) < HEAD_DIM
        pos = lax.broadcasted_iota(I32, (seq_len, 1), 0).astype(F32)
        qpos = lax.broadcasted_iota(I32, (qb, 1), 0).astype(F32)
        kpos = lax.broadcasted_iota(I32, (1, seq_len), 1).astype(F32)
        gm = gm_ref[...]
        for c in range(2):
            cs = slice(c * LANES, (c + 1) * LANES)
            lgf = jnp.where(low, lg_ref[0, 2 * c], lg_ref[0, 2 * c + 1])
            lgb = jnp.where(low, lg_ref[1, 2 * c], lg_ref[1, 2 * c + 1])
            qc = q_ref[:, cs]
            kc = k_ref[:, cs]
            vc = v_ref[:, cs]
            kf = kc.astype(F32) * scale
            zeta_f = jnp.exp((seq_len - 1.0 - pos) * lgf)
            zeta_b = jnp.exp(pos * lgb)
            for d, zeta, lgd in ((0, zeta_f, lgf), (1, zeta_b, lgb)):
                st = _dot_tn((kf * zeta).astype(BF16), vc)
                if has_init:
                    st = st + r0_ref[0, d, c] * jnp.exp(seq_len * lgd)
                st_ref[0, d, c] = st
            y_blocks = [None] * nq
            for j in range(2):
                sel = low if j == 0 else jnp.logical_not(low)
                kh = jnp.where(sel, kc, jnp.zeros_like(kc))
                vh = jnp.where(sel, vc, jnp.zeros_like(vc))
                lf = lg_ref[0, 2 * c + j]
                lb = lg_ref[1, 2 * c + j]
                for i in range(nq):
                    s = _dot_nt(qc[i * qb:(i + 1) * qb], kh)
                    diff = (qpos + float(i * qb)) - kpos
                    dec = jnp.exp(jnp.where(diff >= 0, diff * lf, -diff * lb))
                    dec = dec * jnp.where(diff == 0, 2.0 * scale, scale)
                    y = _dot((s * dec).astype(BF16), vh)
                    y_blocks[i] = y if y_blocks[i] is None else y_blocks[i] + y
            y = jnp.concatenate(y_blocks, axis=0) if nq > 1 else y_blocks[0]
            if has_init:
                xi_f = jnp.exp((pos + 1.0) * lgf)
                xi_b = jnp.exp((seq_len - pos) * lgb)
                y = y + _dot(qc, r0_ref[0, 0, c].astype(BF16)) * xi_f
                y = y + _dot(qc, r0_ref[0, 1, c].astype(BF16)) * xi_b
            y_hi, y_lo = _split_bf16(y)
            mu = _dot(y_hi, gm) + _dot(y_lo, gm)
            dlt = y - mu
            var = _dot((dlt * dlt).astype(BF16), gm)
            yn = dlt * lax.rsqrt(var + LN_EPS) * gn_ref[:, cs]
            gate = g_ref[:, cs].astype(F32)
            o_ref[:, cs] = (gate * jax.nn.sigmoid(gate) * yn).astype(BF16)
    return kern


def _retention(p, out, n_seq, seq_len, row0, log_g, gn_g, gmat, r0):
    n = p.shape[0]
    srow0 = row0 // seq_len
    has_init = r0 is not None
    body = _make_ret_kernel(seq_len, has_init)
    nin = 8 if has_init else 7

    def kern(*refs):
        body(*refs[:nin], *refs[nin + 1:])

    col = OFF_RET // RET_W
    in_specs = [pl.BlockSpec(memory_space=pltpu.SMEM)]
    in_specs += [pl.BlockSpec((seq_len, RET_W), functools.partial(lambda s, j: (srow0 + s, col + j), j=j))
                 for j in range(4)]
    in_specs += [pl.BlockSpec((1, RET_W), lambda s: (0, 0)), pl.BlockSpec((LANES, LANES), lambda s: (0, 0))]
    args = [log_g, p, p, p, p, gn_g, gmat]
    st_spec = pl.BlockSpec((1, 2, 2, LANES, LANES), lambda s: (s, 0, 0, 0, 0))
    if has_init:
        in_specs.append(st_spec)
        args.append(r0)
    in_specs.append(pl.BlockSpec(memory_space=pl.ANY))
    args.append(out)
    return pl.pallas_call(
        kern,
        grid=(n_seq,),
        in_specs=in_specs,
        out_specs=[pl.BlockSpec((seq_len, RET_W), lambda s: (srow0 + s, 0)), st_spec],
        out_shape=[jax.ShapeDtypeStruct((n, RET_W), BF16),
                   jax.ShapeDtypeStruct((n_seq, 2, 2, LANES, LANES), F32)],
        input_output_aliases={len(args) - 1: 0},
        compiler_params=_cparams("arbitrary"),
    )(*args)


def _outproj_kernel(a_ref, c_ref, r_ref, x_ref, g1_ref, sc_ref, sh_ref, n2_ref, wo_ref, rwt_ref, rb_ref,
                    x1_ref, h2_ref, te_ref, tg_ref):
    mixed = (_dot(a_ref[...], wo_ref[0:ATTN_W, :])
             + _dot(c_ref[...], wo_ref[ATTN_W:ATTN_W + CONV_CH, :])
             + _dot(r_ref[...], wo_ref[ATTN_W + CONV_CH:, :]))
    x1 = x_ref[...] + g1_ref[0] * mixed
    x1_ref[...] = x1
    inv = lax.rsqrt(jnp.mean(x1 * x1, axis=-1, keepdims=True) + EPS)
    h2 = (x1 * inv * n2_ref[...]) * (1.0 + sc_ref[0]) + sh_ref[0]
    h2_ref[...] = h2.astype(BF16)
    logits = _dot_nt(rwt_ref[...], h2, precision=lax.Precision.HIGHEST) + rb_ref[...]
    t = logits.shape[1]
    eidx = lax.broadcasted_iota(I32, (N_EXPERTS, t), 0).astype(F32)
    vals = logits
    tops, idxs = [], []
    for _ in range(TOP_K):
        m = jnp.max(vals, axis=0, keepdims=True)
        idx = jnp.min(jnp.where(vals == m, eidx, float(N_EXPERTS)), axis=0, keepdims=True)
        tops.append(m)
        idxs.append(idx)
        vals = jnp.where(eidx == idx, -jnp.inf, vals)
    es = [jnp.exp(m - tops[0]) for m in tops]
    tot = es[0] + es[1] + es[2] + es[3]
    te_ref[...] = jnp.concatenate(idxs + [jnp.zeros((8 - TOP_K, t), F32)], axis=0).astype(I32)
    tg_ref[...] = jnp.concatenate([e / tot for e in es] + [jnp.zeros((8 - TOP_K, t), F32)], axis=0)


def _outproj(attn, conv, ret, x, mod, norm_g, w_out_bf, rwt, rb, n_ctx, dec_seq):
    n, d = x.shape
    t = TOK_TILE
    grp = functools.partial(_group_of_tile, tile=t, n_ctx=n_ctx, dec_seq=dec_seq)

    def mod_spec(which):
        return pl.BlockSpec((1, 1, d), lambda i: (grp(i) * 6 + which, 0, 0))

    const = lambda shape: pl.BlockSpec(shape, lambda i: (0,) * len(shape))
    row = lambda w: pl.BlockSpec((t, w), lambda i: (i, 0))
    lane = lambda: pl.BlockSpec((8, t), lambda i: (0, i))
    return pl.pallas_call(
        _outproj_kernel,
        grid=(n // t,),
        in_specs=[row(ATTN_W), row(CONV_CH), row(RET_W), row(d),
                  mod_spec(MOD_G1), mod_spec(MOD_SC2), mod_spec(MOD_SH2),
                  const((1, d)), const((d, d)), const((N_EXPERTS, d)), const((N_EXPERTS, 1))],
        out_specs=[row(d), row(d), lane(), lane()],
        out_shape=[jax.ShapeDtypeStruct((n, d), F32), jax.ShapeDtypeStruct((n, d), BF16),
                   jax.ShapeDtypeStruct((8, n), I32), jax.ShapeDtypeStruct((8, n), F32)],
        compiler_params=_cparams("arbitrary"),
    )(attn, conv, ret, x, mod, mod, mod, norm_g, w_out_bf, rwt, rb)


def _round_up(x, m):
    return jnp.floor((x + (m - 1.0)) * (1.0 / m)) * m


def _route_kernel(te_ref, u_ref, ltri_ref, lpos_ref, run8_ref, loff_ref, goff_ref, seg_ref, run_ref, start_ref):
    ph = pl.program_id(0)
    i = pl.program_id(1)
    te = te_ref[...]
    t = te.shape[1]
    eidx = lax.broadcasted_iota(I32, (N_EXPERTS, t), 0)
    hits = [eidx == te[k:k + 1, :] for k in range(TOP_K)]
    onehot = sum(h.astype(F32) for h in hits)
    run8 = _round_up(jnp.sum(onehot, axis=1, keepdims=True), SUBLANES)

    @pl.when(jnp.logical_and(ph == 0, i == 0))
    def _():
        run_ref[...] = jnp.zeros_like(run_ref)

    @pl.when(ph == 0)
    def _():
        run_ref[...] = run_ref[...] + run8

    @pl.when(jnp.logical_and(ph == 1, i == 0))
    def _():
        seg = run_ref[...]
        seg_ref[...] = seg
        start_ref[...] = _dot(ltri_ref[...], _round_up(seg, MOE_BM), precision=lax.Precision.HIGHEST)
        run_ref[...] = jnp.zeros_like(run_ref)

    @pl.when(ph == 1)
    def _():
        before = _dot(onehot.astype(BF16), u_ref[...])
        run8_b = jnp.broadcast_to(run8, (N_EXPERTS, LANES))
        loff = _dot(ltri_ref[...], run8_b, precision=lax.Precision.HIGHEST)
        base = before + loff[:, 0:1]
        rows = [jnp.sum(jnp.where(h, base, 0.0), axis=0, keepdims=True) for h in hits]
        rows.append(jnp.zeros((8 - TOP_K, t), F32))
        lpos_ref[...] = jnp.concatenate(rows, axis=0).astype(I32)
        run8_ref[0] = run8_b.astype(I32)
        loff_ref[0] = loff.astype(I32)
        goff_ref[0] = (start_ref[...] + run_ref[...]).astype(I32)
        run_ref[...] = run_ref[...] + run8


def _route(top_e, upper, ltri):
    n = top_e.shape[1]
    t = MOVE_TILE
    nt = n // t
    table = pl.BlockSpec((1, N_EXPERTS, LANES), lambda ph, i: (i * ph, 0, 0))
    table_shape = jax.ShapeDtypeStruct((nt, N_EXPERTS, LANES), I32)
    return pl.pallas_call(
        _route_kernel,
        grid=(2, nt),
        in_specs=[pl.BlockSpec((8, t), lambda ph, i: (0, i)),
                  pl.BlockSpec((t, t), lambda ph, i: (0, 0)),
                  pl.BlockSpec((N_EXPERTS, N_EXPERTS), lambda ph, i: (0, 0))],
        out_specs=[pl.BlockSpec((8, t), lambda ph, i: (0, i * ph)), table, table, table,
                   pl.BlockSpec((N_EXPERTS, LANES), lambda ph, i: (0, 0))],
        out_shape=[jax.ShapeDtypeStruct((8, n), I32), table_shape, table_shape, table_shape,
                   jax.ShapeDtypeStruct((N_EXPERTS, LANES), F32)],
        scratch_shapes=[pltpu.VMEM((N_EXPERTS, LANES), F32), pltpu.VMEM((N_EXPERTS, LANES), F32)],
        compiler_params=_cparams("arbitrary", "arbitrary"),
    )(top_e, upper, ltri)


def _for_each_run_chunk(tile, run8_ref, loff_ref, goff_ref, move, fill):
    def per_expert(e, total):
        idx = tile * N_EXPERTS + e
        rows = run8_ref[idx]
        lo = loff_ref[idx]
        go = goff_ref[idx]
        for size in CHUNK_SIZES:
            hit = (rows & size) != 0

            @pl.when(hit)
            def _(lo=lo, go=go, size=size):
                move(pl.multiple_of(lo, SUBLANES), pl.multiple_of(go, SUBLANES), size)

            lo = lo + jnp.where(hit, size, 0)
            go = go + jnp.where(hit, size, 0)
        return total + rows

    total = lax.fori_loop(0, N_EXPERTS, per_expert, jnp.int32(0))
    spare = COMPACT_ROWS - total
    off = jnp.int32(0)
    for size in CHUNK_SIZES:
        hit = (spare & size) != 0

        @pl.when(hit)
        def _(off=off, size=size):
            fill(pl.multiple_of(off, SUBLANES), size)

        off = off + jnp.where(hit, size, 0)


def _make_dispatch_kernel(n_tiles, n_blocks):
    t = MOVE_TILE
    cb = COMPACT_ROWS
    spare_row0 = n_blocks * MOE_BM

    def kern(run8_ref, loff_ref, goff_ref, seg_ref, start_ref, nv_ref, h_ref, lpos_ref, xs_ref,
             buf, zeros_ref, sem, zsem):
        step = pl.program_id(0)
        slot = lax.rem(step, 2)

        def wait_tile(s):
            pltpu.make_async_copy(buf.at[s], xs_ref.at[pl.ds(0, cb)], sem.at[s]).wait()

        def for_each_zero_chunk(action):
            def per_expert(e, carry):
                seg = seg_ref[e]
                padlen = (-seg) & (MOE_BM - 1)
                row = start_ref[e] + seg
                for size in CHUNK_SIZES[1:]:
                    hit = (padlen & size) != 0

                    @pl.when(hit)
                    def _(row=row, size=size):
                        dst = xs_ref.at[pl.ds(pl.multiple_of(row, SUBLANES), size)]
                        action(pltpu.make_async_copy(zeros_ref.at[pl.ds(0, size)], dst, zsem))

                    row = row + jnp.where(hit, size, 0)
                return carry

            lax.fori_loop(0, N_EXPERTS, per_expert, 0)

            def per_tail_block(b, carry):
                action(pltpu.make_async_copy(zeros_ref, xs_ref.at[pl.ds(b * MOE_BM, MOE_BM)], zsem))
                return carry

            lax.fori_loop(nv_ref[0], n_blocks + 2, per_tail_block, 0)

        @pl.when(step == 0)
        def _():
            zeros_ref[...] = jnp.zeros_like(zeros_ref)
            for_each_zero_chunk(lambda cp: cp.start())
            for_each_zero_chunk(lambda cp: cp.wait())

        @pl.when(step >= 2)
        def _():
            wait_tile(slot)

        lpos = lpos_ref[...]
        rows = lax.broadcasted_iota(I32, (cb, t), 0)
        onehot = jnp.where(rows == lpos[0:1, :], 1.0, 0.0)
        for k in range(1, TOP_K):
            onehot = onehot + jnp.where(rows == lpos[k:k + 1, :], 1.0, 0.0)
        buf[slot] = _dot(onehot.astype(BF16), h_ref[...])

        def move(lo, go, size):
            pltpu.make_async_copy(buf.at[slot, pl.ds(lo, size)], xs_ref.at[pl.ds(go, size)], sem.at[slot]).start()

        def fill(off, size):
            dst = xs_ref.at[pl.ds(pl.multiple_of(spare_row0 + slot * t + off, SUBLANES), size)]
            pltpu.make_async_copy(buf.at[slot, pl.ds(off, size)], dst, sem.at[slot]).start()

        _for_each_run_chunk(step, run8_ref, loff_ref, goff_ref, move, fill)

        @pl.when(step == n_tiles - 1)
        def _():
            wait_tile(slot)
            if n_tiles > 1:
                wait_tile(1 - slot)
    return kern


def _dispatch(tables, seg, starts, n_valid, h2, lpos, n_blocks):
    n, d = h2.shape
    t = MOVE_TILE
    return pl.pallas_call(
        _make_dispatch_kernel(n // t, n_blocks),
        grid_spec=pltpu.PrefetchScalarGridSpec(
            num_scalar_prefetch=6,
            grid=(n // t,),
            in_specs=[pl.BlockSpec((t, d), lambda i, *_: (i, 0)),
                      pl.BlockSpec((8, t), lambda i, *_: (0, i))],
            out_specs=pl.BlockSpec(memory_space=pl.ANY),
            scratch_shapes=[pltpu.VMEM((2, COMPACT_ROWS, d), F32), pltpu.VMEM((MOE_BM, d), F32),
                            pltpu.SemaphoreType.DMA((2,)), pltpu.SemaphoreType.DMA(())],
        ),
        out_shape=jax.ShapeDtypeStruct(((n_blocks + 2) * MOE_BM, d), F32),
        compiler_params=_cparams("arbitrary"),
    )(*tables, seg, starts, n_valid, h2, lpos)


def _make_expert_kernel(layer):
    def kern(be_ref, bs_ref, nv_ref, ord_ref, nxt_ref, x_ref, wgu_hbm, bgu_ref, wdn_hbm, bdn_ref, y_ref,
             wgu_f, wdn_f, wgu_s, wdn_s, sem):
        _expert_body(layer, be_ref, nv_ref, ord_ref, nxt_ref, x_ref, wgu_hbm, bgu_ref, wdn_hbm, bdn_ref, y_ref,
                     wgu_f, wdn_f, wgu_s, wdn_s, sem)
    return kern


def _expert_body(layer, be_ref, nv_ref, ord_ref, nxt_ref, x_ref, wgu_hbm, bgu_ref, wdn_hbm, bdn_ref, y_ref,
                 wgu_f, wdn_f, wgu_s, wdn_s, sem):
    i = pl.program_id(0)
    e = be_ref[i]
    prev = be_ref[jnp.maximum(i - 1, 0)]
    new_expert = jnp.logical_or(i == 0, e != prev)
    slot = lax.rem(ord_ref[i], 2)

    def weight_copies(expert, s):
        return (pltpu.make_async_copy(wgu_hbm.at[layer, expert], wgu_f.at[s], sem.at[0, s]),
                pltpu.make_async_copy(wdn_hbm.at[layer, expert], wdn_f.at[s], sem.at[1, s]))

    @pl.when(i == 0)
    def _():
        for cp in weight_copies(e, slot):
            cp.start()

    @pl.when(new_expert)
    def _():
        for cp in weight_copies(e, slot):
            cp.wait()
        nxt = nxt_ref[i]

        @pl.when(nxt >= 0)
        def _():
            for cp in weight_copies(nxt, 1 - slot):
                cp.start()

        wgu_s[...] = wgu_f[slot].astype(BF16)
        wdn_s[...] = wdn_f[slot].astype(BF16)

    @pl.when(i < nv_ref[0])
    def _():
        x = x_ref[...].astype(BF16)
        gu = _dot(x, wgu_s[...]) + bgu_ref[0, 0]
        gate = jnp.minimum(gu[:, :D_FF], SWIGLU_LIMIT)
        up = jnp.clip(gu[:, D_FF:], -SWIGLU_LIMIT, SWIGLU_LIMIT)
        hdn = (up + 1.0) * (gate * jax.nn.sigmoid(SWIGLU_ALPHA * gate))
        y_ref[...] = _dot(hdn.astype(BF16), wdn_s[...]) + bdn_ref[0, 0]

    @pl.when(i >= nv_ref[0])
    def _():
        y_ref[...] = jnp.zeros_like(y_ref)


def _experts(blk_e, blk_src, n_valid, blk_ord, blk_next, xs, w_gu, b_gu, w_dn, b_dn, layer, n_blocks):
    d = xs.shape[1]
    depth = w_gu.shape[0]

    def bias(width):
        return pl.BlockSpec((1, 1, 1, width), lambda i, be, *_: (layer, be[i], 0, 0))

    return pl.pallas_call(
        _make_expert_kernel(layer),
        grid_spec=pltpu.PrefetchScalarGridSpec(
            num_scalar_prefetch=5,
            grid=(n_blocks,),
            in_specs=[pl.BlockSpec((MOE_BM, d), lambda i, be, bs, *_: (bs[i], 0)),
                      pl.BlockSpec(memory_space=pl.ANY), bias(2 * D_FF),
                      pl.BlockSpec(memory_space=pl.ANY), bias(d)],
            out_specs=pl.BlockSpec((MOE_BM, d), lambda i, *_: (i, 0)),
            scratch_shapes=[pltpu.VMEM((2, d, 2 * D_FF), F32), pltpu.VMEM((2, D_FF, d), F32),
                            pltpu.VMEM((d, 2 * D_FF), BF16), pltpu.VMEM((D_FF, d), BF16),
                            pltpu.SemaphoreType.DMA((2, 2))],
        ),
        out_shape=jax.ShapeDtypeStruct((n_blocks * MOE_BM, d), F32),
        compiler_params=_cparams("arbitrary"),
    )(blk_e, blk_src, n_valid, blk_ord, blk_next, xs, w_gu, b_gu.reshape(depth, N_EXPERTS, 1, -1), w_dn,
      b_dn.reshape(depth, N_EXPERTS, 1, -1))


def _make_combine_kernel(n_tiles, final):
    t = MOVE_TILE
    cb = COMPACT_ROWS

    def kern(run8_ref, loff_ref, goff_ref, ys_ref, x1_ref, lpt_ref, gt_ref, g2_ref, fg_ref, o_ref,
             ybuf, spare, sem):
        step = pl.program_id(0)
        slot = lax.rem(step, 2)

        def fetch(tile, s):
            def move(lo, go, size):
                pltpu.make_async_copy(ys_ref.at[pl.ds(go, size)], ybuf.at[s, pl.ds(lo, size)], sem.at[s]).start()

            def fill(off, size):
                pltpu.make_async_copy(ys_ref.at[pl.ds(0, size)], spare.at[s, pl.ds(off, size)], sem.at[s]).start()

            _for_each_run_chunk(tile, run8_ref, loff_ref, goff_ref, move, fill)

        @pl.when(step == 0)
        def _():
            ybuf[...] = jnp.zeros_like(ybuf)
            fetch(0, 0)

        @pl.when(step + 1 < n_tiles)
        def _():
            fetch(step + 1, 1 - slot)

        pltpu.make_async_copy(ys_ref.at[pl.ds(0, cb)], ybuf.at[slot], sem.at[slot]).wait()

        lpt = lpt_ref[...]
        gt = gt_ref[...]
        cols = lax.broadcasted_iota(I32, (t, cb), 1)
        g = jnp.where(cols == lpt[:, 0:1], gt[:, 0:1], 0.0)
        for k in range(1, TOP_K):
            g = g + jnp.where(cols == lpt[:, k:k + 1], gt[:, k:k + 1], 0.0)
        g_hi, g_lo = _split_bf16(g)
        yb = ybuf[slot].astype(BF16)
        y = _dot(g_hi, yb) + _dot(g_lo, yb)
        x2 = x1_ref[...] + g2_ref[0] * y
        if final:
            x2 = x2 * lax.rsqrt(jnp.mean(x2 * x2, axis=-1, keepdims=True) + EPS) * fg_ref[...]
        o_ref[...] = x2
    return kern


def _combine(tables, ys, x1, lpos_t, gates_t, mod, final_g, final, n_ctx, dec_seq):
    n, d = x1.shape
    t = MOVE_TILE
    grp = functools.partial(_group_of_tile, tile=t, n_ctx=n_ctx, dec_seq=dec_seq)
    return pl.pallas_call(
        _make_combine_kernel(n // t, final),
        grid_spec=pltpu.PrefetchScalarGridSpec(
            num_scalar_prefetch=3,
            grid=(n // t,),
            in_specs=[pl.BlockSpec(memory_space=pl.ANY),
                      pl.BlockSpec((t, d), lambda i, *_: (i, 0)),
                      pl.BlockSpec((t, 8), lambda i, *_: (i, 0)),
                      pl.BlockSpec((t, 8), lambda i, *_: (i, 0)),
                      pl.BlockSpec((1, 1, d), lambda i, *_: (grp(i) * 6 + MOD_G2, 0, 0)),
                      pl.BlockSpec((1, d), lambda i, *_: (0, 0))],
            out_specs=pl.BlockSpec((t, d), lambda i, *_: (i, 0)),
            scratch_shapes=[pltpu.VMEM((2, COMPACT_ROWS, d), F32), pltpu.VMEM((2, t, d), F32),
                            pltpu.SemaphoreType.DMA((2,))],
        ),
        out_shape=jax.ShapeDtypeStruct((n, d), F32),
        compiler_params=_cparams("arbitrary"),
    )(*tables, ys, x1, lpos_t, gates_t, mod, final_g)


def _blockdiag_pairs(s):
    z = jnp.zeros_like(s[..., 0, :, :])
    def pair(a, b):
        return jnp.concatenate([jnp.concatenate([a, z], axis=-1), jnp.concatenate([z, b], axis=-1)], axis=-2)
    return jnp.stack([pair(s[..., 0, :, :], s[..., 1, :, :]), pair(s[..., 2, :, :], s[..., 3, :, :])], axis=-3)


def _diag_blocks(st):
    h = HEAD_DIM
    blocks = [st[:, :, c, j * h:(j + 1) * h, j * h:(j + 1) * h] for c in range(2) for j in range(2)]
    return jnp.stack(blocks, axis=2)


def kernel(x_prompt, x_sample, cache_k, cache_v, state_ret, c, c_ctx, w_mod, b_mod, norm1_g, norm2_g, w_in,
           q_norm_g, k_norm_g, conv_w, conv_b, conv_ln_g, conv_ln_b, ret_decay_logit, ret_gn_g, w_out,
           router_w, router_b, moe_w_gu, moe_b_gu, moe_w_dn, moe_b_dn, final_g):
    batch, seq, d = x_prompt.shape
    dec_batch, dec_seq, _ = x_sample.shape
    depth = w_mod.shape[0]
    past = cache_k.shape[2]
    n_ctx = batch * seq
    n_lat = dec_batch * dec_seq
    n = n_ctx + n_lat
    assert d == D_MODEL and dec_batch == 2
    assert n_ctx % dec_seq == 0 and dec_seq % TOK_TILE == 0 and seq % ATTN_QB == 0 and dec_seq % ATTN_QB == 0
    assert MOVE_TILE == MOE_BM and n % MOVE_TILE == 0

    x = jnp.concatenate([x_prompt.reshape(n_ctx, d), x_sample.reshape(n_lat, d)], axis=0)
    mods = _modulation(jnp.concatenate([c_ctx[None, :], c], axis=0), w_mod, b_mod)
    rope = _rope_tables(dec_seq)

    head_of_col = jnp.arange(QK_W) // HEAD_DIM
    hsum = (head_of_col[:, None] == jnp.arange(LANES)[None, :]).astype(BF16)
    hbc = hsum.T
    lane_head = jnp.arange(LANES) // HEAD_DIM
    gmat = ((lane_head[:, None] == lane_head[None, :]).astype(F32) / HEAD_DIM).astype(BF16)
    tt = jnp.arange(MOVE_TILE)
    upper = (tt[:, None] < tt[None, :]).astype(BF16)
    ee = jnp.arange(N_EXPERTS)
    ltri = (ee[None, :] < ee[:, None]).astype(F32)

    n_tiles = n // MOVE_TILE
    max_rows = n * TOP_K + n_tiles * N_EXPERTS * (SUBLANES - 1) + N_EXPERTS * (MOE_BM - 1)
    n_blocks = -(-max_rows // MOE_BM)
    cache_k2 = cache_k.reshape(dec_batch, depth, past, KV_W)
    cache_v2 = cache_v.reshape(dec_batch, depth, past, KV_W)

    ks_out, vs_out, ss_out = [], [], []
    for l in range(depth):
        mod = mods[l]
        gqk = jnp.concatenate([jnp.tile(q_norm_g[l], N_Q_HEADS), jnp.tile(k_norm_g[l], N_KV_HEADS)])[None, :]
        p, kn, vv = _inproj(x, mod, norm1_g[l][None, :], w_in[l].astype(BF16), gqk, hsum, hbc, rope,
                            n_ctx, dec_seq)
        ks_out.append(kn[:n_ctx].reshape(batch, seq, N_KV_HEADS, HEAD_DIM))
        vs_out.append(vv[:n_ctx].reshape(batch, seq, N_KV_HEADS, HEAD_DIM))

        attn = jnp.zeros((n, ATTN_W), BF16)
        attn = _attention(p, attn, batch, seq, 0, None, l)
        attn = _attention(p, attn, dec_batch, dec_seq, n_ctx, (cache_k2, cache_v2), l)

        cw, cb = conv_w[l], conv_b[l][None, :]
        clg, clb = conv_ln_g[l][None, :], conv_ln_b[l][None, :]
        conv = jnp.zeros((n, CONV_CH), BF16)
        conv = _conv(p, conv, batch, seq, 0, cw, cb, clg, clb)
        conv = _conv(p, conv, dec_batch, dec_seq, n_ctx, cw, cb, clg, clb)

        log_g = jax.nn.log_sigmoid(ret_decay_logit[l].astype(F32))
        gn = ret_gn_g[l][None, :]
        ret = jnp.zeros((n, RET_W), BF16)
        ret, st_ctx = _retention(p, ret, batch, seq, 0, log_g, gn, gmat, None)
        ret, _ = _retention(p, ret, dec_batch, dec_seq, n_ctx, log_g, gn, gmat,
                            _blockdiag_pairs(state_ret[:, l].astype(F32)))
        ss_out.append(_diag_blocks(st_ctx))

        x1, h2, top_e, top_g = _outproj(attn, conv, ret, x, mod, norm2_g[l][None, :], w_out[l].astype(BF16),
                                        router_w[l].T, router_b[l][:, None], n_ctx, dec_seq)
        lpos, run8, loff, goff, seg = _route(top_e, upper, ltri)
        tables = [tb[:, :, 0].reshape(-1) for tb in (run8, loff, goff)]
        seg = seg[:, 0].astype(I32)
        padded = (seg + MOE_BM - 1) // MOE_BM * MOE_BM
        pad_end = jnp.cumsum(padded)
        n_valid = (pad_end[-1] // MOE_BM).reshape(1)
        blk_src = jnp.minimum(jnp.arange(n_blocks, dtype=I32), n_valid - 1)
        blk_e = jnp.minimum(jnp.sum((pad_end[None, :] <= (blk_src * MOE_BM)[:, None]).astype(I32), axis=1),
                            N_EXPERTS - 1)
        owns = padded > 0
        ord_e = jnp.cumsum(owns.astype(I32)) - 1
        later = jnp.where(owns[None, :] & (ee[None, :] > ee[:, None]), ee[None, :], N_EXPERTS)
        next_e = jnp.min(later, axis=1)
        next_e = jnp.where(next_e == N_EXPERTS, -1, next_e).astype(I32)

        xs = _dispatch(tables, seg, pad_end - padded, n_valid, h2, lpos, n_blocks)
        ys = _experts(blk_e, blk_src, n_valid, ord_e[blk_e], next_e[blk_e], xs,
                      moe_w_gu, moe_b_gu, moe_w_dn, moe_b_dn, l, n_blocks)
        x = _combine(tables, ys, x1, lpos.T, top_g.T, mod, final_g[None, :], l == depth - 1, n_ctx, dec_seq)

    y_prompt = x[:n_ctx].reshape(batch, seq, d)
    y_sample = x[n_ctx:].reshape(dec_batch, dec_seq, d)
    return (y_prompt, y_sample, jnp.stack(ks_out, axis=1), jnp.stack(vs_out, axis=1),
            jnp.stack(ss_out, axis=1))
```

```python
import functools

import numpy as np
import jax
import jax.numpy as jnp
from jax import lax
from jax.experimental import pallas as pl
from jax.experimental.pallas import tpu as pltpu

F32 = jnp.float32
BF16 = jnp.bfloat16
I32 = jnp.int32

D_MODEL = 1024
GRID_W = 64
HEAD_DIM = 64
N_Q_HEADS = 8
N_KV_HEADS = 2
ATTN_W = N_Q_HEADS * HEAD_DIM
KV_W = N_KV_HEADS * HEAD_DIM
QK_W = ATTN_W + KV_W
CONV_CH = 256
CONV_K = 31
CONV_PAD = CONV_K // 2
CONV_HALO = 16
N_RET_HEADS = 4
RET_W = 256
OFF_V = QK_W
OFF_CONV = OFF_V + KV_W
OFF_RET = OFF_CONV + 2 * CONV_CH
IN_COLS = OFF_RET + 4 * RET_W
ROPE_HALF = HEAD_DIM // 2
ROPE_THETA = 10000.0
N_EXPERTS = 32
TOP_K = 4
D_FF = D_MODEL
SWIGLU_LIMIT = 7.0
SWIGLU_ALPHA = 1.702
EPS = 1e-6
LN_EPS = 1e-5

LANES = 128
TOK_TILE = 512
ATTN_QB = 256
RET_QB = 256
CONV_ROWS = 64
MOE_BM = 256
MOVE_TILE = 256
ROUTE_TILES = 4
SUBLANES = 8
COMPACT_ROWS = 1280
assert COMPACT_ROWS >= MOVE_TILE * TOP_K + N_EXPERTS * (SUBLANES - 1) and COMPACT_ROWS % MOE_BM == 0
CHUNK_SIZES = tuple(MOVE_TILE >> s for s in range(6))
VMEM_LIMIT = 56 * 1024 * 1024

MOD_SH1, MOD_SC1, MOD_G1, MOD_SH2, MOD_SC2, MOD_G2 = range(6)


def _cparams(*sem):
    return pltpu.CompilerParams(dimension_semantics=sem, vmem_limit_bytes=VMEM_LIMIT)


def _dot(a, b, **kw):
    return jnp.dot(a, b, preferred_element_type=F32, **kw)


def _dot_nt(a, b, **kw):
    return lax.dot_general(a, b, (((1,), (1,)), ((), ())), preferred_element_type=F32, **kw)


def _dot_tn(a, b, **kw):
    return lax.dot_general(a, b, (((0,), (0,)), ((), ())), preferred_element_type=F32, **kw)


def _split_bf16(x):
    hi = x.astype(BF16)
    lo = (x - hi.astype(F32)).astype(BF16)
    return hi, lo


MOD_TN = 1536


def _mod_kernel(ct_ref, w_ref, b_ref, o_ref):
    s = ct_ref[...]
    s = s * jax.nn.sigmoid(s)
    w = w_ref[0]
    rows = [jnp.sum(w * s[:, r:r + 1], axis=0, keepdims=True) for r in range(3)]
    rows.append(jnp.zeros((5, w.shape[1]), F32))
    o_ref[0] = jnp.concatenate(rows, axis=0) + b_ref[0]


def _modulation(cvec3, w_mod, b_mod):
    depth, d, cols = w_mod.shape
    ct = jnp.zeros((d, 8), F32).at[:, :3].set(cvec3.T)
    out = pl.pallas_call(
        _mod_kernel,
        grid=(depth, cols // MOD_TN),
        in_specs=[
            pl.BlockSpec((d, 8), lambda l, j: (0, 0)),
            pl.BlockSpec((1, d, MOD_TN), lambda l, j: (l, 0, j)),
            pl.BlockSpec((1, 1, MOD_TN), lambda l, j: (l, 0, j)),
        ],
        out_specs=pl.BlockSpec((1, 8, MOD_TN), lambda l, j: (l, 0, j)),
        out_shape=jax.ShapeDtypeStruct((depth, 8, cols), F32),
        compiler_params=_cparams("arbitrary", "arbitrary"),
    )(ct, w_mod, b_mod.reshape(depth, 1, cols))
    return out[:, :3].reshape(depth, 3 * 6, 1, d)


def _inproj_kernel(x_ref, sh_ref, sc_ref, g_ref, w_ref, gqk_ref, hsum_ref, hbc_ref,
                   cos_ref, sa_ref, sb_ref, p_ref, k_ref, v_ref):
    x = x_ref[...]
    inv = lax.rsqrt(jnp.mean(x * x, axis=-1, keepdims=True) + EPS)
    h = (x * inv * g_ref[...]) * (1.0 + sc_ref[0]) + sh_ref[0]
    acc = _dot(h.astype(BF16), w_ref[...])
    qk = acc[:, :QK_W]
    ss = _dot((qk * qk).astype(BF16), hsum_ref[...])
    r = lax.rsqrt(ss * (1.0 / HEAD_DIM) + EPS)
    r_hi, r_lo = _split_bf16(r)
    rb = _dot(r_hi, hbc_ref[...]) + _dot(r_lo, hbc_ref[...])
    qkn = qk * rb * gqk_ref[...]
    k_ref[...] = qkn[:, ATTN_W:QK_W]
    v_ref[...] = acc[:, OFF_V:OFF_CONV]
    cos = cos_ref[...]
    sa = sa_ref[...]
    sb = sb_ref[...]
    for j in range(QK_W // LANES):
        c = qkn[:, j * LANES:(j + 1) * LANES]
        up = pltpu.roll(c, LANES - ROPE_HALF // 2, 1)
        dn = pltpu.roll(c, ROPE_HALF // 2, 1)
        p_ref[:, j * LANES:(j + 1) * LANES] = (c * cos + up * sa + dn * sb).astype(BF16)
    p_ref[:, QK_W:] = acc[:, QK_W:].astype(BF16)


def _rope_tables(dec_seq):
    rows = dec_seq // GRID_W
    row = jnp.repeat(jnp.arange(rows), GRID_W).astype(F32)
    col = jnp.tile(jnp.arange(GRID_W), rows).astype(F32)
    inv = 1.0 / (ROPE_THETA ** (jnp.arange(0, ROPE_HALF, 2, dtype=F32) / ROPE_HALF))
    ar = row[:, None] * inv[None, :]
    ac = col[:, None] * inv[None, :]
    cos = jnp.concatenate([jnp.cos(ar), jnp.cos(ar), jnp.cos(ac), jnp.cos(ac)], axis=-1)
    sin = jnp.concatenate([jnp.sin(ar), jnp.sin(ar), jnp.sin(ac), jnp.sin(ac)], axis=-1)
    first = (jnp.arange(HEAD_DIM) % ROPE_HALF) < ROPE_HALF // 2
    sa = jnp.where(first[None, :], -sin, 0.0)
    sb = jnp.where(first[None, :], 0.0, sin)
    def table(t, ident):
        t = jnp.concatenate([jnp.full((TOK_TILE, HEAD_DIM), ident, F32), t], axis=0)
        return jnp.tile(t, (1, LANES // HEAD_DIM))
    return table(cos, 1.0), table(sa, 0.0), table(sb, 0.0)


def _group_of_tile(i, tile, n_ctx, dec_seq):
    tok = i * tile
    return jnp.where(tok < n_ctx, 0, 1 + (tok - n_ctx) // dec_seq)


def _inproj(x, mod, norm_g, w_in_bf, gqk, hsum, hbc, rope, n_ctx, dec_seq):
    n, d = x.shape
    t = TOK_TILE
    grp = functools.partial(_group_of_tile, tile=t, n_ctx=n_ctx, dec_seq=dec_seq)

    def mod_spec(which):
        return pl.BlockSpec((1, 1, d), lambda i: (grp(i) * 6 + which, 0, 0))

    def rope_idx(i):
        tok = i * t
        return (jnp.where(tok < n_ctx, 0, 1 + ((tok - n_ctx) % dec_seq) // t), 0)

    rope_spec = pl.BlockSpec((t, LANES), rope_idx)
    const = lambda shape: pl.BlockSpec(shape, lambda i: (0,) * len(shape))
    return pl.pallas_call(
        _inproj_kernel,
        grid=(n // t,),
        in_specs=[
            pl.BlockSpec((t, d), lambda i: (i, 0)),
            mod_spec(MOD_SH1), mod_spec(MOD_SC1),
            const((1, d)),
            const((d, IN_COLS)),
            const((1, QK_W)), const((QK_W, LANES)), const((LANES, QK_W)),
            rope_spec, rope_spec, rope_spec,
        ],
        out_specs=[
            pl.BlockSpec((t, IN_COLS), lambda i: (i, 0)),
            pl.BlockSpec((t, KV_W), lambda i: (i, 0)),
            pl.BlockSpec((t, KV_W), lambda i: (i, 0)),
        ],
        out_shape=[
            jax.ShapeDtypeStruct((n, IN_COLS), BF16),
            jax.ShapeDtypeStruct((n, KV_W), F32),
            jax.ShapeDtypeStruct((n, KV_W), F32),
        ],
        compiler_params=_cparams("arbitrary"),
    )(x, mod, mod, norm_g, w_in_bf, gqk, hsum, hbc, *rope)


def _head_halves(x, hkv, low):
    r = pltpu.roll(x, HEAD_DIM, 1)
    rep = jnp.where(low, x, r) if hkv == 0 else jnp.where(low, r, x)
    return jnp.where(low, rep, 0.0).astype(BF16), jnp.where(low, 0.0, rep).astype(BF16)


def _make_attn_kernel(has_ctx):
    def kern(*refs):
        if has_ctx:
            q_ref, k_ref, v_ref, ck_ref, cv_ref, o_ref = refs
        else:
            q_ref, k_ref, v_ref, o_ref = refs
        low = lax.broadcasted_iota(I32, (1, LANES), 1) < HEAD_DIM
        k = k_ref[...].astype(F32)
        v = v_ref[...].astype(F32)
        if has_ctx:
            ck = ck_ref[0, 0]
            cv = cv_ref[0, 0]
        scale = HEAD_DIM ** -0.5
        for hkv in range(N_KV_HEADS):
            kh = _head_halves(k, hkv, low)
            vh = _head_halves(v, hkv, low)
            if has_ctx:
                ckh = _head_halves(ck, hkv, low)
                cvh = _head_halves(cv, hkv, low)
            for c in range(2):
                col = hkv * 2 * LANES + c * LANES
                qc = q_ref[:, col:col + LANES] * scale
                o_c = None
                for j in range(2):
                    s = _dot_nt(qc, kh[j])
                    m = jnp.max(s, axis=-1, keepdims=True)
                    if has_ctx:
                        s2 = _dot_nt(qc, ckh[j])
                        m = jnp.maximum(m, jnp.max(s2, axis=-1, keepdims=True))
                    p = jnp.exp(s - m)
                    l = jnp.sum(p, axis=-1, keepdims=True)
                    o = _dot(p.astype(BF16), vh[j])
                    if has_ctx:
                        p2 = jnp.exp(s2 - m)
                        l = l + jnp.sum(p2, axis=-1, keepdims=True)
                        o = o + _dot(p2.astype(BF16), cvh[j])
                    o = o / l
                    o_c = o if o_c is None else o_c + o
                o_ref[:, col:col + LANES] = o_c.astype(BF16)
    return kern


def _attention(p, out, n_seq, seq_len, row0, ctx_kv, layer):
    n = p.shape[0]
    qb = ATTN_QB
    nq = seq_len // qb
    qrow0 = row0 // qb
    srow0 = row0 // seq_len
    has_ctx = ctx_kv is not None
    in_specs = [
        pl.BlockSpec((qb, ATTN_W), lambda b, i: (qrow0 + b * nq + i, 0)),
        pl.BlockSpec((seq_len, KV_W), lambda b, i: (srow0 + b, ATTN_W // KV_W)),
        pl.BlockSpec((seq_len, KV_W), lambda b, i: (srow0 + b, OFF_V // KV_W)),
    ]
    args = [p, p, p]
    if has_ctx:
        ck, cv = ctx_kv
        past = ck.shape[2]
        spec = pl.BlockSpec((1, 1, past, KV_W), lambda b, i: (b, layer, 0, 0))
        in_specs += [spec, spec]
        args += [ck, cv]
    in_specs.append(pl.BlockSpec(memory_space=pl.ANY))
    args.append(out)

    body = _make_attn_kernel(has_ctx)

    def kern(*refs):
        body(*refs[:-2], refs[-1])

    return pl.pallas_call(
        kern,
        grid=(n_seq, nq),
        in_specs=in_specs,
        out_specs=pl.BlockSpec((qb, ATTN_W), lambda b, i: (qrow0 + b * nq + i, 0)),
        out_shape=jax.ShapeDtypeStruct((n, ATTN_W), BF16),
        input_output_aliases={len(args) - 1: 0},
        compiler_params=_cparams("arbitrary", "arbitrary"),
    )(*args)


def _make_conv_kernel(seq_len):
    shifted_rows = seq_len + 2 * CONV_HALO - SUBLANES

    def kern(a_ref, g_ref, w_ref, b_ref, lng_ref, lnb_ref, o_ref, zp_ref, zs_ref):
        zero = jnp.zeros((CONV_HALO, CONV_CH), F32)
        zp_ref[0:CONV_HALO, :] = zero
        zp_ref[CONV_HALO + seq_len:2 * CONV_HALO + seq_len, :] = zero
        zp_ref[CONV_HALO:CONV_HALO + seq_len, :] = (
            a_ref[...].astype(F32) * jax.nn.sigmoid(g_ref[...].astype(F32)))
        for s in range(1, SUBLANES):
            zs_ref[s] = zp_ref[s:s + shifted_rows, :]
        w = w_ref[...]
        bias = b_ref[...]
        for c in range(seq_len // CONV_ROWS):
            base = c * CONV_ROWS + CONV_HALO - CONV_PAD
            acc = jnp.zeros((CONV_ROWS, CONV_CH), F32) + bias
            for j in range(CONV_K):
                shift = (base + j) % SUBLANES
                row = base + j - shift
                if shift == 0:
                    tap = zp_ref[row:row + CONV_ROWS, :]
                else:
                    tap = zs_ref[shift, row:row + CONV_ROWS, :]
                acc = acc + tap * w[j:j + 1, :]
            mu = jnp.mean(acc, axis=-1, keepdims=True)
            dlt = acc - mu
            var = jnp.mean(dlt * dlt, axis=-1, keepdims=True)
            y = dlt * lax.rsqrt(var + LN_EPS) * lng_ref[...] + lnb_ref[...]
            o_ref[c * CONV_ROWS:(c + 1) * CONV_ROWS, :] = (y * jax.nn.sigmoid(y)).astype(BF16)
    return kern


def _conv(p, out, n_seq, seq_len, row0, w, b, lng, lnb):
    n = p.shape[0]
    srow0 = row0 // seq_len
    const = lambda shape: pl.BlockSpec(shape, lambda s: (0,) * len(shape))
    body = _make_conv_kernel(seq_len)

    def kern(a_ref, g_ref, w_ref, b_ref, lng_ref, lnb_ref, _, o_ref, zp_ref, zs_ref):
        body(a_ref, g_ref, w_ref, b_ref, lng_ref, lnb_ref, o_ref, zp_ref, zs_ref)

    return pl.pallas_call(
        kern,
        grid=(n_seq,),
        in_specs=[
            pl.BlockSpec((seq_len, CONV_CH), lambda s: (srow0 + s, OFF_CONV // CONV_CH)),
            pl.BlockSpec((seq_len, CONV_CH), lambda s: (srow0 + s, OFF_CONV // CONV_CH + 1)),
            const((CONV_K, CONV_CH)), const((1, CONV_CH)), const((1, CONV_CH)), const((1, CONV_CH)),
            pl.BlockSpec(memory_space=pl.ANY),
        ],
        out_specs=pl.BlockSpec((seq_len, CONV_CH), lambda s: (srow0 + s, 0)),
        out_shape=jax.ShapeDtypeStruct((n, CONV_CH), BF16),
        scratch_shapes=[pltpu.VMEM((seq_len + 2 * CONV_HALO, CONV_CH), F32),
                        pltpu.VMEM((SUBLANES, seq_len + 2 * CONV_HALO - SUBLANES, CONV_CH), F32)],
        input_output_aliases={6: 0},
        compiler_params=_cparams("arbitrary"),
    )(p, p, w, b, lng, lnb, out)


def _make_ret_kernel(seq_len, has_init):
    qb = min(seq_len, RET_QB)
    nq = seq_len // qb
    scale = HEAD_DIM ** -0.5

    def kern(*refs):
        if has_init:
            lg_ref, q_ref, k_ref, v_ref, g_ref, gn_ref, gm_ref, r0_ref, o_ref, st_ref = refs
        else:
            lg_ref, q_ref, k_ref, v_ref, g_ref, gn_ref, gm_ref, o_ref, st_ref = refs
        low = lax.broadcasted_iota(I32, (1, LANES), 1) < HEAD_DIM
        pos = lax.broadcasted_iota(I32, (seq_len, 1), 0).astype(F32)
        qpos = lax.broadcasted_iota(I32, (qb, 1), 0).astype(F32)
        kpos = lax.broadcasted_iota(I32, (1, seq_len), 1).astype(F32)
        gm = gm_ref[...]
        for c in range(2):
            cs = slice(c * LANES, (c + 1) * LANES)
            lgf = jnp.where(low, lg_ref[0, 2 * c], lg_ref[0, 2 * c + 1])
            lgb = jnp.where(low, lg_ref[1, 2 * c], lg_ref[1, 2 * c + 1])
            qc = q_ref[:, cs]
            kc = k_ref[:, cs]
            vc = v_ref[:, cs]
            kf = kc.astype(F32) * scale
            zeta_f = jnp.exp((seq_len - 1.0 - pos) * lgf)
            zeta_b = jnp.exp(pos * lgb)
            for d, zeta, lgd in ((0, zeta_f, lgf), (1, zeta_b, lgb)):
                st = _dot_tn((kf * zeta).astype(BF16), vc)
                if has_init:
                    st = st + r0_ref[0, d, c] * jnp.exp(seq_len * lgd)
                st_ref[0, d, c] = st
            y_blocks = [None] * nq
            for j in range(2):
                sel = low if j == 0 else jnp.logical_not(low)
                kh = jnp.where(sel, kc, jnp.zeros_like(kc))
                vh = jnp.where(sel, vc, jnp.zeros_like(vc))
                lf = lg_ref[0, 2 * c + j]
                lb = lg_ref[1, 2 * c + j]
                for i in range(nq):
                    s = _dot_nt(qc[i * qb:(i + 1) * qb], kh)
                    diff = (qpos + float(i * qb)) - kpos
                    dec = jnp.exp(jnp.where(diff >= 0, diff * lf, -diff * lb))
                    dec = dec * jnp.where(diff == 0, 2.0 * scale, scale)
                    y = _dot((s * dec).astype(BF16), vh)
                    y_blocks[i] = y if y_blocks[i] is None else y_blocks[i] + y
            y = jnp.concatenate(y_blocks, axis=0) if nq > 1 else y_blocks[0]
            if has_init:
                xi_f = jnp.exp((pos + 1.0) * lgf)
                xi_b = jnp.exp((seq_len - pos) * lgb)
                y = y + _dot(qc, r0_ref[0, 0, c].astype(BF16)) * xi_f
                y = y + _dot(qc, r0_ref[0, 1, c].astype(BF16)) * xi_b
            y_hi, y_lo = _split_bf16(y)
            mu = _dot(y_hi, gm) + _dot(y_lo, gm)
            dlt = y - mu
            var = _dot((dlt * dlt).astype(BF16), gm)
            yn = dlt * lax.rsqrt(var + LN_EPS) * gn_ref[:, cs]
            gate = g_ref[:, cs].astype(F32)
            o_ref[:, cs] = (gate * jax.nn.sigmoid(gate) * yn).astype(BF16)
    return kern


def _retention(p, out, n_seq, seq_len, row0, log_g, gn_g, gmat, r0):
    n = p.shape[0]
    srow0 = row0 // seq_len
    has_init = r0 is not None
    body = _make_ret_kernel(seq_len, has_init)
    nin = 8 if has_init else 7

    def kern(*refs):
        body(*refs[:nin], *refs[nin + 1:])

    col = OFF_RET // RET_W
    in_specs = [pl.BlockSpec(memory_space=pltpu.SMEM)]
    in_specs += [pl.BlockSpec((seq_len, RET_W), functools.partial(lambda s, j: (srow0 + s, col + j), j=j))
                 for j in range(4)]
    in_specs += [pl.BlockSpec((1, RET_W), lambda s: (0, 0)), pl.BlockSpec((LANES, LANES), lambda s: (0, 0))]
    args = [log_g, p, p, p, p, gn_g, gmat]
    st_spec = pl.BlockSpec((1, 2, 2, LANES, LANES), lambda s: (s, 0, 0, 0, 0))
    if has_init:
        in_specs.append(st_spec)
        args.append(r0)
    in_specs.append(pl.BlockSpec(memory_space=pl.ANY))
    args.append(out)
    return pl.pallas_call(
        kern,
        grid=(n_seq,),
        in_specs=in_specs,
        out_specs=[pl.BlockSpec((seq_len, RET_W), lambda s: (srow0 + s, 0)), st_spec],
        out_shape=[jax.ShapeDtypeStruct((n, RET_W), BF16),
                   jax.ShapeDtypeStruct((n_seq, 2, 2, LANES, LANES), F32)],
        input_output_aliases={len(args) - 1: 0},
        compiler_params=_cparams("arbitrary"),
    )(*args)


def _outproj_kernel(a_ref, c_ref, r_ref, x_ref, g1_ref, sc_ref, sh_ref, n2_ref, wo_ref, rwt_ref, rb_ref,
                    x1_ref, h2_ref, te_ref, tg_ref):
    mixed = (_dot(a_ref[...], wo_ref[0:ATTN_W, :])
             + _dot(c_ref[...], wo_ref[ATTN_W:ATTN_W + CONV_CH, :])
             + _dot(r_ref[...], wo_ref[ATTN_W + CONV_CH:, :]))
    x1 = x_ref[...] + g1_ref[0] * mixed
    x1_ref[...] = x1
    inv = lax.rsqrt(jnp.mean(x1 * x1, axis=-1, keepdims=True) + EPS)
    h2 = (x1 * inv * n2_ref[...]) * (1.0 + sc_ref[0]) + sh_ref[0]
    h2_ref[...] = h2.astype(BF16)
    w_hi, w_lo = _split_bf16(rwt_ref[...])
    h_hi, h_lo = _split_bf16(h2)
    logits = _dot_nt(w_hi, h_hi) + _dot_nt(w_hi, h_lo) + _dot_nt(w_lo, h_hi) + rb_ref[...]
    t = logits.shape[1]
    eidx = lax.broadcasted_iota(I32, (N_EXPERTS, t), 0).astype(F32)
    vals = logits
    tops, idxs = [], []
    for _ in range(TOP_K):
        m = jnp.max(vals, axis=0, keepdims=True)
        idx = jnp.min(jnp.where(vals == m, eidx, float(N_EXPERTS)), axis=0, keepdims=True)
        tops.append(m)
        idxs.append(idx)
        vals = jnp.where(eidx == idx, -jnp.inf, vals)
    es = [jnp.exp(m - tops[0]) for m in tops]
    tot = es[0] + es[1] + es[2] + es[3]
    te_ref[...] = jnp.concatenate(idxs + [jnp.zeros((8 - TOP_K, t), F32)], axis=0).astype(I32)
    tg_ref[...] = jnp.concatenate([e / tot for e in es] + [jnp.zeros((8 - TOP_K, t), F32)], axis=0)


def _outproj(attn, conv, ret, x, mod, norm_g, w_out_bf, rwt, rb, n_ctx, dec_seq):
    n, d = x.shape
    t = TOK_TILE
    grp = functools.partial(_group_of_tile, tile=t, n_ctx=n_ctx, dec_seq=dec_seq)

    def mod_spec(which):
        return pl.BlockSpec((1, 1, d), lambda i: (grp(i) * 6 + which, 0, 0))

    const = lambda shape: pl.BlockSpec(shape, lambda i: (0,) * len(shape))
    row = lambda w: pl.BlockSpec((t, w), lambda i: (i, 0))
    lane = lambda: pl.BlockSpec((8, t), lambda i: (0, i))
    return pl.pallas_call(
        _outproj_kernel,
        grid=(n // t,),
        in_specs=[row(ATTN_W), row(CONV_CH), row(RET_W), row(d),
                  mod_spec(MOD_G1), mod_spec(MOD_SC2), mod_spec(MOD_SH2),
                  const((1, d)), const((d, d)), const((N_EXPERTS, d)), const((N_EXPERTS, 1))],
        out_specs=[row(d), row(d), lane(), lane()],
        out_shape=[jax.ShapeDtypeStruct((n, d), F32), jax.ShapeDtypeStruct((n, d), BF16),
                   jax.ShapeDtypeStruct((8, n), I32), jax.ShapeDtypeStruct((8, n), F32)],
        compiler_params=_cparams("arbitrary"),
    )(attn, conv, ret, x, mod, mod, mod, norm_g, w_out_bf, rwt, rb)


def _round_up(x, m):
    return jnp.floor((x + (m - 1.0)) * (1.0 / m)) * m


def _route_kernel(te_ref, u_ref, ltri_ref, lpos_ref, run8_ref, loff_ref, goff_ref, seg_ref, run_ref, start_ref):
    ph = pl.program_id(0)
    i = pl.program_id(1)
    t = MOVE_TILE
    eidx = lax.broadcasted_iota(I32, (N_EXPERTS, t), 0)

    @pl.when(jnp.logical_and(ph == 0, i == 0))
    def _():
        run_ref[...] = jnp.zeros_like(run_ref)

    @pl.when(jnp.logical_and(ph == 1, i == 0))
    def _():
        seg = run_ref[...]
        seg_ref[...] = seg
        start_ref[...] = _dot(ltri_ref[...], _round_up(seg, MOE_BM), precision=lax.Precision.HIGHEST)
        run_ref[...] = jnp.zeros_like(run_ref)

    for s in range(ROUTE_TILES):
        te = te_ref[:, s * t:(s + 1) * t]
        hits = [eidx == te[k:k + 1, :] for k in range(TOP_K)]
        onehot = sum(h.astype(F32) for h in hits)
        run8 = _round_up(jnp.sum(onehot, axis=1, keepdims=True), SUBLANES)

        @pl.when(ph == 0)
        def _(run8=run8):
            run_ref[...] = run_ref[...] + run8

        @pl.when(ph == 1)
        def _(s=s, hits=hits, onehot=onehot, run8=run8):
            before = _dot(onehot.astype(BF16), u_ref[...])
            run8_b = jnp.broadcast_to(run8, (N_EXPERTS, LANES))
            loff = _dot(ltri_ref[...], run8_b, precision=lax.Precision.HIGHEST)
            base = before + loff[:, 0:1]
            rows = [jnp.sum(jnp.where(h, base, 0.0), axis=0, keepdims=True) for h in hits]
            rows.append(jnp.zeros((8 - TOP_K, t), F32))
            lpos_ref[:, s * t:(s + 1) * t] = jnp.concatenate(rows, axis=0).astype(I32)
            run8_ref[s] = run8_b.astype(I32)
            loff_ref[s] = loff.astype(I32)
            goff_ref[s] = (start_ref[...] + run_ref[...]).astype(I32)
            run_ref[...] = run_ref[...] + run8


def _route(top_e, upper, ltri):
    n = top_e.shape[1]
    t = MOVE_TILE
    nt = n // t
    g = ROUTE_TILES
    assert nt % g == 0
    table = pl.BlockSpec((g, N_EXPERTS, LANES), lambda ph, i: (i * ph, 0, 0))
    table_shape = jax.ShapeDtypeStruct((nt, N_EXPERTS, LANES), I32)
    return pl.pallas_call(
        _route_kernel,
        grid=(2, nt // g),
        in_specs=[pl.BlockSpec((8, g * t), lambda ph, i: (0, i)),
                  pl.BlockSpec((t, t), lambda ph, i: (0, 0)),
                  pl.BlockSpec((N_EXPERTS, N_EXPERTS), lambda ph, i: (0, 0))],
        out_specs=[pl.BlockSpec((8, g * t), lambda ph, i: (0, i * ph)), table, table, table,
                   pl.BlockSpec((N_EXPERTS, LANES), lambda ph, i: (0, 0))],
        out_shape=[jax.ShapeDtypeStruct((8, n), I32), table_shape, table_shape, table_shape,
                   jax.ShapeDtypeStruct((N_EXPERTS, LANES), F32)],
        scratch_shapes=[pltpu.VMEM((N_EXPERTS, LANES), F32), pltpu.VMEM((N_EXPERTS, LANES), F32)],
        compiler_params=_cparams("arbitrary", "arbitrary"),
    )(top_e, upper, ltri)


def _for_each_run_chunk(tile, run8_ref, loff_ref, goff_ref, move, fill):
    def per_expert(e, total):
        idx = tile * N_EXPERTS + e
        rows = run8_ref[idx]
        lo = loff_ref[idx]
        go = goff_ref[idx]
        for size in CHUNK_SIZES:
            hit = (rows & size) != 0

            @pl.when(hit)
            def _(lo=lo, go=go, size=size):
                move(pl.multiple_of(lo, SUBLANES), pl.multiple_of(go, SUBLANES), size)

            lo = lo + jnp.where(hit, size, 0)
            go = go + jnp.where(hit, size, 0)
        return total + rows

    total = lax.fori_loop(0, N_EXPERTS, per_expert, jnp.int32(0))
    spare = COMPACT_ROWS - total
    off = jnp.int32(0)
    for size in CHUNK_SIZES:
        hit = (spare & size) != 0

        @pl.when(hit)
        def _(off=off, size=size):
            fill(pl.multiple_of(off, SUBLANES), size)

        off = off + jnp.where(hit, size, 0)


def _make_dispatch_kernel(n_tiles, n_blocks):
    t = MOVE_TILE
    cb = COMPACT_ROWS
    spare_row0 = n_blocks * MOE_BM

    def kern(run8_ref, loff_ref, goff_ref, seg_ref, start_ref, nv_ref, h_ref, lpos_ref, xs_ref,
             buf, zeros_ref, sem, zsem):
        step = pl.program_id(0)
        slot = lax.rem(step, 2)

        def wait_tile(s):
            pltpu.make_async_copy(buf.at[s], xs_ref.at[pl.ds(0, cb)], sem.at[s]).wait()

        def for_each_zero_chunk(action):
            def per_expert(e, carry):
                seg = seg_ref[e]
                padlen = (-seg) & (MOE_BM - 1)
                row = start_ref[e] + seg
                for size in CHUNK_SIZES[1:]:
                    hit = (padlen & size) != 0

                    @pl.when(hit)
                    def _(row=row, size=size):
                        dst = xs_ref.at[pl.ds(pl.multiple_of(row, SUBLANES), size)]
                        action(pltpu.make_async_copy(zeros_ref.at[pl.ds(0, size)], dst, zsem))

                    row = row + jnp.where(hit, size, 0)
                return carry

            lax.fori_loop(0, N_EXPERTS, per_expert, 0)

            def per_tail_block(b, carry):
                action(pltpu.make_async_copy(zeros_ref, xs_ref.at[pl.ds(b * MOE_BM, MOE_BM)], zsem))
                return carry

            lax.fori_loop(nv_ref[0], n_blocks + 2, per_tail_block, 0)

        @pl.when(step == 0)
        def _():
            zeros_ref[...] = jnp.zeros_like(zeros_ref)
            for_each_zero_chunk(lambda cp: cp.start())
            for_each_zero_chunk(lambda cp: cp.wait())

        @pl.when(step >= 2)
        def _():
            wait_tile(slot)

        lpos = lpos_ref[...]
        rows = lax.broadcasted_iota(I32, (cb, t), 0)
        onehot = jnp.where(rows == lpos[0:1, :], 1.0, 0.0)
        for k in range(1, TOP_K):
            onehot = onehot + jnp.where(rows == lpos[k:k + 1, :], 1.0, 0.0)
        buf[slot] = _dot(onehot.astype(BF16), h_ref[...])

        def move(lo, go, size):
            pltpu.make_async_copy(buf.at[slot, pl.ds(lo, size)], xs_ref.at[pl.ds(go, size)], sem.at[slot]).start()

        def fill(off, size):
            dst = xs_ref.at[pl.ds(pl.multiple_of(spare_row0 + slot * t + off, SUBLANES), size)]
            pltpu.make_async_copy(buf.at[slot, pl.ds(off, size)], dst, sem.at[slot]).start()

        _for_each_run_chunk(step, run8_ref, loff_ref, goff_ref, move, fill)

        @pl.when(step == n_tiles - 1)
        def _():
            wait_tile(slot)
            if n_tiles > 1:
                wait_tile(1 - slot)
    return kern


def _dispatch(tables, seg, starts, n_valid, h2, lpos, n_blocks):
    n, d = h2.shape
    t = MOVE_TILE
    return pl.pallas_call(
        _make_dispatch_kernel(n // t, n_blocks),
        grid_spec=pltpu.PrefetchScalarGridSpec(
            num_scalar_prefetch=6,
            grid=(n // t,),
            in_specs=[pl.BlockSpec((t, d), lambda i, *_: (i, 0)),
                      pl.BlockSpec((8, t), lambda i, *_: (0, i))],
            out_specs=pl.BlockSpec(memory_space=pl.ANY),
            scratch_shapes=[pltpu.VMEM((2, COMPACT_ROWS, d), F32), pltpu.VMEM((MOE_BM, d), F32),
                            pltpu.SemaphoreType.DMA((2,)), pltpu.SemaphoreType.DMA(())],
        ),
        out_shape=jax.ShapeDtypeStruct(((n_blocks + 2) * MOE_BM, d), F32),
        compiler_params=_cparams("arbitrary"),
    )(*tables, seg, starts, n_valid, h2, lpos)


def _make_expert_kernel(layer):
    def kern(be_ref, bs_ref, nv_ref, ord_ref, nxt_ref, x_ref, wgu_hbm, bgu_ref, wdn_hbm, bdn_ref, y_ref,
             wgu_f, wdn_f, wgu_s, wdn_s, sem):
        _expert_body(layer, be_ref, nv_ref, ord_ref, nxt_ref, x_ref, wgu_hbm, bgu_ref, wdn_hbm, bdn_ref, y_ref,
                     wgu_f, wdn_f, wgu_s, wdn_s, sem)
    return kern


def _expert_body(layer, be_ref, nv_ref, ord_ref, nxt_ref, x_ref, wgu_hbm, bgu_ref, wdn_hbm, bdn_ref, y_ref,
                 wgu_f, wdn_f, wgu_s, wdn_s, sem):
    i = pl.program_id(0)
    e = be_ref[i]
    prev = be_ref[jnp.maximum(i - 1, 0)]
    new_expert = jnp.logical_or(i == 0, e != prev)
    slot = lax.rem(ord_ref[i], 2)

    def weight_copies(expert, s):
        return (pltpu.make_async_copy(wgu_hbm.at[layer, expert], wgu_f.at[s], sem.at[0, s]),
                pltpu.make_async_copy(wdn_hbm.at[layer, expert], wdn_f.at[s], sem.at[1, s]))

    @pl.when(i == 0)
    def _():
        for cp in weight_copies(e, slot):
            cp.start()

    @pl.when(new_expert)
    def _():
        for cp in weight_copies(e, slot):
            cp.wait()
        nxt = nxt_ref[i]

        @pl.when(nxt >= 0)
        def _():
            for cp in weight_copies(nxt, 1 - slot):
                cp.start(priority=1)

        wgu_s[...] = wgu_f[slot].astype(BF16)
        wdn_s[...] = wdn_f[slot].astype(BF16)

    @pl.when(i < nv_ref[0])
    def _():
        x = x_ref[...].astype(BF16)
        gu = _dot(x, wgu_s[...]) + bgu_ref[0, 0]
        gate = jnp.minimum(gu[:, :D_FF], SWIGLU_LIMIT)
        up = jnp.clip(gu[:, D_FF:], -SWIGLU_LIMIT, SWIGLU_LIMIT)
        hdn = (up + 1.0) * (gate * jax.nn.sigmoid(SWIGLU_ALPHA * gate))
        y_ref[...] = _dot(hdn.astype(BF16), wdn_s[...]) + bdn_ref[0, 0]

    @pl.when(i >= nv_ref[0])
    def _():
        y_ref[...] = jnp.zeros_like(y_ref)


def _experts(blk_e, blk_src, n_valid, blk_ord, blk_next, xs, w_gu, b_gu, w_dn, b_dn, layer, n_blocks):
    d = xs.shape[1]
    depth = w_gu.shape[0]

    def bias(width):
        return pl.BlockSpec((1, 1, 1, width), lambda i, be, *_: (layer, be[i], 0, 0))

    return pl.pallas_call(
        _make_expert_kernel(layer),
        grid_spec=pltpu.PrefetchScalarGridSpec(
            num_scalar_prefetch=5,
            grid=(n_blocks,),
            in_specs=[pl.BlockSpec((MOE_BM, d), lambda i, be, bs, *_: (bs[i], 0)),
                      pl.BlockSpec(memory_space=pl.ANY), bias(2 * D_FF),
                      pl.BlockSpec(memory_space=pl.ANY), bias(d)],
            out_specs=pl.BlockSpec((MOE_BM, d), lambda i, *_: (i, 0)),
            scratch_shapes=[pltpu.VMEM((2, d, 2 * D_FF), F32), pltpu.VMEM((2, D_FF, d), F32),
                            pltpu.VMEM((d, 2 * D_FF), BF16), pltpu.VMEM((D_FF, d), BF16),
                            pltpu.SemaphoreType.DMA((2, 2))],
        ),
        out_shape=jax.ShapeDtypeStruct((n_blocks * MOE_BM, d), F32),
        compiler_params=_cparams("arbitrary"),
    )(blk_e, blk_src, n_valid, blk_ord, blk_next, xs, w_gu, b_gu.reshape(depth, N_EXPERTS, 1, -1), w_dn,
      b_dn.reshape(depth, N_EXPERTS, 1, -1))


def _make_combine_kernel(n_tiles, final):
    t = MOVE_TILE
    cb = COMPACT_ROWS

    def kern(run8_ref, loff_ref, goff_ref, ys_ref, x1_ref, lpt_ref, gt_ref, g2_ref, fg_ref, o_ref,
             ybuf, spare, sem):
        step = pl.program_id(0)
        slot = lax.rem(step, 2)

        def fetch(tile, s):
            def move(lo, go, size):
                pltpu.make_async_copy(ys_ref.at[pl.ds(go, size)], ybuf.at[s, pl.ds(lo, size)], sem.at[s]).start()

            def fill(off, size):
                pltpu.make_async_copy(ys_ref.at[pl.ds(0, size)], spare.at[s, pl.ds(off, size)], sem.at[s]).start()

            _for_each_run_chunk(tile, run8_ref, loff_ref, goff_ref, move, fill)

        @pl.when(step == 0)
        def _():
            ybuf[...] = jnp.zeros_like(ybuf)
            fetch(0, 0)

        @pl.when(step + 1 < n_tiles)
        def _():
            fetch(step + 1, 1 - slot)

        pltpu.make_async_copy(ys_ref.at[pl.ds(0, cb)], ybuf.at[slot], sem.at[slot]).wait()

        lpt = lpt_ref[...].astype(F32).T.astype(I32)
        gt = gt_ref[...].T
        cols = lax.broadcasted_iota(I32, (t, cb), 1)
        g = jnp.where(cols == lpt[:, 0:1], gt[:, 0:1], 0.0)
        for k in range(1, TOP_K):
            g = g + jnp.where(cols == lpt[:, k:k + 1], gt[:, k:k + 1], 0.0)
        g_hi, g_lo = _split_bf16(g)
        yb = ybuf[slot].astype(BF16)
        y = _dot(g_hi, yb) + _dot(g_lo, yb)
        x2 = x1_ref[...] + g2_ref[0] * y
        if final:
            x2 = x2 * lax.rsqrt(jnp.mean(x2 * x2, axis=-1, keepdims=True) + EPS) * fg_ref[...]
        o_ref[...] = x2
    return kern


def _combine(tables, ys, x1, lpos_t, gates_t, mod, final_g, final, n_ctx, dec_seq):
    n, d = x1.shape
    t = MOVE_TILE
    grp = functools.partial(_group_of_tile, tile=t, n_ctx=n_ctx, dec_seq=dec_seq)
    return pl.pallas_call(
        _make_combine_kernel(n // t, final),
        grid_spec=pltpu.PrefetchScalarGridSpec(
            num_scalar_prefetch=3,
            grid=(n // t,),
            in_specs=[pl.BlockSpec(memory_space=pl.ANY),
                      pl.BlockSpec((t, d), lambda i, *_: (i, 0)),
                      pl.BlockSpec((8, t), lambda i, *_: (0, i)),
                      pl.BlockSpec((8, t), lambda i, *_: (0, i)),
                      pl.BlockSpec((1, 1, d), lambda i, *_: (grp(i) * 6 + MOD_G2, 0, 0)),
                      pl.BlockSpec((1, d), lambda i, *_: (0, 0))],
            out_specs=pl.BlockSpec((t, d), lambda i, *_: (i, 0)),
            scratch_shapes=[pltpu.VMEM((2, COMPACT_ROWS, d), F32), pltpu.VMEM((2, t, d), F32),
                            pltpu.SemaphoreType.DMA((2,))],
        ),
        out_shape=jax.ShapeDtypeStruct((n, d), F32),
        compiler_params=_cparams("arbitrary"),
    )(*tables, ys, x1, lpos_t, gates_t, mod, final_g)


def _blockdiag_pairs(s):
    z = jnp.zeros_like(s[..., 0, :, :])
    def pair(a, b):
        return jnp.concatenate([jnp.concatenate([a, z], axis=-1), jnp.concatenate([z, b], axis=-1)], axis=-2)
    return jnp.stack([pair(s[..., 0, :, :], s[..., 1, :, :]), pair(s[..., 2, :, :], s[..., 3, :, :])], axis=-3)


def _diag_blocks(st):
    h = HEAD_DIM
    blocks = [st[:, :, c, j * h:(j + 1) * h, j * h:(j + 1) * h] for c in range(2) for j in range(2)]
    return jnp.stack(blocks, axis=2)


def kernel(x_prompt, x_sample, cache_k, cache_v, state_ret, c, c_ctx, w_mod, b_mod, norm1_g, norm2_g, w_in,
           q_norm_g, k_norm_g, conv_w, conv_b, conv_ln_g, conv_ln_b, ret_decay_logit, ret_gn_g, w_out,
           router_w, router_b, moe_w_gu, moe_b_gu, moe_w_dn, moe_b_dn, final_g):
    batch, seq, d = x_prompt.shape
    dec_batch, dec_seq, _ = x_sample.shape
    depth = w_mod.shape[0]
    past = cache_k.shape[2]
    n_ctx = batch * seq
    n_lat = dec_batch * dec_seq
    n = n_ctx + n_lat
    assert d == D_MODEL and dec_batch == 2
    assert n_ctx % dec_seq == 0 and dec_seq % TOK_TILE == 0 and seq % ATTN_QB == 0 and dec_seq % ATTN_QB == 0
    assert MOVE_TILE == MOE_BM and n % MOVE_TILE == 0

    x = jnp.concatenate([x_prompt.reshape(n_ctx, d), x_sample.reshape(n_lat, d)], axis=0)
    mods = _modulation(jnp.concatenate([c_ctx[None, :], c], axis=0), w_mod, b_mod)
    rope = _rope_tables(dec_seq)

    head_of_col = jnp.arange(QK_W) // HEAD_DIM
    hsum = (head_of_col[:, None] == jnp.arange(LANES)[None, :]).astype(BF16)
    hbc = hsum.T
    lane_head = jnp.arange(LANES) // HEAD_DIM
    gmat = ((lane_head[:, None] == lane_head[None, :]).astype(F32) / HEAD_DIM).astype(BF16)
    tt = jnp.arange(MOVE_TILE)
    upper = (tt[:, None] < tt[None, :]).astype(BF16)
    ee = jnp.arange(N_EXPERTS)
    ltri = (ee[None, :] < ee[:, None]).astype(F32)

    n_tiles = n // MOVE_TILE
    max_rows = n * TOP_K + n_tiles * N_EXPERTS * (SUBLANES - 1) + N_EXPERTS * (MOE_BM - 1)
    n_blocks = -(-max_rows // MOE_BM)
    cache_k2 = cache_k.reshape(dec_batch, depth, past, KV_W)
    cache_v2 = cache_v.reshape(dec_batch, depth, past, KV_W)

    ks_out, vs_out, ss_out = [], [], []
    for l in range(depth):
        mod = mods[l]
        gqk = jnp.concatenate([jnp.tile(q_norm_g[l], N_Q_HEADS), jnp.tile(k_norm_g[l], N_KV_HEADS)])[None, :]
        p, kn, vv = _inproj(x, mod, norm1_g[l][None, :], w_in[l].astype(BF16), gqk, hsum, hbc, rope,
                            n_ctx, dec_seq)
        ks_out.append(kn[:n_ctx].reshape(batch, seq, N_KV_HEADS, HEAD_DIM))
        vs_out.append(vv[:n_ctx].reshape(batch, seq, N_KV_HEADS, HEAD_DIM))

        attn = jnp.zeros((n, ATTN_W), BF16)
        attn = _attention(p, attn, batch, seq, 0, None, l)
        attn = _attention(p, attn, dec_batch, dec_seq, n_ctx, (cache_k2, cache_v2), l)

        cw, cb = conv_w[l], conv_b[l][None, :]
        clg, clb = conv_ln_g[l][None, :], conv_ln_b[l][None, :]
        conv = jnp.zeros((n, CONV_CH), BF16)
        conv = _conv(p, conv, batch, seq, 0, cw, cb, clg, clb)
        conv = _conv(p, conv, dec_batch, dec_seq, n_ctx, cw, cb, clg, clb)

        log_g = jax.nn.log_sigmoid(ret_decay_logit[l].astype(F32))
        gn = ret_gn_g[l][None, :]
        ret = jnp.zeros((n, RET_W), BF16)
        ret, st_ctx = _retention(p, ret, batch, seq, 0, log_g, gn, gmat, None)
        ret, _ = _retention(p, ret, dec_batch, dec_seq, n_ctx, log_g, gn, gmat,
                            _blockdiag_pairs(state_ret[:, l].astype(F32)))
        ss_out.append(_diag_blocks(st_ctx))

        x1, h2, top_e, top_g = _outproj(attn, conv, ret, x, mod, norm2_g[l][None, :], w_out[l].astype(BF16),
                                        router_w[l].T, router_b[l][:, None], n_ctx, dec_seq)
        lpos, run8, loff, goff, seg = _route(top_e, upper, ltri)
        tables = [tb[:, :, 0].reshape(-1) for tb in (run8, loff, goff)]
        seg = seg[:, 0].astype(I32)
        padded = (seg + MOE_BM - 1) // MOE_BM * MOE_BM
        pad_end = jnp.cumsum(padded)
        n_valid = (pad_end[-1] // MOE_BM).reshape(1)
        blk_src = jnp.minimum(jnp.arange(n_blocks, dtype=I32), n_valid - 1)
        blk_e = jnp.minimum(jnp.sum((pad_end[None, :] <= (blk_src * MOE_BM)[:, None]).astype(I32), axis=1),
                            N_EXPERTS - 1)
        owns = padded > 0
        ord_e = jnp.cumsum(owns.astype(I32)) - 1
        later = jnp.where(owns[None, :] & (ee[None, :] > ee[:, None]), ee[None, :], N_EXPERTS)
        next_e = jnp.min(later, axis=1)
        next_e = jnp.where(next_e == N_EXPERTS, -1, next_e).astype(I32)

        xs = _dispatch(tables, seg, pad_end - padded, n_valid, h2, lpos, n_blocks)
        ys = _experts(blk_e, blk_src, n_valid, ord_e[blk_e], next_e[blk_e], xs,
                      moe_w_gu, moe_b_gu, moe_w_dn, moe_b_dn, l, n_blocks)
        x = _combine(tables, ys, x1, lpos, top_g, mod, final_g[None, :], l == depth - 1, n_ctx, dec_seq)

    y_prompt = x[:n_ctx].reshape(batch, seq, d)
    y_sample = x[n_ctx:].reshape(dec_batch, dec_seq, d)
    return (y_prompt, y_sample, jnp.stack(ks_out, axis=1), jnp.stack(vs_out, axis=1),
            jnp.stack(ss_out, axis=1))
```

```python
import functools

import numpy as np
import jax
import jax.numpy as jnp
from jax import lax
from jax.experimental import pallas as pl
from jax.experimental.pallas import tpu as pltpu

F32 = jnp.float32
BF16 = jnp.bfloat16
I32 = jnp.int32

D_MODEL = 1024
GRID_W = 64
HEAD_DIM = 64
N_Q_HEADS = 8
N_KV_HEADS = 2
ATTN_W = N_Q_HEADS * HEAD_DIM
KV_W = N_KV_HEADS * HEAD_DIM
QK_W = ATTN_W + KV_W
CONV_CH = 256
CONV_K = 31
CONV_PAD = CONV_K // 2
CONV_HALO = 16
N_RET_HEADS = 4
RET_W = 256
OFF_V = QK_W
OFF_CONV = OFF_V + KV_W
OFF_RET = OFF_CONV + 2 * CONV_CH
IN_COLS = OFF_RET + 4 * RET_W
ROPE_HALF = HEAD_DIM // 2
ROPE_THETA = 10000.0
N_EXPERTS = 32
TOP_K = 4
D_FF = D_MODEL
SWIGLU_LIMIT = 7.0
SWIGLU_ALPHA = 1.702
EPS = 1e-6
LN_EPS = 1e-5

LANES = 128
TOK_TILE = 512
ATTN_QB = 256
RET_QB = 256
CONV_ROWS = 64
MOE_BM = 256
MOVE_TILE = 256
ROUTE_TILES = 4
SUBLANES = 8
COMPACT_ROWS = 1280
assert COMPACT_ROWS >= MOVE_TILE * TOP_K + N_EXPERTS * (SUBLANES - 1) and COMPACT_ROWS % MOE_BM == 0
CHUNK_SIZES = tuple(MOVE_TILE >> s for s in range(6))
VMEM_LIMIT = 56 * 1024 * 1024

MOD_SH1, MOD_SC1, MOD_G1, MOD_SH2, MOD_SC2, MOD_G2 = range(6)


def _cparams(*sem):
    return pltpu.CompilerParams(dimension_semantics=sem, vmem_limit_bytes=VMEM_LIMIT)


def _dot(a, b, **kw):
    return jnp.dot(a, b, preferred_element_type=F32, **kw)


def _dot_nt(a, b, **kw):
    return lax.dot_general(a, b, (((1,), (1,)), ((), ())), preferred_element_type=F32, **kw)


def _dot_tn(a, b, **kw):
    return lax.dot_general(a, b, (((0,), (0,)), ((), ())), preferred_element_type=F32, **kw)


def _split_bf16(x):
    hi = x.astype(BF16)
    lo = (x - hi.astype(F32)).astype(BF16)
    return hi, lo


MOD_TN = 1536


def _mod_kernel(ct_ref, w_ref, b_ref, o_ref):
    s = ct_ref[...]
    s = s * jax.nn.sigmoid(s)
    w = w_ref[0]
    rows = [jnp.sum(w * s[:, r:r + 1], axis=0, keepdims=True) for r in range(3)]
    rows.append(jnp.zeros((5, w.shape[1]), F32))
    o_ref[0] = jnp.concatenate(rows, axis=0) + b_ref[0]


def _modulation(cvec3, w_mod, b_mod):
    depth, d, cols = w_mod.shape
    ct = jnp.zeros((d, 8), F32).at[:, :3].set(cvec3.T)
    out = pl.pallas_call(
        _mod_kernel,
        grid=(depth, cols // MOD_TN),
        in_specs=[
            pl.BlockSpec((d, 8), lambda l, j: (0, 0)),
            pl.BlockSpec((1, d, MOD_TN), lambda l, j: (l, 0, j)),
            pl.BlockSpec((1, 1, MOD_TN), lambda l, j: (l, 0, j)),
        ],
        out_specs=pl.BlockSpec((1, 8, MOD_TN), lambda l, j: (l, 0, j)),
        out_shape=jax.ShapeDtypeStruct((depth, 8, cols), F32),
        compiler_params=_cparams("arbitrary", "arbitrary"),
    )(ct, w_mod, b_mod.reshape(depth, 1, cols))
    return out[:, :3].reshape(depth, 3 * 6, 1, d)


def _token_specs(parts, tile, n_ctx):
    d = parts[0].shape[1]
    if len(parts) == 1:
        return [pl.BlockSpec((tile, d), lambda i, *_: (i, 0))]
    ctx_tiles = n_ctx // tile
    return [pl.BlockSpec((tile, d), lambda i, *_: (jnp.minimum(i, ctx_tiles - 1), 0)),
            pl.BlockSpec((tile, d), lambda i, *_: (jnp.maximum(i - ctx_tiles, 0), 0))]


def _token_tile(refs, ctx_tiles):
    if len(refs) == 1:
        return refs[0][...]
    return jnp.where(pl.program_id(0) < ctx_tiles, refs[0][...], refs[1][...])


def _inproj_kernel(n_src, ctx_tiles, *refs):
    x_refs = refs[:n_src]
    (sh_ref, sc_ref, g_ref, w_ref, gqk_ref, hsum_ref, hbc_ref, cos_ref, sa_ref, sb_ref,
     p_ref, k_ref, v_ref) = refs[n_src:]
    x = _token_tile(x_refs, ctx_tiles)
    inv = lax.rsqrt(jnp.mean(x * x, axis=-1, keepdims=True) + EPS)
    h = (x * inv * g_ref[...]) * (1.0 + sc_ref[0]) + sh_ref[0]
    acc = _dot(h.astype(BF16), w_ref[...])
    qk = acc[:, :QK_W]
    ss = _dot((qk * qk).astype(BF16), hsum_ref[...])
    r = lax.rsqrt(ss * (1.0 / HEAD_DIM) + EPS)
    r_hi, r_lo = _split_bf16(r)
    rb = _dot(r_hi, hbc_ref[...]) + _dot(r_lo, hbc_ref[...])
    qkn = qk * rb * gqk_ref[...]
    k_ref[...] = qkn[:, ATTN_W:QK_W]
    v_ref[...] = acc[:, OFF_V:OFF_CONV]
    cos = cos_ref[...]
    sa = sa_ref[...]
    sb = sb_ref[...]
    for j in range(QK_W // LANES):
        c = qkn[:, j * LANES:(j + 1) * LANES]
        up = pltpu.roll(c, LANES - ROPE_HALF // 2, 1)
        dn = pltpu.roll(c, ROPE_HALF // 2, 1)
        p_ref[:, j * LANES:(j + 1) * LANES] = (c * cos + up * sa + dn * sb).astype(BF16)
    p_ref[:, QK_W:] = acc[:, QK_W:].astype(BF16)


def _rope_tables(dec_seq):
    rows = dec_seq // GRID_W
    row = jnp.repeat(jnp.arange(rows), GRID_W).astype(F32)
    col = jnp.tile(jnp.arange(GRID_W), rows).astype(F32)
    inv = 1.0 / (ROPE_THETA ** (jnp.arange(0, ROPE_HALF, 2, dtype=F32) / ROPE_HALF))
    ar = row[:, None] * inv[None, :]
    ac = col[:, None] * inv[None, :]
    cos = jnp.concatenate([jnp.cos(ar), jnp.cos(ar), jnp.cos(ac), jnp.cos(ac)], axis=-1)
    sin = jnp.concatenate([jnp.sin(ar), jnp.sin(ar), jnp.sin(ac), jnp.sin(ac)], axis=-1)
    first = (jnp.arange(HEAD_DIM) % ROPE_HALF) < ROPE_HALF // 2
    sa = jnp.where(first[None, :], -sin, 0.0)
    sb = jnp.where(first[None, :], 0.0, sin)
    def table(t, ident):
        t = jnp.concatenate([jnp.full((TOK_TILE, HEAD_DIM), ident, F32), t], axis=0)
        return jnp.tile(t, (1, LANES // HEAD_DIM))
    return table(cos, 1.0), table(sa, 0.0), table(sb, 0.0)


def _group_of_tile(i, tile, n_ctx, dec_seq):
    tok = i * tile
    return jnp.where(tok < n_ctx, 0, 1 + (tok - n_ctx) // dec_seq)


def _inproj(x_parts, mod, norm_g, w_in_bf, gqk, hsum, hbc, rope, n_ctx, dec_seq):
    n = sum(part.shape[0] for part in x_parts)
    d = x_parts[0].shape[1]
    t = TOK_TILE
    grp = functools.partial(_group_of_tile, tile=t, n_ctx=n_ctx, dec_seq=dec_seq)

    def mod_spec(which):
        return pl.BlockSpec((1, 1, d), lambda i: (grp(i) * 6 + which, 0, 0))

    def rope_idx(i):
        tok = i * t
        return (jnp.where(tok < n_ctx, 0, 1 + ((tok - n_ctx) % dec_seq) // t), 0)

    rope_spec = pl.BlockSpec((t, LANES), rope_idx)
    const = lambda shape: pl.BlockSpec(shape, lambda i: (0,) * len(shape))
    return pl.pallas_call(
        functools.partial(_inproj_kernel, len(x_parts), n_ctx // t),
        grid=(n // t,),
        in_specs=_token_specs(x_parts, t, n_ctx) + [
            mod_spec(MOD_SH1), mod_spec(MOD_SC1),
            const((1, d)),
            const((d, IN_COLS)),
            const((1, QK_W)), const((QK_W, LANES)), const((LANES, QK_W)),
            rope_spec, rope_spec, rope_spec,
        ],
        out_specs=[
            pl.BlockSpec((t, IN_COLS), lambda i: (i, 0)),
            pl.BlockSpec((t, KV_W), lambda i: (i, 0)),
            pl.BlockSpec((t, KV_W), lambda i: (i, 0)),
        ],
        out_shape=[
            jax.ShapeDtypeStruct((n, IN_COLS), BF16),
            jax.ShapeDtypeStruct((n, KV_W), F32),
            jax.ShapeDtypeStruct((n, KV_W), F32),
        ],
        compiler_params=_cparams("arbitrary"),
    )(*x_parts, mod, mod, norm_g, w_in_bf, gqk, hsum, hbc, *rope)


def _head_halves(x, hkv, low):
    r = pltpu.roll(x, HEAD_DIM, 1)
    rep = jnp.where(low, x, r) if hkv == 0 else jnp.where(low, r, x)
    return jnp.where(low, rep, 0.0).astype(BF16), jnp.where(low, 0.0, rep).astype(BF16)


def _make_attn_kernel(has_ctx):
    def kern(*refs):
        if has_ctx:
            q_ref, k_ref, v_ref, ck_ref, cv_ref, o_ref = refs
        else:
            q_ref, k_ref, v_ref, o_ref = refs
        low = lax.broadcasted_iota(I32, (1, LANES), 1) < HEAD_DIM
        k = k_ref[...].astype(F32)
        v = v_ref[...].astype(F32)
        if has_ctx:
            ck = ck_ref[0, 0]
            cv = cv_ref[0, 0]
        scale = HEAD_DIM ** -0.5
        for hkv in range(N_KV_HEADS):
            kh = _head_halves(k, hkv, low)
            vh = _head_halves(v, hkv, low)
            if has_ctx:
                ckh = _head_halves(ck, hkv, low)
                cvh = _head_halves(cv, hkv, low)
            for c in range(2):
                col = hkv * 2 * LANES + c * LANES
                qc = q_ref[:, col:col + LANES] * scale
                o_c = None
                for j in range(2):
                    s = _dot_nt(qc, kh[j])
                    m = jnp.max(s, axis=-1, keepdims=True)
                    if has_ctx:
                        s2 = _dot_nt(qc, ckh[j])
                        m = jnp.maximum(m, jnp.max(s2, axis=-1, keepdims=True))
                    p = jnp.exp(s - m)
                    l = jnp.sum(p, axis=-1, keepdims=True)
                    o = _dot(p.astype(BF16), vh[j])
                    if has_ctx:
                        p2 = jnp.exp(s2 - m)
                        l = l + jnp.sum(p2, axis=-1, keepdims=True)
                        o = o + _dot(p2.astype(BF16), cvh[j])
                    o = o / l
                    o_c = o if o_c is None else o_c + o
                o_ref[:, col:col + LANES] = o_c.astype(BF16)
    return kern


def _attention(p, out, n_seq, seq_len, row0, ctx_kv, layer):
    n = p.shape[0]
    qb = ATTN_QB
    nq = seq_len // qb
    qrow0 = row0 // qb
    srow0 = row0 // seq_len
    has_ctx = ctx_kv is not None
    in_specs = [
        pl.BlockSpec((qb, ATTN_W), lambda b, i: (qrow0 + b * nq + i, 0)),
        pl.BlockSpec((seq_len, KV_W), lambda b, i: (srow0 + b, ATTN_W // KV_W)),
        pl.BlockSpec((seq_len, KV_W), lambda b, i: (srow0 + b, OFF_V // KV_W)),
    ]
    args = [p, p, p]
    if has_ctx:
        ck, cv = ctx_kv
        past = ck.shape[2]
        spec = pl.BlockSpec((1, 1, past, KV_W), lambda b, i: (b, layer, 0, 0))
        in_specs += [spec, spec]
        args += [ck, cv]
    in_specs.append(pl.BlockSpec(memory_space=pl.ANY))
    args.append(out)

    body = _make_attn_kernel(has_ctx)

    def kern(*refs):
        body(*refs[:-2], refs[-1])

    return pl.pallas_call(
        kern,
        grid=(n_seq, nq),
        in_specs=in_specs,
        out_specs=pl.BlockSpec((qb, ATTN_W), lambda b, i: (qrow0 + b * nq + i, 0)),
        out_shape=jax.ShapeDtypeStruct((n, ATTN_W), BF16),
        input_output_aliases={len(args) - 1: 0},
        compiler_params=_cparams("arbitrary", "arbitrary"),
    )(*args)


def _make_conv_kernel(seq_len):
    shifted_rows = seq_len + 2 * CONV_HALO - SUBLANES

    def kern(a_ref, g_ref, w_ref, b_ref, lng_ref, lnb_ref, o_ref, zp_ref, zs_ref):
        zero = jnp.zeros((CONV_HALO, CONV_CH), F32)
        zp_ref[0:CONV_HALO, :] = zero
        zp_ref[CONV_HALO + seq_len:2 * CONV_HALO + seq_len, :] = zero
        zp_ref[CONV_HALO:CONV_HALO + seq_len, :] = (
            a_ref[...].astype(F32) * jax.nn.sigmoid(g_ref[...].astype(F32)))
        for s in range(1, SUBLANES):
            zs_ref[s] = zp_ref[s:s + shifted_rows, :]
        w = w_ref[...]
        bias = b_ref[...]
        for c in range(seq_len // CONV_ROWS):
            base = c * CONV_ROWS + CONV_HALO - CONV_PAD
            acc = jnp.zeros((CONV_ROWS, CONV_CH), F32) + bias
            for j in range(CONV_K):
                shift = (base + j) % SUBLANES
                row = base + j - shift
                if shift == 0:
                    tap = zp_ref[row:row + CONV_ROWS, :]
                else:
                    tap = zs_ref[shift, row:row + CONV_ROWS, :]
                acc = acc + tap * w[j:j + 1, :]
            mu = jnp.mean(acc, axis=-1, keepdims=True)
            dlt = acc - mu
            var = jnp.mean(dlt * dlt, axis=-1, keepdims=True)
            y = dlt * lax.rsqrt(var + LN_EPS) * lng_ref[...] + lnb_ref[...]
            o_ref[c * CONV_ROWS:(c + 1) * CONV_ROWS, :] = (y * jax.nn.sigmoid(y)).astype(BF16)
    return kern


def _conv(p, out, n_seq, seq_len, row0, w, b, lng, lnb):
    n = p.shape[0]
    srow0 = row0 // seq_len
    const = lambda shape: pl.BlockSpec(shape, lambda s: (0,) * len(shape))
    body = _make_conv_kernel(seq_len)

    def kern(a_ref, g_ref, w_ref, b_ref, lng_ref, lnb_ref, _, o_ref, zp_ref, zs_ref):
        body(a_ref, g_ref, w_ref, b_ref, lng_ref, lnb_ref, o_ref, zp_ref, zs_ref)

    return pl.pallas_call(
        kern,
        grid=(n_seq,),
        in_specs=[
            pl.BlockSpec((seq_len, CONV_CH), lambda s: (srow0 + s, OFF_CONV // CONV_CH)),
            pl.BlockSpec((seq_len, CONV_CH), lambda s: (srow0 + s, OFF_CONV // CONV_CH + 1)),
            const((CONV_K, CONV_CH)), const((1, CONV_CH)), const((1, CONV_CH)), const((1, CONV_CH)),
            pl.BlockSpec(memory_space=pl.ANY),
        ],
        out_specs=pl.BlockSpec((seq_len, CONV_CH), lambda s: (srow0 + s, 0)),
        out_shape=jax.ShapeDtypeStruct((n, CONV_CH), BF16),
        scratch_shapes=[pltpu.VMEM((seq_len + 2 * CONV_HALO, CONV_CH), F32),
                        pltpu.VMEM((SUBLANES, seq_len + 2 * CONV_HALO - SUBLANES, CONV_CH), F32)],
        input_output_aliases={6: 0},
        compiler_params=_cparams("arbitrary"),
    )(p, p, w, b, lng, lnb, out)


def _make_ret_kernel(seq_len, has_init):
    qb = min(seq_len, RET_QB)
    nq = seq_len // qb
    scale = HEAD_DIM ** -0.5

    def kern(*refs):
        if has_init:
            lg_ref, q_ref, k_ref, v_ref, g_ref, gn_ref, gm_ref, r0_ref, o_ref, st_ref = refs
        else:
            lg_ref, q_ref, k_ref, v_ref, g_ref, gn_ref, gm_ref, o_ref, st_ref = refs
        low = lax.broadcasted_iota(I32, (1, LANES), 1) < HEAD_DIM
        pos = lax.broadcasted_iota(I32, (seq_len, 1), 0).astype(F32)
        qpos = lax.broadcasted_iota(I32, (qb, 1), 0).astype(F32)
        kpos = lax.broadcasted_iota(I32, (1, seq_len), 1).astype(F32)
        gm = gm_ref[...]
        for c in range(2):
            cs = slice(c * LANES, (c + 1) * LANES)
            lgf = jnp.where(low, lg_ref[0, 2 * c], lg_ref[0, 2 * c + 1])
            lgb = jnp.where(low, lg_ref[1, 2 * c], lg_ref[1, 2 * c + 1])
            qc = q_ref[:, cs]
            kc = k_ref[:, cs]
            vc = v_ref[:, cs]
            kf = kc.astype(F32) * scale
            zeta_f = jnp.exp((seq_len - 1.0 - pos) * lgf)
            zeta_b = jnp.exp(pos * lgb)
            for d, zeta, lgd in ((0, zeta_f, lgf), (1, zeta_b, lgb)):
                st = _dot_tn((kf * zeta).astype(BF16), vc)
                if has_init:
                    st = st + r0_ref[0, d, c] * jnp.exp(seq_len * lgd)
                st_ref[0, d, c] = st
            y_blocks = [None] * nq
            for j in range(2):
                sel = low if j == 0 else jnp.logical_not(low)
                kh = jnp.where(sel, kc, jnp.zeros_like(kc))
                vh = jnp.where(sel, vc, jnp.zeros_like(vc))
                lf = lg_ref[0, 2 * c + j]
                lb = lg_ref[1, 2 * c + j]
                for i in range(nq):
                    s = _dot_nt(qc[i * qb:(i + 1) * qb], kh)
                    diff = (qpos + float(i * qb)) - kpos
                    dec = jnp.exp(jnp.where(diff >= 0, diff * lf, -diff * lb))
                    dec = dec * jnp.where(diff == 0, 2.0 * scale, scale)
                    y = _dot((s * dec).astype(BF16), vh)
                    y_blocks[i] = y if y_blocks[i] is None else y_blocks[i] + y
            y = jnp.concatenate(y_blocks, axis=0) if nq > 1 else y_blocks[0]
            if has_init:
                xi_f = jnp.exp((pos + 1.0) * lgf)
                xi_b = jnp.exp((seq_len - pos) * lgb)
                y = y + _dot(qc, r0_ref[0, 0, c].astype(BF16)) * xi_f
                y = y + _dot(qc, r0_ref[0, 1, c].astype(BF16)) * xi_b
            y_hi, y_lo = _split_bf16(y)
            mu = _dot(y_hi, gm) + _dot(y_lo, gm)
            dlt = y - mu
            var = _dot((dlt * dlt).astype(BF16), gm)
            yn = dlt * lax.rsqrt(var + LN_EPS) * gn_ref[:, cs]
            gate = g_ref[:, cs].astype(F32)
            o_ref[:, cs] = (gate * jax.nn.sigmoid(gate) * yn).astype(BF16)
    return kern


def _retention(p, out, n_seq, seq_len, row0, log_g, gn_g, gmat, r0):
    n = p.shape[0]
    srow0 = row0 // seq_len
    has_init = r0 is not None
    body = _make_ret_kernel(seq_len, has_init)
    nin = 8 if has_init else 7

    def kern(*refs):
        body(*refs[:nin], *refs[nin + 1:])

    col = OFF_RET // RET_W
    in_specs = [pl.BlockSpec(memory_space=pltpu.SMEM)]
    in_specs += [pl.BlockSpec((seq_len, RET_W), functools.partial(lambda s, j: (srow0 + s, col + j), j=j))
                 for j in range(4)]
    in_specs += [pl.BlockSpec((1, RET_W), lambda s: (0, 0)), pl.BlockSpec((LANES, LANES), lambda s: (0, 0))]
    args = [log_g, p, p, p, p, gn_g, gmat]
    st_spec = pl.BlockSpec((1, 2, 2, LANES, LANES), lambda s: (s, 0, 0, 0, 0))
    if has_init:
        in_specs.append(st_spec)
        args.append(r0)
    in_specs.append(pl.BlockSpec(memory_space=pl.ANY))
    args.append(out)
    return pl.pallas_call(
        kern,
        grid=(n_seq,),
        in_specs=in_specs,
        out_specs=[pl.BlockSpec((seq_len, RET_W), lambda s: (srow0 + s, 0)), st_spec],
        out_shape=[jax.ShapeDtypeStruct((n, RET_W), BF16),
                   jax.ShapeDtypeStruct((n_seq, 2, 2, LANES, LANES), F32)],
        input_output_aliases={len(args) - 1: 0},
        compiler_params=_cparams("arbitrary"),
    )(*args)


def _outproj_kernel(n_src, ctx_tiles, *refs):
    x_refs = refs[:n_src]
    (a_ref, c_ref, r_ref, g1_ref, sc_ref, sh_ref, n2_ref, wo_ref, rwt_ref, rb_ref,
     x1_ref, h2_ref, te_ref, tg_ref) = refs[n_src:]
    mixed = (_dot(a_ref[...], wo_ref[0:ATTN_W, :])
             + _dot(c_ref[...], wo_ref[ATTN_W:ATTN_W + CONV_CH, :])
             + _dot(r_ref[...], wo_ref[ATTN_W + CONV_CH:, :]))
    x1 = _token_tile(x_refs, ctx_tiles) + g1_ref[0] * mixed
    x1_ref[...] = x1
    inv = lax.rsqrt(jnp.mean(x1 * x1, axis=-1, keepdims=True) + EPS)
    h2 = (x1 * inv * n2_ref[...]) * (1.0 + sc_ref[0]) + sh_ref[0]
    h2_ref[...] = h2.astype(BF16)
    w_hi, w_lo = _split_bf16(rwt_ref[...])
    h_hi, h_lo = _split_bf16(h2)
    logits = _dot_nt(w_hi, h_hi) + _dot_nt(w_hi, h_lo) + _dot_nt(w_lo, h_hi) + rb_ref[...]
    t = logits.shape[1]
    eidx = lax.broadcasted_iota(I32, (N_EXPERTS, t), 0).astype(F32)
    vals = logits
    tops, idxs = [], []
    for _ in range(TOP_K):
        m = jnp.max(vals, axis=0, keepdims=True)
        idx = jnp.min(jnp.where(vals == m, eidx, float(N_EXPERTS)), axis=0, keepdims=True)
        tops.append(m)
        idxs.append(idx)
        vals = jnp.where(eidx == idx, -jnp.inf, vals)
    es = [jnp.exp(m - tops[0]) for m in tops]
    tot = es[0] + es[1] + es[2] + es[3]
    te_ref[...] = jnp.concatenate(idxs + [jnp.zeros((8 - TOP_K, t), F32)], axis=0).astype(I32)
    tg_ref[...] = jnp.concatenate([e / tot for e in es] + [jnp.zeros((8 - TOP_K, t), F32)], axis=0)


def _outproj(attn, conv, ret, x_parts, mod, norm_g, w_out_bf, rwt, rb, n_ctx, dec_seq):
    n = attn.shape[0]
    d = x_parts[0].shape[1]
    t = TOK_TILE
    grp = functools.partial(_group_of_tile, tile=t, n_ctx=n_ctx, dec_seq=dec_seq)

    def mod_spec(which):
        return pl.BlockSpec((1, 1, d), lambda i: (grp(i) * 6 + which, 0, 0))

    const = lambda shape: pl.BlockSpec(shape, lambda i: (0,) * len(shape))
    row = lambda w: pl.BlockSpec((t, w), lambda i: (i, 0))
    lane = lambda: pl.BlockSpec((8, t), lambda i: (0, i))
    return pl.pallas_call(
        functools.partial(_outproj_kernel, len(x_parts), n_ctx // t),
        grid=(n // t,),
        in_specs=_token_specs(x_parts, t, n_ctx) + [
            row(ATTN_W), row(CONV_CH), row(RET_W),
            mod_spec(MOD_G1), mod_spec(MOD_SC2), mod_spec(MOD_SH2),
            const((1, d)), const((d, d)), const((N_EXPERTS, d)), const((N_EXPERTS, 1))],
        out_specs=[row(d), row(d), lane(), lane()],
        out_shape=[jax.ShapeDtypeStruct((n, d), F32), jax.ShapeDtypeStruct((n, d), BF16),
                   jax.ShapeDtypeStruct((8, n), I32), jax.ShapeDtypeStruct((8, n), F32)],
        compiler_params=_cparams("arbitrary"),
    )(*x_parts, attn, conv, ret, mod, mod, mod, norm_g, w_out_bf, rwt, rb)


def _round_up(x, m):
    return jnp.floor((x + (m - 1.0)) * (1.0 / m)) * m


def _route_kernel(te_ref, u_ref, ltri_ref, lpos_ref, run8_ref, loff_ref, goff_ref, seg_ref, run_ref, start_ref):
    ph = pl.program_id(0)
    i = pl.program_id(1)
    t = MOVE_TILE
    eidx = lax.broadcasted_iota(I32, (N_EXPERTS, t), 0)

    @pl.when(jnp.logical_and(ph == 0, i == 0))
    def _():
        run_ref[...] = jnp.zeros_like(run_ref)

    @pl.when(jnp.logical_and(ph == 1, i == 0))
    def _():
        seg = run_ref[...]
        seg_ref[...] = seg
        start_ref[...] = _dot(ltri_ref[...], _round_up(seg, MOE_BM), precision=lax.Precision.HIGHEST)
        run_ref[...] = jnp.zeros_like(run_ref)

    for s in range(ROUTE_TILES):
        te = te_ref[:, s * t:(s + 1) * t]
        hits = [eidx == te[k:k + 1, :] for k in range(TOP_K)]
        onehot = sum(h.astype(F32) for h in hits)
        run8 = _round_up(jnp.sum(onehot, axis=1, keepdims=True), SUBLANES)

        @pl.when(ph == 0)
        def _(run8=run8):
            run_ref[...] = run_ref[...] + run8

        @pl.when(ph == 1)
        def _(s=s, hits=hits, onehot=onehot, run8=run8):
            before = _dot(onehot.astype(BF16), u_ref[...])
            run8_b = jnp.broadcast_to(run8, (N_EXPERTS, LANES))
            loff = _dot(ltri_ref[...], run8_b, precision=lax.Precision.HIGHEST)
            base = before + loff[:, 0:1]
            rows = [jnp.sum(jnp.where(h, base, 0.0), axis=0, keepdims=True) for h in hits]
            rows.append(jnp.zeros((8 - TOP_K, t), F32))
            lpos_ref[:, s * t:(s + 1) * t] = jnp.concatenate(rows, axis=0).astype(I32)
            run8_ref[s] = run8_b.astype(I32)
            loff_ref[s] = loff.astype(I32)
            goff_ref[s] = (start_ref[...] + run_ref[...]).astype(I32)
            run_ref[...] = run_ref[...] + run8


def _route(top_e, upper, ltri):
    n = top_e.shape[1]
    t = MOVE_TILE
    nt = n // t
    g = ROUTE_TILES
    assert nt % g == 0
    table = pl.BlockSpec((g, N_EXPERTS, LANES), lambda ph, i: (i * ph, 0, 0))
    table_shape = jax.ShapeDtypeStruct((nt, N_EXPERTS, LANES), I32)
    return pl.pallas_call(
        _route_kernel,
        grid=(2, nt // g),
        in_specs=[pl.BlockSpec((8, g * t), lambda ph, i: (0, i)),
                  pl.BlockSpec((t, t), lambda ph, i: (0, 0)),
                  pl.BlockSpec((N_EXPERTS, N_EXPERTS), lambda ph, i: (0, 0))],
        out_specs=[pl.BlockSpec((8, g * t), lambda ph, i: (0, i * ph)), table, table, table,
                   pl.BlockSpec((N_EXPERTS, LANES), lambda ph, i: (0, 0))],
        out_shape=[jax.ShapeDtypeStruct((8, n), I32), table_shape, table_shape, table_shape,
                   jax.ShapeDtypeStruct((N_EXPERTS, LANES), F32)],
        scratch_shapes=[pltpu.VMEM((N_EXPERTS, LANES), F32), pltpu.VMEM((N_EXPERTS, LANES), F32)],
        compiler_params=_cparams("arbitrary", "arbitrary"),
    )(top_e, upper, ltri)


def _for_each_run_chunk(tile, run8_ref, loff_ref, goff_ref, move, fill):
    def per_expert(e, total):
        idx = tile * N_EXPERTS + e
        rows = run8_ref[idx]
        lo = loff_ref[idx]
        go = goff_ref[idx]
        for size in CHUNK_SIZES:
            hit = (rows & size) != 0

            @pl.when(hit)
            def _(lo=lo, go=go, size=size):
                move(pl.multiple_of(lo, SUBLANES), pl.multiple_of(go, SUBLANES), size)

            lo = lo + jnp.where(hit, size, 0)
            go = go + jnp.where(hit, size, 0)
        return total + rows

    total = lax.fori_loop(0, N_EXPERTS, per_expert, jnp.int32(0))
    spare = COMPACT_ROWS - total
    off = jnp.int32(0)
    for size in CHUNK_SIZES:
        hit = (spare & size) != 0

        @pl.when(hit)
        def _(off=off, size=size):
            fill(pl.multiple_of(off, SUBLANES), size)

        off = off + jnp.where(hit, size, 0)


def _make_dispatch_kernel(n_tiles, n_blocks):
    t = MOVE_TILE
    cb = COMPACT_ROWS
    spare_row0 = n_blocks * MOE_BM

    def kern(run8_ref, loff_ref, goff_ref, seg_ref, start_ref, nv_ref, h_ref, lpos_ref, xs_ref,
             buf, zeros_ref, sem, zsem):
        step = pl.program_id(0)
        slot = lax.rem(step, 2)

        def wait_tile(s):
            pltpu.make_async_copy(buf.at[s], xs_ref.at[pl.ds(0, cb)], sem.at[s]).wait()

        def for_each_zero_chunk(action):
            def per_expert(e, carry):
                seg = seg_ref[e]
                padlen = (-seg) & (MOE_BM - 1)
                row = start_ref[e] + seg
                for size in CHUNK_SIZES[1:]:
                    hit = (padlen & size) != 0

                    @pl.when(hit)
                    def _(row=row, size=size):
                        dst = xs_ref.at[pl.ds(pl.multiple_of(row, SUBLANES), size)]
                        action(pltpu.make_async_copy(zeros_ref.at[pl.ds(0, size)], dst, zsem))

                    row = row + jnp.where(hit, size, 0)
                return carry

            lax.fori_loop(0, N_EXPERTS, per_expert, 0)

            def per_tail_block(b, carry):
                action(pltpu.make_async_copy(zeros_ref, xs_ref.at[pl.ds(b * MOE_BM, MOE_BM)], zsem))
                return carry

            lax.fori_loop(nv_ref[0], n_blocks + 2, per_tail_block, 0)

        @pl.when(step == 0)
        def _():
            zeros_ref[...] = jnp.zeros_like(zeros_ref)
            for_each_zero_chunk(lambda cp: cp.start())
            for_each_zero_chunk(lambda cp: cp.wait())

        @pl.when(step >= 2)
        def _():
            wait_tile(slot)

        lpos = lpos_ref[...]
        rows = lax.broadcasted_iota(I32, (cb, t), 0)
        onehot = jnp.where(rows == lpos[0:1, :], 1.0, 0.0)
        for k in range(1, TOP_K):
            onehot = onehot + jnp.where(rows == lpos[k:k + 1, :], 1.0, 0.0)
        buf[slot] = _dot(onehot.astype(BF16), h_ref[...])

        def move(lo, go, size):
            pltpu.make_async_copy(buf.at[slot, pl.ds(lo, size)], xs_ref.at[pl.ds(go, size)], sem.at[slot]).start()

        def fill(off, size):
            dst = xs_ref.at[pl.ds(pl.multiple_of(spare_row0 + slot * t + off, SUBLANES), size)]
            pltpu.make_async_copy(buf.at[slot, pl.ds(off, size)], dst, sem.at[slot]).start()

        _for_each_run_chunk(step, run8_ref, loff_ref, goff_ref, move, fill)

        @pl.when(step == n_tiles - 1)
        def _():
            wait_tile(slot)
            if n_tiles > 1:
                wait_tile(1 - slot)
    return kern


def _dispatch(tables, seg, starts, n_valid, h2, lpos, n_blocks):
    n, d = h2.shape
    t = MOVE_TILE
    return pl.pallas_call(
        _make_dispatch_kernel(n // t, n_blocks),
        grid_spec=pltpu.PrefetchScalarGridSpec(
            num_scalar_prefetch=6,
            grid=(n // t,),
            in_specs=[pl.BlockSpec((t, d), lambda i, *_: (i, 0)),
                      pl.BlockSpec((8, t), lambda i, *_: (0, i))],
            out_specs=pl.BlockSpec(memory_space=pl.ANY),
            scratch_shapes=[pltpu.VMEM((2, COMPACT_ROWS, d), F32), pltpu.VMEM((MOE_BM, d), F32),
                            pltpu.SemaphoreType.DMA((2,)), pltpu.SemaphoreType.DMA(())],
        ),
        out_shape=jax.ShapeDtypeStruct(((n_blocks + 2) * MOE_BM, d), F32),
        compiler_params=_cparams("arbitrary"),
    )(*tables, seg, starts, n_valid, h2, lpos)


def _make_expert_kernel(layer):
    def kern(be_ref, bs_ref, nv_ref, ord_ref, nxt_ref, x_ref, wgu_hbm, bgu_ref, wdn_hbm, bdn_ref, y_ref,
             wgu_f, wdn_f, wgu_s, wdn_s, sem):
        _expert_body(layer, be_ref, nv_ref, ord_ref, nxt_ref, x_ref, wgu_hbm, bgu_ref, wdn_hbm, bdn_ref, y_ref,
                     wgu_f, wdn_f, wgu_s, wdn_s, sem)
    return kern


def _expert_body(layer, be_ref, nv_ref, ord_ref, nxt_ref, x_ref, wgu_hbm, bgu_ref, wdn_hbm, bdn_ref, y_ref,
                 wgu_f, wdn_f, wgu_s, wdn_s, sem):
    i = pl.program_id(0)
    e = be_ref[i]
    prev = be_ref[jnp.maximum(i - 1, 0)]
    new_expert = jnp.logical_or(i == 0, e != prev)
    slot = lax.rem(ord_ref[i], 2)

    def weight_copies(expert, s):
        return (pltpu.make_async_copy(wgu_hbm.at[layer, expert], wgu_f.at[s], sem.at[0, s]),
                pltpu.make_async_copy(wdn_hbm.at[layer, expert], wdn_f.at[s], sem.at[1, s]))

    @pl.when(i == 0)
    def _():
        for cp in weight_copies(e, slot):
            cp.start()

    @pl.when(new_expert)
    def _():
        for cp in weight_copies(e, slot):
            cp.wait()
        nxt = nxt_ref[i]

        @pl.when(nxt >= 0)
        def _():
            for cp in weight_copies(nxt, 1 - slot):
                cp.start(priority=1)

        wgu_s[...] = wgu_f[slot].astype(BF16)
        wdn_s[...] = wdn_f[slot].astype(BF16)

    @pl.when(i < nv_ref[0])
    def _():
        x = x_ref[...].astype(BF16)
        gu = _dot(x, wgu_s[...]) + bgu_ref[0, 0]
        gate = jnp.minimum(gu[:, :D_FF], SWIGLU_LIMIT)
        up = jnp.clip(gu[:, D_FF:], -SWIGLU_LIMIT, SWIGLU_LIMIT)
        hdn = (up + 1.0) * (gate * jax.nn.sigmoid(SWIGLU_ALPHA * gate))
        y_ref[...] = _dot(hdn.astype(BF16), wdn_s[...]) + bdn_ref[0, 0]

    @pl.when(i >= nv_ref[0])
    def _():
        y_ref[...] = jnp.zeros_like(y_ref)


def _experts(blk_e, blk_src, n_valid, blk_ord, blk_next, xs, w_gu, b_gu, w_dn, b_dn, layer, n_blocks):
    d = xs.shape[1]
    depth = w_gu.shape[0]

    def bias(width):
        return pl.BlockSpec((1, 1, 1, width), lambda i, be, *_: (layer, be[i], 0, 0))

    return pl.pallas_call(
        _make_expert_kernel(layer),
        grid_spec=pltpu.PrefetchScalarGridSpec(
            num_scalar_prefetch=5,
            grid=(n_blocks,),
            in_specs=[pl.BlockSpec((MOE_BM, d), lambda i, be, bs, *_: (bs[i], 0)),
                      pl.BlockSpec(memory_space=pl.ANY), bias(2 * D_FF),
                      pl.BlockSpec(memory_space=pl.ANY), bias(d)],
            out_specs=pl.BlockSpec((MOE_BM, d), lambda i, *_: (i, 0)),
            scratch_shapes=[pltpu.VMEM((2, d, 2 * D_FF), F32), pltpu.VMEM((2, D_FF, d), F32),
                            pltpu.VMEM((d, 2 * D_FF), BF16), pltpu.VMEM((D_FF, d), BF16),
                            pltpu.SemaphoreType.DMA((2, 2))],
        ),
        out_shape=jax.ShapeDtypeStruct((n_blocks * MOE_BM, d), F32),
        compiler_params=_cparams("arbitrary"),
    )(blk_e, blk_src, n_valid, blk_ord, blk_next, xs, w_gu, b_gu.reshape(depth, N_EXPERTS, 1, -1), w_dn,
      b_dn.reshape(depth, N_EXPERTS, 1, -1))


def _make_combine_kernel(n_tiles, ctx_tiles, final):
    t = MOVE_TILE
    cb = COMPACT_ROWS

    def kern(run8_ref, loff_ref, goff_ref, ys_ref, x1_ref, lpt_ref, gt_ref, g2_ref, fg_ref, *rest):
        o_refs = rest[:2] if final else rest[:1]
        ybuf, spare, sem = rest[len(o_refs):]
        step = pl.program_id(0)
        slot = lax.rem(step, 2)

        def fetch(tile, s):
            def move(lo, go, size):
                pltpu.make_async_copy(ys_ref.at[pl.ds(go, size)], ybuf.at[s, pl.ds(lo, size)], sem.at[s]).start()

            def fill(off, size):
                pltpu.make_async_copy(ys_ref.at[pl.ds(0, size)], spare.at[s, pl.ds(off, size)], sem.at[s]).start()

            _for_each_run_chunk(tile, run8_ref, loff_ref, goff_ref, move, fill)

        @pl.when(step == 0)
        def _():
            ybuf[...] = jnp.zeros_like(ybuf)
            fetch(0, 0)

        @pl.when(step + 1 < n_tiles)
        def _():
            fetch(step + 1, 1 - slot)

        pltpu.make_async_copy(ys_ref.at[pl.ds(0, cb)], ybuf.at[slot], sem.at[slot]).wait()

        lpt = lpt_ref[...].astype(F32).T.astype(I32)
        gt = gt_ref[...].T
        cols = lax.broadcasted_iota(I32, (t, cb), 1)
        g = jnp.where(cols == lpt[:, 0:1], gt[:, 0:1], 0.0)
        for k in range(1, TOP_K):
            g = g + jnp.where(cols == lpt[:, k:k + 1], gt[:, k:k + 1], 0.0)
        g_hi, g_lo = _split_bf16(g)
        yb = ybuf[slot].astype(BF16)
        y = _dot(g_hi, yb) + _dot(g_lo, yb)
        x2 = x1_ref[...] + g2_ref[0] * y
        if not final:
            o_refs[0][...] = x2
        else:
            x2 = x2 * lax.rsqrt(jnp.mean(x2 * x2, axis=-1, keepdims=True) + EPS) * fg_ref[...]

            @pl.when(step < ctx_tiles)
            def _():
                o_refs[0][...] = x2

            @pl.when(step >= ctx_tiles)
            def _():
                o_refs[1][...] = x2
    return kern


def _combine(tables, ys, x1, lpos_t, gates_t, mod, final_g, final, n_ctx, dec_seq):
    n, d = x1.shape
    t = MOVE_TILE
    ctx_tiles = n_ctx // t
    grp = functools.partial(_group_of_tile, tile=t, n_ctx=n_ctx, dec_seq=dec_seq)
    if final:
        out_specs = [pl.BlockSpec((t, d), lambda i, *_: (jnp.minimum(i, ctx_tiles - 1), 0)),
                     pl.BlockSpec((t, d), lambda i, *_: (jnp.maximum(i - ctx_tiles, 0), 0))]
        out_shape = [jax.ShapeDtypeStruct((n_ctx, d), F32), jax.ShapeDtypeStruct((n - n_ctx, d), F32)]
    else:
        out_specs = pl.BlockSpec((t, d), lambda i, *_: (i, 0))
        out_shape = jax.ShapeDtypeStruct((n, d), F32)
    return pl.pallas_call(
        _make_combine_kernel(n // t, ctx_tiles, final),
        grid_spec=pltpu.PrefetchScalarGridSpec(
            num_scalar_prefetch=3,
            grid=(n // t,),
            in_specs=[pl.BlockSpec(memory_space=pl.ANY),
                      pl.BlockSpec((t, d), lambda i, *_: (i, 0)),
                      pl.BlockSpec((8, t), lambda i, *_: (0, i)),
                      pl.BlockSpec((8, t), lambda i, *_: (0, i)),
                      pl.BlockSpec((1, 1, d), lambda i, *_: (grp(i) * 6 + MOD_G2, 0, 0)),
                      pl.BlockSpec((1, d), lambda i, *_: (0, 0))],
            out_specs=out_specs,
            scratch_shapes=[pltpu.VMEM((2, COMPACT_ROWS, d), F32), pltpu.VMEM((2, t, d), F32),
                            pltpu.SemaphoreType.DMA((2,))],
        ),
        out_shape=out_shape,
        compiler_params=_cparams("arbitrary"),
    )(*tables, ys, x1, lpos_t, gates_t, mod, final_g)


def _blockdiag_pairs(s):
    z = jnp.zeros_like(s[..., 0, :, :])
    def pair(a, b):
        return jnp.concatenate([jnp.concatenate([a, z], axis=-1), jnp.concatenate([z, b], axis=-1)], axis=-2)
    return jnp.stack([pair(s[..., 0, :, :], s[..., 1, :, :]), pair(s[..., 2, :, :], s[..., 3, :, :])], axis=-3)


def _diag_blocks(st):
    h = HEAD_DIM
    blocks = [st[:, :, c, j * h:(j + 1) * h, j * h:(j + 1) * h] for c in range(2) for j in range(2)]
    return jnp.stack(blocks, axis=2)


def kernel(x_prompt, x_sample, cache_k, cache_v, state_ret, c, c_ctx, w_mod, b_mod, norm1_g, norm2_g, w_in,
           q_norm_g, k_norm_g, conv_w, conv_b, conv_ln_g, conv_ln_b, ret_decay_logit, ret_gn_g, w_out,
           router_w, router_b, moe_w_gu, moe_b_gu, moe_w_dn, moe_b_dn, final_g):
    batch, seq, d = x_prompt.shape
    dec_batch, dec_seq, _ = x_sample.shape
    depth = w_mod.shape[0]
    past = cache_k.shape[2]
    n_ctx = batch * seq
    n_lat = dec_batch * dec_seq
    n = n_ctx + n_lat
    assert d == D_MODEL and dec_batch == 2
    assert n_ctx % dec_seq == 0 and dec_seq % TOK_TILE == 0 and seq % ATTN_QB == 0 and dec_seq % ATTN_QB == 0
    assert MOVE_TILE == MOE_BM and n % MOVE_TILE == 0

    x_parts = (x_prompt.reshape(n_ctx, d), x_sample.reshape(n_lat, d))
    mods = _modulation(jnp.concatenate([c_ctx[None, :], c], axis=0), w_mod, b_mod)
    rope = _rope_tables(dec_seq)

    head_of_col = jnp.arange(QK_W) // HEAD_DIM
    hsum = (head_of_col[:, None] == jnp.arange(LANES)[None, :]).astype(BF16)
    hbc = hsum.T
    lane_head = jnp.arange(LANES) // HEAD_DIM
    gmat = ((lane_head[:, None] == lane_head[None, :]).astype(F32) / HEAD_DIM).astype(BF16)
    tt = jnp.arange(MOVE_TILE)
    upper = (tt[:, None] < tt[None, :]).astype(BF16)
    ee = jnp.arange(N_EXPERTS)
    ltri = (ee[None, :] < ee[:, None]).astype(F32)

    n_tiles = n // MOVE_TILE
    max_rows = n * TOP_K + n_tiles * N_EXPERTS * (SUBLANES - 1) + N_EXPERTS * (MOE_BM - 1)
    n_blocks = -(-max_rows // MOE_BM)
    cache_k2 = cache_k.reshape(dec_batch, depth, past, KV_W)
    cache_v2 = cache_v.reshape(dec_batch, depth, past, KV_W)

    ks_out, vs_out, ss_out = [], [], []
    for l in range(depth):
        mod = mods[l]
        gqk = jnp.concatenate([jnp.tile(q_norm_g[l], N_Q_HEADS), jnp.tile(k_norm_g[l], N_KV_HEADS)])[None, :]
        p, kn, vv = _inproj(x_parts, mod, norm1_g[l][None, :], w_in[l].astype(BF16), gqk, hsum, hbc, rope,
                            n_ctx, dec_seq)
        ks_out.append(kn[:n_ctx].reshape(batch, seq, N_KV_HEADS, HEAD_DIM))
        vs_out.append(vv[:n_ctx].reshape(batch, seq, N_KV_HEADS, HEAD_DIM))

        attn = jnp.zeros((n, ATTN_W), BF16)
        attn = _attention(p, attn, batch, seq, 0, None, l)
        attn = _attention(p, attn, dec_batch, dec_seq, n_ctx, (cache_k2, cache_v2), l)

        cw, cb = conv_w[l], conv_b[l][None, :]
        clg, clb = conv_ln_g[l][None, :], conv_ln_b[l][None, :]
        conv = jnp.zeros((n, CONV_CH), BF16)
        conv = _conv(p, conv, batch, seq, 0, cw, cb, clg, clb)
        conv = _conv(p, conv, dec_batch, dec_seq, n_ctx, cw, cb, clg, clb)

        log_g = jax.nn.log_sigmoid(ret_decay_logit[l].astype(F32))
        gn = ret_gn_g[l][None, :]
        ret = jnp.zeros((n, RET_W), BF16)
        ret, st_ctx = _retention(p, ret, batch, seq, 0, log_g, gn, gmat, None)
        ret, _ = _retention(p, ret, dec_batch, dec_seq, n_ctx, log_g, gn, gmat,
                            _blockdiag_pairs(state_ret[:, l].astype(F32)))
        ss_out.append(_diag_blocks(st_ctx))

        x1, h2, top_e, top_g = _outproj(attn, conv, ret, x_parts, mod, norm2_g[l][None, :], w_out[l].astype(BF16),
                                        router_w[l].T, router_b[l][:, None], n_ctx, dec_seq)
        lpos, run8, loff, goff, seg = _route(top_e, upper, ltri)
        tables = [tb[:, :, 0].reshape(-1) for tb in (run8, loff, goff)]
        seg = seg[:, 0].astype(I32)
        padded = (seg + MOE_BM - 1) // MOE_BM * MOE_BM
        pad_end = jnp.cumsum(padded)
        n_valid = (pad_end[-1] // MOE_BM).reshape(1)
        blk_src = jnp.minimum(jnp.arange(n_blocks, dtype=I32), n_valid - 1)
        blk_e = jnp.minimum(jnp.sum((pad_end[None, :] <= (blk_src * MOE_BM)[:, None]).astype(I32), axis=1),
                            N_EXPERTS - 1)
        owns = padded > 0
        ord_e = jnp.cumsum(owns.astype(I32)) - 1
        later = jnp.where(owns[None, :] & (ee[None, :] > ee[:, None]), ee[None, :], N_EXPERTS)
        next_e = jnp.min(later, axis=1)
        next_e = jnp.where(next_e == N_EXPERTS, -1, next_e).astype(I32)

        xs = _dispatch(tables, seg, pad_end - padded, n_valid, h2, lpos, n_blocks)
        of_blk = blk_e[:, None] == ee[None, :]
        blk_ord = jnp.sum(jnp.where(of_blk, ord_e[None, :], 0), axis=1)
        blk_next = jnp.sum(jnp.where(of_blk, next_e[None, :], 0), axis=1)
        ys = _experts(blk_e, blk_src, n_valid, blk_ord, blk_next, xs,
                      moe_w_gu, moe_b_gu, moe_w_dn, moe_b_dn, l, n_blocks)
        out = _combine(tables, ys, x1, lpos, top_g, mod, final_g[None, :], l == depth - 1, n_ctx, dec_seq)
        x_parts = (out,)

    y_prompt = out[0].reshape(batch, seq, d)
    y_sample = out[1].reshape(dec_batch, dec_seq, d)
    return (y_prompt, y_sample, jnp.stack(ks_out, axis=1), jnp.stack(vs_out, axis=1),
            jnp.stack(ss_out, axis=1))
```

```python
import functools

import numpy as np
import jax
import jax.numpy as jnp
from jax import lax
from jax.experimental import pallas as pl
from jax.experimental.pallas import tpu as pltpu

F32 = jnp.float32
BF16 = jnp.bfloat16
I32 = jnp.int32

D_MODEL = 1024
GRID_W = 64
HEAD_DIM = 64
N_Q_HEADS = 8
N_KV_HEADS = 2
ATTN_W = N_Q_HEADS * HEAD_DIM
KV_W = N_KV_HEADS * HEAD_DIM
QK_W = ATTN_W + KV_W
CONV_CH = 256
CONV_K = 31
CONV_PAD = CONV_K // 2
CONV_HALO = 16
N_RET_HEADS = 4
RET_W = 256
OFF_V = QK_W
OFF_CONV = OFF_V + KV_W
OFF_RET = OFF_CONV + 2 * CONV_CH
IN_COLS = OFF_RET + 4 * RET_W
ROPE_HALF = HEAD_DIM // 2
ROPE_THETA = 10000.0
N_EXPERTS = 32
TOP_K = 4
D_FF = D_MODEL
SWIGLU_LIMIT = 7.0
SWIGLU_ALPHA = 1.702
EPS = 1e-6
LN_EPS = 1e-5

LANES = 128
TOK_TILE = 512
ATTN_QB = 256
RET_QB = 256
CONV_ROWS = 64
MOE_BM = 256
MOVE_TILE = 256
ROUTE_TILES = 4
SUBLANES = 8
COMPACT_ROWS = 1280
assert COMPACT_ROWS >= MOVE_TILE * TOP_K + N_EXPERTS * (SUBLANES - 1) and COMPACT_ROWS % MOE_BM == 0
CHUNK_SIZES = tuple(MOVE_TILE >> s for s in range(6))
RARE_SIZES = 3
VMEM_LIMIT = 56 * 1024 * 1024

MOD_SH1, MOD_SC1, MOD_G1, MOD_SH2, MOD_SC2, MOD_G2 = range(6)


def _cparams(*sem):
    return pltpu.CompilerParams(dimension_semantics=sem, vmem_limit_bytes=VMEM_LIMIT)


def _dot(a, b, **kw):
    return jnp.dot(a, b, preferred_element_type=F32, **kw)


def _dot_nt(a, b, **kw):
    return lax.dot_general(a, b, (((1,), (1,)), ((), ())), preferred_element_type=F32, **kw)


def _dot_tn(a, b, **kw):
    return lax.dot_general(a, b, (((0,), (0,)), ((), ())), preferred_element_type=F32, **kw)


def _split_bf16(x):
    hi = x.astype(BF16)
    lo = (x - hi.astype(F32)).astype(BF16)
    return hi, lo


MOD_TN = 1536


def _mod_kernel(ct_ref, w_ref, b_ref, o_ref):
    s = ct_ref[...]
    s = s * jax.nn.sigmoid(s)
    w = w_ref[0]
    rows = [jnp.sum(w * s[:, r:r + 1], axis=0, keepdims=True) for r in range(3)]
    rows.append(jnp.zeros((5, w.shape[1]), F32))
    o_ref[0] = jnp.concatenate(rows, axis=0) + b_ref[0]


def _modulation(cvec3, w_mod, b_mod):
    depth, d, cols = w_mod.shape
    ct = jnp.zeros((d, 8), F32).at[:, :3].set(cvec3.T)
    out = pl.pallas_call(
        _mod_kernel,
        grid=(depth, cols // MOD_TN),
        in_specs=[
            pl.BlockSpec((d, 8), lambda l, j: (0, 0)),
            pl.BlockSpec((1, d, MOD_TN), lambda l, j: (l, 0, j)),
            pl.BlockSpec((1, 1, MOD_TN), lambda l, j: (l, 0, j)),
        ],
        out_specs=pl.BlockSpec((1, 8, MOD_TN), lambda l, j: (l, 0, j)),
        out_shape=jax.ShapeDtypeStruct((depth, 8, cols), F32),
        compiler_params=_cparams("arbitrary", "arbitrary"),
    )(ct, w_mod, b_mod.reshape(depth, 1, cols))
    return out[:, :3].reshape(depth, 3 * 6, 1, d)


def _token_specs(parts, tile, n_ctx):
    d = parts[0].shape[1]
    if len(parts) == 1:
        return [pl.BlockSpec((tile, d), lambda i, *_: (i, 0))]
    ctx_tiles = n_ctx // tile
    return [pl.BlockSpec((tile, d), lambda i, *_: (jnp.minimum(i, ctx_tiles - 1), 0)),
            pl.BlockSpec((tile, d), lambda i, *_: (jnp.maximum(i - ctx_tiles, 0), 0))]


def _token_tile(refs, ctx_tiles):
    if len(refs) == 1:
        return refs[0][...]
    return jnp.where(pl.program_id(0) < ctx_tiles, refs[0][...], refs[1][...])


def _inproj_kernel(n_src, ctx_tiles, *refs):
    x_refs = refs[:n_src]
    (sh_ref, sc_ref, g_ref, w_ref, gqk_ref, hsum_ref, hbc_ref, cos_ref, sa_ref, sb_ref,
     p_ref, k_ref, v_ref) = refs[n_src:]
    x = _token_tile(x_refs, ctx_tiles)
    inv = lax.rsqrt(jnp.mean(x * x, axis=-1, keepdims=True) + EPS)
    h = (x * inv * g_ref[...]) * (1.0 + sc_ref[0]) + sh_ref[0]
    acc = _dot(h.astype(BF16), w_ref[...])
    qk = acc[:, :QK_W]
    ss = _dot((qk * qk).astype(BF16), hsum_ref[...])
    r = lax.rsqrt(ss * (1.0 / HEAD_DIM) + EPS)
    r_hi, r_lo = _split_bf16(r)
    rb = _dot(r_hi, hbc_ref[...]) + _dot(r_lo, hbc_ref[...])
    qkn = qk * rb * gqk_ref[...]
    @pl.when(pl.program_id(0) < ctx_tiles)
    def _():
        k_ref[...] = qkn[:, ATTN_W:QK_W]
        v_ref[...] = acc[:, OFF_V:OFF_CONV]

    cos = cos_ref[...]
    sa = sa_ref[...]
    sb = sb_ref[...]
    for j in range(QK_W // LANES):
        c = qkn[:, j * LANES:(j + 1) * LANES]
        up = pltpu.roll(c, LANES - ROPE_HALF // 2, 1)
        dn = pltpu.roll(c, ROPE_HALF // 2, 1)
        p_ref[:, j * LANES:(j + 1) * LANES] = (c * cos + up * sa + dn * sb).astype(BF16)
    p_ref[:, QK_W:] = acc[:, QK_W:].astype(BF16)


def _rope_tables(dec_seq):
    f32 = np.float32
    rows = dec_seq // GRID_W
    row = np.repeat(np.arange(rows), GRID_W).astype(f32)
    col = np.tile(np.arange(GRID_W), rows).astype(f32)
    inv = (f32(1.0) / (f32(ROPE_THETA) ** (np.arange(0, ROPE_HALF, 2).astype(f32) / f32(ROPE_HALF)))).astype(f32)
    ar = row[:, None] * inv[None, :]
    ac = col[:, None] * inv[None, :]
    cos = np.concatenate([np.cos(ar), np.cos(ar), np.cos(ac), np.cos(ac)], axis=-1)
    sin = np.concatenate([np.sin(ar), np.sin(ar), np.sin(ac), np.sin(ac)], axis=-1)
    first = (np.arange(HEAD_DIM) % ROPE_HALF) < ROPE_HALF // 2
    sa = np.where(first[None, :], -sin, 0.0)
    sb = np.where(first[None, :], 0.0, sin)
    def table(t, ident):
        t = np.concatenate([np.full((TOK_TILE, HEAD_DIM), ident, f32), t.astype(f32)], axis=0)
        return jnp.asarray(np.tile(t, (1, LANES // HEAD_DIM)))
    return table(cos, 1.0), table(sa, 0.0), table(sb, 0.0)


def _group_of_tile(i, tile, n_ctx, dec_seq):
    tok = i * tile
    return jnp.where(tok < n_ctx, 0, 1 + (tok - n_ctx) // dec_seq)


def _inproj(x_parts, mod, norm_g, w_in_bf, gqk, hsum, hbc, rope, n_ctx, dec_seq):
    n = sum(part.shape[0] for part in x_parts)
    d = x_parts[0].shape[1]
    t = TOK_TILE
    grp = functools.partial(_group_of_tile, tile=t, n_ctx=n_ctx, dec_seq=dec_seq)

    def mod_spec(which):
        return pl.BlockSpec((1, 1, d), lambda i: (grp(i) * 6 + which, 0, 0))

    def rope_idx(i):
        tok = i * t
        return (jnp.where(tok < n_ctx, 0, 1 + ((tok - n_ctx) % dec_seq) // t), 0)

    rope_spec = pl.BlockSpec((t, LANES), rope_idx)
    const = lambda shape: pl.BlockSpec(shape, lambda i: (0,) * len(shape))
    return pl.pallas_call(
        functools.partial(_inproj_kernel, len(x_parts), n_ctx // t),
        grid=(n // t,),
        in_specs=_token_specs(x_parts, t, n_ctx) + [
            mod_spec(MOD_SH1), mod_spec(MOD_SC1),
            const((1, d)),
            const((d, IN_COLS)),
            const((1, QK_W)), const((QK_W, LANES)), const((LANES, QK_W)),
            rope_spec, rope_spec, rope_spec,
        ],
        out_specs=[
            pl.BlockSpec((t, IN_COLS), lambda i: (i, 0)),
            pl.BlockSpec((t, KV_W), lambda i: (jnp.minimum(i, n_ctx // t - 1), 0)),
            pl.BlockSpec((t, KV_W), lambda i: (jnp.minimum(i, n_ctx // t - 1), 0)),
        ],
        out_shape=[
            jax.ShapeDtypeStruct((n, IN_COLS), BF16),
            jax.ShapeDtypeStruct((n_ctx, KV_W), F32),
            jax.ShapeDtypeStruct((n_ctx, KV_W), F32),
        ],
        compiler_params=_cparams("arbitrary"),
    )(*x_parts, mod, mod, norm_g, w_in_bf, gqk, hsum, hbc, *rope)


def _head_halves(x, hkv, low):
    r = pltpu.roll(x, HEAD_DIM, 1)
    rep = jnp.where(low, x, r) if hkv == 0 else jnp.where(low, r, x)
    return jnp.where(low, rep, 0.0).astype(BF16), jnp.where(low, 0.0, rep).astype(BF16)


def _make_attn_kernel(has_ctx):
    def kern(*refs):
        if has_ctx:
            q_ref, k_ref, v_ref, ck_ref, cv_ref, o_ref = refs
        else:
            q_ref, k_ref, v_ref, o_ref = refs
        low = lax.broadcasted_iota(I32, (1, LANES), 1) < HEAD_DIM
        k = k_ref[...].astype(F32)
        v = v_ref[...].astype(F32)
        if has_ctx:
            ck = ck_ref[0, 0]
            cv = cv_ref[0, 0]
        scale = HEAD_DIM ** -0.5
        for hkv in range(N_KV_HEADS):
            kh = _head_halves(k, hkv, low)
            vh = _head_halves(v, hkv, low)
            if has_ctx:
                ckh = _head_halves(ck, hkv, low)
                cvh = _head_halves(cv, hkv, low)
            for c in range(2):
                col = hkv * 2 * LANES + c * LANES
                qc = q_ref[:, col:col + LANES] * scale
                o_c = None
                for j in range(2):
                    s = _dot_nt(qc, kh[j])
                    m = jnp.max(s, axis=-1, keepdims=True)
                    if has_ctx:
                        s2 = _dot_nt(qc, ckh[j])
                        m = jnp.maximum(m, jnp.max(s2, axis=-1, keepdims=True))
                    p = jnp.exp(s - m)
                    l = jnp.sum(p, axis=-1, keepdims=True)
                    o = _dot(p.astype(BF16), vh[j])
                    if has_ctx:
                        p2 = jnp.exp(s2 - m)
                        l = l + jnp.sum(p2, axis=-1, keepdims=True)
                        o = o + _dot(p2.astype(BF16), cvh[j])
                    o = o / l
                    o_c = o if o_c is None else o_c + o
                o_ref[:, col:col + LANES] = o_c.astype(BF16)
    return kern


def _attention(p, n_seq, seq_len, row0, ctx_kv, layer):
    qb = ATTN_QB
    nq = seq_len // qb
    qrow0 = row0 // qb
    srow0 = row0 // seq_len
    has_ctx = ctx_kv is not None
    in_specs = [
        pl.BlockSpec((qb, ATTN_W), lambda b, i: (qrow0 + b * nq + i, 0)),
        pl.BlockSpec((seq_len, KV_W), lambda b, i: (srow0 + b, ATTN_W // KV_W)),
        pl.BlockSpec((seq_len, KV_W), lambda b, i: (srow0 + b, OFF_V // KV_W)),
    ]
    args = [p, p, p]
    if has_ctx:
        ck, cv = ctx_kv
        past = ck.shape[2]
        spec = pl.BlockSpec((1, 1, past, KV_W), lambda b, i: (b, layer, 0, 0))
        in_specs += [spec, spec]
        args += [ck, cv]
    return pl.pallas_call(
        _make_attn_kernel(has_ctx),
        grid=(n_seq, nq),
        in_specs=in_specs,
        out_specs=pl.BlockSpec((qb, ATTN_W), lambda b, i: (b * nq + i, 0)),
        out_shape=jax.ShapeDtypeStruct((n_seq * seq_len, ATTN_W), BF16),
        compiler_params=_cparams("arbitrary", "arbitrary"),
    )(*args)


def _make_conv_kernel(seq_len):
    shifted_rows = seq_len + 2 * CONV_HALO - SUBLANES

    def kern(a_ref, g_ref, w_ref, b_ref, lng_ref, lnb_ref, o_ref, zp_ref, zs_ref):
        zero = jnp.zeros((CONV_HALO, CONV_CH), F32)
        zp_ref[0:CONV_HALO, :] = zero
        zp_ref[CONV_HALO + seq_len:2 * CONV_HALO + seq_len, :] = zero
        zp_ref[CONV_HALO:CONV_HALO + seq_len, :] = (
            a_ref[...].astype(F32) * jax.nn.sigmoid(g_ref[...].astype(F32)))
        for s in range(1, SUBLANES):
            zs_ref[s] = zp_ref[s:s + shifted_rows, :]
        w = w_ref[...]
        bias = b_ref[...]
        for c in range(seq_len // CONV_ROWS):
            base = c * CONV_ROWS + CONV_HALO - CONV_PAD
            acc = jnp.zeros((CONV_ROWS, CONV_CH), F32) + bias
            for j in range(CONV_K):
                shift = (base + j) % SUBLANES
                row = base + j - shift
                if shift == 0:
                    tap = zp_ref[row:row + CONV_ROWS, :]
                else:
                    tap = zs_ref[shift, row:row + CONV_ROWS, :]
                acc = acc + tap * w[j:j + 1, :]
            mu = jnp.mean(acc, axis=-1, keepdims=True)
            dlt = acc - mu
            var = jnp.mean(dlt * dlt, axis=-1, keepdims=True)
            y = dlt * lax.rsqrt(var + LN_EPS) * lng_ref[...] + lnb_ref[...]
            o_ref[c * CONV_ROWS:(c + 1) * CONV_ROWS, :] = (y * jax.nn.sigmoid(y)).astype(BF16)
    return kern


def _conv(p, n_seq, seq_len, row0, w, b, lng, lnb):
    srow0 = row0 // seq_len
    const = lambda shape: pl.BlockSpec(shape, lambda s: (0,) * len(shape))
    return pl.pallas_call(
        _make_conv_kernel(seq_len),
        grid=(n_seq,),
        in_specs=[
            pl.BlockSpec((seq_len, CONV_CH), lambda s: (srow0 + s, OFF_CONV // CONV_CH)),
            pl.BlockSpec((seq_len, CONV_CH), lambda s: (srow0 + s, OFF_CONV // CONV_CH + 1)),
            const((CONV_K, CONV_CH)), const((1, CONV_CH)), const((1, CONV_CH)), const((1, CONV_CH)),
        ],
        out_specs=pl.BlockSpec((seq_len, CONV_CH), lambda s: (s, 0)),
        out_shape=jax.ShapeDtypeStruct((n_seq * seq_len, CONV_CH), BF16),
        scratch_shapes=[pltpu.VMEM((seq_len + 2 * CONV_HALO, CONV_CH), F32),
                        pltpu.VMEM((SUBLANES, seq_len + 2 * CONV_HALO - SUBLANES, CONV_CH), F32)],
        compiler_params=_cparams("arbitrary"),
    )(p, p, w, b, lng, lnb)


def _make_ret_kernel(seq_len, has_init):
    qb = min(seq_len, RET_QB)
    nq = seq_len // qb
    scale = HEAD_DIM ** -0.5

    def kern(*refs):
        if has_init:
            lg_ref, q_ref, k_ref, v_ref, g_ref, gn_ref, gm_ref, r0_ref, o_ref, st_ref = refs
        else:
            lg_ref, q_ref, k_ref, v_ref, g_ref, gn_ref, gm_ref, o_ref, st_ref = refs
        low = lax.broadcasted_iota(I32, (1, LANES), 1) < HEAD_DIM
        pos = lax.broadcasted_iota(I32, (seq_len, 1), 0).astype(F32)
        qpos = lax.broadcasted_iota(I32, (qb, 1), 0).astype(F32)
        kpos = lax.broadcasted_iota(I32, (1, seq_len), 1).astype(F32)
        gm = gm_ref[...]
        for c in range(2):
            cs = slice(c * LANES, (c + 1) * LANES)
            lgf = jnp.where(low, lg_ref[0, 2 * c], lg_ref[0, 2 * c + 1])
            lgb = jnp.where(low, lg_ref[1, 2 * c], lg_ref[1, 2 * c + 1])
            qc = q_ref[:, cs]
            kc = k_ref[:, cs]
            vc = v_ref[:, cs]
            kf = kc.astype(F32) * scale
            zeta_f = jnp.exp((seq_len - 1.0 - pos) * lgf)
            zeta_b = jnp.exp(pos * lgb)
            for d, zeta, lgd in ((0, zeta_f, lgf), (1, zeta_b, lgb)):
                st = _dot_tn((kf * zeta).astype(BF16), vc)
                if has_init:
                    st = st + r0_ref[0, d, c] * jnp.exp(seq_len * lgd)
                st_ref[0, d, c] = st
            y_blocks = [None] * nq
            for j in range(2):
                sel = low if j == 0 else jnp.logical_not(low)
                kh = jnp.where(sel, kc, jnp.zeros_like(kc))
                vh = jnp.where(sel, vc, jnp.zeros_like(vc))
                lf = lg_ref[0, 2 * c + j]
                lb = lg_ref[1, 2 * c + j]
                for i in range(nq):
                    s = _dot_nt(qc[i * qb:(i + 1) * qb], kh)
                    diff = (qpos + float(i * qb)) - kpos
                    dec = jnp.exp(jnp.where(diff >= 0, diff * lf, -diff * lb))
                    dec = dec * jnp.where(diff == 0, 2.0 * scale, scale)
                    y = _dot((s * dec).astype(BF16), vh)
                    y_blocks[i] = y if y_blocks[i] is None else y_blocks[i] + y
            y = jnp.concatenate(y_blocks, axis=0) if nq > 1 else y_blocks[0]
            if has_init:
                xi_f = jnp.exp((pos + 1.0) * lgf)
                xi_b = jnp.exp((seq_len - pos) * lgb)
                y = y + _dot(qc, r0_ref[0, 0, c].astype(BF16)) * xi_f
                y = y + _dot(qc, r0_ref[0, 1, c].astype(BF16)) * xi_b
            y_hi, y_lo = _split_bf16(y)
            mu = _dot(y_hi, gm) + _dot(y_lo, gm)
            dlt = y - mu
            var = _dot((dlt * dlt).astype(BF16), gm)
            yn = dlt * lax.rsqrt(var + LN_EPS) * gn_ref[:, cs]
            gate = g_ref[:, cs].astype(F32)
            o_ref[:, cs] = (gate * jax.nn.sigmoid(gate) * yn).astype(BF16)
    return kern


def _retention(p, n_seq, seq_len, row0, log_g, gn_g, gmat, r0):
    srow0 = row0 // seq_len
    has_init = r0 is not None
    col = OFF_RET // RET_W
    in_specs = [pl.BlockSpec(memory_space=pltpu.SMEM)]
    in_specs += [pl.BlockSpec((seq_len, RET_W), functools.partial(lambda s, j: (srow0 + s, col + j), j=j))
                 for j in range(4)]
    in_specs += [pl.BlockSpec((1, RET_W), lambda s: (0, 0)), pl.BlockSpec((LANES, LANES), lambda s: (0, 0))]
    args = [log_g, p, p, p, p, gn_g, gmat]
    st_spec = pl.BlockSpec((1, 2, 2, LANES, LANES), lambda s: (s, 0, 0, 0, 0))
    if has_init:
        in_specs.append(st_spec)
        args.append(r0)
    return pl.pallas_call(
        _make_ret_kernel(seq_len, has_init),
        grid=(n_seq,),
        in_specs=in_specs,
        out_specs=[pl.BlockSpec((seq_len, RET_W), lambda s: (s, 0)), st_spec],
        out_shape=[jax.ShapeDtypeStruct((n_seq * seq_len, RET_W), BF16),
                   jax.ShapeDtypeStruct((n_seq, 2, 2, LANES, LANES), F32)],
        compiler_params=_cparams("arbitrary"),
    )(*args)


def _outproj_kernel(n_src, ctx_tiles, *refs):
    x_refs = refs[:n_src]
    a_refs, c_refs, r_refs = refs[n_src:n_src + 2], refs[n_src + 2:n_src + 4], refs[n_src + 4:n_src + 6]
    (g1_ref, sc_ref, sh_ref, n2_ref, wo_ref, rwt_ref, rb_ref,
     x1_ref, h2_ref, te_ref, tg_ref) = refs[n_src + 6:]
    mixed = (_dot(_token_tile(a_refs, ctx_tiles), wo_ref[0:ATTN_W, :])
             + _dot(_token_tile(c_refs, ctx_tiles), wo_ref[ATTN_W:ATTN_W + CONV_CH, :])
             + _dot(_token_tile(r_refs, ctx_tiles), wo_ref[ATTN_W + CONV_CH:, :]))
    x1 = _token_tile(x_refs, ctx_tiles) + g1_ref[0] * mixed
    x1_ref[...] = x1
    inv = lax.rsqrt(jnp.mean(x1 * x1, axis=-1, keepdims=True) + EPS)
    h2 = (x1 * inv * n2_ref[...]) * (1.0 + sc_ref[0]) + sh_ref[0]
    h2_ref[...] = h2.astype(BF16)
    w_hi, w_lo = _split_bf16(rwt_ref[...])
    h_hi, h_lo = _split_bf16(h2)
    logits = _dot_nt(w_hi, h_hi) + _dot_nt(w_hi, h_lo) + _dot_nt(w_lo, h_hi) + rb_ref[...]
    t = logits.shape[1]
    eidx = lax.broadcasted_iota(I32, (N_EXPERTS, t), 0).astype(F32)
    vals = logits
    tops, idxs = [], []
    for _ in range(TOP_K):
        m = jnp.max(vals, axis=0, keepdims=True)
        idx = jnp.min(jnp.where(vals == m, eidx, float(N_EXPERTS)), axis=0, keepdims=True)
        tops.append(m)
        idxs.append(idx)
        vals = jnp.where(eidx == idx, -jnp.inf, vals)
    es = [jnp.exp(m - tops[0]) for m in tops]
    tot = es[0] + es[1] + es[2] + es[3]
    te_ref[...] = jnp.concatenate(idxs + [jnp.zeros((8 - TOP_K, t), F32)], axis=0).astype(I32)
    tg_ref[...] = jnp.concatenate([e / tot for e in es] + [jnp.zeros((8 - TOP_K, t), F32)], axis=0)


def _outproj(attn, conv, ret, x_parts, mod, norm_g, w_out_bf, rwt, rb, n_ctx, dec_seq):
    n = attn[0].shape[0] + attn[1].shape[0]
    d = x_parts[0].shape[1]
    t = TOK_TILE
    grp = functools.partial(_group_of_tile, tile=t, n_ctx=n_ctx, dec_seq=dec_seq)

    def mod_spec(which):
        return pl.BlockSpec((1, 1, d), lambda i: (grp(i) * 6 + which, 0, 0))

    const = lambda shape: pl.BlockSpec(shape, lambda i: (0,) * len(shape))
    row = lambda w: pl.BlockSpec((t, w), lambda i: (i, 0))
    lane = lambda: pl.BlockSpec((8, t), lambda i: (0, i))
    return pl.pallas_call(
        functools.partial(_outproj_kernel, len(x_parts), n_ctx // t),
        grid=(n // t,),
        in_specs=_token_specs(x_parts, t, n_ctx) + _token_specs(attn, t, n_ctx)
        + _token_specs(conv, t, n_ctx) + _token_specs(ret, t, n_ctx) + [
            mod_spec(MOD_G1), mod_spec(MOD_SC2), mod_spec(MOD_SH2),
            const((1, d)), const((d, d)), const((N_EXPERTS, d)), const((N_EXPERTS, 1))],
        out_specs=[row(d), row(d), lane(), lane()],
        out_shape=[jax.ShapeDtypeStruct((n, d), F32), jax.ShapeDtypeStruct((n, d), BF16),
                   jax.ShapeDtypeStruct((8, n), I32), jax.ShapeDtypeStruct((8, n), F32)],
        compiler_params=_cparams("arbitrary"),
    )(*x_parts, *attn, *conv, *ret, mod, mod, mod, norm_g, w_out_bf, rwt, rb)


def _round_up(x, m):
    return jnp.floor((x + (m - 1.0)) * (1.0 / m)) * m


def _route_kernel(te_ref, u_ref, ltri_ref, lpos_ref, run8_ref, loff_ref, goff_ref, seg_ref, run_ref, start_ref):
    ph = pl.program_id(0)
    i = pl.program_id(1)
    t = MOVE_TILE
    eidx = lax.broadcasted_iota(I32, (N_EXPERTS, t), 0)

    @pl.when(jnp.logical_and(ph == 0, i == 0))
    def _():
        run_ref[...] = jnp.zeros_like(run_ref)

    @pl.when(jnp.logical_and(ph == 1, i == 0))
    def _():
        seg = run_ref[...]
        seg_ref[...] = seg
        start_ref[...] = _dot(ltri_ref[...], _round_up(seg, MOE_BM), precision=lax.Precision.HIGHEST)
        run_ref[...] = jnp.zeros_like(run_ref)

    for s in range(ROUTE_TILES):
        te = te_ref[:, s * t:(s + 1) * t]
        hits = [eidx == te[k:k + 1, :] for k in range(TOP_K)]
        onehot = sum(h.astype(F32) for h in hits)
        run8 = _round_up(jnp.sum(onehot, axis=1, keepdims=True), SUBLANES)

        @pl.when(ph == 0)
        def _(run8=run8):
            run_ref[...] = run_ref[...] + run8

        @pl.when(ph == 1)
        def _(s=s, hits=hits, onehot=onehot, run8=run8):
            before = _dot(onehot.astype(BF16), u_ref[...])
            run8_b = jnp.broadcast_to(run8, (N_EXPERTS, LANES))
            loff = _dot(ltri_ref[...], run8_b, precision=lax.Precision.HIGHEST)
            base = before + loff[:, 0:1]
            rows = [jnp.sum(jnp.where(h, base, 0.0), axis=0, keepdims=True) for h in hits]
            rows.append(jnp.zeros((8 - TOP_K, t), F32))
            lpos_ref[:, s * t:(s + 1) * t] = jnp.concatenate(rows, axis=0).astype(I32)
            run8_ref[s] = run8_b.astype(I32)
            loff_ref[s] = loff.astype(I32)
            goff_ref[s] = (start_ref[...] + run_ref[...]).astype(I32)
            run_ref[...] = run_ref[...] + run8


def _route(top_e, upper, ltri):
    n = top_e.shape[1]
    t = MOVE_TILE
    nt = n // t
    g = ROUTE_TILES
    assert nt % g == 0
    table = pl.BlockSpec((g, N_EXPERTS, LANES), lambda ph, i: (i * ph, 0, 0))
    table_shape = jax.ShapeDtypeStruct((nt, N_EXPERTS, LANES), I32)
    return pl.pallas_call(
        _route_kernel,
        grid=(2, nt // g),
        in_specs=[pl.BlockSpec((8, g * t), lambda ph, i: (0, i)),
                  pl.BlockSpec((t, t), lambda ph, i: (0, 0)),
                  pl.BlockSpec((N_EXPERTS, N_EXPERTS), lambda ph, i: (0, 0))],
        out_specs=[pl.BlockSpec((8, g * t), lambda ph, i: (0, i * ph)), table, table, table,
                   pl.BlockSpec((N_EXPERTS, LANES), lambda ph, i: (0, 0))],
        out_shape=[jax.ShapeDtypeStruct((8, n), I32), table_shape, table_shape, table_shape,
                   jax.ShapeDtypeStruct((N_EXPERTS, LANES), F32)],
        scratch_shapes=[pltpu.VMEM((N_EXPERTS, LANES), F32), pltpu.VMEM((N_EXPERTS, LANES), F32)],
        compiler_params=_cparams("arbitrary", "arbitrary"),
    )(top_e, upper, ltri)


def _for_each_run_chunk(tile, run8_ref, loff_ref, goff_ref, move, fill):
    def pieces(rows, lo, go, sizes):
        for size in sizes:
            hit = (rows & size) != 0

            @pl.when(hit)
            def _(lo=lo, go=go, size=size):
                move(pl.multiple_of(lo, SUBLANES), pl.multiple_of(go, SUBLANES), size)

            lo = lo + jnp.where(hit, size, 0)
            go = go + jnp.where(hit, size, 0)
        return lo, go

    def per_expert(e, total):
        idx = tile * N_EXPERTS + e
        rows = run8_ref[idx]
        lo, go = pieces(rows, loff_ref[idx], goff_ref[idx], CHUNK_SIZES[RARE_SIZES:])

        @pl.when(rows >= CHUNK_SIZES[RARE_SIZES - 1])
        def _():
            pieces(rows, lo, go, CHUNK_SIZES[:RARE_SIZES])

        return total + rows

    total = lax.fori_loop(0, N_EXPERTS, per_expert, jnp.int32(0))
    spare = COMPACT_ROWS - total
    off = jnp.int32(0)
    for size in CHUNK_SIZES:
        hit = (spare & size) != 0

        @pl.when(hit)
        def _(off=off, size=size):
            fill(pl.multiple_of(off, SUBLANES), size)

        off = off + jnp.where(hit, size, 0)


def _make_dispatch_kernel(n_tiles, n_blocks):
    t = MOVE_TILE
    cb = COMPACT_ROWS
    spare_row0 = n_blocks * MOE_BM

    def kern(run8_ref, loff_ref, goff_ref, seg_ref, start_ref, nv_ref, h_ref, lpos_ref, xs_ref,
             buf, zeros_ref, sem, zsem):
        step = pl.program_id(0)
        slot = lax.rem(step, 2)

        def wait_tile(s):
            pltpu.make_async_copy(buf.at[s], xs_ref.at[pl.ds(0, cb)], sem.at[s]).wait()

        def for_each_zero_chunk(action):
            def per_expert(e, carry):
                seg = seg_ref[e]
                padlen = (-seg) & (MOE_BM - 1)
                row = start_ref[e] + seg
                for size in CHUNK_SIZES[1:]:
                    hit = (padlen & size) != 0

                    @pl.when(hit)
                    def _(row=row, size=size):
                        dst = xs_ref.at[pl.ds(pl.multiple_of(row, SUBLANES), size)]
                        action(pltpu.make_async_copy(zeros_ref.at[pl.ds(0, size)], dst, zsem))

                    row = row + jnp.where(hit, size, 0)
                return carry

            lax.fori_loop(0, N_EXPERTS, per_expert, 0)

            def per_tail_block(b, carry):
                action(pltpu.make_async_copy(zeros_ref, xs_ref.at[pl.ds(b * MOE_BM, MOE_BM)], zsem))
                return carry

            lax.fori_loop(nv_ref[0], n_blocks + 2, per_tail_block, 0)

        @pl.when(step == 0)
        def _():
            zeros_ref[...] = jnp.zeros_like(zeros_ref)
            for_each_zero_chunk(lambda cp: cp.start())
            for_each_zero_chunk(lambda cp: cp.wait())

        @pl.when(step >= 2)
        def _():
            wait_tile(slot)

        lpos = lpos_ref[...]
        rows = lax.broadcasted_iota(I32, (cb, t), 0)
        onehot = jnp.where(rows == lpos[0:1, :], 1.0, 0.0)
        for k in range(1, TOP_K):
            onehot = onehot + jnp.where(rows == lpos[k:k + 1, :], 1.0, 0.0)
        buf[slot] = _dot(onehot.astype(BF16), h_ref[...])

        def move(lo, go, size):
            pltpu.make_async_copy(buf.at[slot, pl.ds(lo, size)], xs_ref.at[pl.ds(go, size)], sem.at[slot]).start()

        def fill(off, size):
            dst = xs_ref.at[pl.ds(pl.multiple_of(spare_row0 + slot * t + off, SUBLANES), size)]
            pltpu.make_async_copy(buf.at[slot, pl.ds(off, size)], dst, sem.at[slot]).start()

        _for_each_run_chunk(step, run8_ref, loff_ref, goff_ref, move, fill)

        @pl.when(step == n_tiles - 1)
        def _():
            wait_tile(slot)
            if n_tiles > 1:
                wait_tile(1 - slot)
    return kern


def _dispatch(tables, seg, starts, n_valid, h2, lpos, n_blocks):
    n, d = h2.shape
    t = MOVE_TILE
    return pl.pallas_call(
        _make_dispatch_kernel(n // t, n_blocks),
        grid_spec=pltpu.PrefetchScalarGridSpec(
            num_scalar_prefetch=6,
            grid=(n // t,),
            in_specs=[pl.BlockSpec((t, d), lambda i, *_: (i, 0)),
                      pl.BlockSpec((8, t), lambda i, *_: (0, i))],
            out_specs=pl.BlockSpec(memory_space=pl.ANY),
            scratch_shapes=[pltpu.VMEM((2, COMPACT_ROWS, d), F32), pltpu.VMEM((MOE_BM, d), F32),
                            pltpu.SemaphoreType.DMA((2,)), pltpu.SemaphoreType.DMA(())],
        ),
        out_shape=jax.ShapeDtypeStruct(((n_blocks + 2) * MOE_BM, d), F32),
        compiler_params=_cparams("arbitrary"),
    )(*tables, seg, starts, n_valid, h2, lpos)


def _make_expert_kernel(layer):
    def kern(be_ref, bs_ref, nv_ref, ord_ref, nxt_ref, x_ref, wgu_hbm, bgu_ref, wdn_hbm, bdn_ref, y_ref,
             wgu_f, wdn_f, wgu_s, wdn_s, sem):
        _expert_body(layer, be_ref, nv_ref, ord_ref, nxt_ref, x_ref, wgu_hbm, bgu_ref, wdn_hbm, bdn_ref, y_ref,
                     wgu_f, wdn_f, wgu_s, wdn_s, sem)
    return kern


def _expert_body(layer, be_ref, nv_ref, ord_ref, nxt_ref, x_ref, wgu_hbm, bgu_ref, wdn_hbm, bdn_ref, y_ref,
                 wgu_f, wdn_f, wgu_s, wdn_s, sem):
    i = pl.program_id(0)
    e = be_ref[i]
    prev = be_ref[jnp.maximum(i - 1, 0)]
    new_expert = jnp.logical_or(i == 0, e != prev)
    slot = lax.rem(ord_ref[i], 2)

    def weight_copies(expert, s):
        return (pltpu.make_async_copy(wgu_hbm.at[layer, expert], wgu_f.at[s], sem.at[0, s]),
                pltpu.make_async_copy(wdn_hbm.at[layer, expert], wdn_f.at[s], sem.at[1, s]))

    @pl.when(i == 0)
    def _():
        for cp in weight_copies(e, slot):
            cp.start()

    @pl.when(new_expert)
    def _():
        for cp in weight_copies(e, slot):
            cp.wait()
        nxt = nxt_ref[i]

        @pl.when(nxt >= 0)
        def _():
            for cp in weight_copies(nxt, 1 - slot):
                cp.start(priority=1)

        wgu_s[...] = wgu_f[slot].astype(BF16)
        wdn_s[...] = wdn_f[slot].astype(BF16)

    @pl.when(i < nv_ref[0])
    def _():
        x = x_ref[...].astype(BF16)
        gu = _dot(x, wgu_s[...]) + bgu_ref[0, 0]
        gate = jnp.minimum(gu[:, :D_FF], SWIGLU_LIMIT)
        up = jnp.clip(gu[:, D_FF:], -SWIGLU_LIMIT, SWIGLU_LIMIT)
        hdn = (up + 1.0) * (gate * jax.nn.sigmoid(SWIGLU_ALPHA * gate))
        y_ref[...] = _dot(hdn.astype(BF16), wdn_s[...]) + bdn_ref[0, 0]

    @pl.when(i >= nv_ref[0])
    def _():
        y_ref[...] = jnp.zeros_like(y_ref)


def _experts(blk_e, blk_src, n_valid, blk_ord, blk_next, xs, w_gu, b_gu, w_dn, b_dn, layer, n_blocks):
    d = xs.shape[1]
    depth = w_gu.shape[0]

    def bias(width):
        return pl.BlockSpec((1, 1, 1, width), lambda i, be, *_: (layer, be[i], 0, 0))

    return pl.pallas_call(
        _make_expert_kernel(layer),
        grid_spec=pltpu.PrefetchScalarGridSpec(
            num_scalar_prefetch=5,
            grid=(n_blocks,),
            in_specs=[pl.BlockSpec((MOE_BM, d), lambda i, be, bs, *_: (bs[i], 0)),
                      pl.BlockSpec(memory_space=pl.ANY), bias(2 * D_FF),
                      pl.BlockSpec(memory_space=pl.ANY), bias(d)],
            out_specs=pl.BlockSpec((MOE_BM, d), lambda i, *_: (i, 0)),
            scratch_shapes=[pltpu.VMEM((2, d, 2 * D_FF), F32), pltpu.VMEM((2, D_FF, d), F32),
                            pltpu.VMEM((d, 2 * D_FF), BF16), pltpu.VMEM((D_FF, d), BF16),
                            pltpu.SemaphoreType.DMA((2, 2))],
        ),
        out_shape=jax.ShapeDtypeStruct((n_blocks * MOE_BM, d), F32),
        compiler_params=_cparams("arbitrary"),
    )(blk_e, blk_src, n_valid, blk_ord, blk_next, xs, w_gu, b_gu.reshape(depth, N_EXPERTS, 1, -1), w_dn,
      b_dn.reshape(depth, N_EXPERTS, 1, -1))


def _make_combine_kernel(n_tiles, ctx_tiles, final):
    t = MOVE_TILE
    cb = COMPACT_ROWS

    def kern(run8_ref, loff_ref, goff_ref, ys_ref, x1_ref, lpt_ref, gt_ref, g2_ref, fg_ref, *rest):
        o_refs = rest[:2] if final else rest[:1]
        ybuf, spare, sem = rest[len(o_refs):]
        step = pl.program_id(0)
        slot = lax.rem(step, 2)

        def fetch(tile, s):
            def move(lo, go, size):
                pltpu.make_async_copy(ys_ref.at[pl.ds(go, size)], ybuf.at[s, pl.ds(lo, size)], sem.at[s]).start()

            def fill(off, size):
                pltpu.make_async_copy(ys_ref.at[pl.ds(0, size)], spare.at[s, pl.ds(off, size)], sem.at[s]).start()

            _for_each_run_chunk(tile, run8_ref, loff_ref, goff_ref, move, fill)

        @pl.when(step == 0)
        def _():
            ybuf[...] = jnp.zeros_like(ybuf)
            fetch(0, 0)

        @pl.when(step + 1 < n_tiles)
        def _():
            fetch(step + 1, 1 - slot)

        pltpu.make_async_copy(ys_ref.at[pl.ds(0, cb)], ybuf.at[slot], sem.at[slot]).wait()

        lpos = lpt_ref[...]
        gates = gt_ref[...]
        rows = lax.broadcasted_iota(I32, (cb, t), 0)
        placed = jnp.where(rows == lpos[0:1, :], gates[0:1, :], 0.0)
        for k in range(1, TOP_K):
            placed = placed + jnp.where(rows == lpos[k:k + 1, :], gates[k:k + 1, :], 0.0)
        row_gate = jnp.sum(placed, axis=1, keepdims=True)
        yb = (ybuf[slot] * row_gate).astype(BF16)
        lpt = lpos.astype(F32).T.astype(I32)
        cols = lax.broadcasted_iota(I32, (t, cb), 1)
        unsort = jnp.where(cols == lpt[:, 0:1], 1.0, 0.0)
        for k in range(1, TOP_K):
            unsort = unsort + jnp.where(cols == lpt[:, k:k + 1], 1.0, 0.0)
        y = _dot(unsort.astype(BF16), yb)
        x2 = x1_ref[...] + g2_ref[0] * y
        if not final:
            o_refs[0][...] = x2
        else:
            x2 = x2 * lax.rsqrt(jnp.mean(x2 * x2, axis=-1, keepdims=True) + EPS) * fg_ref[...]

            @pl.when(step < ctx_tiles)
            def _():
                o_refs[0][...] = x2

            @pl.when(step >= ctx_tiles)
            def _():
                o_refs[1][...] = x2
    return kern


def _combine(tables, ys, x1, lpos_t, gates_t, mod, final_g, final, n_ctx, dec_seq):
    n, d = x1.shape
    t = MOVE_TILE
    ctx_tiles = n_ctx // t
    grp = functools.partial(_group_of_tile, tile=t, n_ctx=n_ctx, dec_seq=dec_seq)
    if final:
        out_specs = [pl.BlockSpec((t, d), lambda i, *_: (jnp.minimum(i, ctx_tiles - 1), 0)),
                     pl.BlockSpec((t, d), lambda i, *_: (jnp.maximum(i - ctx_tiles, 0), 0))]
        out_shape = [jax.ShapeDtypeStruct((n_ctx, d), F32), jax.ShapeDtypeStruct((n - n_ctx, d), F32)]
    else:
        out_specs = pl.BlockSpec((t, d), lambda i, *_: (i, 0))
        out_shape = jax.ShapeDtypeStruct((n, d), F32)
    return pl.pallas_call(
        _make_combine_kernel(n // t, ctx_tiles, final),
        grid_spec=pltpu.PrefetchScalarGridSpec(
            num_scalar_prefetch=3,
            grid=(n // t,),
            in_specs=[pl.BlockSpec(memory_space=pl.ANY),
                      pl.BlockSpec((t, d), lambda i, *_: (i, 0)),
                      pl.BlockSpec((8, t), lambda i, *_: (0, i)),
                      pl.BlockSpec((8, t), lambda i, *_: (0, i)),
                      pl.BlockSpec((1, 1, d), lambda i, *_: (grp(i) * 6 + MOD_G2, 0, 0)),
                      pl.BlockSpec((1, d), lambda i, *_: (0, 0))],
            out_specs=out_specs,
            scratch_shapes=[pltpu.VMEM((2, COMPACT_ROWS, d), F32), pltpu.VMEM((2, t, d), F32),
                            pltpu.SemaphoreType.DMA((2,))],
        ),
        out_shape=out_shape,
        compiler_params=_cparams("arbitrary"),
    )(*tables, ys, x1, lpos_t, gates_t, mod, final_g)


def _blockdiag_pairs(s):
    z = jnp.zeros_like(s[..., 0, :, :])
    def pair(a, b):
        return jnp.concatenate([jnp.concatenate([a, z], axis=-1), jnp.concatenate([z, b], axis=-1)], axis=-2)
    return jnp.stack([pair(s[..., 0, :, :], s[..., 1, :, :]), pair(s[..., 2, :, :], s[..., 3, :, :])], axis=-3)


def _diag_blocks(st):
    h = HEAD_DIM
    blocks = [st[:, :, c, j * h:(j + 1) * h, j * h:(j + 1) * h] for c in range(2) for j in range(2)]
    return jnp.stack(blocks, axis=2)


def kernel(x_prompt, x_sample, cache_k, cache_v, state_ret, c, c_ctx, w_mod, b_mod, norm1_g, norm2_g, w_in,
           q_norm_g, k_norm_g, conv_w, conv_b, conv_ln_g, conv_ln_b, ret_decay_logit, ret_gn_g, w_out,
           router_w, router_b, moe_w_gu, moe_b_gu, moe_w_dn, moe_b_dn, final_g):
    batch, seq, d = x_prompt.shape
    dec_batch, dec_seq, _ = x_sample.shape
    depth = w_mod.shape[0]
    past = cache_k.shape[2]
    n_ctx = batch * seq
    n_lat = dec_batch * dec_seq
    n = n_ctx + n_lat
    assert d == D_MODEL and dec_batch == 2
    assert n_ctx % dec_seq == 0 and dec_seq % TOK_TILE == 0 and seq % ATTN_QB == 0 and dec_seq % ATTN_QB == 0
    assert MOVE_TILE == MOE_BM and n % MOVE_TILE == 0

    x_parts = (x_prompt.reshape(n_ctx, d), x_sample.reshape(n_lat, d))
    mods = _modulation(jnp.concatenate([c_ctx[None, :], c], axis=0), w_mod, b_mod)
    rope = _rope_tables(dec_seq)

    head_of_col = np.arange(QK_W) // HEAD_DIM
    hsum_np = (head_of_col[:, None] == np.arange(LANES)[None, :]).astype(np.float32)
    hsum = jnp.asarray(hsum_np, BF16)
    hbc = jnp.asarray(hsum_np.T, BF16)
    lane_head = np.arange(LANES) // HEAD_DIM
    gmat = jnp.asarray((lane_head[:, None] == lane_head[None, :]).astype(np.float32) / HEAD_DIM, BF16)
    tt = np.arange(MOVE_TILE)
    upper = jnp.asarray((tt[:, None] < tt[None, :]).astype(np.float32), BF16)
    ee = jnp.arange(N_EXPERTS)
    ltri = jnp.asarray((np.arange(N_EXPERTS)[None, :] < np.arange(N_EXPERTS)[:, None]).astype(np.float32))

    n_tiles = n // MOVE_TILE
    max_rows = n * TOP_K + n_tiles * N_EXPERTS * (SUBLANES - 1) + N_EXPERTS * (MOE_BM - 1)
    n_blocks = -(-max_rows // MOE_BM)
    cache_k2 = cache_k.reshape(dec_batch, depth, past, KV_W)
    cache_v2 = cache_v.reshape(dec_batch, depth, past, KV_W)

    ks_out, vs_out, ss_out = [], [], []
    for l in range(depth):
        mod = mods[l]
        gqk = jnp.concatenate([jnp.tile(q_norm_g[l], N_Q_HEADS), jnp.tile(k_norm_g[l], N_KV_HEADS)])[None, :]
        p, kn, vv = _inproj(x_parts, mod, norm1_g[l][None, :], w_in[l].astype(BF16), gqk, hsum, hbc, rope,
                            n_ctx, dec_seq)
        ks_out.append(kn.reshape(batch, seq, N_KV_HEADS, HEAD_DIM))
        vs_out.append(vv.reshape(batch, seq, N_KV_HEADS, HEAD_DIM))

        attn = (_attention(p, batch, seq, 0, None, l),
                _attention(p, dec_batch, dec_seq, n_ctx, (cache_k2, cache_v2), l))

        cw, cb = conv_w[l], conv_b[l][None, :]
        clg, clb = conv_ln_g[l][None, :], conv_ln_b[l][None, :]
        conv = (_conv(p, batch, seq, 0, cw, cb, clg, clb),
                _conv(p, dec_batch, dec_seq, n_ctx, cw, cb, clg, clb))

        log_g = jax.nn.log_sigmoid(ret_decay_logit[l].astype(F32))
        gn = ret_gn_g[l][None, :]
        ret_ctx, st_ctx = _retention(p, batch, seq, 0, log_g, gn, gmat, None)
        ret_lat, _ = _retention(p, dec_batch, dec_seq, n_ctx, log_g, gn, gmat,
                                _blockdiag_pairs(state_ret[:, l].astype(F32)))
        ret = (ret_ctx, ret_lat)
        ss_out.append(_diag_blocks(st_ctx))

        x1, h2, top_e, top_g = _outproj(attn, conv, ret, x_parts, mod, norm2_g[l][None, :], w_out[l].astype(BF16),
                                        router_w[l].T, router_b[l][:, None], n_ctx, dec_seq)
        lpos, run8, loff, goff, seg = _route(top_e, upper, ltri)
        tables = [tb[:, :, 0].reshape(-1) for tb in (run8, loff, goff)]
        seg = seg[:, 0].astype(I32)
        padded = (seg + MOE_BM - 1) // MOE_BM * MOE_BM
        pad_end = jnp.cumsum(padded)
        n_valid = (pad_end[-1] // MOE_BM).reshape(1)
        blk_src = jnp.minimum(jnp.arange(n_blocks, dtype=I32), n_valid - 1)
        blk_e = jnp.minimum(jnp.sum((pad_end[None, :] <= (blk_src * MOE_BM)[:, None]).astype(I32), axis=1),
                            N_EXPERTS - 1)
        owns = padded > 0
        ord_e = jnp.cumsum(owns.astype(I32)) - 1
        later = jnp.where(owns[None, :] & (ee[None, :] > ee[:, None]), ee[None, :], N_EXPERTS)
        next_e = jnp.min(later, axis=1)
        next_e = jnp.where(next_e == N_EXPERTS, -1, next_e).astype(I32)

        xs = _dispatch(tables, seg, pad_end - padded, n_valid, h2, lpos, n_blocks)
        of_blk = blk_e[:, None] == ee[None, :]
        blk_ord = jnp.sum(jnp.where(of_blk, ord_e[None, :], 0), axis=1)
        blk_next = jnp.sum(jnp.where(of_blk, next_e[None, :], 0), axis=1)
        ys = _experts(blk_e, blk_src, n_valid, blk_ord, blk_next, xs,
                      moe_w_gu, moe_b_gu, moe_w_dn, moe_b_dn, l, n_blocks)
        out = _combine(tables, ys, x1, lpos, top_g, mod, final_g[None, :], l == depth - 1, n_ctx, dec_seq)
        x_parts = (out,)

    y_prompt = out[0].reshape(batch, seq, d)
    y_sample = out[1].reshape(dec_batch, dec_seq, d)
    return (y_prompt, y_sample, jnp.stack(ks_out, axis=1), jnp.stack(vs_out, axis=1),
            jnp.stack(ss_out, axis=1))
```

```python
import functools

import numpy as np
import jax
import jax.numpy as jnp
from jax import lax
from jax.experimental import pallas as pl
from jax.experimental.pallas import tpu as pltpu

F32 = jnp.float32
BF16 = jnp.bfloat16
I32 = jnp.int32

D_MODEL = 1024
GRID_W = 64
HEAD_DIM = 64
N_Q_HEADS = 8
N_KV_HEADS = 2
ATTN_W = N_Q_HEADS * HEAD_DIM
KV_W = N_KV_HEADS * HEAD_DIM
QK_W = ATTN_W + KV_W
CONV_CH = 256
CONV_K = 31
CONV_PAD = CONV_K // 2
CONV_HALO = 16
N_RET_HEADS = 4
RET_W = 256
OFF_V = QK_W
OFF_CONV = OFF_V + KV_W
OFF_RET = OFF_CONV + 2 * CONV_CH
IN_COLS = OFF_RET + 4 * RET_W
ROPE_HALF = HEAD_DIM // 2
ROPE_THETA = 10000.0
N_EXPERTS = 32
TOP_K = 4
D_FF = D_MODEL
SWIGLU_LIMIT = 7.0
SWIGLU_ALPHA = 1.702
EPS = 1e-6
LN_EPS = 1e-5

LANES = 128
TOK_TILE = 1024
ATTN_QB = 256
RET_QB = 256
CONV_ROWS = 64
MOE_BM = 256
MOVE_TILE = 256
ROUTE_TILES = 4
SUBLANES = 8
COMPACT_ROWS = 1280
assert COMPACT_ROWS >= MOVE_TILE * TOP_K + N_EXPERTS * (SUBLANES - 1) and COMPACT_ROWS % MOE_BM == 0
CHUNK_SIZES = tuple(MOVE_TILE >> s for s in range(6))
RARE_SIZES = 3
VMEM_LIMIT = 56 * 1024 * 1024

MOD_SH1, MOD_SC1, MOD_G1, MOD_SH2, MOD_SC2, MOD_G2 = range(6)


def _cparams(*sem):
    return pltpu.CompilerParams(dimension_semantics=sem, vmem_limit_bytes=VMEM_LIMIT)


def _dot(a, b, **kw):
    return jnp.dot(a, b, preferred_element_type=F32, **kw)


def _dot_nt(a, b, **kw):
    return lax.dot_general(a, b, (((1,), (1,)), ((), ())), preferred_element_type=F32, **kw)


def _dot_tn(a, b, **kw):
    return lax.dot_general(a, b, (((0,), (0,)), ((), ())), preferred_element_type=F32, **kw)


def _split_bf16(x):
    hi = x.astype(BF16)
    lo = (x - hi.astype(F32)).astype(BF16)
    return hi, lo


MOD_TN = 1536


def _mod_kernel(ct_ref, w_ref, b_ref, o_ref):
    s = ct_ref[...]
    s = s * jax.nn.sigmoid(s)
    w = w_ref[0]
    rows = [jnp.sum(w * s[:, r:r + 1], axis=0, keepdims=True) for r in range(3)]
    rows.append(jnp.zeros((5, w.shape[1]), F32))
    o_ref[0] = jnp.concatenate(rows, axis=0) + b_ref[0]


def _modulation(cvec3, w_mod, b_mod):
    depth, d, cols = w_mod.shape
    ct = jnp.zeros((d, 8), F32).at[:, :3].set(cvec3.T)
    out = pl.pallas_call(
        _mod_kernel,
        grid=(depth, cols // MOD_TN),
        in_specs=[
            pl.BlockSpec((d, 8), lambda l, j: (0, 0)),
            pl.BlockSpec((1, d, MOD_TN), lambda l, j: (l, 0, j)),
            pl.BlockSpec((1, 1, MOD_TN), lambda l, j: (l, 0, j)),
        ],
        out_specs=pl.BlockSpec((1, 8, MOD_TN), lambda l, j: (l, 0, j)),
        out_shape=jax.ShapeDtypeStruct((depth, 8, cols), F32),
        compiler_params=_cparams("arbitrary", "arbitrary"),
    )(ct, w_mod, b_mod.reshape(depth, 1, cols))
    return out[:, :3].reshape(depth, 3 * 6, 1, d)


def _token_specs(parts, tile, n_ctx):
    d = parts[0].shape[1]
    if len(parts) == 1:
        return [pl.BlockSpec((tile, d), lambda i, *_: (i, 0))]
    ctx_tiles = n_ctx // tile
    return [pl.BlockSpec((tile, d), lambda i, *_: (jnp.minimum(i, ctx_tiles - 1), 0)),
            pl.BlockSpec((tile, d), lambda i, *_: (jnp.maximum(i - ctx_tiles, 0), 0))]


def _token_tile(refs, ctx_tiles):
    if len(refs) == 1:
        return refs[0][...]
    return jnp.where(pl.program_id(0) < ctx_tiles, refs[0][...], refs[1][...])


def _inproj_kernel(n_src, ctx_tiles, *refs):
    x_refs = refs[:n_src]
    (sh_ref, sc_ref, g_ref, w_ref, gqk_ref, hsum_ref, hbc_ref, cos_ref, sa_ref, sb_ref,
     p_ref, k_ref, v_ref) = refs[n_src:]
    x = _token_tile(x_refs, ctx_tiles)
    inv = lax.rsqrt(jnp.mean(x * x, axis=-1, keepdims=True) + EPS)
    h = (x * inv * g_ref[...]) * (1.0 + sc_ref[0]) + sh_ref[0]
    acc = _dot(h.astype(BF16), w_ref[...])
    qk = acc[:, :QK_W]
    ss = _dot((qk * qk).astype(BF16), hsum_ref[...])
    r = lax.rsqrt(ss * (1.0 / HEAD_DIM) + EPS)
    r_hi, r_lo = _split_bf16(r)
    rb = _dot(r_hi, hbc_ref[...]) + _dot(r_lo, hbc_ref[...])
    qkn = qk * rb * gqk_ref[...]
    cos = cos_ref[...]
    sa = sa_ref[...]
    sb = sb_ref[...]
    for j in range(QK_W // LANES):
        c = qkn[:, j * LANES:(j + 1) * LANES]
        up = pltpu.roll(c, LANES - ROPE_HALF // 2, 1)
        dn = pltpu.roll(c, ROPE_HALF // 2, 1)
        p_ref[:, j * LANES:(j + 1) * LANES] = (c * cos + up * sa + dn * sb).astype(BF16)
    p_ref[:, QK_W:] = acc[:, QK_W:].astype(BF16)

    @pl.when(pl.program_id(0) < ctx_tiles)
    def _():
        k_ref[...] = qkn[:, ATTN_W:QK_W]
        v_ref[...] = acc[:, OFF_V:OFF_CONV]


def _rope_tables(dec_seq):
    f32 = np.float32
    rows = dec_seq // GRID_W
    row = np.repeat(np.arange(rows), GRID_W).astype(f32)
    col = np.tile(np.arange(GRID_W), rows).astype(f32)
    inv = (f32(1.0) / (f32(ROPE_THETA) ** (np.arange(0, ROPE_HALF, 2).astype(f32) / f32(ROPE_HALF)))).astype(f32)
    ar = row[:, None] * inv[None, :]
    ac = col[:, None] * inv[None, :]
    cos = np.concatenate([np.cos(ar), np.cos(ar), np.cos(ac), np.cos(ac)], axis=-1)
    sin = np.concatenate([np.sin(ar), np.sin(ar), np.sin(ac), np.sin(ac)], axis=-1)
    first = (np.arange(HEAD_DIM) % ROPE_HALF) < ROPE_HALF // 2
    sa = np.where(first[None, :], -sin, 0.0)
    sb = np.where(first[None, :], 0.0, sin)
    def table(t, ident):
        t = np.concatenate([np.full((TOK_TILE, HEAD_DIM), ident, f32), t.astype(f32)], axis=0)
        return jnp.asarray(np.tile(t, (1, LANES // HEAD_DIM)))
    return table(cos, 1.0), table(sa, 0.0), table(sb, 0.0)


def _group_of_tile(i, tile, n_ctx, dec_seq):
    tok = i * tile
    return jnp.where(tok < n_ctx, 0, 1 + (tok - n_ctx) // dec_seq)


def _inproj(x_parts, mod, norm_g, w_in_bf, gqk, hsum, hbc, rope, n_ctx, dec_seq):
    n = sum(part.shape[0] for part in x_parts)
    d = x_parts[0].shape[1]
    t = TOK_TILE
    grp = functools.partial(_group_of_tile, tile=t, n_ctx=n_ctx, dec_seq=dec_seq)

    def mod_spec(which):
        return pl.BlockSpec((1, 1, d), lambda i: (grp(i) * 6 + which, 0, 0))

    def rope_idx(i):
        tok = i * t
        return (jnp.where(tok < n_ctx, 0, 1 + ((tok - n_ctx) % dec_seq) // t), 0)

    rope_spec = pl.BlockSpec((t, LANES), rope_idx)
    const = lambda shape: pl.BlockSpec(shape, lambda i: (0,) * len(shape))
    return pl.pallas_call(
        functools.partial(_inproj_kernel, len(x_parts), n_ctx // t),
        grid=(n // t,),
        in_specs=_token_specs(x_parts, t, n_ctx) + [
            mod_spec(MOD_SH1), mod_spec(MOD_SC1),
            const((1, d)),
            const((d, IN_COLS)),
            const((1, QK_W)), const((QK_W, LANES)), const((LANES, QK_W)),
            rope_spec, rope_spec, rope_spec,
        ],
        out_specs=[
            pl.BlockSpec((t, IN_COLS), lambda i: (i, 0)),
            pl.BlockSpec((t, KV_W), lambda i: (jnp.minimum(i, n_ctx // t - 1), 0)),
            pl.BlockSpec((t, KV_W), lambda i: (jnp.minimum(i, n_ctx // t - 1), 0)),
        ],
        out_shape=[
            jax.ShapeDtypeStruct((n, IN_COLS), BF16),
            jax.ShapeDtypeStruct((n_ctx, KV_W), F32),
            jax.ShapeDtypeStruct((n_ctx, KV_W), F32),
        ],
        compiler_params=_cparams("arbitrary"),
    )(*x_parts, mod, mod, norm_g, w_in_bf, gqk, hsum, hbc, *rope)


def _head_halves(x, hkv, low):
    r = pltpu.roll(x, HEAD_DIM, 1)
    rep = jnp.where(low, x, r) if hkv == 0 else jnp.where(low, r, x)
    return jnp.where(low, rep, 0.0).astype(BF16), jnp.where(low, 0.0, rep).astype(BF16)


def _make_attn_kernel(has_ctx):
    def kern(*refs):
        if has_ctx:
            q_ref, k_ref, v_ref, ck_ref, cv_ref, o_ref = refs
        else:
            q_ref, k_ref, v_ref, o_ref = refs
        low = lax.broadcasted_iota(I32, (1, LANES), 1) < HEAD_DIM
        k = k_ref[...].astype(F32)
        v = v_ref[...].astype(F32)
        if has_ctx:
            ck = ck_ref[0, 0]
            cv = cv_ref[0, 0]
        scale = HEAD_DIM ** -0.5
        for hkv in range(N_KV_HEADS):
            kh = _head_halves(k, hkv, low)
            vh = _head_halves(v, hkv, low)
            if has_ctx:
                ckh = _head_halves(ck, hkv, low)
                cvh = _head_halves(cv, hkv, low)
            for c in range(2):
                col = hkv * 2 * LANES + c * LANES
                qc = q_ref[:, col:col + LANES] * scale
                o_c = None
                for j in range(2):
                    s = _dot_nt(qc, kh[j])
                    m = jnp.max(s, axis=-1, keepdims=True)
                    if has_ctx:
                        s2 = _dot_nt(qc, ckh[j])
                        m = jnp.maximum(m, jnp.max(s2, axis=-1, keepdims=True))
                    p = jnp.exp(s - m)
                    l = jnp.sum(p, axis=-1, keepdims=True)
                    o = _dot(p.astype(BF16), vh[j])
                    if has_ctx:
                        p2 = jnp.exp(s2 - m)
                        l = l + jnp.sum(p2, axis=-1, keepdims=True)
                        o = o + _dot(p2.astype(BF16), cvh[j])
                    o = o / l
                    o_c = o if o_c is None else o_c + o
                o_ref[:, col:col + LANES] = o_c.astype(BF16)
    return kern


def _attention(p, n_seq, seq_len, row0, ctx_kv, layer):
    qb = ATTN_QB
    nq = seq_len // qb
    qrow0 = row0 // qb
    srow0 = row0 // seq_len
    has_ctx = ctx_kv is not None
    in_specs = [
        pl.BlockSpec((qb, ATTN_W), lambda b, i: (qrow0 + b * nq + i, 0)),
        pl.BlockSpec((seq_len, KV_W), lambda b, i: (srow0 + b, ATTN_W // KV_W)),
        pl.BlockSpec((seq_len, KV_W), lambda b, i: (srow0 + b, OFF_V // KV_W)),
    ]
    args = [p, p, p]
    if has_ctx:
        ck, cv = ctx_kv
        past = ck.shape[2]
        spec = pl.BlockSpec((1, 1, past, KV_W), lambda b, i: (b, layer, 0, 0))
        in_specs += [spec, spec]
        args += [ck, cv]
    return pl.pallas_call(
        _make_attn_kernel(has_ctx),
        grid=(n_seq, nq),
        in_specs=in_specs,
        out_specs=pl.BlockSpec((qb, ATTN_W), lambda b, i: (b * nq + i, 0)),
        out_shape=jax.ShapeDtypeStruct((n_seq * seq_len, ATTN_W), BF16),
        compiler_params=_cparams("arbitrary", "arbitrary"),
    )(*args)


def _make_conv_kernel(seq_len):
    shifted_rows = seq_len + 2 * CONV_HALO - SUBLANES

    def kern(a_ref, g_ref, w_ref, b_ref, lng_ref, lnb_ref, o_ref, zp_ref, zs_ref):
        zero = jnp.zeros((CONV_HALO, CONV_CH), F32)
        zp_ref[0:CONV_HALO, :] = zero
        zp_ref[CONV_HALO + seq_len:2 * CONV_HALO + seq_len, :] = zero
        zp_ref[CONV_HALO:CONV_HALO + seq_len, :] = (
            a_ref[...].astype(F32) * jax.nn.sigmoid(g_ref[...].astype(F32)))
        for s in range(1, SUBLANES):
            zs_ref[s] = zp_ref[s:s + shifted_rows, :]
        w = w_ref[...]
        bias = b_ref[...]
        for c in range(seq_len // CONV_ROWS):
            base = c * CONV_ROWS + CONV_HALO - CONV_PAD
            acc = jnp.zeros((CONV_ROWS, CONV_CH), F32) + bias
            for j in range(CONV_K):
                shift = (base + j) % SUBLANES
                row = base + j - shift
                if shift == 0:
                    tap = zp_ref[row:row + CONV_ROWS, :]
                else:
                    tap = zs_ref[shift, row:row + CONV_ROWS, :]
                acc = acc + tap * w[j:j + 1, :]
            mu = jnp.mean(acc, axis=-1, keepdims=True)
            dlt = acc - mu
            var = jnp.mean(dlt * dlt, axis=-1, keepdims=True)
            y = dlt * lax.rsqrt(var + LN_EPS) * lng_ref[...] + lnb_ref[...]
            o_ref[c * CONV_ROWS:(c + 1) * CONV_ROWS, :] = (y * jax.nn.sigmoid(y)).astype(BF16)
    return kern


def _conv(p, n_seq, seq_len, row0, w, b, lng, lnb):
    srow0 = row0 // seq_len
    const = lambda shape: pl.BlockSpec(shape, lambda s: (0,) * len(shape))
    return pl.pallas_call(
        _make_conv_kernel(seq_len),
        grid=(n_seq,),
        in_specs=[
            pl.BlockSpec((seq_len, CONV_CH), lambda s: (srow0 + s, OFF_CONV // CONV_CH)),
            pl.BlockSpec((seq_len, CONV_CH), lambda s: (srow0 + s, OFF_CONV // CONV_CH + 1)),
            const((CONV_K, CONV_CH)), const((1, CONV_CH)), const((1, CONV_CH)), const((1, CONV_CH)),
        ],
        out_specs=pl.BlockSpec((seq_len, CONV_CH), lambda s: (s, 0)),
        out_shape=jax.ShapeDtypeStruct((n_seq * seq_len, CONV_CH), BF16),
        scratch_shapes=[pltpu.VMEM((seq_len + 2 * CONV_HALO, CONV_CH), F32),
                        pltpu.VMEM((SUBLANES, seq_len + 2 * CONV_HALO - SUBLANES, CONV_CH), F32)],
        compiler_params=_cparams("arbitrary"),
    )(p, p, w, b, lng, lnb)


def _make_ret_kernel(seq_len, has_init):
    qb = min(seq_len, RET_QB)
    nq = seq_len // qb
    scale = HEAD_DIM ** -0.5

    def kern(*refs):
        if has_init:
            lg_ref, q_ref, k_ref, v_ref, g_ref, gn_ref, gm_ref, r0_ref, o_ref, st_ref = refs
        else:
            lg_ref, q_ref, k_ref, v_ref, g_ref, gn_ref, gm_ref, o_ref, st_ref = refs
        low = lax.broadcasted_iota(I32, (1, LANES), 1) < HEAD_DIM
        pos = lax.broadcasted_iota(I32, (seq_len, 1), 0).astype(F32)
        qpos = lax.broadcasted_iota(I32, (qb, 1), 0).astype(F32)
        kpos = lax.broadcasted_iota(I32, (1, seq_len), 1).astype(F32)
        gm = gm_ref[...]
        for c in range(2):
            cs = slice(c * LANES, (c + 1) * LANES)
            lgf = jnp.where(low, lg_ref[0, 2 * c], lg_ref[0, 2 * c + 1])
            lgb = jnp.where(low, lg_ref[1, 2 * c], lg_ref[1, 2 * c + 1])
            qc = q_ref[:, cs]
            kc = k_ref[:, cs]
            vc = v_ref[:, cs]
            kf = kc.astype(F32) * scale
            zeta_f = jnp.exp((seq_len - 1.0 - pos) * lgf)
            zeta_b = jnp.exp(pos * lgb)
            for d, zeta, lgd in ((0, zeta_f, lgf), (1, zeta_b, lgb)):
                st = _dot_tn((kf * zeta).astype(BF16), vc)
                if has_init:
                    st = st + r0_ref[0, d, c] * jnp.exp(seq_len * lgd)
                st_ref[0, d, c] = st
            y_blocks = [None] * nq
            for j in range(2):
                sel = low if j == 0 else jnp.logical_not(low)
                kh = jnp.where(sel, kc, jnp.zeros_like(kc))
                vh = jnp.where(sel, vc, jnp.zeros_like(vc))
                lf = lg_ref[0, 2 * c + j]
                lb = lg_ref[1, 2 * c + j]
                for i in range(nq):
                    s = _dot_nt(qc[i * qb:(i + 1) * qb], kh)
                    diff = (qpos + float(i * qb)) - kpos
                    dec = jnp.exp(jnp.where(diff >= 0, diff * lf, -diff * lb))
                    dec = dec * jnp.where(diff == 0, 2.0 * scale, scale)
                    y = _dot((s * dec).astype(BF16), vh)
                    y_blocks[i] = y if y_blocks[i] is None else y_blocks[i] + y
            y = jnp.concatenate(y_blocks, axis=0) if nq > 1 else y_blocks[0]
            if has_init:
                xi_f = jnp.exp((pos + 1.0) * lgf)
                xi_b = jnp.exp((seq_len - pos) * lgb)
                y = y + _dot(qc, r0_ref[0, 0, c].astype(BF16)) * xi_f
                y = y + _dot(qc, r0_ref[0, 1, c].astype(BF16)) * xi_b
            y_hi, y_lo = _split_bf16(y)
            mu = _dot(y_hi, gm) + _dot(y_lo, gm)
            dlt = y - mu
            var = _dot((dlt * dlt).astype(BF16), gm)
            yn = dlt * lax.rsqrt(var + LN_EPS) * gn_ref[:, cs]
            gate = g_ref[:, cs].astype(F32)
            o_ref[:, cs] = (gate * jax.nn.sigmoid(gate) * yn).astype(BF16)
    return kern


def _retention(p, n_seq, seq_len, row0, log_g, gn_g, gmat, r0):
    srow0 = row0 // seq_len
    has_init = r0 is not None
    col = OFF_RET // RET_W
    in_specs = [pl.BlockSpec(memory_space=pltpu.SMEM)]
    in_specs += [pl.BlockSpec((seq_len, RET_W), functools.partial(lambda s, j: (srow0 + s, col + j), j=j))
                 for j in range(4)]
    in_specs += [pl.BlockSpec((1, RET_W), lambda s: (0, 0)), pl.BlockSpec((LANES, LANES), lambda s: (0, 0))]
    args = [log_g, p, p, p, p, gn_g, gmat]
    st_spec = pl.BlockSpec((1, 2, 2, LANES, LANES), lambda s: (s, 0, 0, 0, 0))
    if has_init:
        in_specs.append(st_spec)
        args.append(r0)
    return pl.pallas_call(
        _make_ret_kernel(seq_len, has_init),
        grid=(n_seq,),
        in_specs=in_specs,
        out_specs=[pl.BlockSpec((seq_len, RET_W), lambda s: (s, 0)), st_spec],
        out_shape=[jax.ShapeDtypeStruct((n_seq * seq_len, RET_W), BF16),
                   jax.ShapeDtypeStruct((n_seq, 2, 2, LANES, LANES), F32)],
        compiler_params=_cparams("arbitrary"),
    )(*args)


def _outproj_kernel(n_src, ctx_tiles, *refs):
    x_refs = refs[:n_src]
    a_refs, c_refs, r_refs = refs[n_src:n_src + 2], refs[n_src + 2:n_src + 4], refs[n_src + 4:n_src + 6]
    (g1_ref, sc_ref, sh_ref, n2_ref, wo_ref, rwt_ref, rb_ref,
     x1_ref, h2_ref, te_ref, tg_ref) = refs[n_src + 6:]
    mixed = (_dot(_token_tile(a_refs, ctx_tiles), wo_ref[0:ATTN_W, :])
             + _dot(_token_tile(c_refs, ctx_tiles), wo_ref[ATTN_W:ATTN_W + CONV_CH, :])
             + _dot(_token_tile(r_refs, ctx_tiles), wo_ref[ATTN_W + CONV_CH:, :]))
    x1 = _token_tile(x_refs, ctx_tiles) + g1_ref[0] * mixed
    x1_ref[...] = x1
    inv = lax.rsqrt(jnp.mean(x1 * x1, axis=-1, keepdims=True) + EPS)
    h2 = (x1 * inv * n2_ref[...]) * (1.0 + sc_ref[0]) + sh_ref[0]
    h2_ref[...] = h2.astype(BF16)
    w_hi, w_lo = _split_bf16(rwt_ref[...])
    h_hi, h_lo = _split_bf16(h2)
    logits = _dot_nt(w_hi, h_hi) + _dot_nt(w_hi, h_lo) + _dot_nt(w_lo, h_hi) + rb_ref[...]
    t = logits.shape[1]
    eidx = lax.broadcasted_iota(I32, (N_EXPERTS, t), 0).astype(F32)
    vals = logits
    tops, idxs = [], []
    for _ in range(TOP_K):
        m = jnp.max(vals, axis=0, keepdims=True)
        idx = jnp.min(jnp.where(vals == m, eidx, float(N_EXPERTS)), axis=0, keepdims=True)
        tops.append(m)
        idxs.append(idx)
        vals = jnp.where(eidx == idx, -jnp.inf, vals)
    es = [jnp.exp(m - tops[0]) for m in tops]
    tot = es[0] + es[1] + es[2] + es[3]
    te_ref[...] = jnp.concatenate(idxs + [jnp.zeros((8 - TOP_K, t), F32)], axis=0).astype(I32)
    tg_ref[...] = jnp.concatenate([e / tot for e in es] + [jnp.zeros((8 - TOP_K, t), F32)], axis=0)


def _outproj(attn, conv, ret, x_parts, mod, norm_g, w_out_bf, rwt, rb, n_ctx, dec_seq):
    n = attn[0].shape[0] + attn[1].shape[0]
    d = x_parts[0].shape[1]
    t = TOK_TILE
    grp = functools.partial(_group_of_tile, tile=t, n_ctx=n_ctx, dec_seq=dec_seq)

    def mod_spec(which):
        return pl.BlockSpec((1, 1, d), lambda i: (grp(i) * 6 + which, 0, 0))

    const = lambda shape: pl.BlockSpec(shape, lambda i: (0,) * len(shape))
    row = lambda w: pl.BlockSpec((t, w), lambda i: (i, 0))
    lane = lambda: pl.BlockSpec((8, t), lambda i: (0, i))
    return pl.pallas_call(
        functools.partial(_outproj_kernel, len(x_parts), n_ctx // t),
        grid=(n // t,),
        in_specs=_token_specs(x_parts, t, n_ctx) + _token_specs(attn, t, n_ctx)
        + _token_specs(conv, t, n_ctx) + _token_specs(ret, t, n_ctx) + [
            mod_spec(MOD_G1), mod_spec(MOD_SC2), mod_spec(MOD_SH2),
            const((1, d)), const((d, d)), const((N_EXPERTS, d)), const((N_EXPERTS, 1))],
        out_specs=[row(d), row(d), lane(), lane()],
        out_shape=[jax.ShapeDtypeStruct((n, d), F32), jax.ShapeDtypeStruct((n, d), BF16),
                   jax.ShapeDtypeStruct((8, n), I32), jax.ShapeDtypeStruct((8, n), F32)],
        compiler_params=_cparams("arbitrary"),
    )(*x_parts, *attn, *conv, *ret, mod, mod, mod, norm_g, w_out_bf, rwt, rb)


def _round_up(x, m):
    return jnp.floor((x + (m - 1.0)) * (1.0 / m)) * m


def _route_kernel(te_ref, u_ref, ltri_ref, lpos_ref, run8_ref, loff_ref, goff_ref, seg_ref, run_ref, start_ref):
    ph = pl.program_id(0)
    i = pl.program_id(1)
    t = MOVE_TILE
    eidx = lax.broadcasted_iota(I32, (N_EXPERTS, t), 0)

    @pl.when(jnp.logical_and(ph == 0, i == 0))
    def _():
        run_ref[...] = jnp.zeros_like(run_ref)

    @pl.when(jnp.logical_and(ph == 1, i == 0))
    def _():
        seg = run_ref[...]
        seg_ref[...] = seg
        start_ref[...] = _dot(ltri_ref[...], _round_up(seg, MOE_BM), precision=lax.Precision.HIGHEST)
        run_ref[...] = jnp.zeros_like(run_ref)

    for s in range(ROUTE_TILES):
        te = te_ref[:, s * t:(s + 1) * t]
        hits = [eidx == te[k:k + 1, :] for k in range(TOP_K)]
        onehot = sum(h.astype(F32) for h in hits)
        run8 = _round_up(jnp.sum(onehot, axis=1, keepdims=True), SUBLANES)

        @pl.when(ph == 0)
        def _(run8=run8):
            run_ref[...] = run_ref[...] + run8

        @pl.when(ph == 1)
        def _(s=s, hits=hits, onehot=onehot, run8=run8):
            before = _dot(onehot.astype(BF16), u_ref[...])
            run8_b = jnp.broadcast_to(run8, (N_EXPERTS, LANES))
            loff = _dot(ltri_ref[...], run8_b, precision=lax.Precision.HIGHEST)
            base = before + loff[:, 0:1]
            rows = [jnp.sum(jnp.where(h, base, 0.0), axis=0, keepdims=True) for h in hits]
            rows.append(jnp.zeros((8 - TOP_K, t), F32))
            lpos_ref[:, s * t:(s + 1) * t] = jnp.concatenate(rows, axis=0).astype(I32)
            run8_ref[s] = run8_b.astype(I32)
            loff_ref[s] = loff.astype(I32)
            goff_ref[s] = (start_ref[...] + run_ref[...]).astype(I32)
            run_ref[...] = run_ref[...] + run8


def _route(top_e, upper, ltri):
    n = top_e.shape[1]
    t = MOVE_TILE
    nt = n // t
    g = ROUTE_TILES
    assert nt % g == 0
    table = pl.BlockSpec((g, N_EXPERTS, LANES), lambda ph, i: (i * ph, 0, 0))
    table_shape = jax.ShapeDtypeStruct((nt, N_EXPERTS, LANES), I32)
    return pl.pallas_call(
        _route_kernel,
        grid=(2, nt // g),
        in_specs=[pl.BlockSpec((8, g * t), lambda ph, i: (0, i)),
                  pl.BlockSpec((t, t), lambda ph, i: (0, 0)),
                  pl.BlockSpec((N_EXPERTS, N_EXPERTS), lambda ph, i: (0, 0))],
        out_specs=[pl.BlockSpec((8, g * t), lambda ph, i: (0, i * ph)), table, table, table,
                   pl.BlockSpec((N_EXPERTS, LANES), lambda ph, i: (0, 0))],
        out_shape=[jax.ShapeDtypeStruct((8, n), I32), table_shape, table_shape, table_shape,
                   jax.ShapeDtypeStruct((N_EXPERTS, LANES), F32)],
        scratch_shapes=[pltpu.VMEM((N_EXPERTS, LANES), F32), pltpu.VMEM((N_EXPERTS, LANES), F32)],
        compiler_params=_cparams("arbitrary", "arbitrary"),
    )(top_e, upper, ltri)


def _for_each_run_chunk(tile, run8_ref, loff_ref, goff_ref, move, fill):
    def pieces(rows, lo, go, sizes, queue):
        for size in sizes:
            hit = (rows & size) != 0

            @pl.when(hit)
            def _(lo=lo, go=go, size=size):
                move(pl.multiple_of(lo, SUBLANES), pl.multiple_of(go, SUBLANES), size, queue)

            lo = lo + jnp.where(hit, size, 0)
            go = go + jnp.where(hit, size, 0)
        return lo, go

    def per_expert(e, queue):
        idx = tile * N_EXPERTS + e
        rows = run8_ref[idx]
        lo, go = pieces(rows, loff_ref[idx], goff_ref[idx], CHUNK_SIZES[RARE_SIZES:], queue)

        @pl.when(rows >= CHUNK_SIZES[RARE_SIZES - 1])
        def _():
            pieces(rows, lo, go, CHUNK_SIZES[:RARE_SIZES], queue)

        return rows

    def per_pair(j, total):
        return total + per_expert(2 * j, 0) + per_expert(2 * j + 1, 1)

    total = lax.fori_loop(0, N_EXPERTS // 2, per_pair, jnp.int32(0))
    spare = COMPACT_ROWS - total
    off = jnp.int32(0)
    for size in CHUNK_SIZES:
        hit = (spare & size) != 0

        @pl.when(hit)
        def _(off=off, size=size):
            fill(pl.multiple_of(off, SUBLANES), size)

        off = off + jnp.where(hit, size, 0)


def _make_dispatch_kernel(n_tiles, n_blocks):
    t = MOVE_TILE
    cb = COMPACT_ROWS
    spare_row0 = n_blocks * MOE_BM

    def kern(run8_ref, loff_ref, goff_ref, seg_ref, start_ref, nv_ref, h_ref, lpos_ref, xs_ref,
             buf, zeros_ref, sem, zsem):
        step = pl.program_id(0)
        slot = lax.rem(step, 2)

        def wait_tile(s):
            pltpu.make_async_copy(buf.at[s], xs_ref.at[pl.ds(0, cb)], sem.at[s]).wait()

        def for_each_zero_chunk(action):
            def per_expert(e, carry):
                seg = seg_ref[e]
                padlen = (-seg) & (MOE_BM - 1)
                row = start_ref[e] + seg
                for size in CHUNK_SIZES[1:]:
                    hit = (padlen & size) != 0

                    @pl.when(hit)
                    def _(row=row, size=size):
                        dst = xs_ref.at[pl.ds(pl.multiple_of(row, SUBLANES), size)]
                        action(pltpu.make_async_copy(zeros_ref.at[pl.ds(0, size)], dst, zsem))

                    row = row + jnp.where(hit, size, 0)
                return carry

            lax.fori_loop(0, N_EXPERTS, per_expert, 0)

            def per_tail_block(b, carry):
                action(pltpu.make_async_copy(zeros_ref, xs_ref.at[pl.ds(b * MOE_BM, MOE_BM)], zsem))
                return carry

            lax.fori_loop(nv_ref[0], n_blocks + 2, per_tail_block, 0)

        @pl.when(step == 0)
        def _():
            zeros_ref[...] = jnp.zeros_like(zeros_ref)
            for_each_zero_chunk(lambda cp: cp.start())
            for_each_zero_chunk(lambda cp: cp.wait())

        @pl.when(step >= 2)
        def _():
            wait_tile(slot)

        lpos = lpos_ref[...]
        rows = lax.broadcasted_iota(I32, (cb, t), 0)
        onehot = jnp.where(rows == lpos[0:1, :], 1.0, 0.0)
        for k in range(1, TOP_K):
            onehot = onehot + jnp.where(rows == lpos[k:k + 1, :], 1.0, 0.0)
        buf[slot] = _dot(onehot.astype(BF16), h_ref[...])

        def move(lo, go, size, queue):
            pltpu.make_async_copy(buf.at[slot, pl.ds(lo, size)], xs_ref.at[pl.ds(go, size)],
                                  sem.at[slot]).start(priority=queue)

        def fill(off, size):
            dst = xs_ref.at[pl.ds(pl.multiple_of(spare_row0 + slot * t + off, SUBLANES), size)]
            pltpu.make_async_copy(buf.at[slot, pl.ds(off, size)], dst, sem.at[slot]).start()

        _for_each_run_chunk(step, run8_ref, loff_ref, goff_ref, move, fill)

        @pl.when(step == n_tiles - 1)
        def _():
            wait_tile(slot)
            if n_tiles > 1:
                wait_tile(1 - slot)
    return kern


def _dispatch(tables, seg, starts, n_valid, h2, lpos, n_blocks):
    n, d = h2.shape
    t = MOVE_TILE
    return pl.pallas_call(
        _make_dispatch_kernel(n // t, n_blocks),
        grid_spec=pltpu.PrefetchScalarGridSpec(
            num_scalar_prefetch=6,
            grid=(n // t,),
            in_specs=[pl.BlockSpec((t, d), lambda i, *_: (i, 0)),
                      pl.BlockSpec((8, t), lambda i, *_: (0, i))],
            out_specs=pl.BlockSpec(memory_space=pl.ANY),
            scratch_shapes=[pltpu.VMEM((2, COMPACT_ROWS, d), F32), pltpu.VMEM((MOE_BM, d), F32),
                            pltpu.SemaphoreType.DMA((2,)), pltpu.SemaphoreType.DMA(())],
        ),
        out_shape=jax.ShapeDtypeStruct(((n_blocks + 2) * MOE_BM, d), F32),
        compiler_params=_cparams("arbitrary"),
    )(*tables, seg, starts, n_valid, h2, lpos)


def _make_expert_kernel(layer):
    def kern(be_ref, bs_ref, nv_ref, ord_ref, nxt_ref, x_ref, wgu_hbm, bgu_ref, wdn_hbm, bdn_ref, y_ref,
             wgu_f, wdn_f, wgu_s, wdn_s, sem):
        _expert_body(layer, be_ref, nv_ref, ord_ref, nxt_ref, x_ref, wgu_hbm, bgu_ref, wdn_hbm, bdn_ref, y_ref,
                     wgu_f, wdn_f, wgu_s, wdn_s, sem)
    return kern


def _expert_body(layer, be_ref, nv_ref, ord_ref, nxt_ref, x_ref, wgu_hbm, bgu_ref, wdn_hbm, bdn_ref, y_ref,
                 wgu_f, wdn_f, wgu_s, wdn_s, sem):
    i = pl.program_id(0)
    e = be_ref[i]
    prev = be_ref[jnp.maximum(i - 1, 0)]
    new_expert = jnp.logical_or(i == 0, e != prev)
    slot = lax.rem(ord_ref[i], 2)

    def weight_copies(expert, s):
        return (pltpu.make_async_copy(wgu_hbm.at[layer, expert], wgu_f.at[s], sem.at[0, s]),
                pltpu.make_async_copy(wdn_hbm.at[layer, expert], wdn_f.at[s], sem.at[1, s]))

    @pl.when(i == 0)
    def _():
        for cp in weight_copies(e, slot):
            cp.start()

    @pl.when(new_expert)
    def _():
        for cp in weight_copies(e, slot):
            cp.wait()
        nxt = nxt_ref[i]

        @pl.when(nxt >= 0)
        def _():
            for cp in weight_copies(nxt, 1 - slot):
                cp.start(priority=1)

        wgu_s[...] = wgu_f[slot].astype(BF16)
        wdn_s[...] = wdn_f[slot].astype(BF16)

    @pl.when(i < nv_ref[0])
    def _():
        x = x_ref[...].astype(BF16)
        gu = _dot(x, wgu_s[...]) + bgu_ref[0, 0]
        gate = jnp.minimum(gu[:, :D_FF], SWIGLU_LIMIT)
        up = jnp.clip(gu[:, D_FF:], -SWIGLU_LIMIT, SWIGLU_LIMIT)
        hdn = (up + 1.0) * (gate * jax.nn.sigmoid(SWIGLU_ALPHA * gate))
        y_ref[...] = _dot(hdn.astype(BF16), wdn_s[...]) + bdn_ref[0, 0]

    @pl.when(i >= nv_ref[0])
    def _():
        y_ref[...] = jnp.zeros_like(y_ref)


def _experts(blk_e, blk_src, n_valid, blk_ord, blk_next, xs, w_gu, b_gu, w_dn, b_dn, layer, n_blocks):
    d = xs.shape[1]
    depth = w_gu.shape[0]

    def bias(width):
        return pl.BlockSpec((1, 1, 1, width), lambda i, be, *_: (layer, be[i], 0, 0))

    return pl.pallas_call(
        _make_expert_kernel(layer),
        grid_spec=pltpu.PrefetchScalarGridSpec(
            num_scalar_prefetch=5,
            grid=(n_blocks,),
            in_specs=[pl.BlockSpec((MOE_BM, d), lambda i, be, bs, *_: (bs[i], 0)),
                      pl.BlockSpec(memory_space=pl.ANY), bias(2 * D_FF),
                      pl.BlockSpec(memory_space=pl.ANY), bias(d)],
            out_specs=pl.BlockSpec((MOE_BM, d), lambda i, *_: (i, 0)),
            scratch_shapes=[pltpu.VMEM((2, d, 2 * D_FF), F32), pltpu.VMEM((2, D_FF, d), F32),
                            pltpu.VMEM((d, 2 * D_FF), BF16), pltpu.VMEM((D_FF, d), BF16),
                            pltpu.SemaphoreType.DMA((2, 2))],
        ),
        out_shape=jax.ShapeDtypeStruct((n_blocks * MOE_BM, d), F32),
        compiler_params=_cparams("arbitrary"),
    )(blk_e, blk_src, n_valid, blk_ord, blk_next, xs, w_gu, b_gu.reshape(depth, N_EXPERTS, 1, -1), w_dn,
      b_dn.reshape(depth, N_EXPERTS, 1, -1))


def _make_combine_kernel(n_tiles, ctx_tiles, final):
    t = MOVE_TILE
    cb = COMPACT_ROWS

    def kern(run8_ref, loff_ref, goff_ref, ys_ref, x1_ref, lpt_ref, gt_ref, g2_ref, fg_ref, *rest):
        o_refs = rest[:2] if final else rest[:1]
        ybuf, spare, sem = rest[len(o_refs):]
        step = pl.program_id(0)
        slot = lax.rem(step, 2)

        def fetch(tile, s):
            def move(lo, go, size, queue):
                pltpu.make_async_copy(ys_ref.at[pl.ds(go, size)], ybuf.at[s, pl.ds(lo, size)],
                                      sem.at[s]).start(priority=queue)

            def fill(off, size):
                pltpu.make_async_copy(ys_ref.at[pl.ds(0, size)], spare.at[s, pl.ds(off, size)], sem.at[s]).start()

            _for_each_run_chunk(tile, run8_ref, loff_ref, goff_ref, move, fill)

        @pl.when(step == 0)
        def _():
            ybuf[...] = jnp.zeros_like(ybuf)
            fetch(0, 0)

        @pl.when(step + 1 < n_tiles)
        def _():
            fetch(step + 1, 1 - slot)

        pltpu.make_async_copy(ys_ref.at[pl.ds(0, cb)], ybuf.at[slot], sem.at[slot]).wait()

        lpos = lpt_ref[...]
        gates = gt_ref[...]
        rows = lax.broadcasted_iota(I32, (cb, t), 0)
        placed = jnp.where(rows == lpos[0:1, :], gates[0:1, :], 0.0)
        for k in range(1, TOP_K):
            placed = placed + jnp.where(rows == lpos[k:k + 1, :], gates[k:k + 1, :], 0.0)
        row_gate = jnp.sum(placed, axis=1, keepdims=True)
        yb = (ybuf[slot] * row_gate).astype(BF16)
        lpt = lpos.astype(F32).T.astype(I32)
        cols = lax.broadcasted_iota(I32, (t, cb), 1)
        unsort = jnp.where(cols == lpt[:, 0:1], 1.0, 0.0)
        for k in range(1, TOP_K):
            unsort = unsort + jnp.where(cols == lpt[:, k:k + 1], 1.0, 0.0)
        y = _dot(unsort.astype(BF16), yb)
        x2 = x1_ref[...] + g2_ref[0] * y
        if not final:
            o_refs[0][...] = x2
        else:
            x2 = x2 * lax.rsqrt(jnp.mean(x2 * x2, axis=-1, keepdims=True) + EPS) * fg_ref[...]

            @pl.when(step < ctx_tiles)
            def _():
                o_refs[0][...] = x2

            @pl.when(step >= ctx_tiles)
            def _():
                o_refs[1][...] = x2
    return kern


def _combine(tables, ys, x1, lpos_t, gates_t, mod, final_g, final, n_ctx, dec_seq):
    n, d = x1.shape
    t = MOVE_TILE
    ctx_tiles = n_ctx // t
    grp = functools.partial(_group_of_tile, tile=t, n_ctx=n_ctx, dec_seq=dec_seq)
    if final:
        out_specs = [pl.BlockSpec((t, d), lambda i, *_: (jnp.minimum(i, ctx_tiles - 1), 0)),
                     pl.BlockSpec((t, d), lambda i, *_: (jnp.maximum(i - ctx_tiles, 0), 0))]
        out_shape = [jax.ShapeDtypeStruct((n_ctx, d), F32), jax.ShapeDtypeStruct((n - n_ctx, d), F32)]
    else:
        out_specs = pl.BlockSpec((t, d), lambda i, *_: (i, 0))
        out_shape = jax.ShapeDtypeStruct((n, d), F32)
    return pl.pallas_call(
        _make_combine_kernel(n // t, ctx_tiles, final),
        grid_spec=pltpu.PrefetchScalarGridSpec(
            num_scalar_prefetch=3,
            grid=(n // t,),
            in_specs=[pl.BlockSpec(memory_space=pl.ANY),
                      pl.BlockSpec((t, d), lambda i, *_: (i, 0)),
                      pl.BlockSpec((8, t), lambda i, *_: (0, i)),
                      pl.BlockSpec((8, t), lambda i, *_: (0, i)),
                      pl.BlockSpec((1, 1, d), lambda i, *_: (grp(i) * 6 + MOD_G2, 0, 0)),
                      pl.BlockSpec((1, d), lambda i, *_: (0, 0))],
            out_specs=out_specs,
            scratch_shapes=[pltpu.VMEM((2, COMPACT_ROWS, d), F32), pltpu.VMEM((2, t, d), F32),
                            pltpu.SemaphoreType.DMA((2,))],
        ),
        out_shape=out_shape,
        compiler_params=_cparams("arbitrary"),
    )(*tables, ys, x1, lpos_t, gates_t, mod, final_g)


def _blockdiag_pairs(s):
    z = jnp.zeros_like(s[..., 0, :, :])
    def pair(a, b):
        return jnp.concatenate([jnp.concatenate([a, z], axis=-1), jnp.concatenate([z, b], axis=-1)], axis=-2)
    return jnp.stack([pair(s[..., 0, :, :], s[..., 1, :, :]), pair(s[..., 2, :, :], s[..., 3, :, :])], axis=-3)


def _diag_blocks(st):
    h = HEAD_DIM
    blocks = [st[:, :, c, j * h:(j + 1) * h, j * h:(j + 1) * h] for c in range(2) for j in range(2)]
    return jnp.stack(blocks, axis=2)


def kernel(x_prompt, x_sample, cache_k, cache_v, state_ret, c, c_ctx, w_mod, b_mod, norm1_g, norm2_g, w_in,
           q_norm_g, k_norm_g, conv_w, conv_b, conv_ln_g, conv_ln_b, ret_decay_logit, ret_gn_g, w_out,
           router_w, router_b, moe_w_gu, moe_b_gu, moe_w_dn, moe_b_dn, final_g):
    batch, seq, d = x_prompt.shape
    dec_batch, dec_seq, _ = x_sample.shape
    depth = w_mod.shape[0]
    past = cache_k.shape[2]
    n_ctx = batch * seq
    n_lat = dec_batch * dec_seq
    n = n_ctx + n_lat
    assert d == D_MODEL and dec_batch == 2
    assert n_ctx % dec_seq == 0 and dec_seq % TOK_TILE == 0 and seq % ATTN_QB == 0 and dec_seq % ATTN_QB == 0
    assert MOVE_TILE == MOE_BM and n % MOVE_TILE == 0

    x_parts = (x_prompt.reshape(n_ctx, d), x_sample.reshape(n_lat, d))
    mods = _modulation(jnp.concatenate([c_ctx[None, :], c], axis=0), w_mod, b_mod)
    rope = _rope_tables(dec_seq)

    head_of_col = np.arange(QK_W) // HEAD_DIM
    hsum_np = (head_of_col[:, None] == np.arange(LANES)[None, :]).astype(np.float32)
    hsum = jnp.asarray(hsum_np, BF16)
    hbc = jnp.asarray(hsum_np.T, BF16)
    lane_head = np.arange(LANES) // HEAD_DIM
    gmat = jnp.asarray((lane_head[:, None] == lane_head[None, :]).astype(np.float32) / HEAD_DIM, BF16)
    tt = np.arange(MOVE_TILE)
    upper = jnp.asarray((tt[:, None] < tt[None, :]).astype(np.float32), BF16)
    ee = jnp.arange(N_EXPERTS)
    ltri = jnp.asarray((np.arange(N_EXPERTS)[None, :] < np.arange(N_EXPERTS)[:, None]).astype(np.float32))

    n_tiles = n // MOVE_TILE
    max_rows = n * TOP_K + n_tiles * N_EXPERTS * (SUBLANES - 1) + N_EXPERTS * (MOE_BM - 1)
    n_blocks = -(-max_rows // MOE_BM)
    cache_k2 = cache_k.reshape(dec_batch, depth, past, KV_W)
    cache_v2 = cache_v.reshape(dec_batch, depth, past, KV_W)

    ks_out, vs_out, ss_out = [], [], []
    for l in range(depth):
        mod = mods[l]
        gqk = jnp.concatenate([jnp.tile(q_norm_g[l], N_Q_HEADS), jnp.tile(k_norm_g[l], N_KV_HEADS)])[None, :]
        p, kn, vv = _inproj(x_parts, mod, norm1_g[l][None, :], w_in[l].astype(BF16), gqk, hsum, hbc, rope,
                            n_ctx, dec_seq)
        ks_out.append(kn.reshape(batch, seq, N_KV_HEADS, HEAD_DIM))
        vs_out.append(vv.reshape(batch, seq, N_KV_HEADS, HEAD_DIM))

        attn = (_attention(p, batch, seq, 0, None, l),
                _attention(p, dec_batch, dec_seq, n_ctx, (cache_k2, cache_v2), l))

        cw, cb = conv_w[l], conv_b[l][None, :]
        clg, clb = conv_ln_g[l][None, :], conv_ln_b[l][None, :]
        conv = (_conv(p, batch, seq, 0, cw, cb, clg, clb),
                _conv(p, dec_batch, dec_seq, n_ctx, cw, cb, clg, clb))

        log_g = jax.nn.log_sigmoid(ret_decay_logit[l].astype(F32))
        gn = ret_gn_g[l][None, :]
        ret_ctx, st_ctx = _retention(p, batch, seq, 0, log_g, gn, gmat, None)
        ret_lat, _ = _retention(p, dec_batch, dec_seq, n_ctx, log_g, gn, gmat,
                                _blockdiag_pairs(state_ret[:, l].astype(F32)))
        ret = (ret_ctx, ret_lat)
        ss_out.append(_diag_blocks(st_ctx))

        x1, h2, top_e, top_g = _outproj(attn, conv, ret, x_parts, mod, norm2_g[l][None, :], w_out[l].astype(BF16),
                                        router_w[l].T, router_b[l][:, None], n_ctx, dec_seq)
        lpos, run8, loff, goff, seg = _route(top_e, upper, ltri)
        tables = [tb[:, :, 0].reshape(-1) for tb in (run8, loff, goff)]
        seg = seg[:, 0].astype(I32)
        padded = (seg + MOE_BM - 1) // MOE_BM * MOE_BM
        pad_end = jnp.cumsum(padded)
        n_valid = (pad_end[-1] // MOE_BM).reshape(1)
        blk_src = jnp.minimum(jnp.arange(n_blocks, dtype=I32), n_valid - 1)
        blk_e = jnp.minimum(jnp.sum((pad_end[None, :] <= (blk_src * MOE_BM)[:, None]).astype(I32), axis=1),
                            N_EXPERTS - 1)
        owns = padded > 0
        ord_e = jnp.cumsum(owns.astype(I32)) - 1
        later = jnp.where(owns[None, :] & (ee[None, :] > ee[:, None]), ee[None, :], N_EXPERTS)
        next_e = jnp.min(later, axis=1)
        next_e = jnp.where(next_e == N_EXPERTS, -1, next_e).astype(I32)

        xs = _dispatch(tables, seg, pad_end - padded, n_valid, h2, lpos, n_blocks)
        of_blk = blk_e[:, None] == ee[None, :]
        blk_ord = jnp.sum(jnp.where(of_blk, ord_e[None, :], 0), axis=1)
        blk_next = jnp.sum(jnp.where(of_blk, next_e[None, :], 0), axis=1)
        ys = _experts(blk_e, blk_src, n_valid, blk_ord, blk_next, xs,
                      moe_w_gu, moe_b_gu, moe_w_dn, moe_b_dn, l, n_blocks)
        out = _combine(tables, ys, x1, lpos, top_g, mod, final_g[None, :], l == depth - 1, n_ctx, dec_seq)
        x_parts = (out,)

    y_prompt = out[0].reshape(batch, seq, d)
    y_sample = out[1].reshape(dec_batch, dec_seq, d)
    return (y_prompt, y_sample, jnp.stack(ks_out, axis=1), jnp.stack(vs_out, axis=1),
            jnp.stack(ss_out, axis=1))
```

```python
import functools

import numpy as np
import jax
import jax.numpy as jnp
from jax import lax
from jax.experimental import pallas as pl
from jax.experimental.pallas import tpu as pltpu

F32 = jnp.float32
BF16 = jnp.bfloat16
I32 = jnp.int32

D_MODEL = 1024
GRID_W = 64
HEAD_DIM = 64
N_Q_HEADS = 8
N_KV_HEADS = 2
ATTN_W = N_Q_HEADS * HEAD_DIM
KV_W = N_KV_HEADS * HEAD_DIM
QK_W = ATTN_W + KV_W
CONV_CH = 256
CONV_K = 31
CONV_PAD = CONV_K // 2
CONV_HALO = 16
N_RET_HEADS = 4
RET_W = 256
OFF_V = QK_W
OFF_CONV = OFF_V + KV_W
OFF_RET = OFF_CONV + 2 * CONV_CH
IN_COLS = OFF_RET + 4 * RET_W
ROPE_HALF = HEAD_DIM // 2
ROPE_THETA = 10000.0
N_EXPERTS = 32
TOP_K = 4
D_FF = D_MODEL
SWIGLU_LIMIT = 7.0
SWIGLU_ALPHA = 1.702
EPS = 1e-6
LN_EPS = 1e-5

LANES = 128
TOK_TILE = 1024
ATTN_QB = 256
RET_QB = 256
CONV_ROWS = 64
MOE_BM = 256
MOVE_TILE = 256
ROUTE_TILES = 4
SUBLANES = 8
COMPACT_ROWS = 1280
assert COMPACT_ROWS >= MOVE_TILE * TOP_K + N_EXPERTS * (SUBLANES - 1) and COMPACT_ROWS % MOE_BM == 0
CHUNK_SIZES = tuple(MOVE_TILE >> s for s in range(6))
RARE_SIZES = 3
VMEM_LIMIT = 56 * 1024 * 1024

MOD_SH1, MOD_SC1, MOD_G1, MOD_SH2, MOD_SC2, MOD_G2 = range(6)


def _cparams(*sem):
    return pltpu.CompilerParams(dimension_semantics=sem, vmem_limit_bytes=VMEM_LIMIT)


def _dot(a, b, **kw):
    return jnp.dot(a, b, preferred_element_type=F32, **kw)


def _dot_nt(a, b, **kw):
    return lax.dot_general(a, b, (((1,), (1,)), ((), ())), preferred_element_type=F32, **kw)


def _dot_tn(a, b, **kw):
    return lax.dot_general(a, b, (((0,), (0,)), ((), ())), preferred_element_type=F32, **kw)


def _split_bf16(x):
    hi = x.astype(BF16)
    lo = (x - hi.astype(F32)).astype(BF16)
    return hi, lo


MOD_TN = 1536


def _mod_kernel(ct_ref, w_ref, b_ref, o_ref):
    s = ct_ref[...]
    s = s * jax.nn.sigmoid(s)
    w = w_ref[0]
    rows = [jnp.sum(w * s[:, r:r + 1], axis=0, keepdims=True) for r in range(3)]
    rows.append(jnp.zeros((5, w.shape[1]), F32))
    o_ref[0] = jnp.concatenate(rows, axis=0) + b_ref[0]


def _modulation(cvec3, w_mod, b_mod):
    depth, d, cols = w_mod.shape
    ct = jnp.zeros((d, 8), F32).at[:, :3].set(cvec3.T)
    out = pl.pallas_call(
        _mod_kernel,
        grid=(depth, cols // MOD_TN),
        in_specs=[
            pl.BlockSpec((d, 8), lambda l, j: (0, 0)),
            pl.BlockSpec((1, d, MOD_TN), lambda l, j: (l, 0, j)),
            pl.BlockSpec((1, 1, MOD_TN), lambda l, j: (l, 0, j)),
        ],
        out_specs=pl.BlockSpec((1, 8, MOD_TN), lambda l, j: (l, 0, j)),
        out_shape=jax.ShapeDtypeStruct((depth, 8, cols), F32),
        compiler_params=_cparams("arbitrary", "arbitrary"),
    )(ct, w_mod, b_mod.reshape(depth, 1, cols))
    return out[:, :3].reshape(depth, 3 * 6, 1, d)


def _token_specs(parts, tile, n_ctx):
    d = parts[0].shape[1]
    if len(parts) == 1:
        return [pl.BlockSpec((tile, d), lambda i, *_: (i, 0))]
    ctx_tiles = n_ctx // tile
    return [pl.BlockSpec((tile, d), lambda i, *_: (jnp.minimum(i, ctx_tiles - 1), 0)),
            pl.BlockSpec((tile, d), lambda i, *_: (jnp.maximum(i - ctx_tiles, 0), 0))]


def _token_tile(refs, ctx_tiles):
    if len(refs) == 1:
        return refs[0][...]
    return jnp.where(pl.program_id(0) < ctx_tiles, refs[0][...], refs[1][...])


def _inproj_kernel(n_src, ctx_tiles, *refs):
    x_refs = refs[:n_src]
    (sh_ref, sc_ref, g_ref, w_ref, gqk_ref, hsum_ref, hbc_ref, cos_ref, sa_ref, sb_ref,
     p_ref, k_ref, v_ref) = refs[n_src:]
    x = _token_tile(x_refs, ctx_tiles)
    inv = lax.rsqrt(jnp.mean(x * x, axis=-1, keepdims=True) + EPS)
    h = (x * inv * g_ref[...]) * (1.0 + sc_ref[0]) + sh_ref[0]
    acc = _dot(h.astype(BF16), w_ref[...])
    qk = acc[:, :QK_W]
    ss = _dot((qk * qk).astype(BF16), hsum_ref[...])
    r = lax.rsqrt(ss * (1.0 / HEAD_DIM) + EPS)
    r_hi, r_lo = _split_bf16(r)
    rb = _dot(r_hi, hbc_ref[...]) + _dot(r_lo, hbc_ref[...])
    qkn = qk * rb * gqk_ref[...]
    cos = cos_ref[...]
    sa = sa_ref[...]
    sb = sb_ref[...]
    for j in range(QK_W // LANES):
        c = qkn[:, j * LANES:(j + 1) * LANES]
        up = pltpu.roll(c, LANES - ROPE_HALF // 2, 1)
        dn = pltpu.roll(c, ROPE_HALF // 2, 1)
        p_ref[:, j * LANES:(j + 1) * LANES] = (c * cos + up * sa + dn * sb).astype(BF16)
    p_ref[:, QK_W:] = acc[:, QK_W:].astype(BF16)

    @pl.when(pl.program_id(0) < ctx_tiles)
    def _():
        k_ref[...] = qkn[:, ATTN_W:QK_W]
        v_ref[...] = acc[:, OFF_V:OFF_CONV]


def _rope_tables(dec_seq):
    f32 = np.float32
    rows = dec_seq // GRID_W
    row = np.repeat(np.arange(rows), GRID_W).astype(f32)
    col = np.tile(np.arange(GRID_W), rows).astype(f32)
    inv = (f32(1.0) / (f32(ROPE_THETA) ** (np.arange(0, ROPE_HALF, 2).astype(f32) / f32(ROPE_HALF)))).astype(f32)
    ar = row[:, None] * inv[None, :]
    ac = col[:, None] * inv[None, :]
    cos = np.concatenate([np.cos(ar), np.cos(ar), np.cos(ac), np.cos(ac)], axis=-1)
    sin = np.concatenate([np.sin(ar), np.sin(ar), np.sin(ac), np.sin(ac)], axis=-1)
    first = (np.arange(HEAD_DIM) % ROPE_HALF) < ROPE_HALF // 2
    sa = np.where(first[None, :], -sin, 0.0)
    sb = np.where(first[None, :], 0.0, sin)
    def table(t, ident):
        t = np.concatenate([np.full((TOK_TILE, HEAD_DIM), ident, f32), t.astype(f32)], axis=0)
        return jnp.asarray(np.tile(t, (1, LANES // HEAD_DIM)))
    return table(cos, 1.0), table(sa, 0.0), table(sb, 0.0)


def _group_of_tile(i, tile, n_ctx, dec_seq):
    tok = i * tile
    return jnp.where(tok < n_ctx, 0, 1 + (tok - n_ctx) // dec_seq)


def _inproj(x_parts, mod, norm_g, w_in_bf, gqk, hsum, hbc, rope, n_ctx, dec_seq):
    n = sum(part.shape[0] for part in x_parts)
    d = x_parts[0].shape[1]
    t = TOK_TILE
    grp = functools.partial(_group_of_tile, tile=t, n_ctx=n_ctx, dec_seq=dec_seq)

    def mod_spec(which):
        return pl.BlockSpec((1, 1, d), lambda i: (grp(i) * 6 + which, 0, 0))

    def rope_idx(i):
        tok = i * t
        return (jnp.where(tok < n_ctx, 0, 1 + ((tok - n_ctx) % dec_seq) // t), 0)

    rope_spec = pl.BlockSpec((t, LANES), rope_idx)
    const = lambda shape: pl.BlockSpec(shape, lambda i: (0,) * len(shape))
    return pl.pallas_call(
        functools.partial(_inproj_kernel, len(x_parts), n_ctx // t),
        grid=(n // t,),
        in_specs=_token_specs(x_parts, t, n_ctx) + [
            mod_spec(MOD_SH1), mod_spec(MOD_SC1),
            const((1, d)),
            const((d, IN_COLS)),
            const((1, QK_W)), const((QK_W, LANES)), const((LANES, QK_W)),
            rope_spec, rope_spec, rope_spec,
        ],
        out_specs=[
            pl.BlockSpec((t, IN_COLS), lambda i: (i, 0)),
            pl.BlockSpec((t, KV_W), lambda i: (jnp.minimum(i, n_ctx // t - 1), 0)),
            pl.BlockSpec((t, KV_W), lambda i: (jnp.minimum(i, n_ctx // t - 1), 0)),
        ],
        out_shape=[
            jax.ShapeDtypeStruct((n, IN_COLS), BF16),
            jax.ShapeDtypeStruct((n_ctx, KV_W), F32),
            jax.ShapeDtypeStruct((n_ctx, KV_W), F32),
        ],
        compiler_params=_cparams("arbitrary"),
    )(*x_parts, mod, mod, norm_g, w_in_bf, gqk, hsum, hbc, *rope)


def _head_halves(x, hkv, low):
    r = pltpu.roll(x, HEAD_DIM, 1)
    rep = jnp.where(low, x, r) if hkv == 0 else jnp.where(low, r, x)
    return jnp.where(low, rep, 0.0).astype(BF16), jnp.where(low, 0.0, rep).astype(BF16)


def _make_attn_kernel(has_ctx):
    def kern(*refs):
        if has_ctx:
            q_ref, k_ref, v_ref, ck_ref, cv_ref, o_ref = refs
        else:
            q_ref, k_ref, v_ref, o_ref = refs
        low = lax.broadcasted_iota(I32, (1, LANES), 1) < HEAD_DIM
        k = k_ref[...].astype(F32)
        v = v_ref[...].astype(F32)
        if has_ctx:
            ck = ck_ref[0, 0]
            cv = cv_ref[0, 0]
        scale = HEAD_DIM ** -0.5
        for hkv in range(N_KV_HEADS):
            kh = _head_halves(k, hkv, low)
            vh = _head_halves(v, hkv, low)
            if has_ctx:
                ckh = _head_halves(ck, hkv, low)
                cvh = _head_halves(cv, hkv, low)
            for c in range(2):
                col = hkv * 2 * LANES + c * LANES
                qc = q_ref[:, col:col + LANES] * scale
                o_c = None
                for j in range(2):
                    s = _dot_nt(qc, kh[j])
                    m = jnp.max(s, axis=-1, keepdims=True)
                    if has_ctx:
                        s2 = _dot_nt(qc, ckh[j])
                        m = jnp.maximum(m, jnp.max(s2, axis=-1, keepdims=True))
                    p = jnp.exp(s - m)
                    l = jnp.sum(p, axis=-1, keepdims=True)
                    o = _dot(p.astype(BF16), vh[j])
                    if has_ctx:
                        p2 = jnp.exp(s2 - m)
                        l = l + jnp.sum(p2, axis=-1, keepdims=True)
                        o = o + _dot(p2.astype(BF16), cvh[j])
                    o = o / l
                    o_c = o if o_c is None else o_c + o
                o_ref[:, col:col + LANES] = o_c.astype(BF16)
    return kern


def _attention(p, n_seq, seq_len, row0, ctx_kv, layer):
    qb = ATTN_QB
    nq = seq_len // qb
    qrow0 = row0 // qb
    srow0 = row0 // seq_len
    has_ctx = ctx_kv is not None
    in_specs = [
        pl.BlockSpec((qb, ATTN_W), lambda b, i: (qrow0 + b * nq + i, 0)),
        pl.BlockSpec((seq_len, KV_W), lambda b, i: (srow0 + b, ATTN_W // KV_W)),
        pl.BlockSpec((seq_len, KV_W), lambda b, i: (srow0 + b, OFF_V // KV_W)),
    ]
    args = [p, p, p]
    if has_ctx:
        ck, cv = ctx_kv
        past = ck.shape[2]
        spec = pl.BlockSpec((1, 1, past, KV_W), lambda b, i: (b, layer, 0, 0))
        in_specs += [spec, spec]
        args += [ck, cv]
    return pl.pallas_call(
        _make_attn_kernel(has_ctx),
        grid=(n_seq, nq),
        in_specs=in_specs,
        out_specs=pl.BlockSpec((qb, ATTN_W), lambda b, i: (b * nq + i, 0)),
        out_shape=jax.ShapeDtypeStruct((n_seq * seq_len, ATTN_W), BF16),
        compiler_params=_cparams("arbitrary", "arbitrary"),
    )(*args)


def _make_conv_kernel(seq_len):
    shifted_rows = seq_len + 2 * CONV_HALO - SUBLANES

    def kern(a_ref, g_ref, w_ref, b_ref, lng_ref, lnb_ref, o_ref, zp_ref, zs_ref):
        zero = jnp.zeros((CONV_HALO, CONV_CH), F32)
        zp_ref[0:CONV_HALO, :] = zero
        zp_ref[CONV_HALO + seq_len:2 * CONV_HALO + seq_len, :] = zero
        zp_ref[CONV_HALO:CONV_HALO + seq_len, :] = (
            a_ref[...].astype(F32) * jax.nn.sigmoid(g_ref[...].astype(F32)))
        for s in range(1, SUBLANES):
            zs_ref[s] = zp_ref[s:s + shifted_rows, :]
        w = w_ref[...]
        bias = b_ref[...]
        for c in range(seq_len // CONV_ROWS):
            base = c * CONV_ROWS + CONV_HALO - CONV_PAD
            acc = jnp.zeros((CONV_ROWS, CONV_CH), F32) + bias
            for j in range(CONV_K):
                shift = (base + j) % SUBLANES
                row = base + j - shift
                if shift == 0:
                    tap = zp_ref[row:row + CONV_ROWS, :]
                else:
                    tap = zs_ref[shift, row:row + CONV_ROWS, :]
                acc = acc + tap * w[j:j + 1, :]
            mu = jnp.mean(acc, axis=-1, keepdims=True)
            dlt = acc - mu
            var = jnp.mean(dlt * dlt, axis=-1, keepdims=True)
            y = dlt * lax.rsqrt(var + LN_EPS) * lng_ref[...] + lnb_ref[...]
            o_ref[c * CONV_ROWS:(c + 1) * CONV_ROWS, :] = (y * jax.nn.sigmoid(y)).astype(BF16)
    return kern


def _conv(p, n_seq, seq_len, row0, w, b, lng, lnb):
    srow0 = row0 // seq_len
    const = lambda shape: pl.BlockSpec(shape, lambda s: (0,) * len(shape))
    return pl.pallas_call(
        _make_conv_kernel(seq_len),
        grid=(n_seq,),
        in_specs=[
            pl.BlockSpec((seq_len, CONV_CH), lambda s: (srow0 + s, OFF_CONV // CONV_CH)),
            pl.BlockSpec((seq_len, CONV_CH), lambda s: (srow0 + s, OFF_CONV // CONV_CH + 1)),
            const((CONV_K, CONV_CH)), const((1, CONV_CH)), const((1, CONV_CH)), const((1, CONV_CH)),
        ],
        out_specs=pl.BlockSpec((seq_len, CONV_CH), lambda s: (s, 0)),
        out_shape=jax.ShapeDtypeStruct((n_seq * seq_len, CONV_CH), BF16),
        scratch_shapes=[pltpu.VMEM((seq_len + 2 * CONV_HALO, CONV_CH), F32),
                        pltpu.VMEM((SUBLANES, seq_len + 2 * CONV_HALO - SUBLANES, CONV_CH), F32)],
        compiler_params=_cparams("arbitrary"),
    )(p, p, w, b, lng, lnb)


def _make_ret_kernel(seq_len, has_init):
    qb = min(seq_len, RET_QB)
    nq = seq_len // qb
    scale = HEAD_DIM ** -0.5

    def kern(*refs):
        if has_init:
            lg_ref, q_ref, k_ref, v_ref, g_ref, gn_ref, gm_ref, r0_ref, o_ref, st_ref = refs
        else:
            lg_ref, q_ref, k_ref, v_ref, g_ref, gn_ref, gm_ref, o_ref, st_ref = refs
        low = lax.broadcasted_iota(I32, (1, LANES), 1) < HEAD_DIM
        pos = lax.broadcasted_iota(I32, (seq_len, 1), 0).astype(F32)
        qpos = lax.broadcasted_iota(I32, (qb, 1), 0).astype(F32)
        kpos = lax.broadcasted_iota(I32, (1, seq_len), 1).astype(F32)
        gm = gm_ref[...]
        for c in range(2):
            cs = slice(c * LANES, (c + 1) * LANES)
            lgf = jnp.where(low, lg_ref[0, 2 * c], lg_ref[0, 2 * c + 1])
            lgb = jnp.where(low, lg_ref[1, 2 * c], lg_ref[1, 2 * c + 1])
            qc = q_ref[:, cs]
            kc = k_ref[:, cs]
            vc = v_ref[:, cs]
            kf = kc.astype(F32) * scale
            zeta_f = jnp.exp((seq_len - 1.0 - pos) * lgf)
            zeta_b = jnp.exp(pos * lgb)
            for d, zeta, lgd in ((0, zeta_f, lgf), (1, zeta_b, lgb)):
                st = _dot_tn((kf * zeta).astype(BF16), vc)
                if has_init:
                    st = st + r0_ref[0, d, c] * jnp.exp(seq_len * lgd)
                st_ref[0, d, c] = st
            y_blocks = [None] * nq
            for j in range(2):
                sel = low if j == 0 else jnp.logical_not(low)
                kh = jnp.where(sel, kc, jnp.zeros_like(kc))
                vh = jnp.where(sel, vc, jnp.zeros_like(vc))
                lf = lg_ref[0, 2 * c + j]
                lb = lg_ref[1, 2 * c + j]
                for i in range(nq):
                    s = _dot_nt(qc[i * qb:(i + 1) * qb], kh)
                    diff = (qpos + float(i * qb)) - kpos
                    dec = jnp.exp(jnp.where(diff >= 0, diff * lf, -diff * lb))
                    dec = dec * jnp.where(diff == 0, 2.0 * scale, scale)
                    y = _dot((s * dec).astype(BF16), vh)
                    y_blocks[i] = y if y_blocks[i] is None else y_blocks[i] + y
            y = jnp.concatenate(y_blocks, axis=0) if nq > 1 else y_blocks[0]
            if has_init:
                xi_f = jnp.exp((pos + 1.0) * lgf)
                xi_b = jnp.exp((seq_len - pos) * lgb)
                y = y + _dot(qc, r0_ref[0, 0, c].astype(BF16)) * xi_f
                y = y + _dot(qc, r0_ref[0, 1, c].astype(BF16)) * xi_b
            y_hi, y_lo = _split_bf16(y)
            mu = _dot(y_hi, gm) + _dot(y_lo, gm)
            dlt = y - mu
            var = _dot((dlt * dlt).astype(BF16), gm)
            yn = dlt * lax.rsqrt(var + LN_EPS) * gn_ref[:, cs]
            gate = g_ref[:, cs].astype(F32)
            o_ref[:, cs] = (gate * jax.nn.sigmoid(gate) * yn).astype(BF16)
    return kern


def _retention(p, n_seq, seq_len, row0, log_g, gn_g, gmat, r0):
    srow0 = row0 // seq_len
    has_init = r0 is not None
    col = OFF_RET // RET_W
    in_specs = [pl.BlockSpec(memory_space=pltpu.SMEM)]
    in_specs += [pl.BlockSpec((seq_len, RET_W), functools.partial(lambda s, j: (srow0 + s, col + j), j=j))
                 for j in range(4)]
    in_specs += [pl.BlockSpec((1, RET_W), lambda s: (0, 0)), pl.BlockSpec((LANES, LANES), lambda s: (0, 0))]
    args = [log_g, p, p, p, p, gn_g, gmat]
    st_spec = pl.BlockSpec((1, 2, 2, LANES, LANES), lambda s: (s, 0, 0, 0, 0))
    if has_init:
        in_specs.append(st_spec)
        args.append(r0)
    return pl.pallas_call(
        _make_ret_kernel(seq_len, has_init),
        grid=(n_seq,),
        in_specs=in_specs,
        out_specs=[pl.BlockSpec((seq_len, RET_W), lambda s: (s, 0)), st_spec],
        out_shape=[jax.ShapeDtypeStruct((n_seq * seq_len, RET_W), BF16),
                   jax.ShapeDtypeStruct((n_seq, 2, 2, LANES, LANES), F32)],
        compiler_params=_cparams("arbitrary"),
    )(*args)


def _outproj_kernel(n_src, ctx_tiles, *refs):
    x_refs = refs[:n_src]
    a_refs, c_refs, r_refs = refs[n_src:n_src + 2], refs[n_src + 2:n_src + 4], refs[n_src + 4:n_src + 6]
    (g1_ref, sc_ref, sh_ref, n2_ref, wo_ref, rwt_ref, rb_ref,
     x1_ref, h2_ref, te_ref, tg_ref) = refs[n_src + 6:]
    mixed = (_dot(_token_tile(a_refs, ctx_tiles), wo_ref[0:ATTN_W, :])
             + _dot(_token_tile(c_refs, ctx_tiles), wo_ref[ATTN_W:ATTN_W + CONV_CH, :])
             + _dot(_token_tile(r_refs, ctx_tiles), wo_ref[ATTN_W + CONV_CH:, :]))
    x1 = _token_tile(x_refs, ctx_tiles) + g1_ref[0] * mixed
    x1_ref[...] = x1
    inv = lax.rsqrt(jnp.mean(x1 * x1, axis=-1, keepdims=True) + EPS)
    h2 = (x1 * inv * n2_ref[...]) * (1.0 + sc_ref[0]) + sh_ref[0]
    h2_ref[...] = h2.astype(BF16)
    w_hi, w_lo = _split_bf16(rwt_ref[...])
    h_hi, h_lo = _split_bf16(h2)
    logits = _dot_nt(w_hi, h_hi) + _dot_nt(w_hi, h_lo) + _dot_nt(w_lo, h_hi) + rb_ref[...]
    t = logits.shape[1]
    eidx = lax.broadcasted_iota(I32, (N_EXPERTS, t), 0).astype(F32)
    vals = logits
    tops, idxs = [], []
    for _ in range(TOP_K):
        m = jnp.max(vals, axis=0, keepdims=True)
        idx = jnp.min(jnp.where(vals == m, eidx, float(N_EXPERTS)), axis=0, keepdims=True)
        tops.append(m)
        idxs.append(idx)
        vals = jnp.where(eidx == idx, -jnp.inf, vals)
    es = [jnp.exp(m - tops[0]) for m in tops]
    tot = es[0] + es[1] + es[2] + es[3]
    te_ref[...] = jnp.concatenate(idxs + [jnp.zeros((8 - TOP_K, t), F32)], axis=0).astype(I32)
    tg_ref[...] = jnp.concatenate([e / tot for e in es] + [jnp.zeros((8 - TOP_K, t), F32)], axis=0)


def _outproj(attn, conv, ret, x_parts, mod, norm_g, w_out_bf, rwt, rb, n_ctx, dec_seq):
    n = attn[0].shape[0] + attn[1].shape[0]
    d = x_parts[0].shape[1]
    t = TOK_TILE
    grp = functools.partial(_group_of_tile, tile=t, n_ctx=n_ctx, dec_seq=dec_seq)

    def mod_spec(which):
        return pl.BlockSpec((1, 1, d), lambda i: (grp(i) * 6 + which, 0, 0))

    const = lambda shape: pl.BlockSpec(shape, lambda i: (0,) * len(shape))
    row = lambda w: pl.BlockSpec((t, w), lambda i: (i, 0))
    lane = lambda: pl.BlockSpec((8, t), lambda i: (0, i))
    return pl.pallas_call(
        functools.partial(_outproj_kernel, len(x_parts), n_ctx // t),
        grid=(n // t,),
        in_specs=_token_specs(x_parts, t, n_ctx) + _token_specs(attn, t, n_ctx)
        + _token_specs(conv, t, n_ctx) + _token_specs(ret, t, n_ctx) + [
            mod_spec(MOD_G1), mod_spec(MOD_SC2), mod_spec(MOD_SH2),
            const((1, d)), const((d, d)), const((N_EXPERTS, d)), const((N_EXPERTS, 1))],
        out_specs=[row(d), row(d), lane(), lane()],
        out_shape=[jax.ShapeDtypeStruct((n, d), F32), jax.ShapeDtypeStruct((n, d), BF16),
                   jax.ShapeDtypeStruct((8, n), I32), jax.ShapeDtypeStruct((8, n), F32)],
        compiler_params=_cparams("arbitrary"),
    )(*x_parts, *attn, *conv, *ret, mod, mod, mod, norm_g, w_out_bf, rwt, rb)


def _round_up(x, m):
    return jnp.floor((x + (m - 1.0)) * (1.0 / m)) * m


def _route_kernel(te_ref, u_ref, ltri_ref, lpos_ref, run8_ref, loff_ref, goff_ref, seg_ref, run_ref, start_ref):
    ph = pl.program_id(0)
    i = pl.program_id(1)
    t = MOVE_TILE
    eidx = lax.broadcasted_iota(I32, (N_EXPERTS, t), 0)

    @pl.when(jnp.logical_and(ph == 0, i == 0))
    def _():
        run_ref[...] = jnp.zeros_like(run_ref)

    @pl.when(jnp.logical_and(ph == 1, i == 0))
    def _():
        seg = run_ref[...]
        seg_ref[...] = seg
        start_ref[...] = _dot(ltri_ref[...], _round_up(seg, MOE_BM), precision=lax.Precision.HIGHEST)
        run_ref[...] = jnp.zeros_like(run_ref)

    for s in range(ROUTE_TILES):
        te = te_ref[:, s * t:(s + 1) * t]
        hits = [eidx == te[k:k + 1, :] for k in range(TOP_K)]
        onehot = sum(h.astype(F32) for h in hits)
        run8 = _round_up(jnp.sum(onehot, axis=1, keepdims=True), SUBLANES)

        @pl.when(ph == 0)
        def _(run8=run8):
            run_ref[...] = run_ref[...] + run8

        @pl.when(ph == 1)
        def _(s=s, hits=hits, onehot=onehot, run8=run8):
            before = _dot(onehot.astype(BF16), u_ref[...])
            run8_b = jnp.broadcast_to(run8, (N_EXPERTS, LANES))
            loff = _dot(ltri_ref[...], run8_b, precision=lax.Precision.HIGHEST)
            base = before + loff[:, 0:1]
            rows = [jnp.sum(jnp.where(h, base, 0.0), axis=0, keepdims=True) for h in hits]
            rows.append(jnp.zeros((8 - TOP_K, t), F32))
            lpos_ref[:, s * t:(s + 1) * t] = jnp.concatenate(rows, axis=0).astype(I32)
            run8_ref[s] = run8_b.astype(I32)
            loff_ref[s] = loff.astype(I32)
            goff_ref[s] = (start_ref[...] + run_ref[...]).astype(I32)
            run_ref[...] = run_ref[...] + run8


def _route(top_e, upper, ltri):
    n = top_e.shape[1]
    t = MOVE_TILE
    nt = n // t
    g = ROUTE_TILES
    assert nt % g == 0
    table = pl.BlockSpec((g, N_EXPERTS, LANES), lambda ph, i: (i * ph, 0, 0))
    table_shape = jax.ShapeDtypeStruct((nt, N_EXPERTS, LANES), I32)
    return pl.pallas_call(
        _route_kernel,
        grid=(2, nt // g),
        in_specs=[pl.BlockSpec((8, g * t), lambda ph, i: (0, i)),
                  pl.BlockSpec((t, t), lambda ph, i: (0, 0)),
                  pl.BlockSpec((N_EXPERTS, N_EXPERTS), lambda ph, i: (0, 0))],
        out_specs=[pl.BlockSpec((8, g * t), lambda ph, i: (0, i * ph)), table, table, table,
                   pl.BlockSpec((N_EXPERTS, LANES), lambda ph, i: (0, 0))],
        out_shape=[jax.ShapeDtypeStruct((8, n), I32), table_shape, table_shape, table_shape,
                   jax.ShapeDtypeStruct((N_EXPERTS, LANES), F32)],
        scratch_shapes=[pltpu.VMEM((N_EXPERTS, LANES), F32), pltpu.VMEM((N_EXPERTS, LANES), F32)],
        compiler_params=_cparams("arbitrary", "arbitrary"),
    )(top_e, upper, ltri)


def _for_each_run_chunk(tile, run8_ref, loff_ref, goff_ref, move):
    def pieces(rows, lo, go, sizes, queue):
        for size in sizes:
            hit = (rows & size) != 0

            @pl.when(hit)
            def _(lo=lo, go=go, size=size):
                move(pl.multiple_of(lo, SUBLANES), pl.multiple_of(go, SUBLANES), size, queue)

            lo = lo + jnp.where(hit, size, 0)
            go = go + jnp.where(hit, size, 0)
        return lo, go

    def per_expert(e, queue):
        idx = tile * N_EXPERTS + e
        rows = run8_ref[idx]
        lo, go = pieces(rows, loff_ref[idx], goff_ref[idx], CHUNK_SIZES[RARE_SIZES:], queue)

        @pl.when(rows >= CHUNK_SIZES[RARE_SIZES - 1])
        def _():
            pieces(rows, lo, go, CHUNK_SIZES[:RARE_SIZES], queue)

        return rows

    def per_pair(j, total):
        return total + per_expert(2 * j, 0) + per_expert(2 * j + 1, 1)

    return lax.fori_loop(0, N_EXPERTS // 2, per_pair, jnp.int32(0))


def _wait_rows(total, copy_of_rows):
    size = 1 << (COMPACT_ROWS.bit_length() - 1)
    while size >= SUBLANES:
        @pl.when((total & size) != 0)
        def _(size=size):
            copy_of_rows(size).wait()

        size //= 2


def _make_dispatch_kernel(n_tiles, n_blocks):
    t = MOVE_TILE
    cb = COMPACT_ROWS

    def kern(run8_ref, loff_ref, goff_ref, seg_ref, start_ref, nv_ref, h_ref, lpos_ref, xs_ref,
             buf, zeros_ref, moved_ref, sem, zsem):
        step = pl.program_id(0)
        slot = lax.rem(step, 2)

        def wait_tile(s):
            _wait_rows(moved_ref[s], lambda rows: pltpu.make_async_copy(
                buf.at[s, pl.ds(0, rows)], xs_ref.at[pl.ds(0, rows)], sem.at[s]))

        def for_each_zero_chunk(action):
            def per_expert(e, carry):
                seg = seg_ref[e]
                padlen = (-seg) & (MOE_BM - 1)
                row = start_ref[e] + seg
                for size in CHUNK_SIZES[1:]:
                    hit = (padlen & size) != 0

                    @pl.when(hit)
                    def _(row=row, size=size):
                        dst = xs_ref.at[pl.ds(pl.multiple_of(row, SUBLANES), size)]
                        action(pltpu.make_async_copy(zeros_ref.at[pl.ds(0, size)], dst, zsem))

                    row = row + jnp.where(hit, size, 0)
                return carry

            lax.fori_loop(0, N_EXPERTS, per_expert, 0)

            def per_tail_block(b, carry):
                action(pltpu.make_async_copy(zeros_ref, xs_ref.at[pl.ds(b * MOE_BM, MOE_BM)], zsem))
                return carry

            lax.fori_loop(nv_ref[0], n_blocks, per_tail_block, 0)

        @pl.when(step == 0)
        def _():
            zeros_ref[...] = jnp.zeros_like(zeros_ref)
            for_each_zero_chunk(lambda cp: cp.start())
            for_each_zero_chunk(lambda cp: cp.wait())

        @pl.when(step >= 2)
        def _():
            wait_tile(slot)

        lpos = lpos_ref[...]
        rows = lax.broadcasted_iota(I32, (cb, t), 0)
        onehot = jnp.where(rows == lpos[0:1, :], 1.0, 0.0)
        for k in range(1, TOP_K):
            onehot = onehot + jnp.where(rows == lpos[k:k + 1, :], 1.0, 0.0)
        buf[slot] = _dot(onehot.astype(BF16), h_ref[...])

        def move(lo, go, size, queue):
            pltpu.make_async_copy(buf.at[slot, pl.ds(lo, size)], xs_ref.at[pl.ds(go, size)],
                                  sem.at[slot]).start(priority=queue)

        moved_ref[slot] = _for_each_run_chunk(step, run8_ref, loff_ref, goff_ref, move)

        @pl.when(step == n_tiles - 1)
        def _():
            wait_tile(slot)
            if n_tiles > 1:
                wait_tile(1 - slot)
    return kern


def _dispatch(tables, seg, starts, n_valid, h2, lpos, n_blocks):
    n, d = h2.shape
    t = MOVE_TILE
    return pl.pallas_call(
        _make_dispatch_kernel(n // t, n_blocks),
        grid_spec=pltpu.PrefetchScalarGridSpec(
            num_scalar_prefetch=6,
            grid=(n // t,),
            in_specs=[pl.BlockSpec((t, d), lambda i, *_: (i, 0)),
                      pl.BlockSpec((8, t), lambda i, *_: (0, i))],
            out_specs=pl.BlockSpec(memory_space=pl.ANY),
            scratch_shapes=[pltpu.VMEM((2, COMPACT_ROWS, d), F32), pltpu.VMEM((MOE_BM, d), F32),
                            pltpu.SMEM((2,), I32), pltpu.SemaphoreType.DMA((2,)), pltpu.SemaphoreType.DMA(())],
        ),
        out_shape=jax.ShapeDtypeStruct((n_blocks * MOE_BM, d), F32),
        compiler_params=_cparams("arbitrary"),
    )(*tables, seg, starts, n_valid, h2, lpos)


def _make_expert_kernel(layer):
    def kern(be_ref, bs_ref, nv_ref, ord_ref, nxt_ref, x_ref, wgu_hbm, bgu_ref, wdn_hbm, bdn_ref, y_ref,
             wgu_f, wdn_f, wgu_s, wdn_s, sem):
        _expert_body(layer, be_ref, nv_ref, ord_ref, nxt_ref, x_ref, wgu_hbm, bgu_ref, wdn_hbm, bdn_ref, y_ref,
                     wgu_f, wdn_f, wgu_s, wdn_s, sem)
    return kern


def _expert_body(layer, be_ref, nv_ref, ord_ref, nxt_ref, x_ref, wgu_hbm, bgu_ref, wdn_hbm, bdn_ref, y_ref,
                 wgu_f, wdn_f, wgu_s, wdn_s, sem):
    i = pl.program_id(0)
    e = be_ref[i]
    prev = be_ref[jnp.maximum(i - 1, 0)]
    new_expert = jnp.logical_or(i == 0, e != prev)
    slot = lax.rem(ord_ref[i], 2)

    def weight_copies(expert, s):
        return (pltpu.make_async_copy(wgu_hbm.at[layer, expert], wgu_f.at[s], sem.at[0, s]),
                pltpu.make_async_copy(wdn_hbm.at[layer, expert], wdn_f.at[s], sem.at[1, s]))

    @pl.when(i == 0)
    def _():
        for cp in weight_copies(e, slot):
            cp.start()

    @pl.when(new_expert)
    def _():
        for cp in weight_copies(e, slot):
            cp.wait()
        nxt = nxt_ref[i]

        @pl.when(nxt >= 0)
        def _():
            for cp in weight_copies(nxt, 1 - slot):
                cp.start(priority=1)

        wgu_s[...] = wgu_f[slot].astype(BF16)
        wdn_s[...] = wdn_f[slot].astype(BF16)

    @pl.when(i < nv_ref[0])
    def _():
        x = x_ref[...].astype(BF16)
        gu = _dot(x, wgu_s[...]) + bgu_ref[0, 0]
        gate = jnp.minimum(gu[:, :D_FF], SWIGLU_LIMIT)
        up = jnp.clip(gu[:, D_FF:], -SWIGLU_LIMIT, SWIGLU_LIMIT)
        hdn = (up + 1.0) * (gate * jax.nn.sigmoid(SWIGLU_ALPHA * gate))
        y_ref[...] = _dot(hdn.astype(BF16), wdn_s[...]) + bdn_ref[0, 0]

    @pl.when(i >= nv_ref[0])
    def _():
        y_ref[...] = jnp.zeros_like(y_ref)


def _experts(blk_e, blk_src, n_valid, blk_ord, blk_next, xs, w_gu, b_gu, w_dn, b_dn, layer, n_blocks):
    d = xs.shape[1]
    depth = w_gu.shape[0]

    def bias(width):
        return pl.BlockSpec((1, 1, 1, width), lambda i, be, *_: (layer, be[i], 0, 0))

    return pl.pallas_call(
        _make_expert_kernel(layer),
        grid_spec=pltpu.PrefetchScalarGridSpec(
            num_scalar_prefetch=5,
            grid=(n_blocks,),
            in_specs=[pl.BlockSpec((MOE_BM, d), lambda i, be, bs, *_: (bs[i], 0)),
                      pl.BlockSpec(memory_space=pl.ANY), bias(2 * D_FF),
                      pl.BlockSpec(memory_space=pl.ANY), bias(d)],
            out_specs=pl.BlockSpec((MOE_BM, d), lambda i, *_: (i, 0)),
            scratch_shapes=[pltpu.VMEM((2, d, 2 * D_FF), F32), pltpu.VMEM((2, D_FF, d), F32),
                            pltpu.VMEM((d, 2 * D_FF), BF16), pltpu.VMEM((D_FF, d), BF16),
                            pltpu.SemaphoreType.DMA((2, 2))],
        ),
        out_shape=jax.ShapeDtypeStruct((n_blocks * MOE_BM, d), F32),
        compiler_params=_cparams("arbitrary"),
    )(blk_e, blk_src, n_valid, blk_ord, blk_next, xs, w_gu, b_gu.reshape(depth, N_EXPERTS, 1, -1), w_dn,
      b_dn.reshape(depth, N_EXPERTS, 1, -1))


def _make_combine_kernel(n_tiles, ctx_tiles, final):
    t = MOVE_TILE
    cb = COMPACT_ROWS

    def kern(run8_ref, loff_ref, goff_ref, ys_ref, x1_ref, lpt_ref, gt_ref, g2_ref, fg_ref, *rest):
        o_refs = rest[:2] if final else rest[:1]
        ybuf, moved_ref, sem = rest[len(o_refs):]
        step = pl.program_id(0)
        slot = lax.rem(step, 2)

        def fetch(tile, s):
            def move(lo, go, size, queue):
                pltpu.make_async_copy(ys_ref.at[pl.ds(go, size)], ybuf.at[s, pl.ds(lo, size)],
                                      sem.at[s]).start(priority=queue)

            moved_ref[s] = _for_each_run_chunk(tile, run8_ref, loff_ref, goff_ref, move)

        @pl.when(step == 0)
        def _():
            ybuf[...] = jnp.zeros_like(ybuf)
            fetch(0, 0)

        @pl.when(step + 1 < n_tiles)
        def _():
            fetch(step + 1, 1 - slot)

        _wait_rows(moved_ref[slot], lambda rows: pltpu.make_async_copy(
            ys_ref.at[pl.ds(0, rows)], ybuf.at[slot, pl.ds(0, rows)], sem.at[slot]))

        lpos = lpt_ref[...]
        gates = gt_ref[...]
        rows = lax.broadcasted_iota(I32, (cb, t), 0)
        placed = jnp.where(rows == lpos[0:1, :], gates[0:1, :], 0.0)
        for k in range(1, TOP_K):
            placed = placed + jnp.where(rows == lpos[k:k + 1, :], gates[k:k + 1, :], 0.0)
        row_gate = jnp.sum(placed, axis=1, keepdims=True)
        yb = (ybuf[slot] * row_gate).astype(BF16)
        lpt = lpos.astype(F32).T.astype(I32)
        cols = lax.broadcasted_iota(I32, (t, cb), 1)
        unsort = jnp.where(cols == lpt[:, 0:1], 1.0, 0.0)
        for k in range(1, TOP_K):
            unsort = unsort + jnp.where(cols == lpt[:, k:k + 1], 1.0, 0.0)
        y = _dot(unsort.astype(BF16), yb)
        x2 = x1_ref[...] + g2_ref[0] * y
        if not final:
            o_refs[0][...] = x2
        else:
            x2 = x2 * lax.rsqrt(jnp.mean(x2 * x2, axis=-1, keepdims=True) + EPS) * fg_ref[...]

            @pl.when(step < ctx_tiles)
            def _():
                o_refs[0][...] = x2

            @pl.when(step >= ctx_tiles)
            def _():
                o_refs[1][...] = x2
    return kern


def _combine(tables, ys, x1, lpos_t, gates_t, mod, final_g, final, n_ctx, dec_seq):
    n, d = x1.shape
    t = MOVE_TILE
    ctx_tiles = n_ctx // t
    grp = functools.partial(_group_of_tile, tile=t, n_ctx=n_ctx, dec_seq=dec_seq)
    if final:
        out_specs = [pl.BlockSpec((t, d), lambda i, *_: (jnp.minimum(i, ctx_tiles - 1), 0)),
                     pl.BlockSpec((t, d), lambda i, *_: (jnp.maximum(i - ctx_tiles, 0), 0))]
        out_shape = [jax.ShapeDtypeStruct((n_ctx, d), F32), jax.ShapeDtypeStruct((n - n_ctx, d), F32)]
    else:
        out_specs = pl.BlockSpec((t, d), lambda i, *_: (i, 0))
        out_shape = jax.ShapeDtypeStruct((n, d), F32)
    return pl.pallas_call(
        _make_combine_kernel(n // t, ctx_tiles, final),
        grid_spec=pltpu.PrefetchScalarGridSpec(
            num_scalar_prefetch=3,
            grid=(n // t,),
            in_specs=[pl.BlockSpec(memory_space=pl.ANY),
                      pl.BlockSpec((t, d), lambda i, *_: (i, 0)),
                      pl.BlockSpec((8, t), lambda i, *_: (0, i)),
                      pl.BlockSpec((8, t), lambda i, *_: (0, i)),
                      pl.BlockSpec((1, 1, d), lambda i, *_: (grp(i) * 6 + MOD_G2, 0, 0)),
                      pl.BlockSpec((1, d), lambda i, *_: (0, 0))],
            out_specs=out_specs,
            scratch_shapes=[pltpu.VMEM((2, COMPACT_ROWS, d), F32), pltpu.SMEM((2,), I32),
                            pltpu.SemaphoreType.DMA((2,))],
        ),
        out_shape=out_shape,
        compiler_params=_cparams("arbitrary"),
    )(*tables, ys, x1, lpos_t, gates_t, mod, final_g)


def _blockdiag_pairs(s):
    z = jnp.zeros_like(s[..., 0, :, :])
    def pair(a, b):
        return jnp.concatenate([jnp.concatenate([a, z], axis=-1), jnp.concatenate([z, b], axis=-1)], axis=-2)
    return jnp.stack([pair(s[..., 0, :, :], s[..., 1, :, :]), pair(s[..., 2, :, :], s[..., 3, :, :])], axis=-3)


def _diag_blocks(st):
    h = HEAD_DIM
    blocks = [st[:, :, c, j * h:(j + 1) * h, j * h:(j + 1) * h] for c in range(2) for j in range(2)]
    return jnp.stack(blocks, axis=2)


def kernel(x_prompt, x_sample, cache_k, cache_v, state_ret, c, c_ctx, w_mod, b_mod, norm1_g, norm2_g, w_in,
           q_norm_g, k_norm_g, conv_w, conv_b, conv_ln_g, conv_ln_b, ret_decay_logit, ret_gn_g, w_out,
           router_w, router_b, moe_w_gu, moe_b_gu, moe_w_dn, moe_b_dn, final_g):
    batch, seq, d = x_prompt.shape
    dec_batch, dec_seq, _ = x_sample.shape
    depth = w_mod.shape[0]
    past = cache_k.shape[2]
    n_ctx = batch * seq
    n_lat = dec_batch * dec_seq
    n = n_ctx + n_lat
    assert d == D_MODEL and dec_batch == 2
    assert n_ctx % dec_seq == 0 and dec_seq % TOK_TILE == 0 and seq % ATTN_QB == 0 and dec_seq % ATTN_QB == 0
    assert n % (MOVE_TILE * ROUTE_TILES) == 0

    x_parts = (x_prompt.reshape(n_ctx, d), x_sample.reshape(n_lat, d))
    mods = _modulation(jnp.concatenate([c_ctx[None, :], c], axis=0), w_mod, b_mod)
    rope = _rope_tables(dec_seq)

    head_of_col = np.arange(QK_W) // HEAD_DIM
    hsum_np = (head_of_col[:, None] == np.arange(LANES)[None, :]).astype(np.float32)
    hsum = jnp.asarray(hsum_np, BF16)
    hbc = jnp.asarray(hsum_np.T, BF16)
    lane_head = np.arange(LANES) // HEAD_DIM
    gmat = jnp.asarray((lane_head[:, None] == lane_head[None, :]).astype(np.float32) / HEAD_DIM, BF16)
    tt = np.arange(MOVE_TILE)
    upper = jnp.asarray((tt[:, None] < tt[None, :]).astype(np.float32), BF16)
    ee = jnp.arange(N_EXPERTS)
    ltri = jnp.asarray((np.arange(N_EXPERTS)[None, :] < np.arange(N_EXPERTS)[:, None]).astype(np.float32))

    n_tiles = n // MOVE_TILE
    max_rows = n * TOP_K + n_tiles * N_EXPERTS * (SUBLANES - 1) + N_EXPERTS * (MOE_BM - 1)
    n_blocks = -(-max_rows // MOE_BM)
    cache_k2 = cache_k.reshape(dec_batch, depth, past, KV_W)
    cache_v2 = cache_v.reshape(dec_batch, depth, past, KV_W)

    ks_out, vs_out, ss_out = [], [], []
    for l in range(depth):
        mod = mods[l]
        gqk = jnp.concatenate([jnp.tile(q_norm_g[l], N_Q_HEADS), jnp.tile(k_norm_g[l], N_KV_HEADS)])[None, :]
        p, kn, vv = _inproj(x_parts, mod, norm1_g[l][None, :], w_in[l].astype(BF16), gqk, hsum, hbc, rope,
                            n_ctx, dec_seq)
        ks_out.append(kn.reshape(batch, seq, N_KV_HEADS, HEAD_DIM))
        vs_out.append(vv.reshape(batch, seq, N_KV_HEADS, HEAD_DIM))

        attn = (_attention(p, batch, seq, 0, None, l),
                _attention(p, dec_batch, dec_seq, n_ctx, (cache_k2, cache_v2), l))

        cw, cb = conv_w[l], conv_b[l][None, :]
        clg, clb = conv_ln_g[l][None, :], conv_ln_b[l][None, :]
        conv = (_conv(p, batch, seq, 0, cw, cb, clg, clb),
                _conv(p, dec_batch, dec_seq, n_ctx, cw, cb, clg, clb))

        log_g = jax.nn.log_sigmoid(ret_decay_logit[l].astype(F32))
        gn = ret_gn_g[l][None, :]
        ret_ctx, st_ctx = _retention(p, batch, seq, 0, log_g, gn, gmat, None)
        ret_lat, _ = _retention(p, dec_batch, dec_seq, n_ctx, log_g, gn, gmat,
                                _blockdiag_pairs(state_ret[:, l].astype(F32)))
        ret = (ret_ctx, ret_lat)
        ss_out.append(_diag_blocks(st_ctx))

        x1, h2, top_e, top_g = _outproj(attn, conv, ret, x_parts, mod, norm2_g[l][None, :], w_out[l].astype(BF16),
                                        router_w[l].T, router_b[l][:, None], n_ctx, dec_seq)
        lpos, run8, loff, goff, seg = _route(top_e, upper, ltri)
        tables = [tb[:, :, 0].reshape(-1) for tb in (run8, loff, goff)]
        seg = seg[:, 0].astype(I32)
        padded = (seg + MOE_BM - 1) // MOE_BM * MOE_BM
        pad_end = jnp.cumsum(padded)
        n_valid = (pad_end[-1] // MOE_BM).reshape(1)
        blk_src = jnp.minimum(jnp.arange(n_blocks, dtype=I32), n_valid - 1)
        blk_e = jnp.minimum(jnp.sum((pad_end[None, :] <= (blk_src * MOE_BM)[:, None]).astype(I32), axis=1),
                            N_EXPERTS - 1)
        owns = padded > 0
        ord_e = jnp.cumsum(owns.astype(I32)) - 1
        later = jnp.where(owns[None, :] & (ee[None, :] > ee[:, None]), ee[None, :], N_EXPERTS)
        next_e = jnp.min(later, axis=1)
        next_e = jnp.where(next_e == N_EXPERTS, -1, next_e).astype(I32)

        xs = _dispatch(tables, seg, pad_end - padded, n_valid, h2, lpos, n_blocks)
        of_blk = blk_e[:, None] == ee[None, :]
        blk_ord = jnp.sum(jnp.where(of_blk, ord_e[None, :], 0), axis=1)
        blk_next = jnp.sum(jnp.where(of_blk, next_e[None, :], 0), axis=1)
        ys = _experts(blk_e, blk_src, n_valid, blk_ord, blk_next, xs,
                      moe_w_gu, moe_b_gu, moe_w_dn, moe_b_dn, l, n_blocks)
        out = _combine(tables, ys, x1, lpos, top_g, mod, final_g[None, :], l == depth - 1, n_ctx, dec_seq)
        x_parts = (out,)

    y_prompt = out[0].reshape(batch, seq, d)
    y_sample = out[1].reshape(dec_batch, dec_seq, d)
    return (y_prompt, y_sample, jnp.stack(ks_out, axis=1), jnp.stack(vs_out, axis=1),
            jnp.stack(ss_out, axis=1))
```

```python
import functools

import numpy as np
import jax
import jax.numpy as jnp
from jax import lax
from jax.experimental import pallas as pl
from jax.experimental.pallas import tpu as pltpu

F32 = jnp.float32
BF16 = jnp.bfloat16
I32 = jnp.int32

D_MODEL = 1024
GRID_W = 64
HEAD_DIM = 64
N_Q_HEADS = 8
N_KV_HEADS = 2
ATTN_W = N_Q_HEADS * HEAD_DIM
KV_W = N_KV_HEADS * HEAD_DIM
QK_W = ATTN_W + KV_W
CONV_CH = 256
CONV_K = 31
CONV_PAD = CONV_K // 2
CONV_HALO = 16
N_RET_HEADS = 4
RET_W = 256
OFF_V = QK_W
OFF_CONV = OFF_V + KV_W
OFF_RET = OFF_CONV + 2 * CONV_CH
IN_COLS = OFF_RET + 4 * RET_W
ROPE_HALF = HEAD_DIM // 2
ROPE_THETA = 10000.0
N_EXPERTS = 32
TOP_K = 4
D_FF = D_MODEL
SWIGLU_LIMIT = 7.0
SWIGLU_ALPHA = 1.702
EPS = 1e-6
LN_EPS = 1e-5

LANES = 128
TOK_TILE = 1024
ATTN_QB = 256
RET_QB = 256
CONV_ROWS = 64
MOE_BM = 256
MOVE_TILE = 256
ROUTE_TILES = 4
SUBLANES = 8
COMPACT_ROWS = 1280
assert COMPACT_ROWS >= MOVE_TILE * TOP_K + N_EXPERTS * (SUBLANES - 1) and COMPACT_ROWS % MOE_BM == 0
CHUNK_SIZES = tuple(MOVE_TILE >> s for s in range(6))
RARE_SIZES = 3
VMEM_LIMIT = 56 * 1024 * 1024

MOD_SH1, MOD_SC1, MOD_G1, MOD_SH2, MOD_SC2, MOD_G2 = range(6)


def _cparams(*sem):
    return pltpu.CompilerParams(dimension_semantics=sem, vmem_limit_bytes=VMEM_LIMIT)


def _dot(a, b, **kw):
    return jnp.dot(a, b, preferred_element_type=F32, **kw)


def _dot_nt(a, b, **kw):
    return lax.dot_general(a, b, (((1,), (1,)), ((), ())), preferred_element_type=F32, **kw)


def _dot_tn(a, b, **kw):
    return lax.dot_general(a, b, (((0,), (0,)), ((), ())), preferred_element_type=F32, **kw)


def _split_bf16(x):
    hi = x.astype(BF16)
    lo = (x - hi.astype(F32)).astype(BF16)
    return hi, lo


MOD_TN = 1536


def _mod_kernel(ct_ref, w_ref, b_ref, o_ref):
    s = ct_ref[...]
    s = s * jax.nn.sigmoid(s)
    w = w_ref[0]
    rows = [jnp.sum(w * s[:, r:r + 1], axis=0, keepdims=True) for r in range(3)]
    rows.append(jnp.zeros((5, w.shape[1]), F32))
    o_ref[0] = jnp.concatenate(rows, axis=0) + b_ref[0]


def _modulation(cvec3, w_mod, b_mod):
    depth, d, cols = w_mod.shape
    ct = jnp.zeros((d, 8), F32).at[:, :3].set(cvec3.T)
    out = pl.pallas_call(
        _mod_kernel,
        grid=(depth, cols // MOD_TN),
        in_specs=[
            pl.BlockSpec((d, 8), lambda l, j: (0, 0)),
            pl.BlockSpec((1, d, MOD_TN), lambda l, j: (l, 0, j)),
            pl.BlockSpec((1, 1, MOD_TN), lambda l, j: (l, 0, j)),
        ],
        out_specs=pl.BlockSpec((1, 8, MOD_TN), lambda l, j: (l, 0, j)),
        out_shape=jax.ShapeDtypeStruct((depth, 8, cols), F32),
        compiler_params=_cparams("arbitrary", "arbitrary"),
    )(ct, w_mod, b_mod.reshape(depth, 1, cols))
    return out[:, :3].reshape(depth, 3 * 6, 1, d)


def _token_specs(parts, tile, n_ctx):
    d = parts[0].shape[1]
    if len(parts) == 1:
        return [pl.BlockSpec((tile, d), lambda i, *_: (i, 0))]
    ctx_tiles = n_ctx // tile
    return [pl.BlockSpec((tile, d), lambda i, *_: (jnp.minimum(i, ctx_tiles - 1), 0)),
            pl.BlockSpec((tile, d), lambda i, *_: (jnp.maximum(i - ctx_tiles, 0), 0))]


def _token_tile(refs, ctx_tiles):
    if len(refs) == 1:
        return refs[0][...]
    return jnp.where(pl.program_id(0) < ctx_tiles, refs[0][...], refs[1][...])


def _inproj_kernel(n_src, ctx_tiles, *refs):
    x_refs = refs[:n_src]
    (sh_ref, sc_ref, g_ref, w_ref, gqk_ref, hsum_ref, hbc_ref, cos_ref, sa_ref, sb_ref,
     p_ref, k_ref, v_ref) = refs[n_src:]
    x = _token_tile(x_refs, ctx_tiles)
    inv = lax.rsqrt(jnp.mean(x * x, axis=-1, keepdims=True) + EPS)
    h = (x * inv * g_ref[...]) * (1.0 + sc_ref[0]) + sh_ref[0]
    acc = _dot(h.astype(BF16), w_ref[...])
    qk = acc[:, :QK_W]
    ss = _dot((qk * qk).astype(BF16), hsum_ref[...])
    r = lax.rsqrt(ss * (1.0 / HEAD_DIM) + EPS)
    r_hi, r_lo = _split_bf16(r)
    rb = _dot(r_hi, hbc_ref[...]) + _dot(r_lo, hbc_ref[...])
    qkn = qk * rb * gqk_ref[...]
    cos = cos_ref[...]
    sa = sa_ref[...]
    sb = sb_ref[...]
    for j in range(QK_W // LANES):
        c = qkn[:, j * LANES:(j + 1) * LANES]
        up = pltpu.roll(c, LANES - ROPE_HALF // 2, 1)
        dn = pltpu.roll(c, ROPE_HALF // 2, 1)
        p_ref[:, j * LANES:(j + 1) * LANES] = (c * cos + up * sa + dn * sb).astype(BF16)
    p_ref[:, QK_W:] = acc[:, QK_W:].astype(BF16)

    @pl.when(pl.program_id(0) < ctx_tiles)
    def _():
        k_ref[...] = qkn[:, ATTN_W:QK_W]
        v_ref[...] = acc[:, OFF_V:OFF_CONV]


def _rope_tables(dec_seq):
    f32 = np.float32
    rows = dec_seq // GRID_W
    row = np.repeat(np.arange(rows), GRID_W).astype(f32)
    col = np.tile(np.arange(GRID_W), rows).astype(f32)
    inv = (f32(1.0) / (f32(ROPE_THETA) ** (np.arange(0, ROPE_HALF, 2).astype(f32) / f32(ROPE_HALF)))).astype(f32)
    ar = row[:, None] * inv[None, :]
    ac = col[:, None] * inv[None, :]
    cos = np.concatenate([np.cos(ar), np.cos(ar), np.cos(ac), np.cos(ac)], axis=-1)
    sin = np.concatenate([np.sin(ar), np.sin(ar), np.sin(ac), np.sin(ac)], axis=-1)
    first = (np.arange(HEAD_DIM) % ROPE_HALF) < ROPE_HALF // 2
    sa = np.where(first[None, :], -sin, 0.0)
    sb = np.where(first[None, :], 0.0, sin)
    def table(t, ident):
        t = np.concatenate([np.full((TOK_TILE, HEAD_DIM), ident, f32), t.astype(f32)], axis=0)
        return jnp.asarray(np.tile(t, (1, LANES // HEAD_DIM)))
    return table(cos, 1.0), table(sa, 0.0), table(sb, 0.0)


def _group_of_tile(i, tile, n_ctx, dec_seq):
    tok = i * tile
    return jnp.where(tok < n_ctx, 0, 1 + (tok - n_ctx) // dec_seq)


def _inproj(x_parts, mod, norm_g, w_in_bf, gqk, hsum, hbc, rope, n_ctx, dec_seq):
    n = sum(part.shape[0] for part in x_parts)
    d = x_parts[0].shape[1]
    t = TOK_TILE
    grp = functools.partial(_group_of_tile, tile=t, n_ctx=n_ctx, dec_seq=dec_seq)

    def mod_spec(which):
        return pl.BlockSpec((1, 1, d), lambda i: (grp(i) * 6 + which, 0, 0))

    def rope_idx(i):
        tok = i * t
        return (jnp.where(tok < n_ctx, 0, 1 + ((tok - n_ctx) % dec_seq) // t), 0)

    rope_spec = pl.BlockSpec((t, LANES), rope_idx)
    const = lambda shape: pl.BlockSpec(shape, lambda i: (0,) * len(shape))
    return pl.pallas_call(
        functools.partial(_inproj_kernel, len(x_parts), n_ctx // t),
        grid=(n // t,),
        in_specs=_token_specs(x_parts, t, n_ctx) + [
            mod_spec(MOD_SH1), mod_spec(MOD_SC1),
            const((1, d)),
            const((d, IN_COLS)),
            const((1, QK_W)), const((QK_W, LANES)), const((LANES, QK_W)),
            rope_spec, rope_spec, rope_spec,
        ],
        out_specs=[
            pl.BlockSpec((t, IN_COLS), lambda i: (i, 0)),
            pl.BlockSpec((t, KV_W), lambda i: (jnp.minimum(i, n_ctx // t - 1), 0)),
            pl.BlockSpec((t, KV_W), lambda i: (jnp.minimum(i, n_ctx // t - 1), 0)),
        ],
        out_shape=[
            jax.ShapeDtypeStruct((n, IN_COLS), BF16),
            jax.ShapeDtypeStruct((n_ctx, KV_W), F32),
            jax.ShapeDtypeStruct((n_ctx, KV_W), F32),
        ],
        compiler_params=_cparams("arbitrary"),
    )(*x_parts, mod, mod, norm_g, w_in_bf, gqk, hsum, hbc, *rope)


def _head_halves(x, hkv, low):
    r = pltpu.roll(x, HEAD_DIM, 1)
    rep = jnp.where(low, x, r) if hkv == 0 else jnp.where(low, r, x)
    return jnp.where(low, rep, 0.0).astype(BF16), jnp.where(low, 0.0, rep).astype(BF16)


def _make_attn_kernel(has_ctx):
    def kern(*refs):
        if has_ctx:
            q_ref, k_ref, v_ref, ck_ref, cv_ref, o_ref = refs
        else:
            q_ref, k_ref, v_ref, o_ref = refs
        low = lax.broadcasted_iota(I32, (1, LANES), 1) < HEAD_DIM
        k = k_ref[...].astype(F32)
        v = v_ref[...].astype(F32)
        if has_ctx:
            ck = ck_ref[0, 0]
            cv = cv_ref[0, 0]
        scale = HEAD_DIM ** -0.5
        for hkv in range(N_KV_HEADS):
            kh = _head_halves(k, hkv, low)
            vh = _head_halves(v, hkv, low)
            if has_ctx:
                ckh = _head_halves(ck, hkv, low)
                cvh = _head_halves(cv, hkv, low)
            for c in range(2):
                col = hkv * 2 * LANES + c * LANES
                qc = q_ref[:, col:col + LANES] * scale
                o_c = None
                for j in range(2):
                    s = _dot_nt(qc, kh[j])
                    m = jnp.max(s, axis=-1, keepdims=True)
                    if has_ctx:
                        s2 = _dot_nt(qc, ckh[j])
                        m = jnp.maximum(m, jnp.max(s2, axis=-1, keepdims=True))
                    p = jnp.exp(s - m)
                    l = jnp.sum(p, axis=-1, keepdims=True)
                    o = _dot(p.astype(BF16), vh[j])
                    if has_ctx:
                        p2 = jnp.exp(s2 - m)
                        l = l + jnp.sum(p2, axis=-1, keepdims=True)
                        o = o + _dot(p2.astype(BF16), cvh[j])
                    o = o / l
                    o_c = o if o_c is None else o_c + o
                o_ref[:, col:col + LANES] = o_c.astype(BF16)
    return kern


def _attention(p, n_seq, seq_len, row0, ctx_kv, layer):
    qb = ATTN_QB
    nq = seq_len // qb
    qrow0 = row0 // qb
    srow0 = row0 // seq_len
    has_ctx = ctx_kv is not None
    in_specs = [
        pl.BlockSpec((qb, ATTN_W), lambda b, i: (qrow0 + b * nq + i, 0)),
        pl.BlockSpec((seq_len, KV_W), lambda b, i: (srow0 + b, ATTN_W // KV_W)),
        pl.BlockSpec((seq_len, KV_W), lambda b, i: (srow0 + b, OFF_V // KV_W)),
    ]
    args = [p, p, p]
    if has_ctx:
        ck, cv = ctx_kv
        past = ck.shape[2]
        spec = pl.BlockSpec((1, 1, past, KV_W), lambda b, i: (b, layer, 0, 0))
        in_specs += [spec, spec]
        args += [ck, cv]
    return pl.pallas_call(
        _make_attn_kernel(has_ctx),
        grid=(n_seq, nq),
        in_specs=in_specs,
        out_specs=pl.BlockSpec((qb, ATTN_W), lambda b, i: (b * nq + i, 0)),
        out_shape=jax.ShapeDtypeStruct((n_seq * seq_len, ATTN_W), BF16),
        compiler_params=_cparams("arbitrary", "arbitrary"),
    )(*args)


def _make_conv_kernel(seq_len):
    shifted_rows = seq_len + 2 * CONV_HALO - SUBLANES

    def kern(a_ref, g_ref, w_ref, b_ref, lng_ref, lnb_ref, o_ref, zp_ref, zs_ref):
        zero = jnp.zeros((CONV_HALO, CONV_CH), F32)
        zp_ref[0:CONV_HALO, :] = zero
        zp_ref[CONV_HALO + seq_len:2 * CONV_HALO + seq_len, :] = zero
        zp_ref[CONV_HALO:CONV_HALO + seq_len, :] = (
            a_ref[...].astype(F32) * jax.nn.sigmoid(g_ref[...].astype(F32)))
        for s in range(1, SUBLANES):
            zs_ref[s] = zp_ref[s:s + shifted_rows, :]
        w = w_ref[...]
        bias = b_ref[...]
        for c in range(seq_len // CONV_ROWS):
            base = c * CONV_ROWS + CONV_HALO - CONV_PAD
            acc = jnp.zeros((CONV_ROWS, CONV_CH), F32) + bias
            for j in range(CONV_K):
                shift = (base + j) % SUBLANES
                row = base + j - shift
                if shift == 0:
                    tap = zp_ref[row:row + CONV_ROWS, :]
                else:
                    tap = zs_ref[shift, row:row + CONV_ROWS, :]
                acc = acc + tap * w[j:j + 1, :]
            mu = jnp.mean(acc, axis=-1, keepdims=True)
            dlt = acc - mu
            var = jnp.mean(dlt * dlt, axis=-1, keepdims=True)
            y = dlt * lax.rsqrt(var + LN_EPS) * lng_ref[...] + lnb_ref[...]
            o_ref[c * CONV_ROWS:(c + 1) * CONV_ROWS, :] = (y * jax.nn.sigmoid(y)).astype(BF16)
    return kern


def _conv(p, n_seq, seq_len, row0, w, b, lng, lnb):
    srow0 = row0 // seq_len
    const = lambda shape: pl.BlockSpec(shape, lambda s: (0,) * len(shape))
    return pl.pallas_call(
        _make_conv_kernel(seq_len),
        grid=(n_seq,),
        in_specs=[
            pl.BlockSpec((seq_len, CONV_CH), lambda s: (srow0 + s, OFF_CONV // CONV_CH)),
            pl.BlockSpec((seq_len, CONV_CH), lambda s: (srow0 + s, OFF_CONV // CONV_CH + 1)),
            const((CONV_K, CONV_CH)), const((1, CONV_CH)), const((1, CONV_CH)), const((1, CONV_CH)),
        ],
        out_specs=pl.BlockSpec((seq_len, CONV_CH), lambda s: (s, 0)),
        out_shape=jax.ShapeDtypeStruct((n_seq * seq_len, CONV_CH), BF16),
        scratch_shapes=[pltpu.VMEM((seq_len + 2 * CONV_HALO, CONV_CH), F32),
                        pltpu.VMEM((SUBLANES, seq_len + 2 * CONV_HALO - SUBLANES, CONV_CH), F32)],
        compiler_params=_cparams("arbitrary"),
    )(p, p, w, b, lng, lnb)


def _make_ret_kernel(seq_len, has_init):
    qb = min(seq_len, RET_QB)
    nq = seq_len // qb
    scale = HEAD_DIM ** -0.5

    def kern(*refs):
        if has_init:
            lg_ref, q_ref, k_ref, v_ref, g_ref, gn_ref, gm_ref, r0_ref, o_ref, st_ref = refs
        else:
            lg_ref, q_ref, k_ref, v_ref, g_ref, gn_ref, gm_ref, o_ref, st_ref = refs
        low = lax.broadcasted_iota(I32, (1, LANES), 1) < HEAD_DIM
        pos = lax.broadcasted_iota(I32, (seq_len, 1), 0).astype(F32)
        qpos = lax.broadcasted_iota(I32, (qb, 1), 0).astype(F32)
        kpos = lax.broadcasted_iota(I32, (1, seq_len), 1).astype(F32)
        gm = gm_ref[...]
        for c in range(2):
            cs = slice(c * LANES, (c + 1) * LANES)
            lgf = jnp.where(low, lg_ref[0, 2 * c], lg_ref[0, 2 * c + 1])
            lgb = jnp.where(low, lg_ref[1, 2 * c], lg_ref[1, 2 * c + 1])
            qc = q_ref[:, cs]
            kc = k_ref[:, cs]
            vc = v_ref[:, cs]
            kf = kc.astype(F32) * scale
            zeta_f = jnp.exp((seq_len - 1.0 - pos) * lgf)
            zeta_b = jnp.exp(pos * lgb)
            for d, zeta, lgd in ((0, zeta_f, lgf), (1, zeta_b, lgb)):
                st = _dot_tn((kf * zeta).astype(BF16), vc)
                if has_init:
                    st = st + r0_ref[0, d, c] * jnp.exp(seq_len * lgd)
                st_ref[0, d, c] = st
            y_blocks = [None] * nq
            for j in range(2):
                sel = low if j == 0 else jnp.logical_not(low)
                kh = jnp.where(sel, kc, jnp.zeros_like(kc))
                vh = jnp.where(sel, vc, jnp.zeros_like(vc))
                lf = lg_ref[0, 2 * c + j]
                lb = lg_ref[1, 2 * c + j]
                for i in range(nq):
                    s = _dot_nt(qc[i * qb:(i + 1) * qb], kh)
                    diff = (qpos + float(i * qb)) - kpos
                    dec = jnp.exp(jnp.where(diff >= 0, diff * lf, -diff * lb))
                    dec = dec * jnp.where(diff == 0, 2.0 * scale, scale)
                    y = _dot((s * dec).astype(BF16), vh)
                    y_blocks[i] = y if y_blocks[i] is None else y_blocks[i] + y
            y = jnp.concatenate(y_blocks, axis=0) if nq > 1 else y_blocks[0]
            if has_init:
                xi_f = jnp.exp((pos + 1.0) * lgf)
                xi_b = jnp.exp((seq_len - pos) * lgb)
                y = y + _dot(qc, r0_ref[0, 0, c].astype(BF16)) * xi_f
                y = y + _dot(qc, r0_ref[0, 1, c].astype(BF16)) * xi_b
            y_hi, y_lo = _split_bf16(y)
            mu = _dot(y_hi, gm) + _dot(y_lo, gm)
            dlt = y - mu
            var = _dot((dlt * dlt).astype(BF16), gm)
            yn = dlt * lax.rsqrt(var + LN_EPS) * gn_ref[:, cs]
            gate = g_ref[:, cs].astype(F32)
            o_ref[:, cs] = (gate * jax.nn.sigmoid(gate) * yn).astype(BF16)
    return kern


def _retention(p, n_seq, seq_len, row0, log_g, gn_g, gmat, r0):
    srow0 = row0 // seq_len
    has_init = r0 is not None
    col = OFF_RET // RET_W
    in_specs = [pl.BlockSpec(memory_space=pltpu.SMEM)]
    in_specs += [pl.BlockSpec((seq_len, RET_W), functools.partial(lambda s, j: (srow0 + s, col + j), j=j))
                 for j in range(4)]
    in_specs += [pl.BlockSpec((1, RET_W), lambda s: (0, 0)), pl.BlockSpec((LANES, LANES), lambda s: (0, 0))]
    args = [log_g, p, p, p, p, gn_g, gmat]
    st_spec = pl.BlockSpec((1, 2, 2, LANES, LANES), lambda s: (s, 0, 0, 0, 0))
    if has_init:
        in_specs.append(st_spec)
        args.append(r0)
    return pl.pallas_call(
        _make_ret_kernel(seq_len, has_init),
        grid=(n_seq,),
        in_specs=in_specs,
        out_specs=[pl.BlockSpec((seq_len, RET_W), lambda s: (s, 0)), st_spec],
        out_shape=[jax.ShapeDtypeStruct((n_seq * seq_len, RET_W), BF16),
                   jax.ShapeDtypeStruct((n_seq, 2, 2, LANES, LANES), F32)],
        compiler_params=_cparams("arbitrary"),
    )(*args)


def _outproj_kernel(n_src, ctx_tiles, *refs):
    x_refs = refs[:n_src]
    a_refs, c_refs, r_refs = refs[n_src:n_src + 2], refs[n_src + 2:n_src + 4], refs[n_src + 4:n_src + 6]
    (g1_ref, sc_ref, sh_ref, n2_ref, wo_ref, rwt_ref, rb_ref,
     x1_ref, h2_ref, te_ref, tg_ref) = refs[n_src + 6:]
    mixed = (_dot(_token_tile(a_refs, ctx_tiles), wo_ref[0:ATTN_W, :])
             + _dot(_token_tile(c_refs, ctx_tiles), wo_ref[ATTN_W:ATTN_W + CONV_CH, :])
             + _dot(_token_tile(r_refs, ctx_tiles), wo_ref[ATTN_W + CONV_CH:, :]))
    x1 = _token_tile(x_refs, ctx_tiles) + g1_ref[0] * mixed
    x1_ref[...] = x1
    inv = lax.rsqrt(jnp.mean(x1 * x1, axis=-1, keepdims=True) + EPS)
    h2 = (x1 * inv * n2_ref[...]) * (1.0 + sc_ref[0]) + sh_ref[0]
    h2_ref[...] = h2.astype(BF16)
    w_hi, w_lo = _split_bf16(rwt_ref[...])
    h_hi, h_lo = _split_bf16(h2)
    logits = _dot_nt(w_hi, h_hi) + _dot_nt(w_hi, h_lo) + _dot_nt(w_lo, h_hi) + rb_ref[...]
    t = logits.shape[1]
    eidx = lax.broadcasted_iota(I32, (N_EXPERTS, t), 0).astype(F32)
    vals = logits
    tops, idxs = [], []
    for _ in range(TOP_K):
        m = jnp.max(vals, axis=0, keepdims=True)
        idx = jnp.min(jnp.where(vals == m, eidx, float(N_EXPERTS)), axis=0, keepdims=True)
        tops.append(m)
        idxs.append(idx)
        vals = jnp.where(eidx == idx, -jnp.inf, vals)
    es = [jnp.exp(m - tops[0]) for m in tops]
    tot = es[0] + es[1] + es[2] + es[3]
    te_ref[...] = jnp.concatenate(idxs + [jnp.zeros((8 - TOP_K, t), F32)], axis=0).astype(I32)
    tg_ref[...] = jnp.concatenate([e / tot for e in es] + [jnp.zeros((8 - TOP_K, t), F32)], axis=0)


def _outproj(attn, conv, ret, x_parts, mod, norm_g, w_out_bf, rwt, rb, n_ctx, dec_seq):
    n = attn[0].shape[0] + attn[1].shape[0]
    d = x_parts[0].shape[1]
    t = TOK_TILE
    grp = functools.partial(_group_of_tile, tile=t, n_ctx=n_ctx, dec_seq=dec_seq)

    def mod_spec(which):
        return pl.BlockSpec((1, 1, d), lambda i: (grp(i) * 6 + which, 0, 0))

    const = lambda shape: pl.BlockSpec(shape, lambda i: (0,) * len(shape))
    row = lambda w: pl.BlockSpec((t, w), lambda i: (i, 0))
    lane = lambda: pl.BlockSpec((8, t), lambda i: (0, i))
    return pl.pallas_call(
        functools.partial(_outproj_kernel, len(x_parts), n_ctx // t),
        grid=(n // t,),
        in_specs=_token_specs(x_parts, t, n_ctx) + _token_specs(attn, t, n_ctx)
        + _token_specs(conv, t, n_ctx) + _token_specs(ret, t, n_ctx) + [
            mod_spec(MOD_G1), mod_spec(MOD_SC2), mod_spec(MOD_SH2),
            const((1, d)), const((d, d)), const((N_EXPERTS, d)), const((N_EXPERTS, 1))],
        out_specs=[row(d), row(d), lane(), lane()],
        out_shape=[jax.ShapeDtypeStruct((n, d), F32), jax.ShapeDtypeStruct((n, d), BF16),
                   jax.ShapeDtypeStruct((8, n), I32), jax.ShapeDtypeStruct((8, n), F32)],
        compiler_params=_cparams("arbitrary"),
    )(*x_parts, *attn, *conv, *ret, mod, mod, mod, norm_g, w_out_bf, rwt, rb)


def _round_up(x, m):
    return jnp.floor((x + (m - 1.0)) * (1.0 / m)) * m


def _route_kernel(te_ref, u_ref, ltri_ref, lpos_ref, run8_ref, loff_ref, goff_ref, seg_ref, run_ref, start_ref):
    ph = pl.program_id(0)
    i = pl.program_id(1)
    t = MOVE_TILE
    eidx = lax.broadcasted_iota(I32, (N_EXPERTS, t), 0)

    @pl.when(jnp.logical_and(ph == 0, i == 0))
    def _():
        run_ref[...] = jnp.zeros_like(run_ref)

    @pl.when(jnp.logical_and(ph == 1, i == 0))
    def _():
        seg = run_ref[...]
        seg_ref[...] = seg
        start_ref[...] = _dot(ltri_ref[...], _round_up(seg, MOE_BM), precision=lax.Precision.HIGHEST)
        run_ref[...] = jnp.zeros_like(run_ref)

    for s in range(ROUTE_TILES):
        te = te_ref[:, s * t:(s + 1) * t]
        hits = [eidx == te[k:k + 1, :] for k in range(TOP_K)]
        onehot = sum(h.astype(F32) for h in hits)
        run8 = _round_up(jnp.sum(onehot, axis=1, keepdims=True), SUBLANES)

        @pl.when(ph == 0)
        def _(run8=run8):
            run_ref[...] = run_ref[...] + run8

        @pl.when(ph == 1)
        def _(s=s, hits=hits, onehot=onehot, run8=run8):
            before = _dot(onehot.astype(BF16), u_ref[...])
            run8_b = jnp.broadcast_to(run8, (N_EXPERTS, LANES))
            loff = _dot(ltri_ref[...], run8_b, precision=lax.Precision.HIGHEST)
            base = before + loff[:, 0:1]
            rows = [jnp.sum(jnp.where(h, base, 0.0), axis=0, keepdims=True) for h in hits]
            rows.append(jnp.zeros((8 - TOP_K, t), F32))
            lpos_ref[:, s * t:(s + 1) * t] = jnp.concatenate(rows, axis=0).astype(I32)
            run8_ref[s] = run8_b.astype(I32)
            loff_ref[s] = loff.astype(I32)
            goff_ref[s] = (start_ref[...] + run_ref[...]).astype(I32)
            run_ref[...] = run_ref[...] + run8


def _route(top_e, upper, ltri):
    n = top_e.shape[1]
    t = MOVE_TILE
    nt = n // t
    g = ROUTE_TILES
    assert nt % g == 0
    table = pl.BlockSpec((g, N_EXPERTS, LANES), lambda ph, i: (i * ph, 0, 0))
    table_shape = jax.ShapeDtypeStruct((nt, N_EXPERTS, LANES), I32)
    return pl.pallas_call(
        _route_kernel,
        grid=(2, nt // g),
        in_specs=[pl.BlockSpec((8, g * t), lambda ph, i: (0, i)),
                  pl.BlockSpec((t, t), lambda ph, i: (0, 0)),
                  pl.BlockSpec((N_EXPERTS, N_EXPERTS), lambda ph, i: (0, 0))],
        out_specs=[pl.BlockSpec((8, g * t), lambda ph, i: (0, i * ph)), table, table, table,
                   pl.BlockSpec((N_EXPERTS, LANES), lambda ph, i: (0, 0))],
        out_shape=[jax.ShapeDtypeStruct((8, n), I32), table_shape, table_shape, table_shape,
                   jax.ShapeDtypeStruct((N_EXPERTS, LANES), F32)],
        scratch_shapes=[pltpu.VMEM((N_EXPERTS, LANES), F32), pltpu.VMEM((N_EXPERTS, LANES), F32)],
        compiler_params=_cparams("arbitrary", "arbitrary"),
    )(top_e, upper, ltri)


def _for_each_run_chunk(tile, run8_ref, loff_ref, goff_ref, move):
    def pieces(rows, lo, go, sizes, queue):
        for size in sizes:
            hit = (rows & size) != 0

            @pl.when(hit)
            def _(lo=lo, go=go, size=size):
                move(pl.multiple_of(lo, SUBLANES), pl.multiple_of(go, SUBLANES), size, queue)

            lo = lo + jnp.where(hit, size, 0)
            go = go + jnp.where(hit, size, 0)
        return lo, go

    def per_expert(e, queue):
        idx = tile * N_EXPERTS + e
        rows = run8_ref[idx]
        lo, go = pieces(rows, loff_ref[idx], goff_ref[idx], CHUNK_SIZES[RARE_SIZES:], queue)

        @pl.when(rows >= CHUNK_SIZES[RARE_SIZES - 1])
        def _():
            pieces(rows, lo, go, CHUNK_SIZES[:RARE_SIZES], queue)

        return rows

    def per_pair(j, total):
        return total + per_expert(2 * j, 0) + per_expert(2 * j + 1, 1)

    return lax.fori_loop(0, N_EXPERTS // 2, per_pair, jnp.int32(0))


def _wait_rows(total, copy_of_rows):
    size = 1 << (COMPACT_ROWS.bit_length() - 1)
    while size >= SUBLANES:
        @pl.when((total & size) != 0)
        def _(size=size):
            copy_of_rows(size).wait()

        size //= 2


def _make_dispatch_kernel(n_tiles, n_blocks):
    t = MOVE_TILE
    cb = COMPACT_ROWS

    def kern(run8_ref, loff_ref, goff_ref, seg_ref, start_ref, nv_ref, h_ref, lpos_ref, xs_ref,
             buf, zeros_ref, moved_ref, sem, zsem):
        step = pl.program_id(0)
        slot = lax.rem(step, 2)

        def wait_tile(s):
            _wait_rows(moved_ref[s], lambda rows: pltpu.make_async_copy(
                buf.at[s, pl.ds(0, rows)], xs_ref.at[pl.ds(0, rows)], sem.at[s]))

        def for_each_zero_chunk(action):
            def per_expert(e, carry):
                seg = seg_ref[e]
                padlen = (-seg) & (MOE_BM - 1)
                row = start_ref[e] + seg
                for size in CHUNK_SIZES[1:]:
                    hit = (padlen & size) != 0

                    @pl.when(hit)
                    def _(row=row, size=size):
                        dst = xs_ref.at[pl.ds(pl.multiple_of(row, SUBLANES), size)]
                        action(pltpu.make_async_copy(zeros_ref.at[pl.ds(0, size)], dst, zsem))

                    row = row + jnp.where(hit, size, 0)
                return carry

            lax.fori_loop(0, N_EXPERTS, per_expert, 0)

            def per_tail_block(b, carry):
                action(pltpu.make_async_copy(zeros_ref, xs_ref.at[pl.ds(b * MOE_BM, MOE_BM)], zsem))
                return carry

            lax.fori_loop(nv_ref[0], n_blocks, per_tail_block, 0)

        @pl.when(step == 0)
        def _():
            zeros_ref[...] = jnp.zeros_like(zeros_ref)
            for_each_zero_chunk(lambda cp: cp.start())
            for_each_zero_chunk(lambda cp: cp.wait())

        @pl.when(step >= 2)
        def _():
            wait_tile(slot)

        lpos = lpos_ref[...]
        rows = lax.broadcasted_iota(I32, (cb, t), 0)
        onehot = jnp.where(rows == lpos[0:1, :], 1.0, 0.0)
        for k in range(1, TOP_K):
            onehot = onehot + jnp.where(rows == lpos[k:k + 1, :], 1.0, 0.0)
        buf[slot] = _dot(onehot.astype(BF16), h_ref[...])

        def move(lo, go, size, queue):
            pltpu.make_async_copy(buf.at[slot, pl.ds(lo, size)], xs_ref.at[pl.ds(go, size)],
                                  sem.at[slot]).start(priority=queue)

        moved_ref[slot] = _for_each_run_chunk(step, run8_ref, loff_ref, goff_ref, move)

        @pl.when(step == n_tiles - 1)
        def _():
            wait_tile(slot)
            if n_tiles > 1:
                wait_tile(1 - slot)
    return kern


def _dispatch(tables, seg, starts, n_valid, h2, lpos, n_blocks):
    n, d = h2.shape
    t = MOVE_TILE
    return pl.pallas_call(
        _make_dispatch_kernel(n // t, n_blocks),
        grid_spec=pltpu.PrefetchScalarGridSpec(
            num_scalar_prefetch=6,
            grid=(n // t,),
            in_specs=[pl.BlockSpec((t, d), lambda i, *_: (i, 0)),
                      pl.BlockSpec((8, t), lambda i, *_: (0, i))],
            out_specs=pl.BlockSpec(memory_space=pl.ANY),
            scratch_shapes=[pltpu.VMEM((2, COMPACT_ROWS, d), F32), pltpu.VMEM((MOE_BM, d), F32),
                            pltpu.SMEM((2,), I32), pltpu.SemaphoreType.DMA((2,)), pltpu.SemaphoreType.DMA(())],
        ),
        out_shape=jax.ShapeDtypeStruct((n_blocks * MOE_BM, d), F32),
        compiler_params=_cparams("arbitrary"),
    )(*tables, seg, starts, n_valid, h2, lpos)


def _make_expert_kernel(layer):
    def kern(be_ref, bs_ref, nv_ref, ord_ref, nxt_ref, x_ref, wgu_hbm, bgu_ref, wdn_hbm, bdn_ref, y_ref,
             wgu_f, wdn_f, wgu_s, wdn_s, sem):
        i = pl.program_id(0)
        e = be_ref[i]
        prev = be_ref[jnp.maximum(i - 1, 0)]
        new_expert = jnp.logical_or(i == 0, e != prev)
        slot = lax.rem(ord_ref[i], 2)

        def weight_copies(expert, s):
            return (pltpu.make_async_copy(wgu_hbm.at[layer, expert], wgu_f.at[s], sem.at[0, s]),
                    pltpu.make_async_copy(wdn_hbm.at[layer, expert], wdn_f.at[s], sem.at[1, s]))

        @pl.when(i == 0)
        def _():
            for cp in weight_copies(e, slot):
                cp.start()

        @pl.when(new_expert)
        def _():
            for cp in weight_copies(e, slot):
                cp.wait()
            nxt = nxt_ref[i]

            @pl.when(nxt >= 0)
            def _():
                for cp in weight_copies(nxt, 1 - slot):
                    cp.start(priority=1)

            wgu_s[...] = wgu_f[slot].astype(BF16)
            wdn_s[...] = wdn_f[slot].astype(BF16)

        @pl.when(i < nv_ref[0])
        def _():
            x = x_ref[...].astype(BF16)
            gu = _dot(x, wgu_s[...]) + bgu_ref[0, 0]
            gate = jnp.minimum(gu[:, :D_FF], SWIGLU_LIMIT)
            up = jnp.clip(gu[:, D_FF:], -SWIGLU_LIMIT, SWIGLU_LIMIT)
            hdn = (up + 1.0) * (gate * jax.nn.sigmoid(SWIGLU_ALPHA * gate))
            y_ref[...] = _dot(hdn.astype(BF16), wdn_s[...]) + bdn_ref[0, 0]
    return kern


def _experts(blk_e, blk_src, n_valid, blk_ord, blk_next, xs, w_gu, b_gu, w_dn, b_dn, layer, n_blocks):
    d = xs.shape[1]
    depth = w_gu.shape[0]

    def bias(width):
        return pl.BlockSpec((1, 1, 1, width), lambda i, be, *_: (layer, be[i], 0, 0))

    rows = pl.BlockSpec((MOE_BM, d), lambda i, be, bs, *_: (bs[i], 0))
    return pl.pallas_call(
        _make_expert_kernel(layer),
        grid_spec=pltpu.PrefetchScalarGridSpec(
            num_scalar_prefetch=5,
            grid=(n_blocks,),
            in_specs=[rows,
                      pl.BlockSpec(memory_space=pl.ANY), bias(2 * D_FF),
                      pl.BlockSpec(memory_space=pl.ANY), bias(d)],
            out_specs=rows,
            scratch_shapes=[pltpu.VMEM((2, d, 2 * D_FF), F32), pltpu.VMEM((2, D_FF, d), F32),
                            pltpu.VMEM((d, 2 * D_FF), BF16), pltpu.VMEM((D_FF, d), BF16),
                            pltpu.SemaphoreType.DMA((2, 2))],
        ),
        out_shape=jax.ShapeDtypeStruct(xs.shape, xs.dtype),
        input_output_aliases={5: 0},
        compiler_params=_cparams("arbitrary"),
    )(blk_e, blk_src, n_valid, blk_ord, blk_next, xs, w_gu, b_gu.reshape(depth, N_EXPERTS, 1, -1), w_dn,
      b_dn.reshape(depth, N_EXPERTS, 1, -1))


def _make_combine_kernel(n_tiles, ctx_tiles, final):
    t = MOVE_TILE
    cb = COMPACT_ROWS

    def kern(run8_ref, loff_ref, goff_ref, ys_ref, x1_ref, lpt_ref, gt_ref, g2_ref, fg_ref, *rest):
        o_refs = rest[:2] if final else rest[:1]
        ybuf, moved_ref, sem = rest[len(o_refs):]
        step = pl.program_id(0)
        slot = lax.rem(step, 2)

        def fetch(tile, s):
            def move(lo, go, size, queue):
                pltpu.make_async_copy(ys_ref.at[pl.ds(go, size)], ybuf.at[s, pl.ds(lo, size)],
                                      sem.at[s]).start(priority=queue)

            moved_ref[s] = _for_each_run_chunk(tile, run8_ref, loff_ref, goff_ref, move)

        @pl.when(step == 0)
        def _():
            ybuf[...] = jnp.zeros_like(ybuf)
            fetch(0, 0)

        @pl.when(step + 1 < n_tiles)
        def _():
            fetch(step + 1, 1 - slot)

        _wait_rows(moved_ref[slot], lambda rows: pltpu.make_async_copy(
            ys_ref.at[pl.ds(0, rows)], ybuf.at[slot, pl.ds(0, rows)], sem.at[slot]))

        lpos = lpt_ref[...]
        gates = gt_ref[...]
        rows = lax.broadcasted_iota(I32, (cb, t), 0)
        placed = jnp.where(rows == lpos[0:1, :], gates[0:1, :], 0.0)
        for k in range(1, TOP_K):
            placed = placed + jnp.where(rows == lpos[k:k + 1, :], gates[k:k + 1, :], 0.0)
        row_gate = jnp.sum(placed, axis=1, keepdims=True)
        yb = (ybuf[slot] * row_gate).astype(BF16)
        lpt = lpos.astype(F32).T.astype(I32)
        cols = lax.broadcasted_iota(I32, (t, cb), 1)
        unsort = jnp.where(cols == lpt[:, 0:1], 1.0, 0.0)
        for k in range(1, TOP_K):
            unsort = unsort + jnp.where(cols == lpt[:, k:k + 1], 1.0, 0.0)
        y = _dot(unsort.astype(BF16), yb)
        x2 = x1_ref[...] + g2_ref[0] * y
        if not final:
            o_refs[0][...] = x2
        else:
            x2 = x2 * lax.rsqrt(jnp.mean(x2 * x2, axis=-1, keepdims=True) + EPS) * fg_ref[...]

            @pl.when(step < ctx_tiles)
            def _():
                o_refs[0][...] = x2

            @pl.when(step >= ctx_tiles)
            def _():
                o_refs[1][...] = x2
    return kern


def _combine(tables, ys, x1, lpos_t, gates_t, mod, final_g, final, n_ctx, dec_seq):
    n, d = x1.shape
    t = MOVE_TILE
    ctx_tiles = n_ctx // t
    grp = functools.partial(_group_of_tile, tile=t, n_ctx=n_ctx, dec_seq=dec_seq)
    if final:
        out_specs = [pl.BlockSpec((t, d), lambda i, *_: (jnp.minimum(i, ctx_tiles - 1), 0)),
                     pl.BlockSpec((t, d), lambda i, *_: (jnp.maximum(i - ctx_tiles, 0), 0))]
        out_shape = [jax.ShapeDtypeStruct((n_ctx, d), F32), jax.ShapeDtypeStruct((n - n_ctx, d), F32)]
    else:
        out_specs = pl.BlockSpec((t, d), lambda i, *_: (i, 0))
        out_shape = jax.ShapeDtypeStruct((n, d), F32)
    return pl.pallas_call(
        _make_combine_kernel(n // t, ctx_tiles, final),
        grid_spec=pltpu.PrefetchScalarGridSpec(
            num_scalar_prefetch=3,
            grid=(n // t,),
            in_specs=[pl.BlockSpec(memory_space=pl.ANY),
                      pl.BlockSpec((t, d), lambda i, *_: (i, 0)),
                      pl.BlockSpec((8, t), lambda i, *_: (0, i)),
                      pl.BlockSpec((8, t), lambda i, *_: (0, i)),
                      pl.BlockSpec((1, 1, d), lambda i, *_: (grp(i) * 6 + MOD_G2, 0, 0)),
                      pl.BlockSpec((1, d), lambda i, *_: (0, 0))],
            out_specs=out_specs,
            scratch_shapes=[pltpu.VMEM((2, COMPACT_ROWS, d), F32), pltpu.SMEM((2,), I32),
                            pltpu.SemaphoreType.DMA((2,))],
        ),
        out_shape=out_shape,
        compiler_params=_cparams("arbitrary"),
    )(*tables, ys, x1, lpos_t, gates_t, mod, final_g)


def _blockdiag_pairs(s):
    z = jnp.zeros_like(s[..., 0, :, :])
    def pair(a, b):
        return jnp.concatenate([jnp.concatenate([a, z], axis=-1), jnp.concatenate([z, b], axis=-1)], axis=-2)
    return jnp.stack([pair(s[..., 0, :, :], s[..., 1, :, :]), pair(s[..., 2, :, :], s[..., 3, :, :])], axis=-3)


def _diag_blocks(st):
    h = HEAD_DIM
    blocks = [st[:, :, c, j * h:(j + 1) * h, j * h:(j + 1) * h] for c in range(2) for j in range(2)]
    return jnp.stack(blocks, axis=2)


def kernel(x_prompt, x_sample, cache_k, cache_v, state_ret, c, c_ctx, w_mod, b_mod, norm1_g, norm2_g, w_in,
           q_norm_g, k_norm_g, conv_w, conv_b, conv_ln_g, conv_ln_b, ret_decay_logit, ret_gn_g, w_out,
           router_w, router_b, moe_w_gu, moe_b_gu, moe_w_dn, moe_b_dn, final_g):
    batch, seq, d = x_prompt.shape
    dec_batch, dec_seq, _ = x_sample.shape
    depth = w_mod.shape[0]
    past = cache_k.shape[2]
    n_ctx = batch * seq
    n_lat = dec_batch * dec_seq
    n = n_ctx + n_lat
    assert d == D_MODEL and dec_batch == 2
    assert n_ctx % dec_seq == 0 and dec_seq % TOK_TILE == 0 and seq % ATTN_QB == 0 and dec_seq % ATTN_QB == 0
    assert n % (MOVE_TILE * ROUTE_TILES) == 0

    x_parts = (x_prompt.reshape(n_ctx, d), x_sample.reshape(n_lat, d))
    mods = _modulation(jnp.concatenate([c_ctx[None, :], c], axis=0), w_mod, b_mod)
    rope = _rope_tables(dec_seq)

    head_of_col = np.arange(QK_W) // HEAD_DIM
    hsum_np = (head_of_col[:, None] == np.arange(LANES)[None, :]).astype(np.float32)
    hsum = jnp.asarray(hsum_np, BF16)
    hbc = jnp.asarray(hsum_np.T, BF16)
    lane_head = np.arange(LANES) // HEAD_DIM
    gmat = jnp.asarray((lane_head[:, None] == lane_head[None, :]).astype(np.float32) / HEAD_DIM, BF16)
    tt = np.arange(MOVE_TILE)
    upper = jnp.asarray((tt[:, None] < tt[None, :]).astype(np.float32), BF16)
    ee = jnp.arange(N_EXPERTS)
    ltri = jnp.asarray((np.arange(N_EXPERTS)[None, :] < np.arange(N_EXPERTS)[:, None]).astype(np.float32))

    n_tiles = n // MOVE_TILE
    max_rows = n * TOP_K + n_tiles * N_EXPERTS * (SUBLANES - 1) + N_EXPERTS * (MOE_BM - 1)
    n_blocks = -(-max_rows // MOE_BM)
    cache_k2 = cache_k.reshape(dec_batch, depth, past, KV_W)
    cache_v2 = cache_v.reshape(dec_batch, depth, past, KV_W)

    ks_out, vs_out, ss_out = [], [], []
    for l in range(depth):
        mod = mods[l]
        gqk = jnp.concatenate([jnp.tile(q_norm_g[l], N_Q_HEADS), jnp.tile(k_norm_g[l], N_KV_HEADS)])[None, :]
        p, kn, vv = _inproj(x_parts, mod, norm1_g[l][None, :], w_in[l].astype(BF16), gqk, hsum, hbc, rope,
                            n_ctx, dec_seq)
        ks_out.append(kn.reshape(batch, seq, N_KV_HEADS, HEAD_DIM))
        vs_out.append(vv.reshape(batch, seq, N_KV_HEADS, HEAD_DIM))

        attn = (_attention(p, batch, seq, 0, None, l),
                _attention(p, dec_batch, dec_seq, n_ctx, (cache_k2, cache_v2), l))

        cw, cb = conv_w[l], conv_b[l][None, :]
        clg, clb = conv_ln_g[l][None, :], conv_ln_b[l][None, :]
        conv = (_conv(p, batch, seq, 0, cw, cb, clg, clb),
                _conv(p, dec_batch, dec_seq, n_ctx, cw, cb, clg, clb))

        log_g = jax.nn.log_sigmoid(ret_decay_logit[l].astype(F32))
        gn = ret_gn_g[l][None, :]
        ret_ctx, st_ctx = _retention(p, batch, seq, 0, log_g, gn, gmat, None)
        ret_lat, _ = _retention(p, dec_batch, dec_seq, n_ctx, log_g, gn, gmat,
                                _blockdiag_pairs(state_ret[:, l].astype(F32)))
        ret = (ret_ctx, ret_lat)
        ss_out.append(_diag_blocks(st_ctx))

        x1, h2, top_e, top_g = _outproj(attn, conv, ret, x_parts, mod, norm2_g[l][None, :], w_out[l].astype(BF16),
                                        router_w[l].T, router_b[l][:, None], n_ctx, dec_seq)
        lpos, run8, loff, goff, seg = _route(top_e, upper, ltri)
        tables = [tb[:, :, 0].reshape(-1) for tb in (run8, loff, goff)]
        seg = seg[:, 0].astype(I32)
        padded = (seg + MOE_BM - 1) // MOE_BM * MOE_BM
        pad_end = jnp.cumsum(padded)
        n_valid = (pad_end[-1] // MOE_BM).reshape(1)
        blk_src = jnp.minimum(jnp.arange(n_blocks, dtype=I32), n_valid - 1)
        blk_e = jnp.minimum(jnp.sum((pad_end[None, :] <= (blk_src * MOE_BM)[:, None]).astype(I32), axis=1),
                            N_EXPERTS - 1)
        owns = padded > 0
        ord_e = jnp.cumsum(owns.astype(I32)) - 1
        next_e = jnp.sum(jnp.where(owns[None, :] & (ord_e[None, :] == ord_e[:, None] + 1),
                                   ee[None, :] + 1, 0), axis=1).astype(I32) - 1

        xs = _dispatch(tables, seg, pad_end - padded, n_valid, h2, lpos, n_blocks)
        of_blk = blk_e[:, None] == ee[None, :]
        per_blk = lambda table: jnp.sum(jnp.where(of_blk, table[None, :], 0), axis=1)
        ys = _experts(blk_e, blk_src, n_valid, per_blk(ord_e), per_blk(next_e), xs,
                      moe_w_gu, moe_b_gu, moe_w_dn, moe_b_dn, l, n_blocks)
        out = _combine(tables, ys, x1, lpos, top_g, mod, final_g[None, :], l == depth - 1, n_ctx, dec_seq)
        x_parts = (out,)

    y_prompt = out[0].reshape(batch, seq, d)
    y_sample = out[1].reshape(dec_batch, dec_seq, d)
    return (y_prompt, y_sample, jnp.stack(ks_out, axis=1), jnp.stack(vs_out, axis=1),
            jnp.stack(ss_out, axis=1))
```

```python
import functools

import numpy as np
import jax
import jax.numpy as jnp
from jax import lax
from jax.experimental import pallas as pl
from jax.experimental.pallas import tpu as pltpu

F32 = jnp.float32
BF16 = jnp.bfloat16
I32 = jnp.int32

D_MODEL = 1024
GRID_W = 64
HEAD_DIM = 64
N_Q_HEADS = 8
N_KV_HEADS = 2
ATTN_W = N_Q_HEADS * HEAD_DIM
KV_W = N_KV_HEADS * HEAD_DIM
QK_W = ATTN_W + KV_W
CONV_CH = 256
CONV_K = 31
CONV_PAD = CONV_K // 2
CONV_HALO = 16
N_RET_HEADS = 4
RET_W = 256
OFF_V = QK_W
OFF_CONV = OFF_V + KV_W
OFF_RET = OFF_CONV + 2 * CONV_CH
IN_COLS = OFF_RET + 4 * RET_W
ROPE_HALF = HEAD_DIM // 2
ROPE_THETA = 10000.0
N_EXPERTS = 32
TOP_K = 4
D_FF = D_MODEL
SWIGLU_LIMIT = 7.0
SWIGLU_ALPHA = 1.702
EPS = 1e-6
LN_EPS = 1e-5

LANES = 128
TOK_TILE = 1024
ATTN_QB = 512
MIXER_ROWS = 1024
RET_QB = 256
CONV_ROWS = 64
MOE_BM = 256
MOVE_TILE = 256
ROUTE_TILES = 4
SUBLANES = 8
COMPACT_ROWS = 1280
assert COMPACT_ROWS >= MOVE_TILE * TOP_K + N_EXPERTS * (SUBLANES - 1) and COMPACT_ROWS % MOE_BM == 0
CHUNK_SIZES = tuple(MOVE_TILE >> s for s in range(6))
RARE_SIZES = 3
VMEM_LIMIT = 56 * 1024 * 1024

MOD_SH1, MOD_SC1, MOD_G1, MOD_SH2, MOD_SC2, MOD_G2 = range(6)


def _cparams(*sem):
    return pltpu.CompilerParams(dimension_semantics=sem, vmem_limit_bytes=VMEM_LIMIT)


def _dot(a, b, **kw):
    return jnp.dot(a, b, preferred_element_type=F32, **kw)


def _dot_nt(a, b, **kw):
    return lax.dot_general(a, b, (((1,), (1,)), ((), ())), preferred_element_type=F32, **kw)


def _dot_tn(a, b, **kw):
    return lax.dot_general(a, b, (((0,), (0,)), ((), ())), preferred_element_type=F32, **kw)


def _split_bf16(x):
    hi = x.astype(BF16)
    lo = (x - hi.astype(F32)).astype(BF16)
    return hi, lo


MOD_TN = 1536


def _mod_kernel(ct_ref, w_ref, b_ref, o_ref):
    s = ct_ref[...]
    s = s * jax.nn.sigmoid(s)
    w = w_ref[0]
    rows = [jnp.sum(w * s[:, r:r + 1], axis=0, keepdims=True) for r in range(3)]
    rows.append(jnp.zeros((5, w.shape[1]), F32))
    o_ref[0] = jnp.concatenate(rows, axis=0) + b_ref[0]


def _modulation(cvec3, w_mod, b_mod):
    depth, d, cols = w_mod.shape
    ct = jnp.zeros((d, 8), F32).at[:, :3].set(cvec3.T)
    out = pl.pallas_call(
        _mod_kernel,
        grid=(depth, cols // MOD_TN),
        in_specs=[
            pl.BlockSpec((d, 8), lambda l, j: (0, 0)),
            pl.BlockSpec((1, d, MOD_TN), lambda l, j: (l, 0, j)),
            pl.BlockSpec((1, 1, MOD_TN), lambda l, j: (l, 0, j)),
        ],
        out_specs=pl.BlockSpec((1, 8, MOD_TN), lambda l, j: (l, 0, j)),
        out_shape=jax.ShapeDtypeStruct((depth, 8, cols), F32),
        compiler_params=_cparams("arbitrary", "arbitrary"),
    )(ct, w_mod, b_mod.reshape(depth, 1, cols))
    return out[:, :3].reshape(depth, 3 * 6, 1, d)


def _token_specs(parts, tile, n_ctx):
    d = parts[0].shape[1]
    if len(parts) == 1:
        return [pl.BlockSpec((tile, d), lambda i, *_: (i, 0))]
    ctx_tiles = n_ctx // tile
    return [pl.BlockSpec((tile, d), lambda i, *_: (jnp.minimum(i, ctx_tiles - 1), 0)),
            pl.BlockSpec((tile, d), lambda i, *_: (jnp.maximum(i - ctx_tiles, 0), 0))]


def _token_tile(refs, ctx_tiles):
    if len(refs) == 1:
        return refs[0][...]
    return jnp.where(pl.program_id(0) < ctx_tiles, refs[0][...], refs[1][...])


def _inproj_kernel(n_src, ctx_tiles, *refs):
    x_refs = refs[:n_src]
    (sh_ref, sc_ref, g_ref, w_ref, gqk_ref, hsum_ref, hbc_ref, cos_ref, sa_ref, sb_ref,
     p_ref, k_ref, v_ref) = refs[n_src:]
    x = _token_tile(x_refs, ctx_tiles)
    inv = lax.rsqrt(jnp.mean(x * x, axis=-1, keepdims=True) + EPS)
    h = (x * inv * g_ref[...]) * (1.0 + sc_ref[0]) + sh_ref[0]
    acc = _dot(h.astype(BF16), w_ref[...])
    qk = acc[:, :QK_W]
    ss = _dot((qk * qk).astype(BF16), hsum_ref[...])
    r = lax.rsqrt(ss * (1.0 / HEAD_DIM) + EPS)
    r_hi, r_lo = _split_bf16(r)
    rb = _dot(r_hi, hbc_ref[...]) + _dot(r_lo, hbc_ref[...])
    qkn = qk * rb * gqk_ref[...]
    cos = cos_ref[...]
    sa = sa_ref[...]
    sb = sb_ref[...]
    for j in range(QK_W // LANES):
        c = qkn[:, j * LANES:(j + 1) * LANES]
        up = pltpu.roll(c, LANES - ROPE_HALF // 2, 1)
        dn = pltpu.roll(c, ROPE_HALF // 2, 1)
        p_ref[:, j * LANES:(j + 1) * LANES] = (c * cos + up * sa + dn * sb).astype(BF16)
    p_ref[:, QK_W:] = acc[:, QK_W:].astype(BF16)

    @pl.when(pl.program_id(0) < ctx_tiles)
    def _():
        k_ref[...] = qkn[:, ATTN_W:QK_W]
        v_ref[...] = acc[:, OFF_V:OFF_CONV]


def _rope_tables(dec_seq):
    f32 = np.float32
    rows = dec_seq // GRID_W
    row = np.repeat(np.arange(rows), GRID_W).astype(f32)
    col = np.tile(np.arange(GRID_W), rows).astype(f32)
    inv = (f32(1.0) / (f32(ROPE_THETA) ** (np.arange(0, ROPE_HALF, 2).astype(f32) / f32(ROPE_HALF)))).astype(f32)
    ar = row[:, None] * inv[None, :]
    ac = col[:, None] * inv[None, :]
    cos = np.concatenate([np.cos(ar), np.cos(ar), np.cos(ac), np.cos(ac)], axis=-1)
    sin = np.concatenate([np.sin(ar), np.sin(ar), np.sin(ac), np.sin(ac)], axis=-1)
    first = (np.arange(HEAD_DIM) % ROPE_HALF) < ROPE_HALF // 2
    sa = np.where(first[None, :], -sin, 0.0)
    sb = np.where(first[None, :], 0.0, sin)
    def table(t, ident):
        t = np.concatenate([np.full((TOK_TILE, HEAD_DIM), ident, f32), t.astype(f32)], axis=0)
        return jnp.asarray(np.tile(t, (1, LANES // HEAD_DIM)))
    return table(cos, 1.0), table(sa, 0.0), table(sb, 0.0)


def _group_of_tile(i, tile, n_ctx, dec_seq):
    tok = i * tile
    return jnp.where(tok < n_ctx, 0, 1 + (tok - n_ctx) // dec_seq)


def _inproj(x_parts, mod, norm_g, w_in_bf, gqk, hsum, hbc, rope, n_ctx, dec_seq):
    n = sum(part.shape[0] for part in x_parts)
    d = x_parts[0].shape[1]
    t = TOK_TILE
    grp = functools.partial(_group_of_tile, tile=t, n_ctx=n_ctx, dec_seq=dec_seq)

    def mod_spec(which):
        return pl.BlockSpec((1, 1, d), lambda i: (grp(i) * 6 + which, 0, 0))

    def rope_idx(i):
        tok = i * t
        return (jnp.where(tok < n_ctx, 0, 1 + ((tok - n_ctx) % dec_seq) // t), 0)

    rope_spec = pl.BlockSpec((t, LANES), rope_idx)
    const = lambda shape: pl.BlockSpec(shape, lambda i: (0,) * len(shape))
    return pl.pallas_call(
        functools.partial(_inproj_kernel, len(x_parts), n_ctx // t),
        grid=(n // t,),
        in_specs=_token_specs(x_parts, t, n_ctx) + [
            mod_spec(MOD_SH1), mod_spec(MOD_SC1),
            const((1, d)),
            const((d, IN_COLS)),
            const((1, QK_W)), const((QK_W, LANES)), const((LANES, QK_W)),
            rope_spec, rope_spec, rope_spec,
        ],
        out_specs=[
            pl.BlockSpec((t, IN_COLS), lambda i: (i, 0)),
            pl.BlockSpec((t, KV_W), lambda i: (jnp.minimum(i, n_ctx // t - 1), 0)),
            pl.BlockSpec((t, KV_W), lambda i: (jnp.minimum(i, n_ctx // t - 1), 0)),
        ],
        out_shape=[
            jax.ShapeDtypeStruct((n, IN_COLS), BF16),
            jax.ShapeDtypeStruct((n_ctx, KV_W), F32),
            jax.ShapeDtypeStruct((n_ctx, KV_W), F32),
        ],
        compiler_params=_cparams("arbitrary"),
    )(*x_parts, mod, mod, norm_g, w_in_bf, gqk, hsum, hbc, *rope)


def _head_halves(x, hkv, low):
    r = pltpu.roll(x, HEAD_DIM, 1)
    rep = jnp.where(low, x, r) if hkv == 0 else jnp.where(low, r, x)
    return jnp.where(low, rep, 0.0).astype(BF16), jnp.where(low, 0.0, rep).astype(BF16)


def _make_attn_kernel(has_ctx, group, seq_len):
    def kern(*refs):
        if has_ctx:
            q_ref, k_ref, v_ref, ck_ref, cv_ref, o_ref = refs
        else:
            q_ref, k_ref, v_ref, o_ref = refs
        low = lax.broadcasted_iota(I32, (1, LANES), 1) < HEAD_DIM
        scale = HEAD_DIM ** -0.5
        for g in range(group):
            qrows = slice(g * seq_len, (g + 1) * seq_len) if group > 1 else slice(None)
            k = k_ref[qrows, :].astype(F32)
            v = v_ref[qrows, :].astype(F32)
            if has_ctx:
                ck = ck_ref[0, 0]
                cv = cv_ref[0, 0]
            for hkv in range(N_KV_HEADS):
                kh = _head_halves(k, hkv, low)
                vh = _head_halves(v, hkv, low)
                if has_ctx:
                    ckh = _head_halves(ck, hkv, low)
                    cvh = _head_halves(cv, hkv, low)
                for c in range(2):
                    col = hkv * 2 * LANES + c * LANES
                    qc = q_ref[qrows, col:col + LANES] * scale
                    o_c = None
                    for j in range(2):
                        s = _dot_nt(qc, kh[j])
                        m = jnp.max(s, axis=-1, keepdims=True)
                        if has_ctx:
                            s2 = _dot_nt(qc, ckh[j])
                            m = jnp.maximum(m, jnp.max(s2, axis=-1, keepdims=True))
                        p = jnp.exp(s - m)
                        l = jnp.sum(p, axis=-1, keepdims=True)
                        o = _dot(p.astype(BF16), vh[j])
                        if has_ctx:
                            p2 = jnp.exp(s2 - m)
                            l = l + jnp.sum(p2, axis=-1, keepdims=True)
                            o = o + _dot(p2.astype(BF16), cvh[j])
                        o = o / l
                        o_c = o if o_c is None else o_c + o
                    o_ref[qrows, col:col + LANES] = o_c.astype(BF16)
    return kern


def _attention(p, n_seq, seq_len, row0, ctx_kv, layer):
    has_ctx = ctx_kv is not None
    qb = min(seq_len, ATTN_QB)
    nq = seq_len // qb
    group = _seq_group(n_seq, seq_len, row0) if (nq == 1 and not has_ctx) else 1
    rows = group * seq_len
    qrows = group * qb
    qrow0 = row0 // qrows
    srow0 = row0 // rows
    in_specs = [
        pl.BlockSpec((qrows, ATTN_W), lambda b, i: (qrow0 + b * nq + i, 0)),
        pl.BlockSpec((rows, KV_W), lambda b, i: (srow0 + b, ATTN_W // KV_W)),
        pl.BlockSpec((rows, KV_W), lambda b, i: (srow0 + b, OFF_V // KV_W)),
    ]
    args = [p, p, p]
    if has_ctx:
        ck, cv = ctx_kv
        past = ck.shape[2]
        spec = pl.BlockSpec((1, 1, past, KV_W), lambda b, i: (b, layer, 0, 0))
        in_specs += [spec, spec]
        args += [ck, cv]
    return pl.pallas_call(
        _make_attn_kernel(has_ctx, group, seq_len),
        grid=(n_seq // group, nq),
        in_specs=in_specs,
        out_specs=pl.BlockSpec((qrows, ATTN_W), lambda b, i: (b * nq + i, 0)),
        out_shape=jax.ShapeDtypeStruct((n_seq * seq_len, ATTN_W), BF16),
        compiler_params=_cparams("arbitrary", "arbitrary"),
    )(*args)


def _make_conv_kernel(seq_len, group):
    shifted_rows = seq_len + 2 * CONV_HALO - SUBLANES

    def kern(a_ref, g_ref, w_ref, b_ref, lng_ref, lnb_ref, o_ref, zp_ref, zs_ref):
        zero = jnp.zeros((CONV_HALO, CONV_CH), F32)
        w = w_ref[...]
        bias = b_ref[...]
        for q in range(group):
            r0 = q * seq_len
            zp_ref[q, 0:CONV_HALO, :] = zero
            zp_ref[q, CONV_HALO + seq_len:2 * CONV_HALO + seq_len, :] = zero
            zp_ref[q, CONV_HALO:CONV_HALO + seq_len, :] = (
                a_ref[r0:r0 + seq_len, :].astype(F32) * jax.nn.sigmoid(g_ref[r0:r0 + seq_len, :].astype(F32)))
            for s in range(1, SUBLANES):
                zs_ref[q, s] = zp_ref[q, s:s + shifted_rows, :]
            for c in range(seq_len // CONV_ROWS):
                base = c * CONV_ROWS + CONV_HALO - CONV_PAD
                acc = jnp.zeros((CONV_ROWS, CONV_CH), F32) + bias
                for j in range(CONV_K):
                    shift = (base + j) % SUBLANES
                    row = base + j - shift
                    if shift == 0:
                        tap = zp_ref[q, row:row + CONV_ROWS, :]
                    else:
                        tap = zs_ref[q, shift, row:row + CONV_ROWS, :]
                    acc = acc + tap * w[j:j + 1, :]
                mu = jnp.mean(acc, axis=-1, keepdims=True)
                dlt = acc - mu
                var = jnp.mean(dlt * dlt, axis=-1, keepdims=True)
                y = dlt * lax.rsqrt(var + LN_EPS) * lng_ref[...] + lnb_ref[...]
                o_ref[r0 + c * CONV_ROWS:r0 + (c + 1) * CONV_ROWS, :] = (y * jax.nn.sigmoid(y)).astype(BF16)
    return kern


def _seq_group(n_seq, seq_len, row0):
    group = max(1, MIXER_ROWS // seq_len)
    if n_seq % group or row0 % (group * seq_len):
        group = 1
    return group


def _conv(p, n_seq, seq_len, row0, w, b, lng, lnb):
    group = _seq_group(n_seq, seq_len, row0)
    rows = group * seq_len
    srow0 = row0 // rows
    const = lambda shape: pl.BlockSpec(shape, lambda s: (0,) * len(shape))
    return pl.pallas_call(
        _make_conv_kernel(seq_len, group),
        grid=(n_seq // group,),
        in_specs=[
            pl.BlockSpec((rows, CONV_CH), lambda s: (srow0 + s, OFF_CONV // CONV_CH)),
            pl.BlockSpec((rows, CONV_CH), lambda s: (srow0 + s, OFF_CONV // CONV_CH + 1)),
            const((CONV_K, CONV_CH)), const((1, CONV_CH)), const((1, CONV_CH)), const((1, CONV_CH)),
        ],
        out_specs=pl.BlockSpec((rows, CONV_CH), lambda s: (s, 0)),
        out_shape=jax.ShapeDtypeStruct((n_seq * seq_len, CONV_CH), BF16),
        scratch_shapes=[pltpu.VMEM((group, seq_len + 2 * CONV_HALO, CONV_CH), F32),
                        pltpu.VMEM((group, SUBLANES, seq_len + 2 * CONV_HALO - SUBLANES, CONV_CH), F32)],
        compiler_params=_cparams("arbitrary"),
    )(p, p, w, b, lng, lnb)


def _make_ret_kernel(seq_len, has_init, group):
    qb = min(seq_len, RET_QB)
    nq = seq_len // qb
    scale = HEAD_DIM ** -0.5

    def kern(*refs):
        if has_init:
            lg_ref, q_ref, k_ref, v_ref, g_ref, gn_ref, gm_ref, r0_ref, o_ref, st_ref = refs
        else:
            lg_ref, q_ref, k_ref, v_ref, g_ref, gn_ref, gm_ref, o_ref, st_ref = refs
        low = lax.broadcasted_iota(I32, (1, LANES), 1) < HEAD_DIM
        pos = lax.broadcasted_iota(I32, (seq_len, 1), 0).astype(F32)
        qpos = lax.broadcasted_iota(I32, (qb, 1), 0).astype(F32)
        kpos = lax.broadcasted_iota(I32, (1, seq_len), 1).astype(F32)
        gm = gm_ref[...]
        for c in range(2):
            cs = slice(c * LANES, (c + 1) * LANES)
            lgf = jnp.where(low, lg_ref[0, 2 * c], lg_ref[0, 2 * c + 1])
            lgb = jnp.where(low, lg_ref[1, 2 * c], lg_ref[1, 2 * c + 1])
            zeta_f = jnp.exp((seq_len - 1.0 - pos) * lgf)
            zeta_b = jnp.exp(pos * lgb)
            decs = []
            for j in range(2):
                lf = lg_ref[0, 2 * c + j]
                lb = lg_ref[1, 2 * c + j]
                row = []
                for i in range(nq):
                    diff = (qpos + float(i * qb)) - kpos
                    dec = jnp.exp(jnp.where(diff >= 0, diff * lf, -diff * lb))
                    row.append(dec * jnp.where(diff == 0, 2.0 * scale, scale))
                decs.append(row)
            for q in range(group):
                rs = slice(q * seq_len, (q + 1) * seq_len)
                qc = q_ref[rs, cs]
                kc = k_ref[rs, cs]
                vc = v_ref[rs, cs]
                kf = kc.astype(F32) * scale
                for d, zeta, lgd in ((0, zeta_f, lgf), (1, zeta_b, lgb)):
                    st = _dot_tn((kf * zeta).astype(BF16), vc)
                    if has_init:
                        st = st + r0_ref[q, d, c] * jnp.exp(seq_len * lgd)
                    st_ref[q, d, c] = st
                y_blocks = [None] * nq
                for j in range(2):
                    sel = low if j == 0 else jnp.logical_not(low)
                    kh = jnp.where(sel, kc, jnp.zeros_like(kc))
                    vh = jnp.where(sel, vc, jnp.zeros_like(vc))
                    for i in range(nq):
                        s = _dot_nt(qc[i * qb:(i + 1) * qb], kh)
                        y = _dot((s * decs[j][i]).astype(BF16), vh)
                        y_blocks[i] = y if y_blocks[i] is None else y_blocks[i] + y
                y = jnp.concatenate(y_blocks, axis=0) if nq > 1 else y_blocks[0]
                if has_init:
                    xi_f = jnp.exp((pos + 1.0) * lgf)
                    xi_b = jnp.exp((seq_len - pos) * lgb)
                    y = y + _dot(qc, r0_ref[q, 0, c].astype(BF16)) * xi_f
                    y = y + _dot(qc, r0_ref[q, 1, c].astype(BF16)) * xi_b
                y_hi, y_lo = _split_bf16(y)
                mu = _dot(y_hi, gm) + _dot(y_lo, gm)
                dlt = y - mu
                var = _dot((dlt * dlt).astype(BF16), gm)
                yn = dlt * lax.rsqrt(var + LN_EPS) * gn_ref[:, cs]
                gate = g_ref[rs, cs].astype(F32)
                o_ref[rs, cs] = (gate * jax.nn.sigmoid(gate) * yn).astype(BF16)
    return kern


def _retention(p, n_seq, seq_len, row0, log_g, gn_g, gmat, r0):
    group = _seq_group(n_seq, seq_len, row0)
    rows = group * seq_len
    srow0 = row0 // rows
    has_init = r0 is not None
    col = OFF_RET // RET_W
    in_specs = [pl.BlockSpec(memory_space=pltpu.SMEM)]
    in_specs += [pl.BlockSpec((rows, RET_W), functools.partial(lambda s, j: (srow0 + s, col + j), j=j))
                 for j in range(4)]
    in_specs += [pl.BlockSpec((1, RET_W), lambda s: (0, 0)), pl.BlockSpec((LANES, LANES), lambda s: (0, 0))]
    args = [log_g, p, p, p, p, gn_g, gmat]
    st_spec = pl.BlockSpec((group, 2, 2, LANES, LANES), lambda s: (s, 0, 0, 0, 0))
    if has_init:
        in_specs.append(st_spec)
        args.append(r0)
    return pl.pallas_call(
        _make_ret_kernel(seq_len, has_init, group),
        grid=(n_seq // group,),
        in_specs=in_specs,
        out_specs=[pl.BlockSpec((rows, RET_W), lambda s: (s, 0)), st_spec],
        out_shape=[jax.ShapeDtypeStruct((n_seq * seq_len, RET_W), BF16),
                   jax.ShapeDtypeStruct((n_seq, 2, 2, LANES, LANES), F32)],
        compiler_params=_cparams("arbitrary"),
    )(*args)


def _outproj_kernel(n_src, ctx_tiles, *refs):
    x_refs = refs[:n_src]
    a_refs, c_refs, r_refs = refs[n_src:n_src + 2], refs[n_src + 2:n_src + 4], refs[n_src + 4:n_src + 6]
    (g1_ref, sc_ref, sh_ref, n2_ref, wo_ref, rwt_ref, rb_ref,
     x1_ref, h2_ref, te_ref, tg_ref) = refs[n_src + 6:]
    mixed = (_dot(_token_tile(a_refs, ctx_tiles), wo_ref[0:ATTN_W, :])
             + _dot(_token_tile(c_refs, ctx_tiles), wo_ref[ATTN_W:ATTN_W + CONV_CH, :])
             + _dot(_token_tile(r_refs, ctx_tiles), wo_ref[ATTN_W + CONV_CH:, :]))
    x1 = _token_tile(x_refs, ctx_tiles) + g1_ref[0] * mixed
    x1_ref[...] = x1
    inv = lax.rsqrt(jnp.mean(x1 * x1, axis=-1, keepdims=True) + EPS)
    h2 = (x1 * inv * n2_ref[...]) * (1.0 + sc_ref[0]) + sh_ref[0]
    h2_ref[...] = h2.astype(BF16)
    w_hi, w_lo = _split_bf16(rwt_ref[...])
    h_hi, h_lo = _split_bf16(h2)
    logits = _dot_nt(w_hi, h_hi) + _dot_nt(w_hi, h_lo) + _dot_nt(w_lo, h_hi) + rb_ref[...]
    t = logits.shape[1]
    eidx = lax.broadcasted_iota(I32, (N_EXPERTS, t), 0).astype(F32)
    vals = logits
    tops, idxs = [], []
    for _ in range(TOP_K):
        m = jnp.max(vals, axis=0, keepdims=True)
        idx = jnp.min(jnp.where(vals == m, eidx, float(N_EXPERTS)), axis=0, keepdims=True)
        tops.append(m)
        idxs.append(idx)
        vals = jnp.where(eidx == idx, -jnp.inf, vals)
    es = [jnp.exp(m - tops[0]) for m in tops]
    tot = es[0] + es[1] + es[2] + es[3]
    te_ref[...] = jnp.concatenate(idxs + [jnp.zeros((8 - TOP_K, t), F32)], axis=0).astype(I32)
    tg_ref[...] = jnp.concatenate([e / tot for e in es] + [jnp.zeros((8 - TOP_K, t), F32)], axis=0)


def _outproj(attn, conv, ret, x_parts, mod, norm_g, w_out_bf, rwt, rb, n_ctx, dec_seq):
    n = attn[0].shape[0] + attn[1].shape[0]
    d = x_parts[0].shape[1]
    t = TOK_TILE
    grp = functools.partial(_group_of_tile, tile=t, n_ctx=n_ctx, dec_seq=dec_seq)

    def mod_spec(which):
        return pl.BlockSpec((1, 1, d), lambda i: (grp(i) * 6 + which, 0, 0))

    const = lambda shape: pl.BlockSpec(shape, lambda i: (0,) * len(shape))
    row = lambda w: pl.BlockSpec((t, w), lambda i: (i, 0))
    lane = lambda: pl.BlockSpec((8, t), lambda i: (0, i))
    return pl.pallas_call(
        functools.partial(_outproj_kernel, len(x_parts), n_ctx // t),
        grid=(n // t,),
        in_specs=_token_specs(x_parts, t, n_ctx) + _token_specs(attn, t, n_ctx)
        + _token_specs(conv, t, n_ctx) + _token_specs(ret, t, n_ctx) + [
            mod_spec(MOD_G1), mod_spec(MOD_SC2), mod_spec(MOD_SH2),
            const((1, d)), const((d, d)), const((N_EXPERTS, d)), const((N_EXPERTS, 1))],
        out_specs=[row(d), row(d), lane(), lane()],
        out_shape=[jax.ShapeDtypeStruct((n, d), F32), jax.ShapeDtypeStruct((n, d), BF16),
                   jax.ShapeDtypeStruct((8, n), I32), jax.ShapeDtypeStruct((8, n), F32)],
        compiler_params=_cparams("arbitrary"),
    )(*x_parts, *attn, *conv, *ret, mod, mod, mod, norm_g, w_out_bf, rwt, rb)


def _round_up(x, m):
    return jnp.floor((x + (m - 1.0)) * (1.0 / m)) * m


def _route_kernel(te_ref, u_ref, ltri_ref, lpos_ref, run8_ref, loff_ref, goff_ref, seg_ref, run_ref, start_ref):
    ph = pl.program_id(0)
    i = pl.program_id(1)
    t = MOVE_TILE
    eidx = lax.broadcasted_iota(I32, (N_EXPERTS, t), 0)

    @pl.when(jnp.logical_and(ph == 0, i == 0))
    def _():
        run_ref[...] = jnp.zeros_like(run_ref)

    @pl.when(jnp.logical_and(ph == 1, i == 0))
    def _():
        seg = run_ref[...]
        seg_ref[...] = seg
        start_ref[...] = _dot(ltri_ref[...], _round_up(seg, MOE_BM), precision=lax.Precision.HIGHEST)
        run_ref[...] = jnp.zeros_like(run_ref)

    for s in range(ROUTE_TILES):
        te = te_ref[:, s * t:(s + 1) * t]
        hits = [eidx == te[k:k + 1, :] for k in range(TOP_K)]
        onehot = sum(h.astype(F32) for h in hits)
        run8 = _round_up(jnp.sum(onehot, axis=1, keepdims=True), SUBLANES)

        @pl.when(ph == 0)
        def _(run8=run8):
            run_ref[...] = run_ref[...] + run8

        @pl.when(ph == 1)
        def _(s=s, hits=hits, onehot=onehot, run8=run8):
            before = _dot(onehot.astype(BF16), u_ref[...])
            run8_b = jnp.broadcast_to(run8, (N_EXPERTS, LANES))
            loff = _dot(ltri_ref[...], run8_b, precision=lax.Precision.HIGHEST)
            base = before + loff[:, 0:1]
            rows = [jnp.sum(jnp.where(h, base, 0.0), axis=0, keepdims=True) for h in hits]
            rows.append(jnp.zeros((8 - TOP_K, t), F32))
            lpos_ref[:, s * t:(s + 1) * t] = jnp.concatenate(rows, axis=0).astype(I32)
            run8_ref[s] = run8_b.astype(I32)
            loff_ref[s] = loff.astype(I32)
            goff_ref[s] = (start_ref[...] + run_ref[...]).astype(I32)
            run_ref[...] = run_ref[...] + run8


def _route(top_e, upper, ltri):
    n = top_e.shape[1]
    t = MOVE_TILE
    nt = n // t
    g = ROUTE_TILES
    assert nt % g == 0
    table = pl.BlockSpec((g, N_EXPERTS, LANES), lambda ph, i: (i * ph, 0, 0))
    table_shape = jax.ShapeDtypeStruct((nt, N_EXPERTS, LANES), I32)
    return pl.pallas_call(
        _route_kernel,
        grid=(2, nt // g),
        in_specs=[pl.BlockSpec((8, g * t), lambda ph, i: (0, i)),
                  pl.BlockSpec((t, t), lambda ph, i: (0, 0)),
                  pl.BlockSpec((N_EXPERTS, N_EXPERTS), lambda ph, i: (0, 0))],
        out_specs=[pl.BlockSpec((8, g * t), lambda ph, i: (0, i * ph)), table, table, table,
                   pl.BlockSpec((N_EXPERTS, LANES), lambda ph, i: (0, 0))],
        out_shape=[jax.ShapeDtypeStruct((8, n), I32), table_shape, table_shape, table_shape,
                   jax.ShapeDtypeStruct((N_EXPERTS, LANES), F32)],
        scratch_shapes=[pltpu.VMEM((N_EXPERTS, LANES), F32), pltpu.VMEM((N_EXPERTS, LANES), F32)],
        compiler_params=_cparams("arbitrary", "arbitrary"),
    )(top_e, upper, ltri)


def _for_each_run_chunk(tile, run8_ref, loff_ref, goff_ref, move):
    def pieces(rows, lo, go, sizes, queue):
        for size in sizes:
            hit = (rows & size) != 0

            @pl.when(hit)
            def _(lo=lo, go=go, size=size):
                move(pl.multiple_of(lo, SUBLANES), pl.multiple_of(go, SUBLANES), size, queue)

            lo = lo + jnp.where(hit, size, 0)
            go = go + jnp.where(hit, size, 0)
        return lo, go

    def per_expert(e, queue):
        idx = tile * N_EXPERTS + e
        rows = run8_ref[idx]
        lo, go = pieces(rows, loff_ref[idx], goff_ref[idx], CHUNK_SIZES[RARE_SIZES:], queue)

        @pl.when(rows >= CHUNK_SIZES[RARE_SIZES - 1])
        def _():
            pieces(rows, lo, go, CHUNK_SIZES[:RARE_SIZES], queue)

        return rows

    def per_pair(j, total):
        return total + per_expert(2 * j, 0) + per_expert(2 * j + 1, 1)

    return lax.fori_loop(0, N_EXPERTS // 2, per_pair, jnp.int32(0))


def _wait_rows(total, copy_of_rows):
    size = 1 << (COMPACT_ROWS.bit_length() - 1)
    while size >= SUBLANES:
        @pl.when((total & size) != 0)
        def _(size=size):
            copy_of_rows(size).wait()

        size //= 2


def _make_dispatch_kernel(n_tiles, n_blocks):
    t = MOVE_TILE
    cb = COMPACT_ROWS

    def kern(run8_ref, loff_ref, goff_ref, seg_ref, start_ref, nv_ref, h_ref, lpos_ref, xs_ref,
             buf, zeros_ref, moved_ref, sem, zsem):
        step = pl.program_id(0)
        slot = lax.rem(step, 2)

        def wait_tile(s):
            _wait_rows(moved_ref[s], lambda rows: pltpu.make_async_copy(
                buf.at[s, pl.ds(0, rows)], xs_ref.at[pl.ds(0, rows)], sem.at[s]))

        def for_each_zero_chunk(action):
            def per_expert(e, carry):
                seg = seg_ref[e]
                padlen = (-seg) & (MOE_BM - 1)
                row = start_ref[e] + seg
                for size in CHUNK_SIZES[1:]:
                    hit = (padlen & size) != 0

                    @pl.when(hit)
                    def _(row=row, size=size):
                        dst = xs_ref.at[pl.ds(pl.multiple_of(row, SUBLANES), size)]
                        action(pltpu.make_async_copy(zeros_ref.at[pl.ds(0, size)], dst, zsem))

                    row = row + jnp.where(hit, size, 0)
                return carry

            lax.fori_loop(0, N_EXPERTS, per_expert, 0)

            def per_tail_block(b, carry):
                action(pltpu.make_async_copy(zeros_ref, xs_ref.at[pl.ds(b * MOE_BM, MOE_BM)], zsem))
                return carry

            lax.fori_loop(nv_ref[0], n_blocks, per_tail_block, 0)

        @pl.when(step == 0)
        def _():
            zeros_ref[...] = jnp.zeros_like(zeros_ref)
            for_each_zero_chunk(lambda cp: cp.start())
            for_each_zero_chunk(lambda cp: cp.wait())

        @pl.when(step >= 2)
        def _():
            wait_tile(slot)

        lpos = lpos_ref[...]
        rows = lax.broadcasted_iota(I32, (cb, t), 0)
        onehot = jnp.where(rows == lpos[0:1, :], 1.0, 0.0)
        for k in range(1, TOP_K):
            onehot = onehot + jnp.where(rows == lpos[k:k + 1, :], 1.0, 0.0)
        buf[slot] = _dot(onehot.astype(BF16), h_ref[...])

        def move(lo, go, size, queue):
            pltpu.make_async_copy(buf.at[slot, pl.ds(lo, size)], xs_ref.at[pl.ds(go, size)],
                                  sem.at[slot]).start(priority=queue)

        moved_ref[slot] = _for_each_run_chunk(step, run8_ref, loff_ref, goff_ref, move)

        @pl.when(step == n_tiles - 1)
        def _():
            wait_tile(slot)
            if n_tiles > 1:
                wait_tile(1 - slot)
    return kern


def _dispatch(tables, seg, starts, n_valid, h2, lpos, n_blocks):
    n, d = h2.shape
    t = MOVE_TILE
    return pl.pallas_call(
        _make_dispatch_kernel(n // t, n_blocks),
        grid_spec=pltpu.PrefetchScalarGridSpec(
            num_scalar_prefetch=6,
            grid=(n // t,),
            in_specs=[pl.BlockSpec((t, d), lambda i, *_: (i, 0)),
                      pl.BlockSpec((8, t), lambda i, *_: (0, i))],
            out_specs=pl.BlockSpec(memory_space=pl.ANY),
            scratch_shapes=[pltpu.VMEM((2, COMPACT_ROWS, d), F32), pltpu.VMEM((MOE_BM, d), F32),
                            pltpu.SMEM((2,), I32), pltpu.SemaphoreType.DMA((2,)), pltpu.SemaphoreType.DMA(())],
        ),
        out_shape=jax.ShapeDtypeStruct((n_blocks * MOE_BM, d), F32),
        compiler_params=_cparams("arbitrary"),
    )(*tables, seg, starts, n_valid, h2, lpos)


def _make_expert_kernel(layer):
    def kern(be_ref, bs_ref, nv_ref, ord_ref, nxt_ref, x_ref, wgu_hbm, bgu_ref, wdn_hbm, bdn_ref, y_ref,
             wgu_f, wdn_f, wgu_s, wdn_s, sem):
        i = pl.program_id(0)
        e = be_ref[i]
        prev = be_ref[jnp.maximum(i - 1, 0)]
        new_expert = jnp.logical_or(i == 0, e != prev)
        slot = lax.rem(ord_ref[i], 2)

        def weight_copies(expert, s):
            return (pltpu.make_async_copy(wgu_hbm.at[layer, expert], wgu_f.at[s], sem.at[0, s]),
                    pltpu.make_async_copy(wdn_hbm.at[layer, expert], wdn_f.at[s], sem.at[1, s]))

        @pl.when(i == 0)
        def _():
            for cp in weight_copies(e, slot):
                cp.start()

        @pl.when(new_expert)
        def _():
            for cp in weight_copies(e, slot):
                cp.wait()
            nxt = nxt_ref[i]

            @pl.when(nxt >= 0)
            def _():
                for cp in weight_copies(nxt, 1 - slot):
                    cp.start(priority=1)

            wgu_s[...] = wgu_f[slot].astype(BF16)
            wdn_s[...] = wdn_f[slot].astype(BF16)

        @pl.when(i < nv_ref[0])
        def _():
            x = x_ref[...].astype(BF16)
            gu = _dot(x, wgu_s[...]) + bgu_ref[0, 0]
            gate = jnp.minimum(gu[:, :D_FF], SWIGLU_LIMIT)
            up = jnp.clip(gu[:, D_FF:], -SWIGLU_LIMIT, SWIGLU_LIMIT)
            hdn = (up + 1.0) * (gate * jax.nn.sigmoid(SWIGLU_ALPHA * gate))
            y_ref[...] = _dot(hdn.astype(BF16), wdn_s[...]) + bdn_ref[0, 0]
    return kern


def _experts(blk_e, blk_src, n_valid, blk_ord, blk_next, xs, w_gu, b_gu, w_dn, b_dn, layer, n_blocks):
    d = xs.shape[1]
    depth = w_gu.shape[0]

    def bias(width):
        return pl.BlockSpec((1, 1, 1, width), lambda i, be, *_: (layer, be[i], 0, 0))

    rows = pl.BlockSpec((MOE_BM, d), lambda i, be, bs, *_: (bs[i], 0))
    return pl.pallas_call(
        _make_expert_kernel(layer),
        grid_spec=pltpu.PrefetchScalarGridSpec(
            num_scalar_prefetch=5,
            grid=(n_blocks,),
            in_specs=[rows,
                      pl.BlockSpec(memory_space=pl.ANY), bias(2 * D_FF),
                      pl.BlockSpec(memory_space=pl.ANY), bias(d)],
            out_specs=rows,
            scratch_shapes=[pltpu.VMEM((2, d, 2 * D_FF), F32), pltpu.VMEM((2, D_FF, d), F32),
                            pltpu.VMEM((d, 2 * D_FF), BF16), pltpu.VMEM((D_FF, d), BF16),
                            pltpu.SemaphoreType.DMA((2, 2))],
        ),
        out_shape=jax.ShapeDtypeStruct(xs.shape, xs.dtype),
        input_output_aliases={5: 0},
        compiler_params=_cparams("arbitrary"),
    )(blk_e, blk_src, n_valid, blk_ord, blk_next, xs, w_gu, b_gu.reshape(depth, N_EXPERTS, 1, -1), w_dn,
      b_dn.reshape(depth, N_EXPERTS, 1, -1))


def _make_combine_kernel(n_tiles, ctx_tiles, final):
    t = MOVE_TILE
    cb = COMPACT_ROWS

    def kern(run8_ref, loff_ref, goff_ref, ys_ref, x1_ref, lpt_ref, gt_ref, g2_ref, fg_ref, *rest):
        o_refs = rest[:2] if final else rest[:1]
        ybuf, moved_ref, sem = rest[len(o_refs):]
        step = pl.program_id(0)
        slot = lax.rem(step, 2)

        def fetch(tile, s):
            def move(lo, go, size, queue):
                pltpu.make_async_copy(ys_ref.at[pl.ds(go, size)], ybuf.at[s, pl.ds(lo, size)],
                                      sem.at[s]).start(priority=queue)

            moved_ref[s] = _for_each_run_chunk(tile, run8_ref, loff_ref, goff_ref, move)

        @pl.when(step == 0)
        def _():
            ybuf[...] = jnp.zeros_like(ybuf)
            fetch(0, 0)

        @pl.when(step + 1 < n_tiles)
        def _():
            fetch(step + 1, 1 - slot)

        _wait_rows(moved_ref[slot], lambda rows: pltpu.make_async_copy(
            ys_ref.at[pl.ds(0, rows)], ybuf.at[slot, pl.ds(0, rows)], sem.at[slot]))

        lpos = lpt_ref[...]
        gates = gt_ref[...]
        rows = lax.broadcasted_iota(I32, (cb, t), 0)
        placed = jnp.where(rows == lpos[0:1, :], gates[0:1, :], 0.0)
        for k in range(1, TOP_K):
            placed = placed + jnp.where(rows == lpos[k:k + 1, :], gates[k:k + 1, :], 0.0)
        row_gate = jnp.sum(placed, axis=1, keepdims=True)
        yb = (ybuf[slot] * row_gate).astype(BF16)
        lpt = lpos.astype(F32).T.astype(I32)
        cols = lax.broadcasted_iota(I32, (t, cb), 1)
        unsort = jnp.where(cols == lpt[:, 0:1], 1.0, 0.0)
        for k in range(1, TOP_K):
            unsort = unsort + jnp.where(cols == lpt[:, k:k + 1], 1.0, 0.0)
        y = _dot(unsort.astype(BF16), yb)
        x2 = x1_ref[...] + g2_ref[0] * y
        if not final:
            o_refs[0][...] = x2
        else:
            x2 = x2 * lax.rsqrt(jnp.mean(x2 * x2, axis=-1, keepdims=True) + EPS) * fg_ref[...]

            @pl.when(step < ctx_tiles)
            def _():
                o_refs[0][...] = x2

            @pl.when(step >= ctx_tiles)
            def _():
                o_refs[1][...] = x2
    return kern


def _combine(tables, ys, x1, lpos_t, gates_t, mod, final_g, final, n_ctx, dec_seq):
    n, d = x1.shape
    t = MOVE_TILE
    ctx_tiles = n_ctx // t
    grp = functools.partial(_group_of_tile, tile=t, n_ctx=n_ctx, dec_seq=dec_seq)
    if final:
        out_specs = [pl.BlockSpec((t, d), lambda i, *_: (jnp.minimum(i, ctx_tiles - 1), 0)),
                     pl.BlockSpec((t, d), lambda i, *_: (jnp.maximum(i - ctx_tiles, 0), 0))]
        out_shape = [jax.ShapeDtypeStruct((n_ctx, d), F32), jax.ShapeDtypeStruct((n - n_ctx, d), F32)]
    else:
        out_specs = pl.BlockSpec((t, d), lambda i, *_: (i, 0))
        out_shape = jax.ShapeDtypeStruct((n, d), F32)
    return pl.pallas_call(
        _make_combine_kernel(n // t, ctx_tiles, final),
        grid_spec=pltpu.PrefetchScalarGridSpec(
            num_scalar_prefetch=3,
            grid=(n // t,),
            in_specs=[pl.BlockSpec(memory_space=pl.ANY),
                      pl.BlockSpec((t, d), lambda i, *_: (i, 0)),
                      pl.BlockSpec((8, t), lambda i, *_: (0, i)),
                      pl.BlockSpec((8, t), lambda i, *_: (0, i)),
                      pl.BlockSpec((1, 1, d), lambda i, *_: (grp(i) * 6 + MOD_G2, 0, 0)),
                      pl.BlockSpec((1, d), lambda i, *_: (0, 0))],
            out_specs=out_specs,
            scratch_shapes=[pltpu.VMEM((2, COMPACT_ROWS, d), F32), pltpu.SMEM((2,), I32),
                            pltpu.SemaphoreType.DMA((2,))],
        ),
        out_shape=out_shape,
        compiler_params=_cparams("arbitrary"),
    )(*tables, ys, x1, lpos_t, gates_t, mod, final_g)


def _blockdiag_pairs(s):
    z = jnp.zeros_like(s[..., 0, :, :])
    def pair(a, b):
        return jnp.concatenate([jnp.concatenate([a, z], axis=-1), jnp.concatenate([z, b], axis=-1)], axis=-2)
    return jnp.stack([pair(s[..., 0, :, :], s[..., 1, :, :]), pair(s[..., 2, :, :], s[..., 3, :, :])], axis=-3)


def _diag_blocks(st):
    h = HEAD_DIM
    blocks = [st[:, :, c, j * h:(j + 1) * h, j * h:(j + 1) * h] for c in range(2) for j in range(2)]
    return jnp.stack(blocks, axis=2)


def kernel(x_prompt, x_sample, cache_k, cache_v, state_ret, c, c_ctx, w_mod, b_mod, norm1_g, norm2_g, w_in,
           q_norm_g, k_norm_g, conv_w, conv_b, conv_ln_g, conv_ln_b, ret_decay_logit, ret_gn_g, w_out,
           router_w, router_b, moe_w_gu, moe_b_gu, moe_w_dn, moe_b_dn, final_g):
    batch, seq, d = x_prompt.shape
    dec_batch, dec_seq, _ = x_sample.shape
    depth = w_mod.shape[0]
    past = cache_k.shape[2]
    n_ctx = batch * seq
    n_lat = dec_batch * dec_seq
    n = n_ctx + n_lat
    assert d == D_MODEL and dec_batch == 2
    assert n_ctx % dec_seq == 0 and dec_seq % TOK_TILE == 0 and dec_seq % ATTN_QB == 0
    assert n % (MOVE_TILE * ROUTE_TILES) == 0

    x_parts = (x_prompt.reshape(n_ctx, d), x_sample.reshape(n_lat, d))
    mods = _modulation(jnp.concatenate([c_ctx[None, :], c], axis=0), w_mod, b_mod)
    rope = _rope_tables(dec_seq)

    head_of_col = np.arange(QK_W) // HEAD_DIM
    hsum_np = (head_of_col[:, None] == np.arange(LANES)[None, :]).astype(np.float32)
    hsum = jnp.asarray(hsum_np, BF16)
    hbc = jnp.asarray(hsum_np.T, BF16)
    lane_head = np.arange(LANES) // HEAD_DIM
    gmat = jnp.asarray((lane_head[:, None] == lane_head[None, :]).astype(np.float32) / HEAD_DIM, BF16)
    tt = np.arange(MOVE_TILE)
    upper = jnp.asarray((tt[:, None] < tt[None, :]).astype(np.float32), BF16)
    ee = jnp.arange(N_EXPERTS)
    ltri = jnp.asarray((np.arange(N_EXPERTS)[None, :] < np.arange(N_EXPERTS)[:, None]).astype(np.float32))

    n_tiles = n // MOVE_TILE
    max_rows = n * TOP_K + n_tiles * N_EXPERTS * (SUBLANES - 1) + N_EXPERTS * (MOE_BM - 1)
    n_blocks = -(-max_rows // MOE_BM)
    cache_k2 = cache_k.reshape(dec_batch, depth, past, KV_W)
    cache_v2 = cache_v.reshape(dec_batch, depth, past, KV_W)

    ks_out, vs_out, ss_out = [], [], []
    for l in range(depth):
        mod = mods[l]
        gqk = jnp.concatenate([jnp.tile(q_norm_g[l], N_Q_HEADS), jnp.tile(k_norm_g[l], N_KV_HEADS)])[None, :]
        p, kn, vv = _inproj(x_parts, mod, norm1_g[l][None, :], w_in[l].astype(BF16), gqk, hsum, hbc, rope,
                            n_ctx, dec_seq)
        ks_out.append(kn.reshape(batch, seq, N_KV_HEADS, HEAD_DIM))
        vs_out.append(vv.reshape(batch, seq, N_KV_HEADS, HEAD_DIM))

        attn = (_attention(p, batch, seq, 0, None, l),
                _attention(p, dec_batch, dec_seq, n_ctx, (cache_k2, cache_v2), l))

        cw, cb = conv_w[l], conv_b[l][None, :]
        clg, clb = conv_ln_g[l][None, :], conv_ln_b[l][None, :]
        conv = (_conv(p, batch, seq, 0, cw, cb, clg, clb),
                _conv(p, dec_batch, dec_seq, n_ctx, cw, cb, clg, clb))

        log_g = jax.nn.log_sigmoid(ret_decay_logit[l].astype(F32))
        gn = ret_gn_g[l][None, :]
        ret_ctx, st_ctx = _retention(p, batch, seq, 0, log_g, gn, gmat, None)
        ret_lat, _ = _retention(p, dec_batch, dec_seq, n_ctx, log_g, gn, gmat,
                                _blockdiag_pairs(state_ret[:, l].astype(F32)))
        ret = (ret_ctx, ret_lat)
        ss_out.append(_diag_blocks(st_ctx))

        x1, h2, top_e, top_g = _outproj(attn, conv, ret, x_parts, mod, norm2_g[l][None, :], w_out[l].astype(BF16),
                                        router_w[l].T, router_b[l][:, None], n_ctx, dec_seq)
        lpos, run8, loff, goff, seg = _route(top_e, upper, ltri)
        tables = [tb[:, :, 0].reshape(-1) for tb in (run8, loff, goff)]
        seg = seg[:, 0].astype(I32)
        padded = (seg + MOE_BM - 1) // MOE_BM * MOE_BM
        pad_end = jnp.cumsum(padded)
        n_valid = (pad_end[-1] // MOE_BM).reshape(1)
        blk_src = jnp.minimum(jnp.arange(n_blocks, dtype=I32), n_valid - 1)
        blk_e = jnp.minimum(jnp.sum((pad_end[None, :] <= (blk_src * MOE_BM)[:, None]).astype(I32), axis=1),
                            N_EXPERTS - 1)
        owns = padded > 0
        ord_e = jnp.cumsum(owns.astype(I32)) - 1
        next_e = jnp.sum(jnp.where(owns[None, :] & (ord_e[None, :] == ord_e[:, None] + 1),
                                   ee[None, :] + 1, 0), axis=1).astype(I32) - 1

        xs = _dispatch(tables, seg, pad_end - padded, n_valid, h2, lpos, n_blocks)
        of_blk = blk_e[:, None] == ee[None, :]
        per_blk = lambda table: jnp.sum(jnp.where(of_blk, table[None, :], 0), axis=1)
        ys = _experts(blk_e, blk_src, n_valid, per_blk(ord_e), per_blk(next_e), xs,
                      moe_w_gu, moe_b_gu, moe_w_dn, moe_b_dn, l, n_blocks)
        out = _combine(tables, ys, x1, lpos, top_g, mod, final_g[None, :], l == depth - 1, n_ctx, dec_seq)
        x_parts = (out,)

    y_prompt = out[0].reshape(batch, seq, d)
    y_sample = out[1].reshape(dec_batch, dec_seq, d)
    return (y_prompt, y_sample, jnp.stack(ks_out, axis=1), jnp.stack(vs_out, axis=1),
            jnp.stack(ss_out, axis=1))
```

```python
import functools

import numpy as np
import jax
import jax.numpy as jnp
from jax import lax
from jax.experimental import pallas as pl
from jax.experimental.pallas import tpu as pltpu

F32 = jnp.float32
BF16 = jnp.bfloat16
I32 = jnp.int32

D_MODEL = 1024
GRID_W = 64
HEAD_DIM = 64
N_Q_HEADS = 8
N_KV_HEADS = 2
ATTN_W = N_Q_HEADS * HEAD_DIM
KV_W = N_KV_HEADS * HEAD_DIM
QK_W = ATTN_W + KV_W
CONV_CH = 256
CONV_K = 31
CONV_PAD = CONV_K // 2
CONV_HALO = 16
N_RET_HEADS = 4
RET_W = 256
OFF_V = QK_W
OFF_CONV = OFF_V + KV_W
OFF_RET = OFF_CONV + 2 * CONV_CH
IN_COLS = OFF_RET + 4 * RET_W
ROPE_HALF = HEAD_DIM // 2
ROPE_THETA = 10000.0
N_EXPERTS = 32
TOP_K = 4
D_FF = D_MODEL
SWIGLU_LIMIT = 7.0
SWIGLU_ALPHA = 1.702
EPS = 1e-6
LN_EPS = 1e-5

LANES = 128
TOK_TILE = 1024
ATTN_QB = 512
MIXER_ROWS = 1024
RET_ROWS = 2048
RET_QB = 256
CONV_ROWS = 64
MOE_BM = 256
MOVE_TILE = 256
ROUTE_TILES = 4
SUBLANES = 8
COMPACT_ROWS = 1280
assert COMPACT_ROWS >= MOVE_TILE * TOP_K + N_EXPERTS * (SUBLANES - 1) and COMPACT_ROWS % MOE_BM == 0
CHUNK_SIZES = tuple(MOVE_TILE >> s for s in range(6))
RARE_SIZES = 3
VMEM_LIMIT = 56 * 1024 * 1024

MOD_SH1, MOD_SC1, MOD_G1, MOD_SH2, MOD_SC2, MOD_G2 = range(6)


def _cparams(*sem):
    return pltpu.CompilerParams(dimension_semantics=sem, vmem_limit_bytes=VMEM_LIMIT)


def _dot(a, b, **kw):
    return jnp.dot(a, b, preferred_element_type=F32, **kw)


def _dot_nt(a, b, **kw):
    return lax.dot_general(a, b, (((1,), (1,)), ((), ())), preferred_element_type=F32, **kw)


def _dot_tn(a, b, **kw):
    return lax.dot_general(a, b, (((0,), (0,)), ((), ())), preferred_element_type=F32, **kw)


def _split_bf16(x):
    hi = x.astype(BF16)
    lo = (x - hi.astype(F32)).astype(BF16)
    return hi, lo


MOD_TN = 1536


def _mod_kernel(ct_ref, w_ref, b_ref, o_ref):
    s = ct_ref[...]
    s = s * jax.nn.sigmoid(s)
    w = w_ref[0]
    rows = [jnp.sum(w * s[:, r:r + 1], axis=0, keepdims=True) for r in range(3)]
    rows.append(jnp.zeros((5, w.shape[1]), F32))
    o_ref[0] = jnp.concatenate(rows, axis=0) + b_ref[0]


def _modulation(cvec3, w_mod, b_mod):
    depth, d, cols = w_mod.shape
    ct = jnp.zeros((d, 8), F32).at[:, :3].set(cvec3.T)
    out = pl.pallas_call(
        _mod_kernel,
        grid=(depth, cols // MOD_TN),
        in_specs=[
            pl.BlockSpec((d, 8), lambda l, j: (0, 0)),
            pl.BlockSpec((1, d, MOD_TN), lambda l, j: (l, 0, j)),
            pl.BlockSpec((1, 1, MOD_TN), lambda l, j: (l, 0, j)),
        ],
        out_specs=pl.BlockSpec((1, 8, MOD_TN), lambda l, j: (l, 0, j)),
        out_shape=jax.ShapeDtypeStruct((depth, 8, cols), F32),
        compiler_params=_cparams("arbitrary", "arbitrary"),
    )(ct, w_mod, b_mod.reshape(depth, 1, cols))
    return out[:, :3].reshape(depth, 3 * 6, 1, d)


def _token_specs(parts, tile, n_ctx):
    d = parts[0].shape[1]
    if len(parts) == 1:
        return [pl.BlockSpec((tile, d), lambda i, *_: (i, 0))]
    ctx_tiles = n_ctx // tile
    return [pl.BlockSpec((tile, d), lambda i, *_: (jnp.minimum(i, ctx_tiles - 1), 0)),
            pl.BlockSpec((tile, d), lambda i, *_: (jnp.maximum(i - ctx_tiles, 0), 0))]


def _token_tile(refs, ctx_tiles):
    if len(refs) == 1:
        return refs[0][...]
    return jnp.where(pl.program_id(0) < ctx_tiles, refs[0][...], refs[1][...])


def _inproj_kernel(n_src, ctx_tiles, *refs):
    x_refs = refs[:n_src]
    (sh_ref, sc_ref, g_ref, w_ref, gqk_ref, hsum_ref, hbc_ref, cos_ref, sa_ref, sb_ref,
     p_ref, k_ref, v_ref) = refs[n_src:]
    x = _token_tile(x_refs, ctx_tiles)
    inv = lax.rsqrt(jnp.mean(x * x, axis=-1, keepdims=True) + EPS)
    h = (x * inv * g_ref[...]) * (1.0 + sc_ref[0]) + sh_ref[0]
    acc = _dot(h.astype(BF16), w_ref[...])
    qk = acc[:, :QK_W]
    ss = _dot((qk * qk).astype(BF16), hsum_ref[...])
    r = lax.rsqrt(ss * (1.0 / HEAD_DIM) + EPS)
    r_hi, r_lo = _split_bf16(r)
    rb = _dot(r_hi, hbc_ref[...]) + _dot(r_lo, hbc_ref[...])
    qkn = qk * rb * gqk_ref[...]
    cos = cos_ref[...]
    sa = sa_ref[...]
    sb = sb_ref[...]
    for j in range(QK_W // LANES):
        c = qkn[:, j * LANES:(j + 1) * LANES]
        up = pltpu.roll(c, LANES - ROPE_HALF // 2, 1)
        dn = pltpu.roll(c, ROPE_HALF // 2, 1)
        p_ref[:, j * LANES:(j + 1) * LANES] = (c * cos + up * sa + dn * sb).astype(BF16)
    p_ref[:, QK_W:] = acc[:, QK_W:].astype(BF16)

    @pl.when(pl.program_id(0) < ctx_tiles)
    def _():
        k_ref[...] = qkn[:, ATTN_W:QK_W]
        v_ref[...] = acc[:, OFF_V:OFF_CONV]


def _rope_tables(dec_seq):
    f32 = np.float32
    rows = dec_seq // GRID_W
    row = np.repeat(np.arange(rows), GRID_W).astype(f32)
    col = np.tile(np.arange(GRID_W), rows).astype(f32)
    inv = (f32(1.0) / (f32(ROPE_THETA) ** (np.arange(0, ROPE_HALF, 2).astype(f32) / f32(ROPE_HALF)))).astype(f32)
    ar = row[:, None] * inv[None, :]
    ac = col[:, None] * inv[None, :]
    cos = np.concatenate([np.cos(ar), np.cos(ar), np.cos(ac), np.cos(ac)], axis=-1)
    sin = np.concatenate([np.sin(ar), np.sin(ar), np.sin(ac), np.sin(ac)], axis=-1)
    first = (np.arange(HEAD_DIM) % ROPE_HALF) < ROPE_HALF // 2
    sa = np.where(first[None, :], -sin, 0.0)
    sb = np.where(first[None, :], 0.0, sin)
    def table(t, ident):
        t = np.concatenate([np.full((TOK_TILE, HEAD_DIM), ident, f32), t.astype(f32)], axis=0)
        return jnp.asarray(np.tile(t, (1, LANES // HEAD_DIM)))
    return table(cos, 1.0), table(sa, 0.0), table(sb, 0.0)


def _group_of_tile(i, tile, n_ctx, dec_seq):
    tok = i * tile
    return jnp.where(tok < n_ctx, 0, 1 + (tok - n_ctx) // dec_seq)


def _inproj(x_parts, mod, norm_g, w_in_bf, gqk, hsum, hbc, rope, n_ctx, dec_seq):
    n = sum(part.shape[0] for part in x_parts)
    d = x_parts[0].shape[1]
    t = TOK_TILE
    grp = functools.partial(_group_of_tile, tile=t, n_ctx=n_ctx, dec_seq=dec_seq)

    def mod_spec(which):
        return pl.BlockSpec((1, 1, d), lambda i: (grp(i) * 6 + which, 0, 0))

    def rope_idx(i):
        tok = i * t
        return (jnp.where(tok < n_ctx, 0, 1 + ((tok - n_ctx) % dec_seq) // t), 0)

    rope_spec = pl.BlockSpec((t, LANES), rope_idx)
    const = lambda shape: pl.BlockSpec(shape, lambda i: (0,) * len(shape))
    return pl.pallas_call(
        functools.partial(_inproj_kernel, len(x_parts), n_ctx // t),
        grid=(n // t,),
        in_specs=_token_specs(x_parts, t, n_ctx) + [
            mod_spec(MOD_SH1), mod_spec(MOD_SC1),
            const((1, d)),
            const((d, IN_COLS)),
            const((1, QK_W)), const((QK_W, LANES)), const((LANES, QK_W)),
            rope_spec, rope_spec, rope_spec,
        ],
        out_specs=[
            pl.BlockSpec((t, IN_COLS), lambda i: (i, 0)),
            pl.BlockSpec((t, KV_W), lambda i: (jnp.minimum(i, n_ctx // t - 1), 0)),
            pl.BlockSpec((t, KV_W), lambda i: (jnp.minimum(i, n_ctx // t - 1), 0)),
        ],
        out_shape=[
            jax.ShapeDtypeStruct((n, IN_COLS), BF16),
            jax.ShapeDtypeStruct((n_ctx, KV_W), F32),
            jax.ShapeDtypeStruct((n_ctx, KV_W), F32),
        ],
        compiler_params=_cparams("arbitrary"),
    )(*x_parts, mod, mod, norm_g, w_in_bf, gqk, hsum, hbc, *rope)


def _head_halves(x, hkv, low):
    r = pltpu.roll(x, HEAD_DIM, 1)
    rep = jnp.where(low, x, r) if hkv == 0 else jnp.where(low, r, x)
    return jnp.where(low, rep, 0.0).astype(BF16), jnp.where(low, 0.0, rep).astype(BF16)


def _make_attn_kernel(has_ctx, group, seq_len):
    def kern(*refs):
        if has_ctx:
            q_ref, k_ref, v_ref, ck_ref, cv_ref, o_ref = refs
        else:
            q_ref, k_ref, v_ref, o_ref = refs
        low = lax.broadcasted_iota(I32, (1, LANES), 1) < HEAD_DIM
        scale = HEAD_DIM ** -0.5
        for g in range(group):
            qrows = slice(g * seq_len, (g + 1) * seq_len) if group > 1 else slice(None)
            k = k_ref[qrows, :].astype(F32)
            v = v_ref[qrows, :].astype(F32)
            if has_ctx:
                ck = ck_ref[0, 0]
                cv = cv_ref[0, 0]
            for hkv in range(N_KV_HEADS):
                kh = _head_halves(k, hkv, low)
                vh = _head_halves(v, hkv, low)
                if has_ctx:
                    ckh = _head_halves(ck, hkv, low)
                    cvh = _head_halves(cv, hkv, low)
                for c in range(2):
                    col = hkv * 2 * LANES + c * LANES
                    qc = q_ref[qrows, col:col + LANES] * scale
                    o_c = None
                    for j in range(2):
                        s = _dot_nt(qc, kh[j])
                        m = jnp.max(s, axis=-1, keepdims=True)
                        if has_ctx:
                            s2 = _dot_nt(qc, ckh[j])
                            m = jnp.maximum(m, jnp.max(s2, axis=-1, keepdims=True))
                        p = jnp.exp(s - m)
                        l = jnp.sum(p, axis=-1, keepdims=True)
                        o = _dot(p.astype(BF16), vh[j])
                        if has_ctx:
                            p2 = jnp.exp(s2 - m)
                            l = l + jnp.sum(p2, axis=-1, keepdims=True)
                            o = o + _dot(p2.astype(BF16), cvh[j])
                        o = o / l
                        o_c = o if o_c is None else o_c + o
                    o_ref[qrows, col:col + LANES] = o_c.astype(BF16)
    return kern


def _attention(p, n_seq, seq_len, row0, ctx_kv, layer):
    has_ctx = ctx_kv is not None
    qb = min(seq_len, ATTN_QB)
    nq = seq_len // qb
    group = _seq_group(n_seq, seq_len, row0) if (nq == 1 and not has_ctx) else 1
    rows = group * seq_len
    qrows = group * qb
    qrow0 = row0 // qrows
    srow0 = row0 // rows
    in_specs = [
        pl.BlockSpec((qrows, ATTN_W), lambda b, i: (qrow0 + b * nq + i, 0)),
        pl.BlockSpec((rows, KV_W), lambda b, i: (srow0 + b, ATTN_W // KV_W)),
        pl.BlockSpec((rows, KV_W), lambda b, i: (srow0 + b, OFF_V // KV_W)),
    ]
    args = [p, p, p]
    if has_ctx:
        ck, cv = ctx_kv
        past = ck.shape[2]
        spec = pl.BlockSpec((1, 1, past, KV_W), lambda b, i: (b, layer, 0, 0))
        in_specs += [spec, spec]
        args += [ck, cv]
    return pl.pallas_call(
        _make_attn_kernel(has_ctx, group, seq_len),
        grid=(n_seq // group, nq),
        in_specs=in_specs,
        out_specs=pl.BlockSpec((qrows, ATTN_W), lambda b, i: (b * nq + i, 0)),
        out_shape=jax.ShapeDtypeStruct((n_seq * seq_len, ATTN_W), BF16),
        compiler_params=_cparams("arbitrary", "arbitrary"),
    )(*args)


def _make_conv_kernel(seq_len, group):
    shifted_rows = seq_len + 2 * CONV_HALO - SUBLANES

    def kern(a_ref, g_ref, w_ref, b_ref, lng_ref, lnb_ref, o_ref, zp_ref, zs_ref):
        zero = jnp.zeros((CONV_HALO, CONV_CH), F32)
        w = w_ref[...]
        bias = b_ref[...]
        for q in range(group):
            r0 = q * seq_len
            zp_ref[q, 0:CONV_HALO, :] = zero
            zp_ref[q, CONV_HALO + seq_len:2 * CONV_HALO + seq_len, :] = zero
            zp_ref[q, CONV_HALO:CONV_HALO + seq_len, :] = (
                a_ref[r0:r0 + seq_len, :].astype(F32) * jax.nn.sigmoid(g_ref[r0:r0 + seq_len, :].astype(F32)))
            for s in range(1, SUBLANES):
                zs_ref[q, s] = zp_ref[q, s:s + shifted_rows, :]
            for c in range(seq_len // CONV_ROWS):
                base = c * CONV_ROWS + CONV_HALO - CONV_PAD
                acc = jnp.zeros((CONV_ROWS, CONV_CH), F32) + bias
                for j in range(CONV_K):
                    shift = (base + j) % SUBLANES
                    row = base + j - shift
                    if shift == 0:
                        tap = zp_ref[q, row:row + CONV_ROWS, :]
                    else:
                        tap = zs_ref[q, shift, row:row + CONV_ROWS, :]
                    acc = acc + tap * w[j:j + 1, :]
                mu = jnp.mean(acc, axis=-1, keepdims=True)
                dlt = acc - mu
                var = jnp.mean(dlt * dlt, axis=-1, keepdims=True)
                y = dlt * lax.rsqrt(var + LN_EPS) * lng_ref[...] + lnb_ref[...]
                o_ref[r0 + c * CONV_ROWS:r0 + (c + 1) * CONV_ROWS, :] = (y * jax.nn.sigmoid(y)).astype(BF16)
    return kern


def _seq_group(n_seq, seq_len, row0, max_rows=MIXER_ROWS):
    group = max(1, max_rows // seq_len)
    if n_seq % group or row0 % (group * seq_len):
        group = 1
    return group


def _conv(p, n_seq, seq_len, row0, w, b, lng, lnb):
    group = _seq_group(n_seq, seq_len, row0)
    rows = group * seq_len
    srow0 = row0 // rows
    const = lambda shape: pl.BlockSpec(shape, lambda s: (0,) * len(shape))
    return pl.pallas_call(
        _make_conv_kernel(seq_len, group),
        grid=(n_seq // group,),
        in_specs=[
            pl.BlockSpec((rows, CONV_CH), lambda s: (srow0 + s, OFF_CONV // CONV_CH)),
            pl.BlockSpec((rows, CONV_CH), lambda s: (srow0 + s, OFF_CONV // CONV_CH + 1)),
            const((CONV_K, CONV_CH)), const((1, CONV_CH)), const((1, CONV_CH)), const((1, CONV_CH)),
        ],
        out_specs=pl.BlockSpec((rows, CONV_CH), lambda s: (s, 0)),
        out_shape=jax.ShapeDtypeStruct((n_seq * seq_len, CONV_CH), BF16),
        scratch_shapes=[pltpu.VMEM((group, seq_len + 2 * CONV_HALO, CONV_CH), F32),
                        pltpu.VMEM((group, SUBLANES, seq_len + 2 * CONV_HALO - SUBLANES, CONV_CH), F32)],
        compiler_params=_cparams("arbitrary"),
    )(p, p, w, b, lng, lnb)


def _make_ret_kernel(seq_len, has_init, group):
    qb = min(seq_len, RET_QB)
    nq = seq_len // qb
    scale = HEAD_DIM ** -0.5

    def kern(*refs):
        if has_init:
            lg_ref, q_ref, k_ref, v_ref, g_ref, gn_ref, gm_ref, r0_ref, o_ref, st_ref = refs
        else:
            lg_ref, q_ref, k_ref, v_ref, g_ref, gn_ref, gm_ref, o_ref, st_ref = refs
        low = lax.broadcasted_iota(I32, (1, LANES), 1) < HEAD_DIM
        pos = lax.broadcasted_iota(I32, (seq_len, 1), 0).astype(F32)
        qpos = lax.broadcasted_iota(I32, (qb, 1), 0).astype(F32)
        kpos = lax.broadcasted_iota(I32, (1, seq_len), 1).astype(F32)
        gm = gm_ref[...]
        for c in range(2):
            cs = slice(c * LANES, (c + 1) * LANES)
            lgf = jnp.where(low, lg_ref[0, 2 * c], lg_ref[0, 2 * c + 1])
            lgb = jnp.where(low, lg_ref[1, 2 * c], lg_ref[1, 2 * c + 1])
            zeta_f = jnp.exp((seq_len - 1.0 - pos) * lgf)
            zeta_b = jnp.exp(pos * lgb)
            decs = []
            for j in range(2):
                lf = lg_ref[0, 2 * c + j]
                lb = lg_ref[1, 2 * c + j]
                row = []
                for i in range(nq):
                    diff = (qpos + float(i * qb)) - kpos
                    dec = jnp.exp(jnp.where(diff >= 0, diff * lf, -diff * lb))
                    row.append(dec * jnp.where(diff == 0, 2.0 * scale, scale))
                decs.append(row)
            for q in range(group):
                rs = slice(q * seq_len, (q + 1) * seq_len)
                qc = q_ref[rs, cs]
                kc = k_ref[rs, cs]
                vc = v_ref[rs, cs]
                kf = kc.astype(F32) * scale
                for d, zeta, lgd in ((0, zeta_f, lgf), (1, zeta_b, lgb)):
                    st = _dot_tn((kf * zeta).astype(BF16), vc)
                    if has_init:
                        st = st + r0_ref[q, d, c] * jnp.exp(seq_len * lgd)
                    st_ref[q, d, c] = st
                y_blocks = [None] * nq
                for j in range(2):
                    sel = low if j == 0 else jnp.logical_not(low)
                    kh = jnp.where(sel, kc, jnp.zeros_like(kc))
                    vh = jnp.where(sel, vc, jnp.zeros_like(vc))
                    for i in range(nq):
                        s = _dot_nt(qc[i * qb:(i + 1) * qb], kh)
                        y = _dot((s * decs[j][i]).astype(BF16), vh)
                        y_blocks[i] = y if y_blocks[i] is None else y_blocks[i] + y
                y = jnp.concatenate(y_blocks, axis=0) if nq > 1 else y_blocks[0]
                if has_init:
                    xi_f = jnp.exp((pos + 1.0) * lgf)
                    xi_b = jnp.exp((seq_len - pos) * lgb)
                    y = y + _dot(qc, r0_ref[q, 0, c].astype(BF16)) * xi_f
                    y = y + _dot(qc, r0_ref[q, 1, c].astype(BF16)) * xi_b
                y_hi, y_lo = _split_bf16(y)
                mu = _dot(y_hi, gm) + _dot(y_lo, gm)
                dlt = y - mu
                var = _dot((dlt * dlt).astype(BF16), gm)
                yn = dlt * lax.rsqrt(var + LN_EPS) * gn_ref[:, cs]
                gate = g_ref[rs, cs].astype(F32)
                o_ref[rs, cs] = (gate * jax.nn.sigmoid(gate) * yn).astype(BF16)
    return kern


def _retention(p, n_seq, seq_len, row0, log_g, gn_g, gmat, r0):
    group = _seq_group(n_seq, seq_len, row0, RET_ROWS)
    rows = group * seq_len
    srow0 = row0 // rows
    has_init = r0 is not None
    col = OFF_RET // RET_W
    in_specs = [pl.BlockSpec(memory_space=pltpu.SMEM)]
    in_specs += [pl.BlockSpec((rows, RET_W), functools.partial(lambda s, j: (srow0 + s, col + j), j=j))
                 for j in range(4)]
    in_specs += [pl.BlockSpec((1, RET_W), lambda s: (0, 0)), pl.BlockSpec((LANES, LANES), lambda s: (0, 0))]
    args = [log_g, p, p, p, p, gn_g, gmat]
    st_spec = pl.BlockSpec((group, 2, 2, LANES, LANES), lambda s: (s, 0, 0, 0, 0))
    if has_init:
        in_specs.append(st_spec)
        args.append(r0)
    return pl.pallas_call(
        _make_ret_kernel(seq_len, has_init, group),
        grid=(n_seq // group,),
        in_specs=in_specs,
        out_specs=[pl.BlockSpec((rows, RET_W), lambda s: (s, 0)), st_spec],
        out_shape=[jax.ShapeDtypeStruct((n_seq * seq_len, RET_W), BF16),
                   jax.ShapeDtypeStruct((n_seq, 2, 2, LANES, LANES), F32)],
        compiler_params=_cparams("arbitrary"),
    )(*args)


def _outproj_kernel(n_src, ctx_tiles, *refs):
    x_refs = refs[:n_src]
    a_refs, c_refs, r_refs = refs[n_src:n_src + 2], refs[n_src + 2:n_src + 4], refs[n_src + 4:n_src + 6]
    (g1_ref, sc_ref, sh_ref, n2_ref, wo_ref, rwt_ref, rb_ref,
     x1_ref, h2_ref, te_ref, tg_ref) = refs[n_src + 6:]
    mixed = (_dot(_token_tile(a_refs, ctx_tiles), wo_ref[0:ATTN_W, :])
             + _dot(_token_tile(c_refs, ctx_tiles), wo_ref[ATTN_W:ATTN_W + CONV_CH, :])
             + _dot(_token_tile(r_refs, ctx_tiles), wo_ref[ATTN_W + CONV_CH:, :]))
    x1 = _token_tile(x_refs, ctx_tiles) + g1_ref[0] * mixed
    x1_ref[...] = x1
    inv = lax.rsqrt(jnp.mean(x1 * x1, axis=-1, keepdims=True) + EPS)
    h2 = (x1 * inv * n2_ref[...]) * (1.0 + sc_ref[0]) + sh_ref[0]
    h2_ref[...] = h2.astype(BF16)
    w_hi, w_lo = _split_bf16(rwt_ref[...])
    h_hi, h_lo = _split_bf16(h2)
    logits = _dot_nt(w_hi, h_hi) + _dot_nt(w_hi, h_lo) + _dot_nt(w_lo, h_hi) + rb_ref[...]
    t = logits.shape[1]
    eidx = lax.broadcasted_iota(I32, (N_EXPERTS, t), 0).astype(F32)
    vals = logits
    tops, idxs = [], []
    for _ in range(TOP_K):
        m = jnp.max(vals, axis=0, keepdims=True)
        idx = jnp.min(jnp.where(vals == m, eidx, float(N_EXPERTS)), axis=0, keepdims=True)
        tops.append(m)
        idxs.append(idx)
        vals = jnp.where(eidx == idx, -jnp.inf, vals)
    es = [jnp.exp(m - tops[0]) for m in tops]
    tot = es[0] + es[1] + es[2] + es[3]
    te_ref[...] = jnp.concatenate(idxs + [jnp.zeros((8 - TOP_K, t), F32)], axis=0).astype(I32)
    tg_ref[...] = jnp.concatenate([e / tot for e in es] + [jnp.zeros((8 - TOP_K, t), F32)], axis=0)


def _outproj(attn, conv, ret, x_parts, mod, norm_g, w_out_bf, rwt, rb, n_ctx, dec_seq):
    n = attn[0].shape[0] + attn[1].shape[0]
    d = x_parts[0].shape[1]
    t = TOK_TILE
    grp = functools.partial(_group_of_tile, tile=t, n_ctx=n_ctx, dec_seq=dec_seq)

    def mod_spec(which):
        return pl.BlockSpec((1, 1, d), lambda i: (grp(i) * 6 + which, 0, 0))

    const = lambda shape: pl.BlockSpec(shape, lambda i: (0,) * len(shape))
    row = lambda w: pl.BlockSpec((t, w), lambda i: (i, 0))
    lane = lambda: pl.BlockSpec((8, t), lambda i: (0, i))
    return pl.pallas_call(
        functools.partial(_outproj_kernel, len(x_parts), n_ctx // t),
        grid=(n // t,),
        in_specs=_token_specs(x_parts, t, n_ctx) + _token_specs(attn, t, n_ctx)
        + _token_specs(conv, t, n_ctx) + _token_specs(ret, t, n_ctx) + [
            mod_spec(MOD_G1), mod_spec(MOD_SC2), mod_spec(MOD_SH2),
            const((1, d)), const((d, d)), const((N_EXPERTS, d)), const((N_EXPERTS, 1))],
        out_specs=[row(d), row(d), lane(), lane()],
        out_shape=[jax.ShapeDtypeStruct((n, d), F32), jax.ShapeDtypeStruct((n, d), BF16),
                   jax.ShapeDtypeStruct((8, n), I32), jax.ShapeDtypeStruct((8, n), F32)],
        compiler_params=_cparams("arbitrary"),
    )(*x_parts, *attn, *conv, *ret, mod, mod, mod, norm_g, w_out_bf, rwt, rb)


def _round_up(x, m):
    return jnp.floor((x + (m - 1.0)) * (1.0 / m)) * m


def _route_kernel(te_ref, u_ref, ltri_ref, lpos_ref, run8_ref, loff_ref, goff_ref, seg_ref, run_ref, start_ref):
    ph = pl.program_id(0)
    i = pl.program_id(1)
    t = MOVE_TILE
    eidx = lax.broadcasted_iota(I32, (N_EXPERTS, t), 0)

    @pl.when(jnp.logical_and(ph == 0, i == 0))
    def _():
        run_ref[...] = jnp.zeros_like(run_ref)

    @pl.when(jnp.logical_and(ph == 1, i == 0))
    def _():
        seg = run_ref[...]
        seg_ref[...] = seg
        start_ref[...] = _dot(ltri_ref[...], _round_up(seg, MOE_BM), precision=lax.Precision.HIGHEST)
        run_ref[...] = jnp.zeros_like(run_ref)

    for s in range(ROUTE_TILES):
        te = te_ref[:, s * t:(s + 1) * t]
        hits = [eidx == te[k:k + 1, :] for k in range(TOP_K)]
        onehot = sum(h.astype(F32) for h in hits)
        run8 = _round_up(jnp.sum(onehot, axis=1, keepdims=True), SUBLANES)

        @pl.when(ph == 0)
        def _(run8=run8):
            run_ref[...] = run_ref[...] + run8

        @pl.when(ph == 1)
        def _(s=s, hits=hits, onehot=onehot, run8=run8):
            before = _dot(onehot.astype(BF16), u_ref[...])
            run8_b = jnp.broadcast_to(run8, (N_EXPERTS, LANES))
            loff = _dot(ltri_ref[...], run8_b, precision=lax.Precision.HIGHEST)
            base = before + loff[:, 0:1]
            rows = [jnp.sum(jnp.where(h, base, 0.0), axis=0, keepdims=True) for h in hits]
            rows.append(jnp.zeros((8 - TOP_K, t), F32))
            lpos_ref[:, s * t:(s + 1) * t] = jnp.concatenate(rows, axis=0).astype(I32)
            run8_ref[s] = run8_b.astype(I32)
            loff_ref[s] = loff.astype(I32)
            goff_ref[s] = (start_ref[...] + run_ref[...]).astype(I32)
            run_ref[...] = run_ref[...] + run8


def _route(top_e, upper, ltri):
    n = top_e.shape[1]
    t = MOVE_TILE
    nt = n // t
    g = ROUTE_TILES
    assert nt % g == 0
    table = pl.BlockSpec((g, N_EXPERTS, LANES), lambda ph, i: (i * ph, 0, 0))
    table_shape = jax.ShapeDtypeStruct((nt, N_EXPERTS, LANES), I32)
    return pl.pallas_call(
        _route_kernel,
        grid=(2, nt // g),
        in_specs=[pl.BlockSpec((8, g * t), lambda ph, i: (0, i)),
                  pl.BlockSpec((t, t), lambda ph, i: (0, 0)),
                  pl.BlockSpec((N_EXPERTS, N_EXPERTS), lambda ph, i: (0, 0))],
        out_specs=[pl.BlockSpec((8, g * t), lambda ph, i: (0, i * ph)), table, table, table,
                   pl.BlockSpec((N_EXPERTS, LANES), lambda ph, i: (0, 0))],
        out_shape=[jax.ShapeDtypeStruct((8, n), I32), table_shape, table_shape, table_shape,
                   jax.ShapeDtypeStruct((N_EXPERTS, LANES), F32)],
        scratch_shapes=[pltpu.VMEM((N_EXPERTS, LANES), F32), pltpu.VMEM((N_EXPERTS, LANES), F32)],
        compiler_params=_cparams("arbitrary", "arbitrary"),
    )(top_e, upper, ltri)


def _for_each_run_chunk(tile, run8_ref, loff_ref, goff_ref, move):
    def pieces(rows, lo, go, sizes, queue):
        for size in sizes:
            hit = (rows & size) != 0

            @pl.when(hit)
            def _(lo=lo, go=go, size=size):
                move(pl.multiple_of(lo, SUBLANES), pl.multiple_of(go, SUBLANES), size, queue)

            lo = lo + jnp.where(hit, size, 0)
            go = go + jnp.where(hit, size, 0)
        return lo, go

    def per_expert(e, queue):
        idx = tile * N_EXPERTS + e
        rows = run8_ref[idx]
        lo, go = pieces(rows, loff_ref[idx], goff_ref[idx], CHUNK_SIZES[RARE_SIZES:], queue)

        @pl.when(rows >= CHUNK_SIZES[RARE_SIZES - 1])
        def _():
            pieces(rows, lo, go, CHUNK_SIZES[:RARE_SIZES], queue)

        return rows

    def per_pair(j, total):
        return total + per_expert(2 * j, 0) + per_expert(2 * j + 1, 1)

    return lax.fori_loop(0, N_EXPERTS // 2, per_pair, jnp.int32(0))


def _wait_rows(total, copy_of_rows):
    size = 1 << (COMPACT_ROWS.bit_length() - 1)
    while size >= SUBLANES:
        @pl.when((total & size) != 0)
        def _(size=size):
            copy_of_rows(size).wait()

        size //= 2


def _make_dispatch_kernel(n_tiles, n_blocks):
    t = MOVE_TILE
    cb = COMPACT_ROWS

    def kern(run8_ref, loff_ref, goff_ref, seg_ref, start_ref, nv_ref, h_ref, lpos_ref, xs_ref,
             buf, zeros_ref, moved_ref, sem, zsem):
        step = pl.program_id(0)
        slot = lax.rem(step, 2)

        def wait_tile(s):
            _wait_rows(moved_ref[s], lambda rows: pltpu.make_async_copy(
                buf.at[s, pl.ds(0, rows)], xs_ref.at[pl.ds(0, rows)], sem.at[s]))

        def for_each_zero_chunk(action):
            def per_expert(e, carry):
                seg = seg_ref[e]
                padlen = (-seg) & (MOE_BM - 1)
                row = start_ref[e] + seg
                for size in CHUNK_SIZES[1:]:
                    hit = (padlen & size) != 0

                    @pl.when(hit)
                    def _(row=row, size=size):
                        dst = xs_ref.at[pl.ds(pl.multiple_of(row, SUBLANES), size)]
                        action(pltpu.make_async_copy(zeros_ref.at[pl.ds(0, size)], dst, zsem))

                    row = row + jnp.where(hit, size, 0)
                return carry

            lax.fori_loop(0, N_EXPERTS, per_expert, 0)

            def per_tail_block(b, carry):
                action(pltpu.make_async_copy(zeros_ref, xs_ref.at[pl.ds(b * MOE_BM, MOE_BM)], zsem))
                return carry

            lax.fori_loop(nv_ref[0], n_blocks, per_tail_block, 0)

        @pl.when(step == 0)
        def _():
            zeros_ref[...] = jnp.zeros_like(zeros_ref)
            for_each_zero_chunk(lambda cp: cp.start())
            for_each_zero_chunk(lambda cp: cp.wait())

        @pl.when(step >= 2)
        def _():
            wait_tile(slot)

        lpos = lpos_ref[...]
        rows = lax.broadcasted_iota(I32, (cb, t), 0)
        onehot = jnp.where(rows == lpos[0:1, :], 1.0, 0.0)
        for k in range(1, TOP_K):
            onehot = onehot + jnp.where(rows == lpos[k:k + 1, :], 1.0, 0.0)
        buf[slot] = _dot(onehot.astype(BF16), h_ref[...])

        def move(lo, go, size, queue):
            pltpu.make_async_copy(buf.at[slot, pl.ds(lo, size)], xs_ref.at[pl.ds(go, size)],
                                  sem.at[slot]).start(priority=queue)

        moved_ref[slot] = _for_each_run_chunk(step, run8_ref, loff_ref, goff_ref, move)

        @pl.when(step == n_tiles - 1)
        def _():
            wait_tile(slot)
            if n_tiles > 1:
                wait_tile(1 - slot)
    return kern


def _dispatch(tables, seg, starts, n_valid, h2, lpos, n_blocks):
    n, d = h2.shape
    t = MOVE_TILE
    return pl.pallas_call(
        _make_dispatch_kernel(n // t, n_blocks),
        grid_spec=pltpu.PrefetchScalarGridSpec(
            num_scalar_prefetch=6,
            grid=(n // t,),
            in_specs=[pl.BlockSpec((t, d), lambda i, *_: (i, 0)),
                      pl.BlockSpec((8, t), lambda i, *_: (0, i))],
            out_specs=pl.BlockSpec(memory_space=pl.ANY),
            scratch_shapes=[pltpu.VMEM((2, COMPACT_ROWS, d), F32), pltpu.VMEM((MOE_BM, d), F32),
                            pltpu.SMEM((2,), I32), pltpu.SemaphoreType.DMA((2,)), pltpu.SemaphoreType.DMA(())],
        ),
        out_shape=jax.ShapeDtypeStruct((n_blocks * MOE_BM, d), F32),
        compiler_params=_cparams("arbitrary"),
    )(*tables, seg, starts, n_valid, h2, lpos)


def _make_expert_kernel(layer):
    def kern(be_ref, bs_ref, nv_ref, ord_ref, nxt_ref, x_ref, wgu_hbm, bgu_ref, wdn_hbm, bdn_ref, y_ref,
             wgu_f, wdn_f, wgu_s, wdn_s, sem):
        i = pl.program_id(0)
        e = be_ref[i]
        prev = be_ref[jnp.maximum(i - 1, 0)]
        new_expert = jnp.logical_or(i == 0, e != prev)
        slot = lax.rem(ord_ref[i], 2)

        def weight_copies(expert, s):
            return (pltpu.make_async_copy(wgu_hbm.at[layer, expert], wgu_f.at[s], sem.at[0, s]),
                    pltpu.make_async_copy(wdn_hbm.at[layer, expert], wdn_f.at[s], sem.at[1, s]))

        @pl.when(i == 0)
        def _():
            for cp in weight_copies(e, slot):
                cp.start()

        @pl.when(new_expert)
        def _():
            for cp in weight_copies(e, slot):
                cp.wait()
            nxt = nxt_ref[i]

            @pl.when(nxt >= 0)
            def _():
                for cp in weight_copies(nxt, 1 - slot):
                    cp.start(priority=1)

            wgu_s[...] = wgu_f[slot].astype(BF16)
            wdn_s[...] = wdn_f[slot].astype(BF16)

        @pl.when(i < nv_ref[0])
        def _():
            x = x_ref[...].astype(BF16)
            gu = _dot(x, wgu_s[...]) + bgu_ref[0, 0]
            gate = jnp.minimum(gu[:, :D_FF], SWIGLU_LIMIT)
            up = jnp.clip(gu[:, D_FF:], -SWIGLU_LIMIT, SWIGLU_LIMIT)
            hdn = (up + 1.0) * (gate * jax.nn.sigmoid(SWIGLU_ALPHA * gate))
            y_ref[...] = _dot(hdn.astype(BF16), wdn_s[...]) + bdn_ref[0, 0]
    return kern


def _experts(blk_e, blk_src, n_valid, blk_ord, blk_next, xs, w_gu, b_gu, w_dn, b_dn, layer, n_blocks):
    d = xs.shape[1]
    depth = w_gu.shape[0]

    def bias(width):
        return pl.BlockSpec((1, 1, 1, width), lambda i, be, *_: (layer, be[i], 0, 0))

    rows = pl.BlockSpec((MOE_BM, d), lambda i, be, bs, *_: (bs[i], 0))
    return pl.pallas_call(
        _make_expert_kernel(layer),
        grid_spec=pltpu.PrefetchScalarGridSpec(
            num_scalar_prefetch=5,
            grid=(n_blocks,),
            in_specs=[rows,
                      pl.BlockSpec(memory_space=pl.ANY), bias(2 * D_FF),
                      pl.BlockSpec(memory_space=pl.ANY), bias(d)],
            out_specs=rows,
            scratch_shapes=[pltpu.VMEM((2, d, 2 * D_FF), F32), pltpu.VMEM((2, D_FF, d), F32),
                            pltpu.VMEM((d, 2 * D_FF), BF16), pltpu.VMEM((D_FF, d), BF16),
                            pltpu.SemaphoreType.DMA((2, 2))],
        ),
        out_shape=jax.ShapeDtypeStruct(xs.shape, xs.dtype),
        input_output_aliases={5: 0},
        compiler_params=_cparams("arbitrary"),
    )(blk_e, blk_src, n_valid, blk_ord, blk_next, xs, w_gu, b_gu.reshape(depth, N_EXPERTS, 1, -1), w_dn,
      b_dn.reshape(depth, N_EXPERTS, 1, -1))


def _make_combine_kernel(n_tiles, ctx_tiles, final):
    t = MOVE_TILE
    cb = COMPACT_ROWS

    def kern(run8_ref, loff_ref, goff_ref, ys_ref, x1_ref, lpt_ref, gt_ref, g2_ref, fg_ref, *rest):
        o_refs = rest[:2] if final else rest[:1]
        ybuf, moved_ref, sem = rest[len(o_refs):]
        step = pl.program_id(0)
        slot = lax.rem(step, 2)

        def fetch(tile, s):
            def move(lo, go, size, queue):
                pltpu.make_async_copy(ys_ref.at[pl.ds(go, size)], ybuf.at[s, pl.ds(lo, size)],
                                      sem.at[s]).start(priority=queue)

            moved_ref[s] = _for_each_run_chunk(tile, run8_ref, loff_ref, goff_ref, move)

        @pl.when(step == 0)
        def _():
            ybuf[...] = jnp.zeros_like(ybuf)
            fetch(0, 0)

        @pl.when(step + 1 < n_tiles)
        def _():
            fetch(step + 1, 1 - slot)

        _wait_rows(moved_ref[slot], lambda rows: pltpu.make_async_copy(
            ys_ref.at[pl.ds(0, rows)], ybuf.at[slot, pl.ds(0, rows)], sem.at[slot]))

        lpos = lpt_ref[...]
        gates = gt_ref[...]
        rows = lax.broadcasted_iota(I32, (cb, t), 0)
        placed = jnp.where(rows == lpos[0:1, :], gates[0:1, :], 0.0)
        for k in range(1, TOP_K):
            placed = placed + jnp.where(rows == lpos[k:k + 1, :], gates[k:k + 1, :], 0.0)
        row_gate = jnp.sum(placed, axis=1, keepdims=True)
        yb = (ybuf[slot] * row_gate).astype(BF16)
        lpt = lpos.astype(F32).T.astype(I32)
        cols = lax.broadcasted_iota(I32, (t, cb), 1)
        unsort = jnp.where(cols == lpt[:, 0:1], 1.0, 0.0)
        for k in range(1, TOP_K):
            unsort = unsort + jnp.where(cols == lpt[:, k:k + 1], 1.0, 0.0)
        y = _dot(unsort.astype(BF16), yb)
        x2 = x1_ref[...] + g2_ref[0] * y
        if not final:
            o_refs[0][...] = x2
        else:
            x2 = x2 * lax.rsqrt(jnp.mean(x2 * x2, axis=-1, keepdims=True) + EPS) * fg_ref[...]

            @pl.when(step < ctx_tiles)
            def _():
                o_refs[0][...] = x2

            @pl.when(step >= ctx_tiles)
            def _():
                o_refs[1][...] = x2
    return kern


def _combine(tables, ys, x1, lpos_t, gates_t, mod, final_g, final, n_ctx, dec_seq):
    n, d = x1.shape
    t = MOVE_TILE
    ctx_tiles = n_ctx // t
    grp = functools.partial(_group_of_tile, tile=t, n_ctx=n_ctx, dec_seq=dec_seq)
    if final:
        out_specs = [pl.BlockSpec((t, d), lambda i, *_: (jnp.minimum(i, ctx_tiles - 1), 0)),
                     pl.BlockSpec((t, d), lambda i, *_: (jnp.maximum(i - ctx_tiles, 0), 0))]
        out_shape = [jax.ShapeDtypeStruct((n_ctx, d), F32), jax.ShapeDtypeStruct((n - n_ctx, d), F32)]
    else:
        out_specs = pl.BlockSpec((t, d), lambda i, *_: (i, 0))
        out_shape = jax.ShapeDtypeStruct((n, d), F32)
    return pl.pallas_call(
        _make_combine_kernel(n // t, ctx_tiles, final),
        grid_spec=pltpu.PrefetchScalarGridSpec(
            num_scalar_prefetch=3,
            grid=(n // t,),
            in_specs=[pl.BlockSpec(memory_space=pl.ANY),
                      pl.BlockSpec((t, d), lambda i, *_: (i, 0)),
                      pl.BlockSpec((8, t), lambda i, *_: (0, i)),
                      pl.BlockSpec((8, t), lambda i, *_: (0, i)),
                      pl.BlockSpec((1, 1, d), lambda i, *_: (grp(i) * 6 + MOD_G2, 0, 0)),
                      pl.BlockSpec((1, d), lambda i, *_: (0, 0))],
            out_specs=out_specs,
            scratch_shapes=[pltpu.VMEM((2, COMPACT_ROWS, d), F32), pltpu.SMEM((2,), I32),
                            pltpu.SemaphoreType.DMA((2,))],
        ),
        out_shape=out_shape,
        compiler_params=_cparams("arbitrary"),
    )(*tables, ys, x1, lpos_t, gates_t, mod, final_g)


def _blockdiag_pairs(s):
    z = jnp.zeros_like(s[..., 0, :, :])
    def pair(a, b):
        return jnp.concatenate([jnp.concatenate([a, z], axis=-1), jnp.concatenate([z, b], axis=-1)], axis=-2)
    return jnp.stack([pair(s[..., 0, :, :], s[..., 1, :, :]), pair(s[..., 2, :, :], s[..., 3, :, :])], axis=-3)


def _diag_blocks(st):
    h = HEAD_DIM
    blocks = [st[:, :, c, j * h:(j + 1) * h, j * h:(j + 1) * h] for c in range(2) for j in range(2)]
    return jnp.stack(blocks, axis=2)


def kernel(x_prompt, x_sample, cache_k, cache_v, state_ret, c, c_ctx, w_mod, b_mod, norm1_g, norm2_g, w_in,
           q_norm_g, k_norm_g, conv_w, conv_b, conv_ln_g, conv_ln_b, ret_decay_logit, ret_gn_g, w_out,
           router_w, router_b, moe_w_gu, moe_b_gu, moe_w_dn, moe_b_dn, final_g):
    batch, seq, d = x_prompt.shape
    dec_batch, dec_seq, _ = x_sample.shape
    depth = w_mod.shape[0]
    past = cache_k.shape[2]
    n_ctx = batch * seq
    n_lat = dec_batch * dec_seq
    n = n_ctx + n_lat
    assert d == D_MODEL and dec_batch == 2
    assert n_ctx % dec_seq == 0 and dec_seq % TOK_TILE == 0 and dec_seq % ATTN_QB == 0
    assert n % (MOVE_TILE * ROUTE_TILES) == 0

    x_parts = (x_prompt.reshape(n_ctx, d), x_sample.reshape(n_lat, d))
    mods = _modulation(jnp.concatenate([c_ctx[None, :], c], axis=0), w_mod, b_mod)
    rope = _rope_tables(dec_seq)

    head_of_col = np.arange(QK_W) // HEAD_DIM
    hsum_np = (head_of_col[:, None] == np.arange(LANES)[None, :]).astype(np.float32)
    hsum = jnp.asarray(hsum_np, BF16)
    hbc = jnp.asarray(hsum_np.T, BF16)
    lane_head = np.arange(LANES) // HEAD_DIM
    gmat = jnp.asarray((lane_head[:, None] == lane_head[None, :]).astype(np.float32) / HEAD_DIM, BF16)
    tt = np.arange(MOVE_TILE)
    upper = jnp.asarray((tt[:, None] < tt[None, :]).astype(np.float32), BF16)
    ee = jnp.arange(N_EXPERTS)
    ltri = jnp.asarray((np.arange(N_EXPERTS)[None, :] < np.arange(N_EXPERTS)[:, None]).astype(np.float32))

    n_tiles = n // MOVE_TILE
    max_rows = n * TOP_K + n_tiles * N_EXPERTS * (SUBLANES - 1) + N_EXPERTS * (MOE_BM - 1)
    n_blocks = -(-max_rows // MOE_BM)
    cache_k2 = cache_k.reshape(dec_batch, depth, past, KV_W)
    cache_v2 = cache_v.reshape(dec_batch, depth, past, KV_W)

    ks_out, vs_out, ss_out = [], [], []
    for l in range(depth):
        mod = mods[l]
        gqk = jnp.concatenate([jnp.tile(q_norm_g[l], N_Q_HEADS), jnp.tile(k_norm_g[l], N_KV_HEADS)])[None, :]
        p, kn, vv = _inproj(x_parts, mod, norm1_g[l][None, :], w_in[l].astype(BF16), gqk, hsum, hbc, rope,
                            n_ctx, dec_seq)
        ks_out.append(kn.reshape(batch, seq, N_KV_HEADS, HEAD_DIM))
        vs_out.append(vv.reshape(batch, seq, N_KV_HEADS, HEAD_DIM))

        attn = (_attention(p, batch, seq, 0, None, l),
                _attention(p, dec_batch, dec_seq, n_ctx, (cache_k2, cache_v2), l))

        cw, cb = conv_w[l], conv_b[l][None, :]
        clg, clb = conv_ln_g[l][None, :], conv_ln_b[l][None, :]
        conv = (_conv(p, batch, seq, 0, cw, cb, clg, clb),
                _conv(p, dec_batch, dec_seq, n_ctx, cw, cb, clg, clb))

        log_g = jax.nn.log_sigmoid(ret_decay_logit[l].astype(F32))
        gn = ret_gn_g[l][None, :]
        ret_ctx, st_ctx = _retention(p, batch, seq, 0, log_g, gn, gmat, None)
        ret_lat, _ = _retention(p, dec_batch, dec_seq, n_ctx, log_g, gn, gmat,
                                _blockdiag_pairs(state_ret[:, l].astype(F32)))
        ret = (ret_ctx, ret_lat)
        ss_out.append(_diag_blocks(st_ctx))

        x1, h2, top_e, top_g = _outproj(attn, conv, ret, x_parts, mod, norm2_g[l][None, :], w_out[l].astype(BF16),
                                        router_w[l].T, router_b[l][:, None], n_ctx, dec_seq)
        lpos, run8, loff, goff, seg = _route(top_e, upper, ltri)
        tables = [tb[:, :, 0].reshape(-1) for tb in (run8, loff, goff)]
        seg = seg[:, 0].astype(I32)
        padded = (seg + MOE_BM - 1) // MOE_BM * MOE_BM
        pad_end = jnp.cumsum(padded)
        n_valid = (pad_end[-1] // MOE_BM).reshape(1)
        blk_src = jnp.minimum(jnp.arange(n_blocks, dtype=I32), n_valid - 1)
        blk_e = jnp.minimum(jnp.sum((pad_end[None, :] <= (blk_src * MOE_BM)[:, None]).astype(I32), axis=1),
                            N_EXPERTS - 1)
        owns = padded > 0
        ord_e = jnp.cumsum(owns.astype(I32)) - 1
        next_e = jnp.sum(jnp.where(owns[None, :] & (ord_e[None, :] == ord_e[:, None] + 1),
                                   ee[None, :] + 1, 0), axis=1).astype(I32) - 1

        xs = _dispatch(tables, seg, pad_end - padded, n_valid, h2, lpos, n_blocks)
        of_blk = blk_e[:, None] == ee[None, :]
        per_blk = lambda table: jnp.sum(jnp.where(of_blk, table[None, :], 0), axis=1)
        ys = _experts(blk_e, blk_src, n_valid, per_blk(ord_e), per_blk(next_e), xs,
                      moe_w_gu, moe_b_gu, moe_w_dn, moe_b_dn, l, n_blocks)
        out = _combine(tables, ys, x1, lpos, top_g, mod, final_g[None, :], l == depth - 1, n_ctx, dec_seq)
        x_parts = (out,)

    y_prompt = out[0].reshape(batch, seq, d)
    y_sample = out[1].reshape(dec_batch, dec_seq, d)
    return (y_prompt, y_sample, jnp.stack(ks_out, axis=1), jnp.stack(vs_out, axis=1),
            jnp.stack(ss_out, axis=1))
```

```python
import functools

import numpy as np
import jax
import jax.numpy as jnp
from jax import lax
from jax.experimental import pallas as pl
from jax.experimental.pallas import tpu as pltpu

F32 = jnp.float32
BF16 = jnp.bfloat16
I32 = jnp.int32

D_MODEL = 1024
GRID_W = 64
HEAD_DIM = 64
N_Q_HEADS = 8
N_KV_HEADS = 2
ATTN_W = N_Q_HEADS * HEAD_DIM
KV_W = N_KV_HEADS * HEAD_DIM
QK_W = ATTN_W + KV_W
CONV_CH = 256
CONV_K = 31
CONV_PAD = CONV_K // 2
CONV_HALO = 16
N_RET_HEADS = 4
RET_W = 256
OFF_V = QK_W
OFF_CONV = OFF_V + KV_W
OFF_RET = OFF_CONV + 2 * CONV_CH
IN_COLS = OFF_RET + 4 * RET_W
ROPE_HALF = HEAD_DIM // 2
ROPE_THETA = 10000.0
N_EXPERTS = 32
TOP_K = 4
D_FF = D_MODEL
SWIGLU_LIMIT = 7.0
SWIGLU_ALPHA = 1.702
EPS = 1e-6
LN_EPS = 1e-5

LANES = 128
TOK_TILE = 1024
ATTN_QB = 512
MIXER_ROWS = 1024
RET_QB = 256
CONV_ROWS = 64
MOE_BM = 256
MOVE_TILE = 256
ROUTE_TILES = 4
SUBLANES = 8
COMPACT_ROWS = 1280
assert COMPACT_ROWS >= MOVE_TILE * TOP_K + N_EXPERTS * (SUBLANES - 1) and COMPACT_ROWS % MOE_BM == 0
CHUNK_SIZES = tuple(MOVE_TILE >> s for s in range(6))
RARE_SIZES = 3
VMEM_LIMIT = 56 * 1024 * 1024

MOD_SH1, MOD_SC1, MOD_G1, MOD_SH2, MOD_SC2, MOD_G2 = range(6)


def _cparams(*sem):
    return pltpu.CompilerParams(dimension_semantics=sem, vmem_limit_bytes=VMEM_LIMIT)


def _dot(a, b, **kw):
    return jnp.dot(a, b, preferred_element_type=F32, **kw)


def _dot_nt(a, b, **kw):
    return lax.dot_general(a, b, (((1,), (1,)), ((), ())), preferred_element_type=F32, **kw)


def _dot_tn(a, b, **kw):
    return lax.dot_general(a, b, (((0,), (0,)), ((), ())), preferred_element_type=F32, **kw)


def _split_bf16(x):
    hi = x.astype(BF16)
    lo = (x - hi.astype(F32)).astype(BF16)
    return hi, lo


MOD_TN = 1536


def _mod_kernel(ct_ref, w_ref, b_ref, o_ref):
    s = ct_ref[...]
    s = s * jax.nn.sigmoid(s)
    w = w_ref[0]
    rows = [jnp.sum(w * s[:, r:r + 1], axis=0, keepdims=True) for r in range(3)]
    rows.append(jnp.zeros((5, w.shape[1]), F32))
    o_ref[0] = jnp.concatenate(rows, axis=0) + b_ref[0]


def _modulation(cvec3, w_mod, b_mod):
    depth, d, cols = w_mod.shape
    ct = jnp.zeros((d, 8), F32).at[:, :3].set(cvec3.T)
    out = pl.pallas_call(
        _mod_kernel,
        grid=(depth, cols // MOD_TN),
        in_specs=[
            pl.BlockSpec((d, 8), lambda l, j: (0, 0)),
            pl.BlockSpec((1, d, MOD_TN), lambda l, j: (l, 0, j)),
            pl.BlockSpec((1, 1, MOD_TN), lambda l, j: (l, 0, j)),
        ],
        out_specs=pl.BlockSpec((1, 8, MOD_TN), lambda l, j: (l, 0, j)),
        out_shape=jax.ShapeDtypeStruct((depth, 8, cols), F32),
        compiler_params=_cparams("arbitrary", "arbitrary"),
    )(ct, w_mod, b_mod.reshape(depth, 1, cols))
    return out[:, :3].reshape(depth, 3 * 6, 1, d)


def _token_specs(parts, tile, n_ctx):
    d = parts[0].shape[1]
    if len(parts) == 1:
        return [pl.BlockSpec((tile, d), lambda i, *_: (i, 0))]
    ctx_tiles = n_ctx // tile
    return [pl.BlockSpec((tile, d), lambda i, *_: (jnp.minimum(i, ctx_tiles - 1), 0)),
            pl.BlockSpec((tile, d), lambda i, *_: (jnp.maximum(i - ctx_tiles, 0), 0))]


def _token_tile(refs, ctx_tiles):
    if len(refs) == 1:
        return refs[0][...]
    return jnp.where(pl.program_id(0) < ctx_tiles, refs[0][...], refs[1][...])


def _inproj_kernel(n_src, ctx_tiles, *refs):
    x_refs = refs[:n_src]
    (sh_ref, sc_ref, g_ref, w_ref, gqk_ref, hsum_ref, hbc_ref, cos_ref, sa_ref, sb_ref,
     p_ref, k_ref, v_ref) = refs[n_src:]
    x = _token_tile(x_refs, ctx_tiles)
    inv = lax.rsqrt(jnp.mean(x * x, axis=-1, keepdims=True) + EPS)
    h = (x * inv * g_ref[...]) * (1.0 + sc_ref[0]) + sh_ref[0]
    acc = _dot(h.astype(BF16), w_ref[...])
    qk = acc[:, :QK_W]
    ss = _dot((qk * qk).astype(BF16), hsum_ref[...])
    r = lax.rsqrt(ss * (1.0 / HEAD_DIM) + EPS)
    r_hi, r_lo = _split_bf16(r)
    rb = _dot(r_hi, hbc_ref[...]) + _dot(r_lo, hbc_ref[...])
    qkn = qk * rb * gqk_ref[...]
    cos = cos_ref[...]
    sa = sa_ref[...]
    sb = sb_ref[...]
    for j in range(QK_W // LANES):
        c = qkn[:, j * LANES:(j + 1) * LANES]
        up = pltpu.roll(c, LANES - ROPE_HALF // 2, 1)
        dn = pltpu.roll(c, ROPE_HALF // 2, 1)
        p_ref[:, j * LANES:(j + 1) * LANES] = (c * cos + up * sa + dn * sb).astype(BF16)
    p_ref[:, QK_W:] = acc[:, QK_W:].astype(BF16)

    @pl.when(pl.program_id(0) < ctx_tiles)
    def _():
        k_ref[...] = qkn[:, ATTN_W:QK_W]
        v_ref[...] = acc[:, OFF_V:OFF_CONV]


def _rope_tables(dec_seq):
    f32 = np.float32
    rows = dec_seq // GRID_W
    row = np.repeat(np.arange(rows), GRID_W).astype(f32)
    col = np.tile(np.arange(GRID_W), rows).astype(f32)
    inv = (f32(1.0) / (f32(ROPE_THETA) ** (np.arange(0, ROPE_HALF, 2).astype(f32) / f32(ROPE_HALF)))).astype(f32)
    ar = row[:, None] * inv[None, :]
    ac = col[:, None] * inv[None, :]
    cos = np.concatenate([np.cos(ar), np.cos(ar), np.cos(ac), np.cos(ac)], axis=-1)
    sin = np.concatenate([np.sin(ar), np.sin(ar), np.sin(ac), np.sin(ac)], axis=-1)
    first = (np.arange(HEAD_DIM) % ROPE_HALF) < ROPE_HALF // 2
    sa = np.where(first[None, :], -sin, 0.0)
    sb = np.where(first[None, :], 0.0, sin)
    def table(t, ident):
        t = np.concatenate([np.full((TOK_TILE, HEAD_DIM), ident, f32), t.astype(f32)], axis=0)
        return jnp.asarray(np.tile(t, (1, LANES // HEAD_DIM)))
    return table(cos, 1.0), table(sa, 0.0), table(sb, 0.0)


def _group_of_tile(i, tile, n_ctx, dec_seq):
    tok = i * tile
    return jnp.where(tok < n_ctx, 0, 1 + (tok - n_ctx) // dec_seq)


def _inproj(x_parts, mod, norm_g, w_in_bf, gqk, hsum, hbc, rope, n_ctx, dec_seq):
    n = sum(part.shape[0] for part in x_parts)
    d = x_parts[0].shape[1]
    t = TOK_TILE
    grp = functools.partial(_group_of_tile, tile=t, n_ctx=n_ctx, dec_seq=dec_seq)

    def mod_spec(which):
        return pl.BlockSpec((1, 1, d), lambda i: (grp(i) * 6 + which, 0, 0))

    def rope_idx(i):
        tok = i * t
        return (jnp.where(tok < n_ctx, 0, 1 + ((tok - n_ctx) % dec_seq) // t), 0)

    rope_spec = pl.BlockSpec((t, LANES), rope_idx)
    const = lambda shape: pl.BlockSpec(shape, lambda i: (0,) * len(shape))
    return pl.pallas_call(
        functools.partial(_inproj_kernel, len(x_parts), n_ctx // t),
        grid=(n // t,),
        in_specs=_token_specs(x_parts, t, n_ctx) + [
            mod_spec(MOD_SH1), mod_spec(MOD_SC1),
            const((1, d)),
            const((d, IN_COLS)),
            const((1, QK_W)), const((QK_W, LANES)), const((LANES, QK_W)),
            rope_spec, rope_spec, rope_spec,
        ],
        out_specs=[
            pl.BlockSpec((t, IN_COLS), lambda i: (i, 0)),
            pl.BlockSpec((t, KV_W), lambda i: (jnp.minimum(i, n_ctx // t - 1), 0)),
            pl.BlockSpec((t, KV_W), lambda i: (jnp.minimum(i, n_ctx // t - 1), 0)),
        ],
        out_shape=[
            jax.ShapeDtypeStruct((n, IN_COLS), BF16),
            jax.ShapeDtypeStruct((n_ctx, KV_W), F32),
            jax.ShapeDtypeStruct((n_ctx, KV_W), F32),
        ],
        compiler_params=_cparams("arbitrary"),
    )(*x_parts, mod, mod, norm_g, w_in_bf, gqk, hsum, hbc, *rope)


def _head_halves(x, hkv, low):
    r = pltpu.roll(x, HEAD_DIM, 1)
    rep = jnp.where(low, x, r) if hkv == 0 else jnp.where(low, r, x)
    return jnp.where(low, rep, 0.0).astype(BF16), jnp.where(low, 0.0, rep).astype(BF16)


def _make_attn_kernel(has_ctx, group, seq_len):
    def kern(*refs):
        if has_ctx:
            q_ref, k_ref, v_ref, ck_ref, cv_ref, o_ref = refs
        else:
            q_ref, k_ref, v_ref, o_ref = refs
        low = lax.broadcasted_iota(I32, (1, LANES), 1) < HEAD_DIM
        scale = HEAD_DIM ** -0.5
        for g in range(group):
            qrows = slice(g * seq_len, (g + 1) * seq_len) if group > 1 else slice(None)
            k = k_ref[qrows, :].astype(F32)
            v = v_ref[qrows, :].astype(F32)
            if has_ctx:
                ck = ck_ref[0, 0]
                cv = cv_ref[0, 0]
            for hkv in range(N_KV_HEADS):
                kh = _head_halves(k, hkv, low)
                vh = _head_halves(v, hkv, low)
                if has_ctx:
                    ckh = _head_halves(ck, hkv, low)
                    cvh = _head_halves(cv, hkv, low)
                for c in range(2):
                    col = hkv * 2 * LANES + c * LANES
                    qc = q_ref[qrows, col:col + LANES] * scale
                    o_c = None
                    for j in range(2):
                        s = _dot_nt(qc, kh[j])
                        m = jnp.max(s, axis=-1, keepdims=True)
                        if has_ctx:
                            s2 = _dot_nt(qc, ckh[j])
                            m = jnp.maximum(m, jnp.max(s2, axis=-1, keepdims=True))
                        p = jnp.exp(s - m)
                        l = jnp.sum(p, axis=-1, keepdims=True)
                        o = _dot(p.astype(BF16), vh[j])
                        if has_ctx:
                            p2 = jnp.exp(s2 - m)
                            l = l + jnp.sum(p2, axis=-1, keepdims=True)
                            o = o + _dot(p2.astype(BF16), cvh[j])
                        o = o / l
                        o_c = o if o_c is None else o_c + o
                    o_ref[qrows, col:col + LANES] = o_c.astype(BF16)
    return kern


def _attention(p, n_seq, seq_len, row0, ctx_kv, layer):
    has_ctx = ctx_kv is not None
    qb = min(seq_len, ATTN_QB)
    nq = seq_len // qb
    group = _seq_group(n_seq, seq_len, row0) if (nq == 1 and not has_ctx) else 1
    rows = group * seq_len
    qrows = group * qb
    qrow0 = row0 // qrows
    srow0 = row0 // rows
    in_specs = [
        pl.BlockSpec((qrows, ATTN_W), lambda b, i: (qrow0 + b * nq + i, 0)),
        pl.BlockSpec((rows, KV_W), lambda b, i: (srow0 + b, ATTN_W // KV_W)),
        pl.BlockSpec((rows, KV_W), lambda b, i: (srow0 + b, OFF_V // KV_W)),
    ]
    args = [p, p, p]
    if has_ctx:
        ck, cv = ctx_kv
        past = ck.shape[2]
        spec = pl.BlockSpec((1, 1, past, KV_W), lambda b, i: (b, layer, 0, 0))
        in_specs += [spec, spec]
        args += [ck, cv]
    return pl.pallas_call(
        _make_attn_kernel(has_ctx, group, seq_len),
        grid=(n_seq // group, nq),
        in_specs=in_specs,
        out_specs=pl.BlockSpec((qrows, ATTN_W), lambda b, i: (b * nq + i, 0)),
        out_shape=jax.ShapeDtypeStruct((n_seq * seq_len, ATTN_W), BF16),
        compiler_params=_cparams("arbitrary", "arbitrary"),
    )(*args)


def _make_conv_kernel(seq_len, group):
    shifted_rows = seq_len + 2 * CONV_HALO - SUBLANES

    def kern(a_ref, g_ref, w_ref, b_ref, lng_ref, lnb_ref, o_ref, zp_ref, zs_ref):
        zero = jnp.zeros((CONV_HALO, CONV_CH), F32)
        w = w_ref[...]
        bias = b_ref[...]
        for q in range(group):
            r0 = q * seq_len
            zp_ref[q, 0:CONV_HALO, :] = zero
            zp_ref[q, CONV_HALO + seq_len:2 * CONV_HALO + seq_len, :] = zero
            zp_ref[q, CONV_HALO:CONV_HALO + seq_len, :] = (
                a_ref[r0:r0 + seq_len, :].astype(F32) * jax.nn.sigmoid(g_ref[r0:r0 + seq_len, :].astype(F32)))
            for s in range(1, SUBLANES):
                zs_ref[q, s] = zp_ref[q, s:s + shifted_rows, :]
            for c in range(seq_len // CONV_ROWS):
                base = c * CONV_ROWS + CONV_HALO - CONV_PAD
                acc = jnp.zeros((CONV_ROWS, CONV_CH), F32) + bias
                for j in range(CONV_K):
                    shift = (base + j) % SUBLANES
                    row = base + j - shift
                    if shift == 0:
                        tap = zp_ref[q, row:row + CONV_ROWS, :]
                    else:
                        tap = zs_ref[q, shift, row:row + CONV_ROWS, :]
                    acc = acc + tap * w[j:j + 1, :]
                mu = jnp.mean(acc, axis=-1, keepdims=True)
                dlt = acc - mu
                var = jnp.mean(dlt * dlt, axis=-1, keepdims=True)
                y = dlt * lax.rsqrt(var + LN_EPS) * lng_ref[...] + lnb_ref[...]
                o_ref[r0 + c * CONV_ROWS:r0 + (c + 1) * CONV_ROWS, :] = (y * jax.nn.sigmoid(y)).astype(BF16)
    return kern


def _seq_group(n_seq, seq_len, row0):
    group = max(1, MIXER_ROWS // seq_len)
    if n_seq % group or row0 % (group * seq_len):
        group = 1
    return group


def _conv(p, n_seq, seq_len, row0, w, b, lng, lnb):
    group = _seq_group(n_seq, seq_len, row0)
    rows = group * seq_len
    srow0 = row0 // rows
    const = lambda shape: pl.BlockSpec(shape, lambda s: (0,) * len(shape))
    return pl.pallas_call(
        _make_conv_kernel(seq_len, group),
        grid=(n_seq // group,),
        in_specs=[
            pl.BlockSpec((rows, CONV_CH), lambda s: (srow0 + s, OFF_CONV // CONV_CH)),
            pl.BlockSpec((rows, CONV_CH), lambda s: (srow0 + s, OFF_CONV // CONV_CH + 1)),
            const((CONV_K, CONV_CH)), const((1, CONV_CH)), const((1, CONV_CH)), const((1, CONV_CH)),
        ],
        out_specs=pl.BlockSpec((rows, CONV_CH), lambda s: (s, 0)),
        out_shape=jax.ShapeDtypeStruct((n_seq * seq_len, CONV_CH), BF16),
        scratch_shapes=[pltpu.VMEM((group, seq_len + 2 * CONV_HALO, CONV_CH), F32),
                        pltpu.VMEM((group, SUBLANES, seq_len + 2 * CONV_HALO - SUBLANES, CONV_CH), F32)],
        compiler_params=_cparams("arbitrary"),
    )(p, p, w, b, lng, lnb)


def _make_ret_kernel(seq_len, has_init, group):
    qb = min(seq_len, RET_QB)
    nq = seq_len // qb
    scale = HEAD_DIM ** -0.5

    def kern(*refs):
        if has_init:
            lg_ref, q_ref, k_ref, v_ref, g_ref, gn_ref, gm_ref, r0_ref, o_ref, st_ref = refs
        else:
            lg_ref, q_ref, k_ref, v_ref, g_ref, gn_ref, gm_ref, o_ref, st_ref = refs
        low = lax.broadcasted_iota(I32, (1, LANES), 1) < HEAD_DIM
        pos = lax.broadcasted_iota(I32, (seq_len, 1), 0).astype(F32)
        qpos = lax.broadcasted_iota(I32, (qb, 1), 0).astype(F32)
        kpos = lax.broadcasted_iota(I32, (1, seq_len), 1).astype(F32)
        gm = gm_ref[...]
        for c in range(2):
            cs = slice(c * LANES, (c + 1) * LANES)
            lgf = jnp.where(low, lg_ref[0, 2 * c], lg_ref[0, 2 * c + 1])
            lgb = jnp.where(low, lg_ref[1, 2 * c], lg_ref[1, 2 * c + 1])
            zeta_f = jnp.exp((seq_len - 1.0 - pos) * lgf)
            zeta_b = jnp.exp(pos * lgb)
            decs = []
            for j in range(2):
                lf = lg_ref[0, 2 * c + j]
                lb = lg_ref[1, 2 * c + j]
                row = []
                for i in range(nq):
                    diff = (qpos + float(i * qb)) - kpos
                    dec = jnp.exp(jnp.where(diff >= 0, diff * lf, -diff * lb))
                    row.append(dec * jnp.where(diff == 0, 2.0 * scale, scale))
                decs.append(row)
            for q in range(group):
                rs = slice(q * seq_len, (q + 1) * seq_len)
                qc = q_ref[rs, cs]
                kc = k_ref[rs, cs]
                vc = v_ref[rs, cs]
                kf = kc.astype(F32) * scale
                for d, zeta, lgd in ((0, zeta_f, lgf), (1, zeta_b, lgb)):
                    st = _dot_tn((kf * zeta).astype(BF16), vc)
                    if has_init:
                        st = st + r0_ref[q, d, c] * jnp.exp(seq_len * lgd)
                    st_ref[q, d, c] = st
                y_blocks = [None] * nq
                for j in range(2):
                    sel = low if j == 0 else jnp.logical_not(low)
                    kh = jnp.where(sel, kc, jnp.zeros_like(kc))
                    vh = jnp.where(sel, vc, jnp.zeros_like(vc))
                    for i in range(nq):
                        s = _dot_nt(qc[i * qb:(i + 1) * qb], kh)
                        y = _dot((s * decs[j][i]).astype(BF16), vh)
                        y_blocks[i] = y if y_blocks[i] is None else y_blocks[i] + y
                y = jnp.concatenate(y_blocks, axis=0) if nq > 1 else y_blocks[0]
                if has_init:
                    xi_f = jnp.exp((pos + 1.0) * lgf)
                    xi_b = jnp.exp((seq_len - pos) * lgb)
                    y = y + _dot(qc, r0_ref[q, 0, c].astype(BF16)) * xi_f
                    y = y + _dot(qc, r0_ref[q, 1, c].astype(BF16)) * xi_b
                y_hi, y_lo = _split_bf16(y)
                mu = _dot(y_hi, gm) + _dot(y_lo, gm)
                dlt = y - mu
                var = _dot((dlt * dlt).astype(BF16), gm)
                yn = dlt * lax.rsqrt(var + LN_EPS) * gn_ref[:, cs]
                gate = g_ref[rs, cs].astype(F32)
                o_ref[rs, cs] = (gate * jax.nn.sigmoid(gate) * yn).astype(BF16)
    return kern


def _retention(p, n_seq, seq_len, row0, log_g, gn_g, gmat, r0):
    group = _seq_group(n_seq, seq_len, row0)
    rows = group * seq_len
    srow0 = row0 // rows
    has_init = r0 is not None
    col = OFF_RET // RET_W
    in_specs = [pl.BlockSpec(memory_space=pltpu.SMEM)]
    in_specs += [pl.BlockSpec((rows, RET_W), functools.partial(lambda s, j: (srow0 + s, col + j), j=j))
                 for j in range(4)]
    in_specs += [pl.BlockSpec((1, RET_W), lambda s: (0, 0)), pl.BlockSpec((LANES, LANES), lambda s: (0, 0))]
    args = [log_g, p, p, p, p, gn_g, gmat]
    st_spec = pl.BlockSpec((group, 2, 2, LANES, LANES), lambda s: (s, 0, 0, 0, 0))
    if has_init:
        in_specs.append(st_spec)
        args.append(r0)
    return pl.pallas_call(
        _make_ret_kernel(seq_len, has_init, group),
        grid=(n_seq // group,),
        in_specs=in_specs,
        out_specs=[pl.BlockSpec((rows, RET_W), lambda s: (s, 0)), st_spec],
        out_shape=[jax.ShapeDtypeStruct((n_seq * seq_len, RET_W), BF16),
                   jax.ShapeDtypeStruct((n_seq, 2, 2, LANES, LANES), F32)],
        compiler_params=_cparams("arbitrary"),
    )(*args)


def _outproj_kernel(n_src, ctx_tiles, *refs):
    x_refs = refs[:n_src]
    a_refs, c_refs, r_refs = refs[n_src:n_src + 2], refs[n_src + 2:n_src + 4], refs[n_src + 4:n_src + 6]
    (g1_ref, sc_ref, sh_ref, n2_ref, wo_ref, rwt_ref, rb_ref,
     x1_ref, h2_ref, te_ref, tg_ref) = refs[n_src + 6:]
    mixed = (_dot(_token_tile(a_refs, ctx_tiles), wo_ref[0:ATTN_W, :])
             + _dot(_token_tile(c_refs, ctx_tiles), wo_ref[ATTN_W:ATTN_W + CONV_CH, :])
             + _dot(_token_tile(r_refs, ctx_tiles), wo_ref[ATTN_W + CONV_CH:, :]))
    x1 = _token_tile(x_refs, ctx_tiles) + g1_ref[0] * mixed
    x1_ref[...] = x1
    inv = lax.rsqrt(jnp.mean(x1 * x1, axis=-1, keepdims=True) + EPS)
    h2 = (x1 * inv * n2_ref[...]) * (1.0 + sc_ref[0]) + sh_ref[0]
    h2_ref[...] = h2.astype(BF16)
    w_hi, w_lo = _split_bf16(rwt_ref[...])
    h_hi, h_lo = _split_bf16(h2)
    logits = _dot_nt(w_hi, h_hi) + _dot_nt(w_hi, h_lo) + _dot_nt(w_lo, h_hi) + rb_ref[...]
    t = logits.shape[1]
    eidx = lax.broadcasted_iota(I32, (N_EXPERTS, t), 0).astype(F32)
    vals = logits
    tops, idxs = [], []
    for _ in range(TOP_K):
        m = jnp.max(vals, axis=0, keepdims=True)
        idx = jnp.min(jnp.where(vals == m, eidx, float(N_EXPERTS)), axis=0, keepdims=True)
        tops.append(m)
        idxs.append(idx)
        vals = jnp.where(eidx == idx, -jnp.inf, vals)
    es = [jnp.exp(m - tops[0]) for m in tops]
    tot = es[0] + es[1] + es[2] + es[3]
    te_ref[...] = jnp.concatenate(idxs + [jnp.zeros((8 - TOP_K, t), F32)], axis=0).astype(I32)
    tg_ref[...] = jnp.concatenate([e / tot for e in es] + [jnp.zeros((8 - TOP_K, t), F32)], axis=0)


def _outproj(attn, conv, ret, x_parts, mod, norm_g, w_out_bf, rwt, rb, n_ctx, dec_seq):
    n = attn[0].shape[0] + attn[1].shape[0]
    d = x_parts[0].shape[1]
    t = TOK_TILE
    grp = functools.partial(_group_of_tile, tile=t, n_ctx=n_ctx, dec_seq=dec_seq)

    def mod_spec(which):
        return pl.BlockSpec((1, 1, d), lambda i: (grp(i) * 6 + which, 0, 0))

    const = lambda shape: pl.BlockSpec(shape, lambda i: (0,) * len(shape))
    row = lambda w: pl.BlockSpec((t, w), lambda i: (i, 0))
    lane = lambda: pl.BlockSpec((8, t), lambda i: (0, i))
    return pl.pallas_call(
        functools.partial(_outproj_kernel, len(x_parts), n_ctx // t),
        grid=(n // t,),
        in_specs=_token_specs(x_parts, t, n_ctx) + _token_specs(attn, t, n_ctx)
        + _token_specs(conv, t, n_ctx) + _token_specs(ret, t, n_ctx) + [
            mod_spec(MOD_G1), mod_spec(MOD_SC2), mod_spec(MOD_SH2),
            const((1, d)), const((d, d)), const((N_EXPERTS, d)), const((N_EXPERTS, 1))],
        out_specs=[row(d), row(d), lane(), lane()],
        out_shape=[jax.ShapeDtypeStruct((n, d), F32), jax.ShapeDtypeStruct((n, d), BF16),
                   jax.ShapeDtypeStruct((8, n), I32), jax.ShapeDtypeStruct((8, n), F32)],
        compiler_params=_cparams("arbitrary"),
    )(*x_parts, *attn, *conv, *ret, mod, mod, mod, norm_g, w_out_bf, rwt, rb)


def _round_up(x, m):
    return jnp.floor((x + (m - 1.0)) * (1.0 / m)) * m


def _route_kernel(te_ref, u_ref, ltri_ref, lpos_ref, run8_ref, loff_ref, goff_ref, seg_ref, run_ref, start_ref):
    ph = pl.program_id(0)
    i = pl.program_id(1)
    t = MOVE_TILE
    eidx = lax.broadcasted_iota(I32, (N_EXPERTS, t), 0)

    @pl.when(jnp.logical_and(ph == 0, i == 0))
    def _():
        run_ref[...] = jnp.zeros_like(run_ref)

    @pl.when(jnp.logical_and(ph == 1, i == 0))
    def _():
        seg = run_ref[...]
        seg_ref[...] = seg
        start_ref[...] = _dot(ltri_ref[...], _round_up(seg, MOE_BM), precision=lax.Precision.HIGHEST)
        run_ref[...] = jnp.zeros_like(run_ref)

    for s in range(ROUTE_TILES):
        te = te_ref[:, s * t:(s + 1) * t]
        hits = [eidx == te[k:k + 1, :] for k in range(TOP_K)]
        onehot = sum(h.astype(F32) for h in hits)
        run8 = _round_up(jnp.sum(onehot, axis=1, keepdims=True), SUBLANES)

        @pl.when(ph == 0)
        def _(run8=run8):
            run_ref[...] = run_ref[...] + run8

        @pl.when(ph == 1)
        def _(s=s, hits=hits, onehot=onehot, run8=run8):
            before = _dot(onehot.astype(BF16), u_ref[...])
            run8_b = jnp.broadcast_to(run8, (N_EXPERTS, LANES))
            loff = _dot(ltri_ref[...], run8_b, precision=lax.Precision.HIGHEST)
            base = before + loff[:, 0:1]
            rows = [jnp.sum(jnp.where(h, base, 0.0), axis=0, keepdims=True) for h in hits]
            rows.append(jnp.zeros((8 - TOP_K, t), F32))
            lpos_ref[:, s * t:(s + 1) * t] = jnp.concatenate(rows, axis=0).astype(I32)
            run8_ref[s] = run8_b.astype(I32)
            loff_ref[s] = loff.astype(I32)
            goff_ref[s] = (start_ref[...] + run_ref[...]).astype(I32)
            run_ref[...] = run_ref[...] + run8


def _route(top_e, upper, ltri):
    n = top_e.shape[1]
    t = MOVE_TILE
    nt = n // t
    g = ROUTE_TILES
    assert nt % g == 0
    table = pl.BlockSpec((g, N_EXPERTS, LANES), lambda ph, i: (i * ph, 0, 0))
    table_shape = jax.ShapeDtypeStruct((nt, N_EXPERTS, LANES), I32)
    return pl.pallas_call(
        _route_kernel,
        grid=(2, nt // g),
        in_specs=[pl.BlockSpec((8, g * t), lambda ph, i: (0, i)),
                  pl.BlockSpec((t, t), lambda ph, i: (0, 0)),
                  pl.BlockSpec((N_EXPERTS, N_EXPERTS), lambda ph, i: (0, 0))],
        out_specs=[pl.BlockSpec((8, g * t), lambda ph, i: (0, i * ph)), table, table, table,
                   pl.BlockSpec((N_EXPERTS, LANES), lambda ph, i: (0, 0))],
        out_shape=[jax.ShapeDtypeStruct((8, n), I32), table_shape, table_shape, table_shape,
                   jax.ShapeDtypeStruct((N_EXPERTS, LANES), F32)],
        scratch_shapes=[pltpu.VMEM((N_EXPERTS, LANES), F32), pltpu.VMEM((N_EXPERTS, LANES), F32)],
        compiler_params=_cparams("arbitrary", "arbitrary"),
    )(top_e, upper, ltri)


def _for_each_run_chunk(tile, run8_ref, loff_ref, goff_ref, move):
    def pieces(rows, lo, go, sizes, queue):
        for size in sizes:
            hit = (rows & size) != 0

            @pl.when(hit)
            def _(lo=lo, go=go, size=size):
                move(pl.multiple_of(lo, SUBLANES), pl.multiple_of(go, SUBLANES), size, queue)

            lo = lo + jnp.where(hit, size, 0)
            go = go + jnp.where(hit, size, 0)
        return lo, go

    def per_expert(e, queue):
        idx = tile * N_EXPERTS + e
        rows = run8_ref[idx]
        lo, go = pieces(rows, loff_ref[idx], goff_ref[idx], CHUNK_SIZES[RARE_SIZES:], queue)

        @pl.when(rows >= CHUNK_SIZES[RARE_SIZES - 1])
        def _():
            pieces(rows, lo, go, CHUNK_SIZES[:RARE_SIZES], queue)

        return rows

    def per_pair(j, total):
        return total + per_expert(2 * j, 0) + per_expert(2 * j + 1, 1)

    return lax.fori_loop(0, N_EXPERTS // 2, per_pair, jnp.int32(0))


def _wait_rows(total, copy_of_rows):
    size = 1 << (COMPACT_ROWS.bit_length() - 1)
    while size >= SUBLANES:
        @pl.when((total & size) != 0)
        def _(size=size):
            copy_of_rows(size).wait()

        size //= 2


def _make_dispatch_kernel(n_tiles, n_blocks):
    t = MOVE_TILE
    cb = COMPACT_ROWS

    def kern(run8_ref, loff_ref, goff_ref, seg_ref, start_ref, nv_ref, h_ref, lpos_ref, xs_ref,
             buf, zeros_ref, moved_ref, sem, zsem):
        step = pl.program_id(0)
        slot = lax.rem(step, 2)

        def wait_tile(s):
            _wait_rows(moved_ref[s], lambda rows: pltpu.make_async_copy(
                buf.at[s, pl.ds(0, rows)], xs_ref.at[pl.ds(0, rows)], sem.at[s]))

        def for_each_zero_chunk(action):
            def per_expert(e, carry):
                seg = seg_ref[e]
                padlen = (-seg) & (MOE_BM - 1)
                row = start_ref[e] + seg
                for size in CHUNK_SIZES[1:]:
                    hit = (padlen & size) != 0

                    @pl.when(hit)
                    def _(row=row, size=size):
                        dst = xs_ref.at[pl.ds(pl.multiple_of(row, SUBLANES), size)]
                        action(pltpu.make_async_copy(zeros_ref.at[pl.ds(0, size)], dst, zsem))

                    row = row + jnp.where(hit, size, 0)
                return carry

            lax.fori_loop(0, N_EXPERTS, per_expert, 0)

            def per_tail_block(b, carry):
                action(pltpu.make_async_copy(zeros_ref, xs_ref.at[pl.ds(b * MOE_BM, MOE_BM)], zsem))
                return carry

            lax.fori_loop(nv_ref[0], n_blocks, per_tail_block, 0)

        @pl.when(step == 0)
        def _():
            zeros_ref[...] = jnp.zeros_like(zeros_ref)
            for_each_zero_chunk(lambda cp: cp.start())
            for_each_zero_chunk(lambda cp: cp.wait())

        @pl.when(step >= 2)
        def _():
            wait_tile(slot)

        lpos = lpos_ref[...]
        rows = lax.broadcasted_iota(I32, (cb, t), 0)
        onehot = jnp.where(rows == lpos[0:1, :], 1.0, 0.0)
        for k in range(1, TOP_K):
            onehot = onehot + jnp.where(rows == lpos[k:k + 1, :], 1.0, 0.0)
        buf[slot] = _dot(onehot.astype(BF16), h_ref[...])

        def move(lo, go, size, queue):
            pltpu.make_async_copy(buf.at[slot, pl.ds(lo, size)], xs_ref.at[pl.ds(go, size)],
                                  sem.at[slot]).start(priority=queue)

        moved_ref[slot] = _for_each_run_chunk(step, run8_ref, loff_ref, goff_ref, move)

        @pl.when(step == n_tiles - 1)
        def _():
            wait_tile(slot)
            if n_tiles > 1:
                wait_tile(1 - slot)
    return kern


def _dispatch(tables, seg, starts, n_valid, h2, lpos, n_blocks):
    n, d = h2.shape
    t = MOVE_TILE
    return pl.pallas_call(
        _make_dispatch_kernel(n // t, n_blocks),
        grid_spec=pltpu.PrefetchScalarGridSpec(
            num_scalar_prefetch=6,
            grid=(n // t,),
            in_specs=[pl.BlockSpec((t, d), lambda i, *_: (i, 0)),
                      pl.BlockSpec((8, t), lambda i, *_: (0, i))],
            out_specs=pl.BlockSpec(memory_space=pl.ANY),
            scratch_shapes=[pltpu.VMEM((2, COMPACT_ROWS, d), F32), pltpu.VMEM((MOE_BM, d), F32),
                            pltpu.SMEM((2,), I32), pltpu.SemaphoreType.DMA((2,)), pltpu.SemaphoreType.DMA(())],
        ),
        out_shape=jax.ShapeDtypeStruct((n_blocks * MOE_BM, d), F32),
        compiler_params=_cparams("arbitrary"),
    )(*tables, seg, starts, n_valid, h2, lpos)


def _make_expert_kernel(layer):
    def kern(be_ref, bs_ref, nv_ref, ord_ref, nxt_ref, x_ref, wgu_hbm, bgu_ref, wdn_hbm, bdn_ref, y_ref,
             wgu_f, wdn_f, wgu_s, wdn_s, sem):
        i = pl.program_id(0)
        e = be_ref[i]
        prev = be_ref[jnp.maximum(i - 1, 0)]
        new_expert = jnp.logical_or(i == 0, e != prev)
        slot = lax.rem(ord_ref[i], 2)

        def weight_copies(expert, s):
            return (pltpu.make_async_copy(wgu_hbm.at[layer, expert], wgu_f.at[s], sem.at[0, s]),
                    pltpu.make_async_copy(wdn_hbm.at[layer, expert], wdn_f.at[s], sem.at[1, s]))

        @pl.when(i == 0)
        def _():
            for cp in weight_copies(e, slot):
                cp.start()

        @pl.when(new_expert)
        def _():
            for cp in weight_copies(e, slot):
                cp.wait()
            nxt = nxt_ref[i]

            @pl.when(nxt >= 0)
            def _():
                for cp in weight_copies(nxt, 1 - slot):
                    cp.start(priority=1)

            wgu_s[...] = wgu_f[slot].astype(BF16)
            wdn_s[...] = wdn_f[slot].astype(BF16)

        @pl.when(i < nv_ref[0])
        def _():
            x = x_ref[...].astype(BF16)
            gu = _dot(x, wgu_s[...]) + bgu_ref[0, 0]
            gate = jnp.minimum(gu[:, :D_FF], SWIGLU_LIMIT)
            up = jnp.clip(gu[:, D_FF:], -SWIGLU_LIMIT, SWIGLU_LIMIT)
            hdn = (up + 1.0) * (gate * jax.nn.sigmoid(SWIGLU_ALPHA * gate))
            y_ref[...] = _dot(hdn.astype(BF16), wdn_s[...]) + bdn_ref[0, 0]
    return kern


def _experts(blk_e, blk_src, n_valid, blk_ord, blk_next, xs, w_gu, b_gu, w_dn, b_dn, layer, n_blocks):
    d = xs.shape[1]
    depth = w_gu.shape[0]

    def bias(width):
        return pl.BlockSpec((1, 1, 1, width), lambda i, be, *_: (layer, be[i], 0, 0))

    rows = pl.BlockSpec((MOE_BM, d), lambda i, be, bs, *_: (bs[i], 0))
    return pl.pallas_call(
        _make_expert_kernel(layer),
        grid_spec=pltpu.PrefetchScalarGridSpec(
            num_scalar_prefetch=5,
            grid=(n_blocks,),
            in_specs=[rows,
                      pl.BlockSpec(memory_space=pl.ANY), bias(2 * D_FF),
                      pl.BlockSpec(memory_space=pl.ANY), bias(d)],
            out_specs=rows,
            scratch_shapes=[pltpu.VMEM((2, d, 2 * D_FF), F32), pltpu.VMEM((2, D_FF, d), F32),
                            pltpu.VMEM((d, 2 * D_FF), BF16), pltpu.VMEM((D_FF, d), BF16),
                            pltpu.SemaphoreType.DMA((2, 2))],
        ),
        out_shape=jax.ShapeDtypeStruct(xs.shape, xs.dtype),
        input_output_aliases={5: 0},
        compiler_params=_cparams("arbitrary"),
    )(blk_e, blk_src, n_valid, blk_ord, blk_next, xs, w_gu, b_gu.reshape(depth, N_EXPERTS, 1, -1), w_dn,
      b_dn.reshape(depth, N_EXPERTS, 1, -1))


def _make_combine_kernel(n_tiles, ctx_tiles, final):
    t = MOVE_TILE
    cb = COMPACT_ROWS

    def kern(run8_ref, loff_ref, goff_ref, ys_ref, x1_ref, lpt_ref, gt_ref, g2_ref, fg_ref, *rest):
        o_refs = rest[:2] if final else rest[:1]
        ybuf, moved_ref, sem = rest[len(o_refs):]
        step = pl.program_id(0)
        slot = lax.rem(step, 2)

        def fetch(tile, s):
            def move(lo, go, size, queue):
                pltpu.make_async_copy(ys_ref.at[pl.ds(go, size)], ybuf.at[s, pl.ds(lo, size)],
                                      sem.at[s]).start(priority=queue)

            moved_ref[s] = _for_each_run_chunk(tile, run8_ref, loff_ref, goff_ref, move)

        @pl.when(step == 0)
        def _():
            ybuf[...] = jnp.zeros_like(ybuf)
            fetch(0, 0)

        @pl.when(step + 1 < n_tiles)
        def _():
            fetch(step + 1, 1 - slot)

        _wait_rows(moved_ref[slot], lambda rows: pltpu.make_async_copy(
            ys_ref.at[pl.ds(0, rows)], ybuf.at[slot, pl.ds(0, rows)], sem.at[slot]))

        lpos = lpt_ref[...]
        gates = gt_ref[...]
        rows = lax.broadcasted_iota(I32, (cb, t), 0)
        placed = jnp.where(rows == lpos[0:1, :], gates[0:1, :], 0.0)
        for k in range(1, TOP_K):
            placed = placed + jnp.where(rows == lpos[k:k + 1, :], gates[k:k + 1, :], 0.0)
        row_gate = jnp.sum(placed, axis=1, keepdims=True)
        yb = (ybuf[slot] * row_gate).astype(BF16)
        lpt = lpos.astype(F32).T.astype(I32)
        cols = lax.broadcasted_iota(I32, (t, cb), 1)
        unsort = jnp.where(cols == lpt[:, 0:1], 1.0, 0.0)
        for k in range(1, TOP_K):
            unsort = unsort + jnp.where(cols == lpt[:, k:k + 1], 1.0, 0.0)
        y = _dot(unsort.astype(BF16), yb)
        x2 = x1_ref[...] + g2_ref[0] * y
        if not final:
            o_refs[0][...] = x2
        else:
            x2 = x2 * lax.rsqrt(jnp.mean(x2 * x2, axis=-1, keepdims=True) + EPS) * fg_ref[...]

            @pl.when(step < ctx_tiles)
            def _():
                o_refs[0][...] = x2

            @pl.when(step >= ctx_tiles)
            def _():
                o_refs[1][...] = x2
    return kern


def _combine(tables, ys, x1, lpos_t, gates_t, mod, final_g, final, n_ctx, dec_seq):
    n, d = x1.shape
    t = MOVE_TILE
    ctx_tiles = n_ctx // t
    grp = functools.partial(_group_of_tile, tile=t, n_ctx=n_ctx, dec_seq=dec_seq)
    if final:
        out_specs = [pl.BlockSpec((t, d), lambda i, *_: (jnp.minimum(i, ctx_tiles - 1), 0)),
                     pl.BlockSpec((t, d), lambda i, *_: (jnp.maximum(i - ctx_tiles, 0), 0))]
        out_shape = [jax.ShapeDtypeStruct((n_ctx, d), F32), jax.ShapeDtypeStruct((n - n_ctx, d), F32)]
    else:
        out_specs = pl.BlockSpec((t, d), lambda i, *_: (i, 0))
        out_shape = jax.ShapeDtypeStruct((n, d), F32)
    return pl.pallas_call(
        _make_combine_kernel(n // t, ctx_tiles, final),
        grid_spec=pltpu.PrefetchScalarGridSpec(
            num_scalar_prefetch=3,
            grid=(n // t,),
            in_specs=[pl.BlockSpec(memory_space=pl.ANY),
                      pl.BlockSpec((t, d), lambda i, *_: (i, 0)),
                      pl.BlockSpec((8, t), lambda i, *_: (0, i)),
                      pl.BlockSpec((8, t), lambda i, *_: (0, i)),
                      pl.BlockSpec((1, 1, d), lambda i, *_: (grp(i) * 6 + MOD_G2, 0, 0)),
                      pl.BlockSpec((1, d), lambda i, *_: (0, 0))],
            out_specs=out_specs,
            scratch_shapes=[pltpu.VMEM((2, COMPACT_ROWS, d), F32), pltpu.SMEM((2,), I32),
                            pltpu.SemaphoreType.DMA((2,))],
        ),
        out_shape=out_shape,
        compiler_params=_cparams("arbitrary"),
    )(*tables, ys, x1, lpos_t, gates_t, mod, final_g)


def _blockdiag_pairs(s):
    z = jnp.zeros_like(s[..., 0, :, :])
    def pair(a, b):
        return jnp.concatenate([jnp.concatenate([a, z], axis=-1), jnp.concatenate([z, b], axis=-1)], axis=-2)
    return jnp.stack([pair(s[..., 0, :, :], s[..., 1, :, :]), pair(s[..., 2, :, :], s[..., 3, :, :])], axis=-3)


def _diag_blocks(st):
    h = HEAD_DIM
    blocks = [st[:, :, c, j * h:(j + 1) * h, j * h:(j + 1) * h] for c in range(2) for j in range(2)]
    return jnp.stack(blocks, axis=2)


def kernel(x_prompt, x_sample, cache_k, cache_v, state_ret, c, c_ctx, w_mod, b_mod, norm1_g, norm2_g, w_in,
           q_norm_g, k_norm_g, conv_w, conv_b, conv_ln_g, conv_ln_b, ret_decay_logit, ret_gn_g, w_out,
           router_w, router_b, moe_w_gu, moe_b_gu, moe_w_dn, moe_b_dn, final_g):
    batch, seq, d = x_prompt.shape
    dec_batch, dec_seq, _ = x_sample.shape
    depth = w_mod.shape[0]
    past = cache_k.shape[2]
    n_ctx = batch * seq
    n_lat = dec_batch * dec_seq
    n = n_ctx + n_lat
    assert d == D_MODEL and dec_batch == 2
    assert n_ctx % dec_seq == 0 and dec_seq % TOK_TILE == 0 and dec_seq % ATTN_QB == 0
    assert n % (MOVE_TILE * ROUTE_TILES) == 0

    x_parts = (x_prompt.reshape(n_ctx, d), x_sample.reshape(n_lat, d))
    mods = _modulation(jnp.concatenate([c_ctx[None, :], c], axis=0), w_mod, b_mod)
    rope = _rope_tables(dec_seq)

    head_of_col = np.arange(QK_W) // HEAD_DIM
    hsum_np = (head_of_col[:, None] == np.arange(LANES)[None, :]).astype(np.float32)
    hsum = jnp.asarray(hsum_np, BF16)
    hbc = jnp.asarray(hsum_np.T, BF16)
    lane_head = np.arange(LANES) // HEAD_DIM
    gmat = jnp.asarray((lane_head[:, None] == lane_head[None, :]).astype(np.float32) / HEAD_DIM, BF16)
    tt = np.arange(MOVE_TILE)
    upper = jnp.asarray((tt[:, None] < tt[None, :]).astype(np.float32), BF16)
    ee = jnp.arange(N_EXPERTS)
    ltri = jnp.asarray((np.arange(N_EXPERTS)[None, :] < np.arange(N_EXPERTS)[:, None]).astype(np.float32))

    n_tiles = n // MOVE_TILE
    max_rows = n * TOP_K + n_tiles * N_EXPERTS * (SUBLANES - 1) + N_EXPERTS * (MOE_BM - 1)
    n_blocks = -(-max_rows // MOE_BM)
    cache_k2 = cache_k.reshape(dec_batch, depth, past, KV_W)
    cache_v2 = cache_v.reshape(dec_batch, depth, past, KV_W)

    ks_out, vs_out, ss_out = [], [], []
    for l in range(depth):
        mod = mods[l]
        gqk = jnp.concatenate([jnp.tile(q_norm_g[l], N_Q_HEADS), jnp.tile(k_norm_g[l], N_KV_HEADS)])[None, :]
        p, kn, vv = _inproj(x_parts, mod, norm1_g[l][None, :], w_in[l].astype(BF16), gqk, hsum, hbc, rope,
                            n_ctx, dec_seq)
        ks_out.append(kn.reshape(batch, seq, N_KV_HEADS, HEAD_DIM))
        vs_out.append(vv.reshape(batch, seq, N_KV_HEADS, HEAD_DIM))

        attn = (_attention(p, batch, seq, 0, None, l),
                _attention(p, dec_batch, dec_seq, n_ctx, (cache_k2, cache_v2), l))

        cw, cb = conv_w[l], conv_b[l][None, :]
        clg, clb = conv_ln_g[l][None, :], conv_ln_b[l][None, :]
        conv = (_conv(p, batch, seq, 0, cw, cb, clg, clb),
                _conv(p, dec_batch, dec_seq, n_ctx, cw, cb, clg, clb))

        log_g = jax.nn.log_sigmoid(ret_decay_logit[l].astype(F32))
        gn = ret_gn_g[l][None, :]
        ret_ctx, st_ctx = _retention(p, batch, seq, 0, log_g, gn, gmat, None)
        ret_lat, _ = _retention(p, dec_batch, dec_seq, n_ctx, log_g, gn, gmat,
                                _blockdiag_pairs(state_ret[:, l].astype(F32)))
        ret = (ret_ctx, ret_lat)
        ss_out.append(_diag_blocks(st_ctx))

        x1, h2, top_e, top_g = _outproj(attn, conv, ret, x_parts, mod, norm2_g[l][None, :], w_out[l].astype(BF16),
                                        router_w[l].T, router_b[l][:, None], n_ctx, dec_seq)
        lpos, run8, loff, goff, seg = _route(top_e, upper, ltri)
        tables = [tb[:, :, 0].reshape(-1) for tb in (run8, loff, goff)]
        seg = seg[:, 0].astype(I32)
        padded = (seg + MOE_BM - 1) // MOE_BM * MOE_BM
        pad_end = jnp.cumsum(padded)
        n_valid = (pad_end[-1] // MOE_BM).reshape(1)
        blk_src = jnp.minimum(jnp.arange(n_blocks, dtype=I32), n_valid - 1)
        blk_e = jnp.minimum(jnp.sum((pad_end[None, :] <= (blk_src * MOE_BM)[:, None]).astype(I32), axis=1),
                            N_EXPERTS - 1)
        owns = padded > 0
        ord_e = jnp.cumsum(owns.astype(I32)) - 1
        next_e = jnp.sum(jnp.where(owns[None, :] & (ord_e[None, :] == ord_e[:, None] + 1),
                                   ee[None, :] + 1, 0), axis=1).astype(I32) - 1

        xs = _dispatch(tables, seg, pad_end - padded, n_valid, h2, lpos, n_blocks)
        of_blk = blk_e[:, None] == ee[None, :]
        per_blk = lambda table: jnp.sum(jnp.where(of_blk, table[None, :], 0), axis=1)
        ys = _experts(blk_e, blk_src, n_valid, per_blk(ord_e), per_blk(next_e), xs,
                      moe_w_gu, moe_b_gu, moe_w_dn, moe_b_dn, l, n_blocks)
        out = _combine(tables, ys, x1, lpos, top_g, mod, final_g[None, :], l == depth - 1, n_ctx, dec_seq)
        x_parts = (out,)

    y_prompt = out[0].reshape(batch, seq, d)
    y_sample = out[1].reshape(dec_batch, dec_seq, d)
    return (y_prompt, y_sample, jnp.stack(ks_out, axis=1), jnp.stack(vs_out, axis=1),
            jnp.stack(ss_out, axis=1))
```

```python
import functools

import numpy as np
import jax
import jax.numpy as jnp
from jax import lax
from jax.experimental import pallas as pl
from jax.experimental.pallas import tpu as pltpu

F32 = jnp.float32
BF16 = jnp.bfloat16
I32 = jnp.int32

D_MODEL = 1024
GRID_W = 64
HEAD_DIM = 64
N_Q_HEADS = 8
N_KV_HEADS = 2
ATTN_W = N_Q_HEADS * HEAD_DIM
KV_W = N_KV_HEADS * HEAD_DIM
QK_W = ATTN_W + KV_W
CONV_CH = 256
CONV_K = 31
CONV_PAD = CONV_K // 2
CONV_HALO = 16
N_RET_HEADS = 4
RET_W = 256
OFF_V = QK_W
OFF_CONV = OFF_V + KV_W
OFF_RET = OFF_CONV + 2 * CONV_CH
IN_COLS = OFF_RET + 4 * RET_W
ROPE_HALF = HEAD_DIM // 2
ROPE_THETA = 10000.0
N_EXPERTS = 32
TOP_K = 4
D_FF = D_MODEL
SWIGLU_LIMIT = 7.0
SWIGLU_ALPHA = 1.702
EPS = 1e-6
LN_EPS = 1e-5

LANES = 128
TOK_TILE = 1024
ATTN_QB = 512
MIXER_ROWS = 2048
RET_QB = 256
CONV_ROWS = 64
MOE_BM = 256
MOVE_TILE = 256
ROUTE_TILES = 4
SUBLANES = 8
COMPACT_ROWS = 1280
assert COMPACT_ROWS >= MOVE_TILE * TOP_K + N_EXPERTS * (SUBLANES - 1) and COMPACT_ROWS % MOE_BM == 0
CHUNK_SIZES = tuple(MOVE_TILE >> s for s in range(6))
RARE_SIZES = 3
VMEM_LIMIT = 56 * 1024 * 1024

MOD_SH1, MOD_SC1, MOD_G1, MOD_SH2, MOD_SC2, MOD_G2 = range(6)


def _cparams(*sem):
    return pltpu.CompilerParams(dimension_semantics=sem, vmem_limit_bytes=VMEM_LIMIT)


def _dot(a, b, **kw):
    return jnp.dot(a, b, preferred_element_type=F32, **kw)


def _dot_nt(a, b, **kw):
    return lax.dot_general(a, b, (((1,), (1,)), ((), ())), preferred_element_type=F32, **kw)


def _dot_tn(a, b, **kw):
    return lax.dot_general(a, b, (((0,), (0,)), ((), ())), preferred_element_type=F32, **kw)


def _split_bf16(x):
    hi = x.astype(BF16)
    lo = (x - hi.astype(F32)).astype(BF16)
    return hi, lo


MOD_TN = 1536


def _mod_kernel(ct_ref, w_ref, b_ref, o_ref):
    s = ct_ref[...]
    s = s * jax.nn.sigmoid(s)
    w = w_ref[0]
    rows = [jnp.sum(w * s[:, r:r + 1], axis=0, keepdims=True) for r in range(3)]
    rows.append(jnp.zeros((5, w.shape[1]), F32))
    o_ref[0] = jnp.concatenate(rows, axis=0) + b_ref[0]


def _modulation(cvec3, w_mod, b_mod):
    depth, d, cols = w_mod.shape
    ct = jnp.zeros((d, 8), F32).at[:, :3].set(cvec3.T)
    out = pl.pallas_call(
        _mod_kernel,
        grid=(depth, cols // MOD_TN),
        in_specs=[
            pl.BlockSpec((d, 8), lambda l, j: (0, 0)),
            pl.BlockSpec((1, d, MOD_TN), lambda l, j: (l, 0, j)),
            pl.BlockSpec((1, 1, MOD_TN), lambda l, j: (l, 0, j)),
        ],
        out_specs=pl.BlockSpec((1, 8, MOD_TN), lambda l, j: (l, 0, j)),
        out_shape=jax.ShapeDtypeStruct((depth, 8, cols), F32),
        compiler_params=_cparams("arbitrary", "arbitrary"),
    )(ct, w_mod, b_mod.reshape(depth, 1, cols))
    return out[:, :3].reshape(depth, 3 * 6, 1, d)


def _token_specs(parts, tile, n_ctx):
    d = parts[0].shape[1]
    if len(parts) == 1:
        return [pl.BlockSpec((tile, d), lambda i, *_: (i, 0))]
    ctx_tiles = n_ctx // tile
    return [pl.BlockSpec((tile, d), lambda i, *_: (jnp.minimum(i, ctx_tiles - 1), 0)),
            pl.BlockSpec((tile, d), lambda i, *_: (jnp.maximum(i - ctx_tiles, 0), 0))]


def _token_tile(refs, ctx_tiles):
    if len(refs) == 1:
        return refs[0][...]
    return jnp.where(pl.program_id(0) < ctx_tiles, refs[0][...], refs[1][...])


def _inproj_kernel(n_src, ctx_tiles, *refs):
    x_refs = refs[:n_src]
    (sh_ref, sc_ref, g_ref, w_ref, gqk_ref, hsum_ref, hbc_ref, cos_ref, sa_ref, sb_ref,
     p_ref, k_ref, v_ref) = refs[n_src:]
    x = _token_tile(x_refs, ctx_tiles)
    inv = lax.rsqrt(jnp.mean(x * x, axis=-1, keepdims=True) + EPS)
    h = (x * inv * g_ref[...]) * (1.0 + sc_ref[0]) + sh_ref[0]
    acc = _dot(h.astype(BF16), w_ref[...])
    qk = acc[:, :QK_W]
    ss = _dot((qk * qk).astype(BF16), hsum_ref[...])
    r = lax.rsqrt(ss * (1.0 / HEAD_DIM) + EPS)
    r_hi, r_lo = _split_bf16(r)
    rb = _dot(r_hi, hbc_ref[...]) + _dot(r_lo, hbc_ref[...])
    qkn = qk * rb * gqk_ref[...]
    cos = cos_ref[...]
    sa = sa_ref[...]
    sb = sb_ref[...]
    for j in range(QK_W // LANES):
        c = qkn[:, j * LANES:(j + 1) * LANES]
        up = pltpu.roll(c, LANES - ROPE_HALF // 2, 1)
        dn = pltpu.roll(c, ROPE_HALF // 2, 1)
        p_ref[:, j * LANES:(j + 1) * LANES] = (c * cos + up * sa + dn * sb).astype(BF16)
    p_ref[:, QK_W:] = acc[:, QK_W:].astype(BF16)

    @pl.when(pl.program_id(0) < ctx_tiles)
    def _():
        k_ref[...] = qkn[:, ATTN_W:QK_W]
        v_ref[...] = acc[:, OFF_V:OFF_CONV]


def _rope_tables(dec_seq):
    f32 = np.float32
    rows = dec_seq // GRID_W
    row = np.repeat(np.arange(rows), GRID_W).astype(f32)
    col = np.tile(np.arange(GRID_W), rows).astype(f32)
    inv = (f32(1.0) / (f32(ROPE_THETA) ** (np.arange(0, ROPE_HALF, 2).astype(f32) / f32(ROPE_HALF)))).astype(f32)
    ar = row[:, None] * inv[None, :]
    ac = col[:, None] * inv[None, :]
    cos = np.concatenate([np.cos(ar), np.cos(ar), np.cos(ac), np.cos(ac)], axis=-1)
    sin = np.concatenate([np.sin(ar), np.sin(ar), np.sin(ac), np.sin(ac)], axis=-1)
    first = (np.arange(HEAD_DIM) % ROPE_HALF) < ROPE_HALF // 2
    sa = np.where(first[None, :], -sin, 0.0)
    sb = np.where(first[None, :], 0.0, sin)
    def table(t, ident):
        t = np.concatenate([np.full((TOK_TILE, HEAD_DIM), ident, f32), t.astype(f32)], axis=0)
        return jnp.asarray(np.tile(t, (1, LANES // HEAD_DIM)))
    return table(cos, 1.0), table(sa, 0.0), table(sb, 0.0)


def _group_of_tile(i, tile, n_ctx, dec_seq):
    tok = i * tile
    return jnp.where(tok < n_ctx, 0, 1 + (tok - n_ctx) // dec_seq)


def _inproj(x_parts, mod, norm_g, w_in_bf, gqk, hsum, hbc, rope, n_ctx, dec_seq):
    n = sum(part.shape[0] for part in x_parts)
    d = x_parts[0].shape[1]
    t = TOK_TILE
    grp = functools.partial(_group_of_tile, tile=t, n_ctx=n_ctx, dec_seq=dec_seq)

    def mod_spec(which):
        return pl.BlockSpec((1, 1, d), lambda i: (grp(i) * 6 + which, 0, 0))

    def rope_idx(i):
        tok = i * t
        return (jnp.where(tok < n_ctx, 0, 1 + ((tok - n_ctx) % dec_seq) // t), 0)

    rope_spec = pl.BlockSpec((t, LANES), rope_idx)
    const = lambda shape: pl.BlockSpec(shape, lambda i: (0,) * len(shape))
    return pl.pallas_call(
        functools.partial(_inproj_kernel, len(x_parts), n_ctx // t),
        grid=(n // t,),
        in_specs=_token_specs(x_parts, t, n_ctx) + [
            mod_spec(MOD_SH1), mod_spec(MOD_SC1),
            const((1, d)),
            const((d, IN_COLS)),
            const((1, QK_W)), const((QK_W, LANES)), const((LANES, QK_W)),
            rope_spec, rope_spec, rope_spec,
        ],
        out_specs=[
            pl.BlockSpec((t, IN_COLS), lambda i: (i, 0)),
            pl.BlockSpec((t, KV_W), lambda i: (jnp.minimum(i, n_ctx // t - 1), 0)),
            pl.BlockSpec((t, KV_W), lambda i: (jnp.minimum(i, n_ctx // t - 1), 0)),
        ],
        out_shape=[
            jax.ShapeDtypeStruct((n, IN_COLS), BF16),
            jax.ShapeDtypeStruct((n_ctx, KV_W), F32),
            jax.ShapeDtypeStruct((n_ctx, KV_W), F32),
        ],
        compiler_params=_cparams("arbitrary"),
    )(*x_parts, mod, mod, norm_g, w_in_bf, gqk, hsum, hbc, *rope)


def _head_halves(x, hkv, low):
    r = pltpu.roll(x, HEAD_DIM, 1)
    rep = jnp.where(low, x, r) if hkv == 0 else jnp.where(low, r, x)
    return jnp.where(low, rep, 0.0).astype(BF16), jnp.where(low, 0.0, rep).astype(BF16)


def _make_attn_kernel(has_ctx, group, seq_len):
    def kern(*refs):
        if has_ctx:
            q_ref, k_ref, v_ref, ck_ref, cv_ref, o_ref = refs
        else:
            q_ref, k_ref, v_ref, o_ref = refs
        low = lax.broadcasted_iota(I32, (1, LANES), 1) < HEAD_DIM
        scale = HEAD_DIM ** -0.5
        for g in range(group):
            qrows = slice(g * seq_len, (g + 1) * seq_len) if group > 1 else slice(None)
            k = k_ref[qrows, :].astype(F32)
            v = v_ref[qrows, :].astype(F32)
            if has_ctx:
                ck = ck_ref[0, 0]
                cv = cv_ref[0, 0]
            for hkv in range(N_KV_HEADS):
                kh = _head_halves(k, hkv, low)
                vh = _head_halves(v, hkv, low)
                if has_ctx:
                    ckh = _head_halves(ck, hkv, low)
                    cvh = _head_halves(cv, hkv, low)
                for c in range(2):
                    col = hkv * 2 * LANES + c * LANES
                    qc = q_ref[qrows, col:col + LANES] * scale
                    o_c = None
                    for j in range(2):
                        s = _dot_nt(qc, kh[j])
                        m = jnp.max(s, axis=-1, keepdims=True)
                        if has_ctx:
                            s2 = _dot_nt(qc, ckh[j])
                            m = jnp.maximum(m, jnp.max(s2, axis=-1, keepdims=True))
                        p = jnp.exp(s - m)
                        l = jnp.sum(p, axis=-1, keepdims=True)
                        o = _dot(p.astype(BF16), vh[j])
                        if has_ctx:
                            p2 = jnp.exp(s2 - m)
                            l = l + jnp.sum(p2, axis=-1, keepdims=True)
                            o = o + _dot(p2.astype(BF16), cvh[j])
                        o = o / l
                        o_c = o if o_c is None else o_c + o
                    o_ref[qrows, col:col + LANES] = o_c.astype(BF16)
    return kern


def _attention(p, n_seq, seq_len, row0, ctx_kv, layer):
    has_ctx = ctx_kv is not None
    qb = min(seq_len, ATTN_QB)
    nq = seq_len // qb
    group = _seq_group(n_seq, seq_len, row0) if (nq == 1 and not has_ctx) else 1
    rows = group * seq_len
    qrows = group * qb
    qrow0 = row0 // qrows
    srow0 = row0 // rows
    in_specs = [
        pl.BlockSpec((qrows, ATTN_W), lambda b, i: (qrow0 + b * nq + i, 0)),
        pl.BlockSpec((rows, KV_W), lambda b, i: (srow0 + b, ATTN_W // KV_W)),
        pl.BlockSpec((rows, KV_W), lambda b, i: (srow0 + b, OFF_V // KV_W)),
    ]
    args = [p, p, p]
    if has_ctx:
        ck, cv = ctx_kv
        past = ck.shape[2]
        spec = pl.BlockSpec((1, 1, past, KV_W), lambda b, i: (b, layer, 0, 0))
        in_specs += [spec, spec]
        args += [ck, cv]
    return pl.pallas_call(
        _make_attn_kernel(has_ctx, group, seq_len),
        grid=(n_seq // group, nq),
        in_specs=in_specs,
        out_specs=pl.BlockSpec((qrows, ATTN_W), lambda b, i: (b * nq + i, 0)),
        out_shape=jax.ShapeDtypeStruct((n_seq * seq_len, ATTN_W), BF16),
        compiler_params=_cparams("arbitrary", "arbitrary"),
    )(*args)


def _make_conv_kernel(seq_len, group):
    shifted_rows = seq_len + 2 * CONV_HALO - SUBLANES

    def kern(a_ref, g_ref, w_ref, b_ref, lng_ref, lnb_ref, o_ref, zp_ref, zs_ref):
        zero = jnp.zeros((CONV_HALO, CONV_CH), F32)
        w = w_ref[...]
        bias = b_ref[...]
        for q in range(group):
            r0 = q * seq_len
            zp_ref[q, 0:CONV_HALO, :] = zero
            zp_ref[q, CONV_HALO + seq_len:2 * CONV_HALO + seq_len, :] = zero
            zp_ref[q, CONV_HALO:CONV_HALO + seq_len, :] = (
                a_ref[r0:r0 + seq_len, :].astype(F32) * jax.nn.sigmoid(g_ref[r0:r0 + seq_len, :].astype(F32)))
            for s in range(1, SUBLANES):
                zs_ref[q, s] = zp_ref[q, s:s + shifted_rows, :]
            for c in range(seq_len // CONV_ROWS):
                base = c * CONV_ROWS + CONV_HALO - CONV_PAD
                acc = jnp.zeros((CONV_ROWS, CONV_CH), F32) + bias
                for j in range(CONV_K):
                    shift = (base + j) % SUBLANES
                    row = base + j - shift
                    if shift == 0:
                        tap = zp_ref[q, row:row + CONV_ROWS, :]
                    else:
                        tap = zs_ref[q, shift, row:row + CONV_ROWS, :]
                    acc = acc + tap * w[j:j + 1, :]
                mu = jnp.mean(acc, axis=-1, keepdims=True)
                dlt = acc - mu
                var = jnp.mean(dlt * dlt, axis=-1, keepdims=True)
                y = dlt * lax.rsqrt(var + LN_EPS) * lng_ref[...] + lnb_ref[...]
                o_ref[r0 + c * CONV_ROWS:r0 + (c + 1) * CONV_ROWS, :] = (y * jax.nn.sigmoid(y)).astype(BF16)
    return kern


def _seq_group(n_seq, seq_len, row0):
    group = max(1, MIXER_ROWS // seq_len)
    if n_seq % group or row0 % (group * seq_len):
        group = 1
    return group


def _conv(p, n_seq, seq_len, row0, w, b, lng, lnb):
    group = _seq_group(n_seq, seq_len, row0)
    rows = group * seq_len
    srow0 = row0 // rows
    const = lambda shape: pl.BlockSpec(shape, lambda s: (0,) * len(shape))
    return pl.pallas_call(
        _make_conv_kernel(seq_len, group),
        grid=(n_seq // group,),
        in_specs=[
            pl.BlockSpec((rows, CONV_CH), lambda s: (srow0 + s, OFF_CONV // CONV_CH)),
            pl.BlockSpec((rows, CONV_CH), lambda s: (srow0 + s, OFF_CONV // CONV_CH + 1)),
            const((CONV_K, CONV_CH)), const((1, CONV_CH)), const((1, CONV_CH)), const((1, CONV_CH)),
        ],
        out_specs=pl.BlockSpec((rows, CONV_CH), lambda s: (s, 0)),
        out_shape=jax.ShapeDtypeStruct((n_seq * seq_len, CONV_CH), BF16),
        scratch_shapes=[pltpu.VMEM((group, seq_len + 2 * CONV_HALO, CONV_CH), F32),
                        pltpu.VMEM((group, SUBLANES, seq_len + 2 * CONV_HALO - SUBLANES, CONV_CH), F32)],
        compiler_params=_cparams("arbitrary"),
    )(p, p, w, b, lng, lnb)


def _make_ret_kernel(seq_len, has_init, group):
    qb = min(seq_len, RET_QB)
    nq = seq_len // qb
    scale = HEAD_DIM ** -0.5

    def kern(*refs):
        if has_init:
            lg_ref, q_ref, k_ref, v_ref, g_ref, gn_ref, gm_ref, r0_ref, o_ref, st_ref = refs
        else:
            lg_ref, q_ref, k_ref, v_ref, g_ref, gn_ref, gm_ref, o_ref, st_ref = refs
        low = lax.broadcasted_iota(I32, (1, LANES), 1) < HEAD_DIM
        pos = lax.broadcasted_iota(I32, (seq_len, 1), 0).astype(F32)
        qpos = lax.broadcasted_iota(I32, (qb, 1), 0).astype(F32)
        kpos = lax.broadcasted_iota(I32, (1, seq_len), 1).astype(F32)
        gm = gm_ref[...]
        for c in range(2):
            cs = slice(c * LANES, (c + 1) * LANES)
            lgf = jnp.where(low, lg_ref[0, 2 * c], lg_ref[0, 2 * c + 1])
            lgb = jnp.where(low, lg_ref[1, 2 * c], lg_ref[1, 2 * c + 1])
            zeta_f = jnp.exp((seq_len - 1.0 - pos) * lgf)
            zeta_b = jnp.exp(pos * lgb)
            decs = []
            for j in range(2):
                lf = lg_ref[0, 2 * c + j]
                lb = lg_ref[1, 2 * c + j]
                row = []
                for i in range(nq):
                    diff = (qpos + float(i * qb)) - kpos
                    dec = jnp.exp(jnp.where(diff >= 0, diff * lf, -diff * lb))
                    row.append(dec * jnp.where(diff == 0, 2.0 * scale, scale))
                decs.append(row)
            for q in range(group):
                rs = slice(q * seq_len, (q + 1) * seq_len)
                qc = q_ref[rs, cs]
                kc = k_ref[rs, cs]
                vc = v_ref[rs, cs]
                kf = kc.astype(F32) * scale
                for d, zeta, lgd in ((0, zeta_f, lgf), (1, zeta_b, lgb)):
                    st = _dot_tn((kf * zeta).astype(BF16), vc)
                    if has_init:
                        st = st + r0_ref[q, d, c] * jnp.exp(seq_len * lgd)
                    st_ref[q, d, c] = st
                y_blocks = [None] * nq
                for j in range(2):
                    sel = low if j == 0 else jnp.logical_not(low)
                    kh = jnp.where(sel, kc, jnp.zeros_like(kc))
                    vh = jnp.where(sel, vc, jnp.zeros_like(vc))
                    for i in range(nq):
                        s = _dot_nt(qc[i * qb:(i + 1) * qb], kh)
                        y = _dot((s * decs[j][i]).astype(BF16), vh)
                        y_blocks[i] = y if y_blocks[i] is None else y_blocks[i] + y
                y = jnp.concatenate(y_blocks, axis=0) if nq > 1 else y_blocks[0]
                if has_init:
                    xi_f = jnp.exp((pos + 1.0) * lgf)
                    xi_b = jnp.exp((seq_len - pos) * lgb)
                    y = y + _dot(qc, r0_ref[q, 0, c].astype(BF16)) * xi_f
                    y = y + _dot(qc, r0_ref[q, 1, c].astype(BF16)) * xi_b
                y_hi, y_lo = _split_bf16(y)
                mu = _dot(y_hi, gm) + _dot(y_lo, gm)
                dlt = y - mu
                var = _dot((dlt * dlt).astype(BF16), gm)
                yn = dlt * lax.rsqrt(var + LN_EPS) * gn_ref[:, cs]
                gate = g_ref[rs, cs].astype(F32)
                o_ref[rs, cs] = (gate * jax.nn.sigmoid(gate) * yn).astype(BF16)
    return kern


def _retention(p, n_seq, seq_len, row0, log_g, gn_g, gmat, r0):
    group = _seq_group(n_seq, seq_len, row0)
    rows = group * seq_len
    srow0 = row0 // rows
    has_init = r0 is not None
    col = OFF_RET // RET_W
    in_specs = [pl.BlockSpec(memory_space=pltpu.SMEM)]
    in_specs += [pl.BlockSpec((rows, RET_W), functools.partial(lambda s, j: (srow0 + s, col + j), j=j))
                 for j in range(4)]
    in_specs += [pl.BlockSpec((1, RET_W), lambda s: (0, 0)), pl.BlockSpec((LANES, LANES), lambda s: (0, 0))]
    args = [log_g, p, p, p, p, gn_g, gmat]
    st_spec = pl.BlockSpec((group, 2, 2, LANES, LANES), lambda s: (s, 0, 0, 0, 0))
    if has_init:
        in_specs.append(st_spec)
        args.append(r0)
    return pl.pallas_call(
        _make_ret_kernel(seq_len, has_init, group),
        grid=(n_seq // group,),
        in_specs=in_specs,
        out_specs=[pl.BlockSpec((rows, RET_W), lambda s: (s, 0)), st_spec],
        out_shape=[jax.ShapeDtypeStruct((n_seq * seq_len, RET_W), BF16),
                   jax.ShapeDtypeStruct((n_seq, 2, 2, LANES, LANES), F32)],
        compiler_params=_cparams("arbitrary"),
    )(*args)


def _outproj_kernel(n_src, ctx_tiles, *refs):
    x_refs = refs[:n_src]
    a_refs, c_refs, r_refs = refs[n_src:n_src + 2], refs[n_src + 2:n_src + 4], refs[n_src + 4:n_src + 6]
    (g1_ref, sc_ref, sh_ref, n2_ref, wo_ref, rwt_ref, rb_ref,
     x1_ref, h2_ref, te_ref, tg_ref) = refs[n_src + 6:]
    mixed = (_dot(_token_tile(a_refs, ctx_tiles), wo_ref[0:ATTN_W, :])
             + _dot(_token_tile(c_refs, ctx_tiles), wo_ref[ATTN_W:ATTN_W + CONV_CH, :])
             + _dot(_token_tile(r_refs, ctx_tiles), wo_ref[ATTN_W + CONV_CH:, :]))
    x1 = _token_tile(x_refs, ctx_tiles) + g1_ref[0] * mixed
    x1_ref[...] = x1
    inv = lax.rsqrt(jnp.mean(x1 * x1, axis=-1, keepdims=True) + EPS)
    h2 = (x1 * inv * n2_ref[...]) * (1.0 + sc_ref[0]) + sh_ref[0]
    h2_ref[...] = h2.astype(BF16)
    w_hi, w_lo = _split_bf16(rwt_ref[...])
    h_hi, h_lo = _split_bf16(h2)
    logits = _dot_nt(w_hi, h_hi) + _dot_nt(w_hi, h_lo) + _dot_nt(w_lo, h_hi) + rb_ref[...]
    t = logits.shape[1]
    eidx = lax.broadcasted_iota(I32, (N_EXPERTS, t), 0).astype(F32)
    vals = logits
    tops, idxs = [], []
    for _ in range(TOP_K):
        m = jnp.max(vals, axis=0, keepdims=True)
        idx = jnp.min(jnp.where(vals == m, eidx, float(N_EXPERTS)), axis=0, keepdims=True)
        tops.append(m)
        idxs.append(idx)
        vals = jnp.where(eidx == idx, -jnp.inf, vals)
    es = [jnp.exp(m - tops[0]) for m in tops]
    tot = es[0] + es[1] + es[2] + es[3]
    te_ref[...] = jnp.concatenate(idxs + [jnp.zeros((8 - TOP_K, t), F32)], axis=0).astype(I32)
    tg_ref[...] = jnp.concatenate([e / tot for e in es] + [jnp.zeros((8 - TOP_K, t), F32)], axis=0)


def _outproj(attn, conv, ret, x_parts, mod, norm_g, w_out_bf, rwt, rb, n_ctx, dec_seq):
    n = attn[0].shape[0] + attn[1].shape[0]
    d = x_parts[0].shape[1]
    t = TOK_TILE
    grp = functools.partial(_group_of_tile, tile=t, n_ctx=n_ctx, dec_seq=dec_seq)

    def mod_spec(which):
        return pl.BlockSpec((1, 1, d), lambda i: (grp(i) * 6 + which, 0, 0))

    const = lambda shape: pl.BlockSpec(shape, lambda i: (0,) * len(shape))
    row = lambda w: pl.BlockSpec((t, w), lambda i: (i, 0))
    lane = lambda: pl.BlockSpec((8, t), lambda i: (0, i))
    return pl.pallas_call(
        functools.partial(_outproj_kernel, len(x_parts), n_ctx // t),
        grid=(n // t,),
        in_specs=_token_specs(x_parts, t, n_ctx) + _token_specs(attn, t, n_ctx)
        + _token_specs(conv, t, n_ctx) + _token_specs(ret, t, n_ctx) + [
            mod_spec(MOD_G1), mod_spec(MOD_SC2), mod_spec(MOD_SH2),
            const((1, d)), const((d, d)), const((N_EXPERTS, d)), const((N_EXPERTS, 1))],
        out_specs=[row(d), row(d), lane(), lane()],
        out_shape=[jax.ShapeDtypeStruct((n, d), F32), jax.ShapeDtypeStruct((n, d), BF16),
                   jax.ShapeDtypeStruct((8, n), I32), jax.ShapeDtypeStruct((8, n), F32)],
        compiler_params=_cparams("arbitrary"),
    )(*x_parts, *attn, *conv, *ret, mod, mod, mod, norm_g, w_out_bf, rwt, rb)


def _round_up(x, m):
    return jnp.floor((x + (m - 1.0)) * (1.0 / m)) * m


def _route_kernel(te_ref, u_ref, ltri_ref, lpos_ref, run8_ref, loff_ref, goff_ref, seg_ref, run_ref, start_ref):
    ph = pl.program_id(0)
    i = pl.program_id(1)
    t = MOVE_TILE
    eidx = lax.broadcasted_iota(I32, (N_EXPERTS, t), 0)

    @pl.when(jnp.logical_and(ph == 0, i == 0))
    def _():
        run_ref[...] = jnp.zeros_like(run_ref)

    @pl.when(jnp.logical_and(ph == 1, i == 0))
    def _():
        seg = run_ref[...]
        seg_ref[...] = seg
        start_ref[...] = _dot(ltri_ref[...], _round_up(seg, MOE_BM), precision=lax.Precision.HIGHEST)
        run_ref[...] = jnp.zeros_like(run_ref)

    for s in range(ROUTE_TILES):
        te = te_ref[:, s * t:(s + 1) * t]
        hits = [eidx == te[k:k + 1, :] for k in range(TOP_K)]
        onehot = sum(h.astype(F32) for h in hits)
        run8 = _round_up(jnp.sum(onehot, axis=1, keepdims=True), SUBLANES)

        @pl.when(ph == 0)
        def _(run8=run8):
            run_ref[...] = run_ref[...] + run8

        @pl.when(ph == 1)
        def _(s=s, hits=hits, onehot=onehot, run8=run8):
            before = _dot(onehot.astype(BF16), u_ref[...])
            run8_b = jnp.broadcast_to(run8, (N_EXPERTS, LANES))
            loff = _dot(ltri_ref[...], run8_b, precision=lax.Precision.HIGHEST)
            base = before + loff[:, 0:1]
            rows = [jnp.sum(jnp.where(h, base, 0.0), axis=0, keepdims=True) for h in hits]
            rows.append(jnp.zeros((8 - TOP_K, t), F32))
            lpos_ref[:, s * t:(s + 1) * t] = jnp.concatenate(rows, axis=0).astype(I32)
            run8_ref[s] = run8_b.astype(I32)
            loff_ref[s] = loff.astype(I32)
            goff_ref[s] = (start_ref[...] + run_ref[...]).astype(I32)
            run_ref[...] = run_ref[...] + run8


def _route(top_e, upper, ltri):
    n = top_e.shape[1]
    t = MOVE_TILE
    nt = n // t
    g = ROUTE_TILES
    assert nt % g == 0
    table = pl.BlockSpec((g, N_EXPERTS, LANES), lambda ph, i: (i * ph, 0, 0))
    table_shape = jax.ShapeDtypeStruct((nt, N_EXPERTS, LANES), I32)
    return pl.pallas_call(
        _route_kernel,
        grid=(2, nt // g),
        in_specs=[pl.BlockSpec((8, g * t), lambda ph, i: (0, i)),
                  pl.BlockSpec((t, t), lambda ph, i: (0, 0)),
                  pl.BlockSpec((N_EXPERTS, N_EXPERTS), lambda ph, i: (0, 0))],
        out_specs=[pl.BlockSpec((8, g * t), lambda ph, i: (0, i * ph)), table, table, table,
                   pl.BlockSpec((N_EXPERTS, LANES), lambda ph, i: (0, 0))],
        out_shape=[jax.ShapeDtypeStruct((8, n), I32), table_shape, table_shape, table_shape,
                   jax.ShapeDtypeStruct((N_EXPERTS, LANES), F32)],
        scratch_shapes=[pltpu.VMEM((N_EXPERTS, LANES), F32), pltpu.VMEM((N_EXPERTS, LANES), F32)],
        compiler_params=_cparams("arbitrary", "arbitrary"),
    )(top_e, upper, ltri)


def _for_each_run_chunk(tile, run8_ref, loff_ref, goff_ref, move):
    def pieces(rows, lo, go, sizes, queue):
        for size in sizes:
            hit = (rows & size) != 0

            @pl.when(hit)
            def _(lo=lo, go=go, size=size):
                move(pl.multiple_of(lo, SUBLANES), pl.multiple_of(go, SUBLANES), size, queue)

            lo = lo + jnp.where(hit, size, 0)
            go = go + jnp.where(hit, size, 0)
        return lo, go

    def per_expert(e, queue):
        idx = tile * N_EXPERTS + e
        rows = run8_ref[idx]
        lo, go = pieces(rows, loff_ref[idx], goff_ref[idx], CHUNK_SIZES[RARE_SIZES:], queue)

        @pl.when(rows >= CHUNK_SIZES[RARE_SIZES - 1])
        def _():
            pieces(rows, lo, go, CHUNK_SIZES[:RARE_SIZES], queue)

        return rows

    def per_pair(j, total):
        return total + per_expert(2 * j, 0) + per_expert(2 * j + 1, 1)

    return lax.fori_loop(0, N_EXPERTS // 2, per_pair, jnp.int32(0))


def _wait_rows(total, copy_of_rows):
    size = 1 << (COMPACT_ROWS.bit_length() - 1)
    while size >= SUBLANES:
        @pl.when((total & size) != 0)
        def _(size=size):
            copy_of_rows(size).wait()

        size //= 2


def _make_dispatch_kernel(n_tiles, n_blocks):
    t = MOVE_TILE
    cb = COMPACT_ROWS

    def kern(run8_ref, loff_ref, goff_ref, seg_ref, start_ref, nv_ref, h_ref, lpos_ref, xs_ref,
             buf, zeros_ref, moved_ref, sem, zsem):
        step = pl.program_id(0)
        slot = lax.rem(step, 2)

        def wait_tile(s):
            _wait_rows(moved_ref[s], lambda rows: pltpu.make_async_copy(
                buf.at[s, pl.ds(0, rows)], xs_ref.at[pl.ds(0, rows)], sem.at[s]))

        def for_each_zero_chunk(action):
            def per_expert(e, carry):
                seg = seg_ref[e]
                padlen = (-seg) & (MOE_BM - 1)
                row = start_ref[e] + seg
                for size in CHUNK_SIZES[1:]:
                    hit = (padlen & size) != 0

                    @pl.when(hit)
                    def _(row=row, size=size):
                        dst = xs_ref.at[pl.ds(pl.multiple_of(row, SUBLANES), size)]
                        action(pltpu.make_async_copy(zeros_ref.at[pl.ds(0, size)], dst, zsem))

                    row = row + jnp.where(hit, size, 0)
                return carry

            lax.fori_loop(0, N_EXPERTS, per_expert, 0)

            def per_tail_block(b, carry):
                action(pltpu.make_async_copy(zeros_ref, xs_ref.at[pl.ds(b * MOE_BM, MOE_BM)], zsem))
                return carry

            lax.fori_loop(nv_ref[0], n_blocks, per_tail_block, 0)

        @pl.when(step == 0)
        def _():
            zeros_ref[...] = jnp.zeros_like(zeros_ref)
            for_each_zero_chunk(lambda cp: cp.start())
            for_each_zero_chunk(lambda cp: cp.wait())

        @pl.when(step >= 2)
        def _():
            wait_tile(slot)

        lpos = lpos_ref[...]
        rows = lax.broadcasted_iota(I32, (cb, t), 0)
        onehot = jnp.where(rows == lpos[0:1, :], 1.0, 0.0)
        for k in range(1, TOP_K):
            onehot = onehot + jnp.where(rows == lpos[k:k + 1, :], 1.0, 0.0)
        buf[slot] = _dot(onehot.astype(BF16), h_ref[...])

        def move(lo, go, size, queue):
            pltpu.make_async_copy(buf.at[slot, pl.ds(lo, size)], xs_ref.at[pl.ds(go, size)],
                                  sem.at[slot]).start(priority=queue)

        moved_ref[slot] = _for_each_run_chunk(step, run8_ref, loff_ref, goff_ref, move)

        @pl.when(step == n_tiles - 1)
        def _():
            wait_tile(slot)
            if n_tiles > 1:
                wait_tile(1 - slot)
    return kern


def _dispatch(tables, seg, starts, n_valid, h2, lpos, n_blocks):
    n, d = h2.shape
    t = MOVE_TILE
    return pl.pallas_call(
        _make_dispatch_kernel(n // t, n_blocks),
        grid_spec=pltpu.PrefetchScalarGridSpec(
            num_scalar_prefetch=6,
            grid=(n // t,),
            in_specs=[pl.BlockSpec((t, d), lambda i, *_: (i, 0)),
                      pl.BlockSpec((8, t), lambda i, *_: (0, i))],
            out_specs=pl.BlockSpec(memory_space=pl.ANY),
            scratch_shapes=[pltpu.VMEM((2, COMPACT_ROWS, d), F32), pltpu.VMEM((MOE_BM, d), F32),
                            pltpu.SMEM((2,), I32), pltpu.SemaphoreType.DMA((2,)), pltpu.SemaphoreType.DMA(())],
        ),
        out_shape=jax.ShapeDtypeStruct((n_blocks * MOE_BM, d), F32),
        compiler_params=_cparams("arbitrary"),
    )(*tables, seg, starts, n_valid, h2, lpos)


def _make_expert_kernel(layer):
    def kern(be_ref, bs_ref, nv_ref, ord_ref, nxt_ref, x_ref, wgu_hbm, bgu_ref, wdn_hbm, bdn_ref, y_ref,
             wgu_f, wdn_f, wgu_s, wdn_s, sem):
        i = pl.program_id(0)
        e = be_ref[i]
        prev = be_ref[jnp.maximum(i - 1, 0)]
        new_expert = jnp.logical_or(i == 0, e != prev)
        slot = lax.rem(ord_ref[i], 2)

        def weight_copies(expert, s):
            return (pltpu.make_async_copy(wgu_hbm.at[layer, expert], wgu_f.at[s], sem.at[0, s]),
                    pltpu.make_async_copy(wdn_hbm.at[layer, expert], wdn_f.at[s], sem.at[1, s]))

        @pl.when(i == 0)
        def _():
            for cp in weight_copies(e, slot):
                cp.start()

        @pl.when(new_expert)
        def _():
            for cp in weight_copies(e, slot):
                cp.wait()
            nxt = nxt_ref[i]

            @pl.when(nxt >= 0)
            def _():
                for cp in weight_copies(nxt, 1 - slot):
                    cp.start(priority=1)

            wgu_s[...] = wgu_f[slot].astype(BF16)
            wdn_s[...] = wdn_f[slot].astype(BF16)

        @pl.when(i < nv_ref[0])
        def _():
            x = x_ref[...].astype(BF16)
            gu = _dot(x, wgu_s[...]) + bgu_ref[0, 0]
            gate = jnp.minimum(gu[:, :D_FF], SWIGLU_LIMIT)
            up = jnp.clip(gu[:, D_FF:], -SWIGLU_LIMIT, SWIGLU_LIMIT)
            hdn = (up + 1.0) * (gate * jax.nn.sigmoid(SWIGLU_ALPHA * gate))
            y_ref[...] = _dot(hdn.astype(BF16), wdn_s[...]) + bdn_ref[0, 0]
    return kern


def _experts(blk_e, blk_src, n_valid, blk_ord, blk_next, xs, w_gu, b_gu, w_dn, b_dn, layer, n_blocks):
    d = xs.shape[1]
    depth = w_gu.shape[0]

    def bias(width):
        return pl.BlockSpec((1, 1, 1, width), lambda i, be, *_: (layer, be[i], 0, 0))

    rows = pl.BlockSpec((MOE_BM, d), lambda i, be, bs, *_: (bs[i], 0))
    return pl.pallas_call(
        _make_expert_kernel(layer),
        grid_spec=pltpu.PrefetchScalarGridSpec(
            num_scalar_prefetch=5,
            grid=(n_blocks,),
            in_specs=[rows,
                      pl.BlockSpec(memory_space=pl.ANY), bias(2 * D_FF),
                      pl.BlockSpec(memory_space=pl.ANY), bias(d)],
            out_specs=rows,
            scratch_shapes=[pltpu.VMEM((2, d, 2 * D_FF), F32), pltpu.VMEM((2, D_FF, d), F32),
                            pltpu.VMEM((d, 2 * D_FF), BF16), pltpu.VMEM((D_FF, d), BF16),
                            pltpu.SemaphoreType.DMA((2, 2))],
        ),
        out_shape=jax.ShapeDtypeStruct(xs.shape, xs.dtype),
        input_output_aliases={5: 0},
        compiler_params=_cparams("arbitrary"),
    )(blk_e, blk_src, n_valid, blk_ord, blk_next, xs, w_gu, b_gu.reshape(depth, N_EXPERTS, 1, -1), w_dn,
      b_dn.reshape(depth, N_EXPERTS, 1, -1))


def _make_combine_kernel(n_tiles, ctx_tiles, final):
    t = MOVE_TILE
    cb = COMPACT_ROWS

    def kern(run8_ref, loff_ref, goff_ref, ys_ref, x1_ref, lpt_ref, gt_ref, g2_ref, fg_ref, *rest):
        o_refs = rest[:2] if final else rest[:1]
        ybuf, moved_ref, sem = rest[len(o_refs):]
        step = pl.program_id(0)
        slot = lax.rem(step, 2)

        def fetch(tile, s):
            def move(lo, go, size, queue):
                pltpu.make_async_copy(ys_ref.at[pl.ds(go, size)], ybuf.at[s, pl.ds(lo, size)],
                                      sem.at[s]).start(priority=queue)

            moved_ref[s] = _for_each_run_chunk(tile, run8_ref, loff_ref, goff_ref, move)

        @pl.when(step == 0)
        def _():
            ybuf[...] = jnp.zeros_like(ybuf)
            fetch(0, 0)

        @pl.when(step + 1 < n_tiles)
        def _():
            fetch(step + 1, 1 - slot)

        _wait_rows(moved_ref[slot], lambda rows: pltpu.make_async_copy(
            ys_ref.at[pl.ds(0, rows)], ybuf.at[slot, pl.ds(0, rows)], sem.at[slot]))

        lpos = lpt_ref[...]
        gates = gt_ref[...]
        rows = lax.broadcasted_iota(I32, (cb, t), 0)
        placed = jnp.where(rows == lpos[0:1, :], gates[0:1, :], 0.0)
        for k in range(1, TOP_K):
            placed = placed + jnp.where(rows == lpos[k:k + 1, :], gates[k:k + 1, :], 0.0)
        row_gate = jnp.sum(placed, axis=1, keepdims=True)
        yb = (ybuf[slot] * row_gate).astype(BF16)
        lpt = lpos.astype(F32).T.astype(I32)
        cols = lax.broadcasted_iota(I32, (t, cb), 1)
        unsort = jnp.where(cols == lpt[:, 0:1], 1.0, 0.0)
        for k in range(1, TOP_K):
            unsort = unsort + jnp.where(cols == lpt[:, k:k + 1], 1.0, 0.0)
        y = _dot(unsort.astype(BF16), yb)
        x2 = x1_ref[...] + g2_ref[0] * y
        if not final:
            o_refs[0][...] = x2
        else:
            x2 = x2 * lax.rsqrt(jnp.mean(x2 * x2, axis=-1, keepdims=True) + EPS) * fg_ref[...]

            @pl.when(step < ctx_tiles)
            def _():
                o_refs[0][...] = x2

            @pl.when(step >= ctx_tiles)
            def _():
                o_refs[1][...] = x2
    return kern


def _combine(tables, ys, x1, lpos_t, gates_t, mod, final_g, final, n_ctx, dec_seq):
    n, d = x1.shape
    t = MOVE_TILE
    ctx_tiles = n_ctx // t
    grp = functools.partial(_group_of_tile, tile=t, n_ctx=n_ctx, dec_seq=dec_seq)
    if final:
        out_specs = [pl.BlockSpec((t, d), lambda i, *_: (jnp.minimum(i, ctx_tiles - 1), 0)),
                     pl.BlockSpec((t, d), lambda i, *_: (jnp.maximum(i - ctx_tiles, 0), 0))]
        out_shape = [jax.ShapeDtypeStruct((n_ctx, d), F32), jax.ShapeDtypeStruct((n - n_ctx, d), F32)]
    else:
        out_specs = pl.BlockSpec((t, d), lambda i, *_: (i, 0))
        out_shape = jax.ShapeDtypeStruct((n, d), F32)
    return pl.pallas_call(
        _make_combine_kernel(n // t, ctx_tiles, final),
        grid_spec=pltpu.PrefetchScalarGridSpec(
            num_scalar_prefetch=3,
            grid=(n // t,),
            in_specs=[pl.BlockSpec(memory_space=pl.ANY),
                      pl.BlockSpec((t, d), lambda i, *_: (i, 0)),
                      pl.BlockSpec((8, t), lambda i, *_: (0, i)),
                      pl.BlockSpec((8, t), lambda i, *_: (0, i)),
                      pl.BlockSpec((1, 1, d), lambda i, *_: (grp(i) * 6 + MOD_G2, 0, 0)),
                      pl.BlockSpec((1, d), lambda i, *_: (0, 0))],
            out_specs=out_specs,
            scratch_shapes=[pltpu.VMEM((2, COMPACT_ROWS, d), F32), pltpu.SMEM((2,), I32),
                            pltpu.SemaphoreType.DMA((2,))],
        ),
        out_shape=out_shape,
        compiler_params=_cparams("arbitrary"),
    )(*tables, ys, x1, lpos_t, gates_t, mod, final_g)


def _blockdiag_pairs(s):
    z = jnp.zeros_like(s[..., 0, :, :])
    def pair(a, b):
        return jnp.concatenate([jnp.concatenate([a, z], axis=-1), jnp.concatenate([z, b], axis=-1)], axis=-2)
    return jnp.stack([pair(s[..., 0, :, :], s[..., 1, :, :]), pair(s[..., 2, :, :], s[..., 3, :, :])], axis=-3)


def _diag_blocks(st):
    h = HEAD_DIM
    blocks = [st[:, :, c, j * h:(j + 1) * h, j * h:(j + 1) * h] for c in range(2) for j in range(2)]
    return jnp.stack(blocks, axis=2)


def kernel(x_prompt, x_sample, cache_k, cache_v, state_ret, c, c_ctx, w_mod, b_mod, norm1_g, norm2_g, w_in,
           q_norm_g, k_norm_g, conv_w, conv_b, conv_ln_g, conv_ln_b, ret_decay_logit, ret_gn_g, w_out,
           router_w, router_b, moe_w_gu, moe_b_gu, moe_w_dn, moe_b_dn, final_g):
    batch, seq, d = x_prompt.shape
    dec_batch, dec_seq, _ = x_sample.shape
    depth = w_mod.shape[0]
    past = cache_k.shape[2]
    n_ctx = batch * seq
    n_lat = dec_batch * dec_seq
    n = n_ctx + n_lat
    assert d == D_MODEL and dec_batch == 2
    assert n_ctx % dec_seq == 0 and dec_seq % TOK_TILE == 0 and dec_seq % ATTN_QB == 0
    assert n % (MOVE_TILE * ROUTE_TILES) == 0

    x_parts = (x_prompt.reshape(n_ctx, d), x_sample.reshape(n_lat, d))
    mods = _modulation(jnp.concatenate([c_ctx[None, :], c], axis=0), w_mod, b_mod)
    rope = _rope_tables(dec_seq)

    head_of_col = np.arange(QK_W) // HEAD_DIM
    hsum_np = (head_of_col[:, None] == np.arange(LANES)[None, :]).astype(np.float32)
    hsum = jnp.asarray(hsum_np, BF16)
    hbc = jnp.asarray(hsum_np.T, BF16)
    lane_head = np.arange(LANES) // HEAD_DIM
    gmat = jnp.asarray((lane_head[:, None] == lane_head[None, :]).astype(np.float32) / HEAD_DIM, BF16)
    tt = np.arange(MOVE_TILE)
    upper = jnp.asarray((tt[:, None] < tt[None, :]).astype(np.float32), BF16)
    ee = jnp.arange(N_EXPERTS)
    ltri = jnp.asarray((np.arange(N_EXPERTS)[None, :] < np.arange(N_EXPERTS)[:, None]).astype(np.float32))

    n_tiles = n // MOVE_TILE
    max_rows = n * TOP_K + n_tiles * N_EXPERTS * (SUBLANES - 1) + N_EXPERTS * (MOE_BM - 1)
    n_blocks = -(-max_rows // MOE_BM)
    cache_k2 = cache_k.reshape(dec_batch, depth, past, KV_W)
    cache_v2 = cache_v.reshape(dec_batch, depth, past, KV_W)

    ks_out, vs_out, ss_out = [], [], []
    for l in range(depth):
        mod = mods[l]
        gqk = jnp.concatenate([jnp.tile(q_norm_g[l], N_Q_HEADS), jnp.tile(k_norm_g[l], N_KV_HEADS)])[None, :]
        p, kn, vv = _inproj(x_parts, mod, norm1_g[l][None, :], w_in[l].astype(BF16), gqk, hsum, hbc, rope,
                            n_ctx, dec_seq)
        ks_out.append(kn.reshape(batch, seq, N_KV_HEADS, HEAD_DIM))
        vs_out.append(vv.reshape(batch, seq, N_KV_HEADS, HEAD_DIM))

        attn = (_attention(p, batch, seq, 0, None, l),
                _attention(p, dec_batch, dec_seq, n_ctx, (cache_k2, cache_v2), l))

        cw, cb = conv_w[l], conv_b[l][None, :]
        clg, clb = conv_ln_g[l][None, :], conv_ln_b[l][None, :]
        conv = (_conv(p, batch, seq, 0, cw, cb, clg, clb),
                _conv(p, dec_batch, dec_seq, n_ctx, cw, cb, clg, clb))

        log_g = jax.nn.log_sigmoid(ret_decay_logit[l].astype(F32))
        gn = ret_gn_g[l][None, :]
        ret_ctx, st_ctx = _retention(p, batch, seq, 0, log_g, gn, gmat, None)
        ret_lat, _ = _retention(p, dec_batch, dec_seq, n_ctx, log_g, gn, gmat,
                                _blockdiag_pairs(state_ret[:, l].astype(F32)))
        ret = (ret_ctx, ret_lat)
        ss_out.append(_diag_blocks(st_ctx))

        x1, h2, top_e, top_g = _outproj(attn, conv, ret, x_parts, mod, norm2_g[l][None, :], w_out[l].astype(BF16),
                                        router_w[l].T, router_b[l][:, None], n_ctx, dec_seq)
        lpos, run8, loff, goff, seg = _route(top_e, upper, ltri)
        tables = [tb[:, :, 0].reshape(-1) for tb in (run8, loff, goff)]
        seg = seg[:, 0].astype(I32)
        padded = (seg + MOE_BM - 1) // MOE_BM * MOE_BM
        pad_end = jnp.cumsum(padded)
        n_valid = (pad_end[-1] // MOE_BM).reshape(1)
        blk_src = jnp.minimum(jnp.arange(n_blocks, dtype=I32), n_valid - 1)
        blk_e = jnp.minimum(jnp.sum((pad_end[None, :] <= (blk_src * MOE_BM)[:, None]).astype(I32), axis=1),
                            N_EXPERTS - 1)
        owns = padded > 0
        ord_e = jnp.cumsum(owns.astype(I32)) - 1
        next_e = jnp.sum(jnp.where(owns[None, :] & (ord_e[None, :] == ord_e[:, None] + 1),
                                   ee[None, :] + 1, 0), axis=1).astype(I32) - 1

        xs = _dispatch(tables, seg, pad_end - padded, n_valid, h2, lpos, n_blocks)
        of_blk = blk_e[:, None] == ee[None, :]
        per_blk = lambda table: jnp.sum(jnp.where(of_blk, table[None, :], 0), axis=1)
        ys = _experts(blk_e, blk_src, n_valid, per_blk(ord_e), per_blk(next_e), xs,
                      moe_w_gu, moe_b_gu, moe_w_dn, moe_b_dn, l, n_blocks)
        out = _combine(tables, ys, x1, lpos, top_g, mod, final_g[None, :], l == depth - 1, n_ctx, dec_seq)
        x_parts = (out,)

    y_prompt = out[0].reshape(batch, seq, d)
    y_sample = out[1].reshape(dec_batch, dec_seq, d)
    return (y_prompt, y_sample, jnp.stack(ks_out, axis=1), jnp.stack(vs_out, axis=1),
            jnp.stack(ss_out, axis=1))
```

```python
import functools

import numpy as np
import jax
import jax.numpy as jnp
from jax import lax
from jax.experimental import pallas as pl
from jax.experimental.pallas import tpu as pltpu

F32 = jnp.float32
BF16 = jnp.bfloat16
I32 = jnp.int32

D_MODEL = 1024
GRID_W = 64
HEAD_DIM = 64
N_Q_HEADS = 8
N_KV_HEADS = 2
ATTN_W = N_Q_HEADS * HEAD_DIM
KV_W = N_KV_HEADS * HEAD_DIM
QK_W = ATTN_W + KV_W
CONV_CH = 256
CONV_K = 31
CONV_PAD = CONV_K // 2
CONV_HALO = 16
N_RET_HEADS = 4
RET_W = 256
OFF_V = QK_W
OFF_CONV = OFF_V + KV_W
OFF_RET = OFF_CONV + 2 * CONV_CH
IN_COLS = OFF_RET + 4 * RET_W
ROPE_HALF = HEAD_DIM // 2
ROPE_THETA = 10000.0
N_EXPERTS = 32
TOP_K = 4
D_FF = D_MODEL
SWIGLU_LIMIT = 7.0
SWIGLU_ALPHA = 1.702
EPS = 1e-6
LN_EPS = 1e-5

LANES = 128
TOK_TILE = 1024
ATTN_QB = 512
MIXER_ROWS = 2048
RET_QB = 256
CONV_ROWS = 64
MOE_BM = 256
X_RING = 3
MOVE_TILE = 256
ROUTE_TILES = 4
SUBLANES = 8
COMPACT_ROWS = 1280
assert COMPACT_ROWS >= MOVE_TILE * TOP_K + N_EXPERTS * (SUBLANES - 1) and COMPACT_ROWS % MOE_BM == 0
CHUNK_SIZES = tuple(MOVE_TILE >> s for s in range(6))
RARE_SIZES = 3
VMEM_LIMIT = 56 * 1024 * 1024

MOD_SH1, MOD_SC1, MOD_G1, MOD_SH2, MOD_SC2, MOD_G2 = range(6)


def _cparams(*sem):
    return pltpu.CompilerParams(dimension_semantics=sem, vmem_limit_bytes=VMEM_LIMIT)


def _dot(a, b, **kw):
    return jnp.dot(a, b, preferred_element_type=F32, **kw)


def _dot_nt(a, b, **kw):
    return lax.dot_general(a, b, (((1,), (1,)), ((), ())), preferred_element_type=F32, **kw)


def _dot_tn(a, b, **kw):
    return lax.dot_general(a, b, (((0,), (0,)), ((), ())), preferred_element_type=F32, **kw)


def _split_bf16(x):
    hi = x.astype(BF16)
    lo = (x - hi.astype(F32)).astype(BF16)
    return hi, lo


MOD_TN = 1536


def _mod_kernel(ct_ref, w_ref, b_ref, o_ref):
    s = ct_ref[...]
    s = s * jax.nn.sigmoid(s)
    w = w_ref[0]
    rows = [jnp.sum(w * s[:, r:r + 1], axis=0, keepdims=True) for r in range(3)]
    rows.append(jnp.zeros((5, w.shape[1]), F32))
    o_ref[0] = jnp.concatenate(rows, axis=0) + b_ref[0]


def _modulation(cvec3, w_mod, b_mod):
    depth, d, cols = w_mod.shape
    ct = jnp.zeros((d, 8), F32).at[:, :3].set(cvec3.T)
    out = pl.pallas_call(
        _mod_kernel,
        grid=(depth, cols // MOD_TN),
        in_specs=[
            pl.BlockSpec((d, 8), lambda l, j: (0, 0)),
            pl.BlockSpec((1, d, MOD_TN), lambda l, j: (l, 0, j)),
            pl.BlockSpec((1, 1, MOD_TN), lambda l, j: (l, 0, j)),
        ],
        out_specs=pl.BlockSpec((1, 8, MOD_TN), lambda l, j: (l, 0, j)),
        out_shape=jax.ShapeDtypeStruct((depth, 8, cols), F32),
        compiler_params=_cparams("arbitrary", "arbitrary"),
    )(ct, w_mod, b_mod.reshape(depth, 1, cols))
    return out[:, :3].reshape(depth, 3 * 6, 1, d)


def _token_specs(parts, tile, n_ctx):
    d = parts[0].shape[1]
    if len(parts) == 1:
        return [pl.BlockSpec((tile, d), lambda i, *_: (i, 0))]
    ctx_tiles = n_ctx // tile
    return [pl.BlockSpec((tile, d), lambda i, *_: (jnp.minimum(i, ctx_tiles - 1), 0)),
            pl.BlockSpec((tile, d), lambda i, *_: (jnp.maximum(i - ctx_tiles, 0), 0))]


def _token_tile(refs, ctx_tiles):
    if len(refs) == 1:
        return refs[0][...]
    return jnp.where(pl.program_id(0) < ctx_tiles, refs[0][...], refs[1][...])


def _inproj_kernel(n_src, ctx_tiles, *refs):
    x_refs = refs[:n_src]
    (sh_ref, sc_ref, g_ref, w_ref, gqk_ref, hsum_ref, hbc_ref, cos_ref, sa_ref, sb_ref,
     p_ref, k_ref, v_ref) = refs[n_src:]
    x = _token_tile(x_refs, ctx_tiles)
    inv = lax.rsqrt(jnp.mean(x * x, axis=-1, keepdims=True) + EPS)
    h = (x * inv * g_ref[...]) * (1.0 + sc_ref[0]) + sh_ref[0]
    acc = _dot(h.astype(BF16), w_ref[...])
    qk = acc[:, :QK_W]
    ss = _dot((qk * qk).astype(BF16), hsum_ref[...])
    r = lax.rsqrt(ss * (1.0 / HEAD_DIM) + EPS)
    r_hi, r_lo = _split_bf16(r)
    rb = _dot(r_hi, hbc_ref[...]) + _dot(r_lo, hbc_ref[...])
    qkn = qk * rb * gqk_ref[...]
    cos = cos_ref[...]
    sa = sa_ref[...]
    sb = sb_ref[...]
    for j in range(QK_W // LANES):
        c = qkn[:, j * LANES:(j + 1) * LANES]
        up = pltpu.roll(c, LANES - ROPE_HALF // 2, 1)
        dn = pltpu.roll(c, ROPE_HALF // 2, 1)
        p_ref[:, j * LANES:(j + 1) * LANES] = (c * cos + up * sa + dn * sb).astype(BF16)
    p_ref[:, QK_W:] = acc[:, QK_W:].astype(BF16)

    @pl.when(pl.program_id(0) < ctx_tiles)
    def _():
        k_ref[...] = qkn[:, ATTN_W:QK_W]
        v_ref[...] = acc[:, OFF_V:OFF_CONV]


def _rope_tables(dec_seq):
    f32 = np.float32
    rows = dec_seq // GRID_W
    row = np.repeat(np.arange(rows), GRID_W).astype(f32)
    col = np.tile(np.arange(GRID_W), rows).astype(f32)
    inv = (f32(1.0) / (f32(ROPE_THETA) ** (np.arange(0, ROPE_HALF, 2).astype(f32) / f32(ROPE_HALF)))).astype(f32)
    ar = row[:, None] * inv[None, :]
    ac = col[:, None] * inv[None, :]
    cos = np.concatenate([np.cos(ar), np.cos(ar), np.cos(ac), np.cos(ac)], axis=-1)
    sin = np.concatenate([np.sin(ar), np.sin(ar), np.sin(ac), np.sin(ac)], axis=-1)
    first = (np.arange(HEAD_DIM) % ROPE_HALF) < ROPE_HALF // 2
    sa = np.where(first[None, :], -sin, 0.0)
    sb = np.where(first[None, :], 0.0, sin)
    def table(t, ident):
        t = np.concatenate([np.full((TOK_TILE, HEAD_DIM), ident, f32), t.astype(f32)], axis=0)
        return jnp.asarray(np.tile(t, (1, LANES // HEAD_DIM)))
    return table(cos, 1.0), table(sa, 0.0), table(sb, 0.0)


def _group_of_tile(i, tile, n_ctx, dec_seq):
    tok = i * tile
    return jnp.where(tok < n_ctx, 0, 1 + (tok - n_ctx) // dec_seq)


def _inproj(x_parts, mod, norm_g, w_in_bf, gqk, hsum, hbc, rope, n_ctx, dec_seq):
    n = sum(part.shape[0] for part in x_parts)
    d = x_parts[0].shape[1]
    t = TOK_TILE
    grp = functools.partial(_group_of_tile, tile=t, n_ctx=n_ctx, dec_seq=dec_seq)

    def mod_spec(which):
        return pl.BlockSpec((1, 1, d), lambda i: (grp(i) * 6 + which, 0, 0))

    def rope_idx(i):
        tok = i * t
        return (jnp.where(tok < n_ctx, 0, 1 + ((tok - n_ctx) % dec_seq) // t), 0)

    rope_spec = pl.BlockSpec((t, LANES), rope_idx)
    const = lambda shape: pl.BlockSpec(shape, lambda i: (0,) * len(shape))
    return pl.pallas_call(
        functools.partial(_inproj_kernel, len(x_parts), n_ctx // t),
        grid=(n // t,),
        in_specs=_token_specs(x_parts, t, n_ctx) + [
            mod_spec(MOD_SH1), mod_spec(MOD_SC1),
            const((1, d)),
            const((d, IN_COLS)),
            const((1, QK_W)), const((QK_W, LANES)), const((LANES, QK_W)),
            rope_spec, rope_spec, rope_spec,
        ],
        out_specs=[
            pl.BlockSpec((t, IN_COLS), lambda i: (i, 0)),
            pl.BlockSpec((t, KV_W), lambda i: (jnp.minimum(i, n_ctx // t - 1), 0)),
            pl.BlockSpec((t, KV_W), lambda i: (jnp.minimum(i, n_ctx // t - 1), 0)),
        ],
        out_shape=[
            jax.ShapeDtypeStruct((n, IN_COLS), BF16),
            jax.ShapeDtypeStruct((n_ctx, KV_W), F32),
            jax.ShapeDtypeStruct((n_ctx, KV_W), F32),
        ],
        compiler_params=_cparams("arbitrary"),
    )(*x_parts, mod, mod, norm_g, w_in_bf, gqk, hsum, hbc, *rope)


def _head_halves(x, hkv, low):
    r = pltpu.roll(x, HEAD_DIM, 1)
    rep = jnp.where(low, x, r) if hkv == 0 else jnp.where(low, r, x)
    return jnp.where(low, rep, 0.0).astype(BF16), jnp.where(low, 0.0, rep).astype(BF16)


def _make_attn_kernel(has_ctx, group, seq_len):
    def kern(*refs):
        if has_ctx:
            q_ref, k_ref, v_ref, ck_ref, cv_ref, o_ref = refs
        else:
            q_ref, k_ref, v_ref, o_ref = refs
        low = lax.broadcasted_iota(I32, (1, LANES), 1) < HEAD_DIM
        scale = HEAD_DIM ** -0.5
        for g in range(group):
            qrows = slice(g * seq_len, (g + 1) * seq_len) if group > 1 else slice(None)
            k = k_ref[qrows, :].astype(F32)
            v = v_ref[qrows, :].astype(F32)
            if has_ctx:
                ck = ck_ref[0, 0]
                cv = cv_ref[0, 0]
            for hkv in range(N_KV_HEADS):
                kh = _head_halves(k, hkv, low)
                vh = _head_halves(v, hkv, low)
                if has_ctx:
                    ckh = _head_halves(ck, hkv, low)
                    cvh = _head_halves(cv, hkv, low)
                for c in range(2):
                    col = hkv * 2 * LANES + c * LANES
                    qc = q_ref[qrows, col:col + LANES] * scale
                    o_c = None
                    for j in range(2):
                        s = _dot_nt(qc, kh[j])
                        m = jnp.max(s, axis=-1, keepdims=True)
                        if has_ctx:
                            s2 = _dot_nt(qc, ckh[j])
                            m = jnp.maximum(m, jnp.max(s2, axis=-1, keepdims=True))
                        p = jnp.exp(s - m)
                        l = jnp.sum(p, axis=-1, keepdims=True)
                        o = _dot(p.astype(BF16), vh[j])
                        if has_ctx:
                            p2 = jnp.exp(s2 - m)
                            l = l + jnp.sum(p2, axis=-1, keepdims=True)
                            o = o + _dot(p2.astype(BF16), cvh[j])
                        o = o / l
                        o_c = o if o_c is None else o_c + o
                    o_ref[qrows, col:col + LANES] = o_c.astype(BF16)
    return kern


def _attention(p, n_seq, seq_len, row0, ctx_kv, layer):
    has_ctx = ctx_kv is not None
    qb = min(seq_len, ATTN_QB)
    nq = seq_len // qb
    group = _seq_group(n_seq, seq_len, row0) if (nq == 1 and not has_ctx) else 1
    rows = group * seq_len
    qrows = group * qb
    qrow0 = row0 // qrows
    srow0 = row0 // rows
    in_specs = [
        pl.BlockSpec((qrows, ATTN_W), lambda b, i: (qrow0 + b * nq + i, 0)),
        pl.BlockSpec((rows, KV_W), lambda b, i: (srow0 + b, ATTN_W // KV_W)),
        pl.BlockSpec((rows, KV_W), lambda b, i: (srow0 + b, OFF_V // KV_W)),
    ]
    args = [p, p, p]
    if has_ctx:
        ck, cv = ctx_kv
        past = ck.shape[2]
        spec = pl.BlockSpec((1, 1, past, KV_W), lambda b, i: (b, layer, 0, 0))
        in_specs += [spec, spec]
        args += [ck, cv]
    return pl.pallas_call(
        _make_attn_kernel(has_ctx, group, seq_len),
        grid=(n_seq // group, nq),
        in_specs=in_specs,
        out_specs=pl.BlockSpec((qrows, ATTN_W), lambda b, i: (b * nq + i, 0)),
        out_shape=jax.ShapeDtypeStruct((n_seq * seq_len, ATTN_W), BF16),
        compiler_params=_cparams("arbitrary", "arbitrary"),
    )(*args)


def _make_conv_kernel(seq_len, group):
    shifted_rows = seq_len + 2 * CONV_HALO - SUBLANES

    def kern(a_ref, g_ref, w_ref, b_ref, lng_ref, lnb_ref, o_ref, zp_ref, zs_ref):
        zero = jnp.zeros((CONV_HALO, CONV_CH), F32)
        w = w_ref[...]
        bias = b_ref[...]
        for q in range(group):
            r0 = q * seq_len
            zp_ref[q, 0:CONV_HALO, :] = zero
            zp_ref[q, CONV_HALO + seq_len:2 * CONV_HALO + seq_len, :] = zero
            zp_ref[q, CONV_HALO:CONV_HALO + seq_len, :] = (
                a_ref[r0:r0 + seq_len, :].astype(F32) * jax.nn.sigmoid(g_ref[r0:r0 + seq_len, :].astype(F32)))
            for s in range(1, SUBLANES):
                zs_ref[q, s] = zp_ref[q, s:s + shifted_rows, :]
            for c in range(seq_len // CONV_ROWS):
                base = c * CONV_ROWS + CONV_HALO - CONV_PAD
                acc = jnp.zeros((CONV_ROWS, CONV_CH), F32) + bias
                for j in range(CONV_K):
                    shift = (base + j) % SUBLANES
                    row = base + j - shift
                    if shift == 0:
                        tap = zp_ref[q, row:row + CONV_ROWS, :]
                    else:
                        tap = zs_ref[q, shift, row:row + CONV_ROWS, :]
                    acc = acc + tap * w[j:j + 1, :]
                mu = jnp.mean(acc, axis=-1, keepdims=True)
                dlt = acc - mu
                var = jnp.mean(dlt * dlt, axis=-1, keepdims=True)
                y = dlt * lax.rsqrt(var + LN_EPS) * lng_ref[...] + lnb_ref[...]
                o_ref[r0 + c * CONV_ROWS:r0 + (c + 1) * CONV_ROWS, :] = (y * jax.nn.sigmoid(y)).astype(BF16)
    return kern


def _seq_group(n_seq, seq_len, row0):
    group = max(1, MIXER_ROWS // seq_len)
    if n_seq % group or row0 % (group * seq_len):
        group = 1
    return group


def _conv(p, n_seq, seq_len, row0, w, b, lng, lnb):
    group = _seq_group(n_seq, seq_len, row0)
    rows = group * seq_len
    srow0 = row0 // rows
    const = lambda shape: pl.BlockSpec(shape, lambda s: (0,) * len(shape))
    return pl.pallas_call(
        _make_conv_kernel(seq_len, group),
        grid=(n_seq // group,),
        in_specs=[
            pl.BlockSpec((rows, CONV_CH), lambda s: (srow0 + s, OFF_CONV // CONV_CH)),
            pl.BlockSpec((rows, CONV_CH), lambda s: (srow0 + s, OFF_CONV // CONV_CH + 1)),
            const((CONV_K, CONV_CH)), const((1, CONV_CH)), const((1, CONV_CH)), const((1, CONV_CH)),
        ],
        out_specs=pl.BlockSpec((rows, CONV_CH), lambda s: (s, 0)),
        out_shape=jax.ShapeDtypeStruct((n_seq * seq_len, CONV_CH), BF16),
        scratch_shapes=[pltpu.VMEM((group, seq_len + 2 * CONV_HALO, CONV_CH), F32),
                        pltpu.VMEM((group, SUBLANES, seq_len + 2 * CONV_HALO - SUBLANES, CONV_CH), F32)],
        compiler_params=_cparams("arbitrary"),
    )(p, p, w, b, lng, lnb)


def _make_ret_kernel(seq_len, has_init, group):
    qb = min(seq_len, RET_QB)
    nq = seq_len // qb
    scale = HEAD_DIM ** -0.5

    def kern(*refs):
        if has_init:
            lg_ref, q_ref, k_ref, v_ref, g_ref, gn_ref, gm_ref, r0_ref, o_ref, st_ref = refs
        else:
            lg_ref, q_ref, k_ref, v_ref, g_ref, gn_ref, gm_ref, o_ref, st_ref = refs
        low = lax.broadcasted_iota(I32, (1, LANES), 1) < HEAD_DIM
        pos = lax.broadcasted_iota(I32, (seq_len, 1), 0).astype(F32)
        qpos = lax.broadcasted_iota(I32, (qb, 1), 0).astype(F32)
        kpos = lax.broadcasted_iota(I32, (1, seq_len), 1).astype(F32)
        gm = gm_ref[...]
        for c in range(2):
            cs = slice(c * LANES, (c + 1) * LANES)
            lgf = jnp.where(low, lg_ref[0, 2 * c], lg_ref[0, 2 * c + 1])
            lgb = jnp.where(low, lg_ref[1, 2 * c], lg_ref[1, 2 * c + 1])
            zeta_f = jnp.exp((seq_len - 1.0 - pos) * lgf)
            zeta_b = jnp.exp(pos * lgb)
            decs = []
            for j in range(2):
                lf = lg_ref[0, 2 * c + j]
                lb = lg_ref[1, 2 * c + j]
                row = []
                for i in range(nq):
                    diff = (qpos + float(i * qb)) - kpos
                    dec = jnp.exp(jnp.where(diff >= 0, diff * lf, -diff * lb))
                    row.append(dec * jnp.where(diff == 0, 2.0 * scale, scale))
                decs.append(row)
            for q in range(group):
                rs = slice(q * seq_len, (q + 1) * seq_len)
                qc = q_ref[rs, cs]
                kc = k_ref[rs, cs]
                vc = v_ref[rs, cs]
                kf = kc.astype(F32) * scale
                for d, zeta, lgd in ((0, zeta_f, lgf), (1, zeta_b, lgb)):
                    st = _dot_tn((kf * zeta).astype(BF16), vc)
                    if has_init:
                        st = st + r0_ref[q, d, c] * jnp.exp(seq_len * lgd)
                    st_ref[q, d, c] = st
                y_blocks = [None] * nq
                for j in range(2):
                    sel = low if j == 0 else jnp.logical_not(low)
                    kh = jnp.where(sel, kc, jnp.zeros_like(kc))
                    vh = jnp.where(sel, vc, jnp.zeros_like(vc))
                    for i in range(nq):
                        s = _dot_nt(qc[i * qb:(i + 1) * qb], kh)
                        y = _dot((s * decs[j][i]).astype(BF16), vh)
                        y_blocks[i] = y if y_blocks[i] is None else y_blocks[i] + y
                y = jnp.concatenate(y_blocks, axis=0) if nq > 1 else y_blocks[0]
                if has_init:
                    xi_f = jnp.exp((pos + 1.0) * lgf)
                    xi_b = jnp.exp((seq_len - pos) * lgb)
                    y = y + _dot(qc, r0_ref[q, 0, c].astype(BF16)) * xi_f
                    y = y + _dot(qc, r0_ref[q, 1, c].astype(BF16)) * xi_b
                y_hi, y_lo = _split_bf16(y)
                mu = _dot(y_hi, gm) + _dot(y_lo, gm)
                dlt = y - mu
                var = _dot((dlt * dlt).astype(BF16), gm)
                yn = dlt * lax.rsqrt(var + LN_EPS) * gn_ref[:, cs]
                gate = g_ref[rs, cs].astype(F32)
                o_ref[rs, cs] = (gate * jax.nn.sigmoid(gate) * yn).astype(BF16)
    return kern


def _retention(p, n_seq, seq_len, row0, log_g, gn_g, gmat, r0):
    group = _seq_group(n_seq, seq_len, row0)
    rows = group * seq_len
    srow0 = row0 // rows
    has_init = r0 is not None
    col = OFF_RET // RET_W
    in_specs = [pl.BlockSpec(memory_space=pltpu.SMEM)]
    in_specs += [pl.BlockSpec((rows, RET_W), functools.partial(lambda s, j: (srow0 + s, col + j), j=j))
                 for j in range(4)]
    in_specs += [pl.BlockSpec((1, RET_W), lambda s: (0, 0)), pl.BlockSpec((LANES, LANES), lambda s: (0, 0))]
    args = [log_g, p, p, p, p, gn_g, gmat]
    st_spec = pl.BlockSpec((group, 2, 2, LANES, LANES), lambda s: (s, 0, 0, 0, 0))
    if has_init:
        in_specs.append(st_spec)
        args.append(r0)
    return pl.pallas_call(
        _make_ret_kernel(seq_len, has_init, group),
        grid=(n_seq // group,),
        in_specs=in_specs,
        out_specs=[pl.BlockSpec((rows, RET_W), lambda s: (s, 0)), st_spec],
        out_shape=[jax.ShapeDtypeStruct((n_seq * seq_len, RET_W), BF16),
                   jax.ShapeDtypeStruct((n_seq, 2, 2, LANES, LANES), F32)],
        compiler_params=_cparams("arbitrary"),
    )(*args)


def _outproj_kernel(n_src, ctx_tiles, *refs):
    x_refs = refs[:n_src]
    a_refs, c_refs, r_refs = refs[n_src:n_src + 2], refs[n_src + 2:n_src + 4], refs[n_src + 4:n_src + 6]
    (g1_ref, sc_ref, sh_ref, n2_ref, wo_ref, rwt_ref, rb_ref,
     x1_ref, h2_ref, te_ref, tg_ref) = refs[n_src + 6:]
    mixed = (_dot(_token_tile(a_refs, ctx_tiles), wo_ref[0:ATTN_W, :])
             + _dot(_token_tile(c_refs, ctx_tiles), wo_ref[ATTN_W:ATTN_W + CONV_CH, :])
             + _dot(_token_tile(r_refs, ctx_tiles), wo_ref[ATTN_W + CONV_CH:, :]))
    x1 = _token_tile(x_refs, ctx_tiles) + g1_ref[0] * mixed
    x1_ref[...] = x1
    inv = lax.rsqrt(jnp.mean(x1 * x1, axis=-1, keepdims=True) + EPS)
    h2 = (x1 * inv * n2_ref[...]) * (1.0 + sc_ref[0]) + sh_ref[0]
    h2_ref[...] = h2.astype(BF16)
    w_hi, w_lo = _split_bf16(rwt_ref[...])
    h_hi, h_lo = _split_bf16(h2)
    logits = _dot_nt(w_hi, h_hi) + _dot_nt(w_hi, h_lo) + _dot_nt(w_lo, h_hi) + rb_ref[...]
    t = logits.shape[1]
    eidx = lax.broadcasted_iota(I32, (N_EXPERTS, t), 0).astype(F32)
    vals = logits
    tops, idxs = [], []
    for _ in range(TOP_K):
        m = jnp.max(vals, axis=0, keepdims=True)
        idx = jnp.min(jnp.where(vals == m, eidx, float(N_EXPERTS)), axis=0, keepdims=True)
        tops.append(m)
        idxs.append(idx)
        vals = jnp.where(eidx == idx, -jnp.inf, vals)
    es = [jnp.exp(m - tops[0]) for m in tops]
    tot = es[0] + es[1] + es[2] + es[3]
    te_ref[...] = jnp.concatenate(idxs + [jnp.zeros((8 - TOP_K, t), F32)], axis=0).astype(I32)
    tg_ref[...] = jnp.concatenate([e / tot for e in es] + [jnp.zeros((8 - TOP_K, t), F32)], axis=0)


def _outproj(attn, conv, ret, x_parts, mod, norm_g, w_out_bf, rwt, rb, n_ctx, dec_seq):
    n = attn[0].shape[0] + attn[1].shape[0]
    d = x_parts[0].shape[1]
    t = TOK_TILE
    grp = functools.partial(_group_of_tile, tile=t, n_ctx=n_ctx, dec_seq=dec_seq)

    def mod_spec(which):
        return pl.BlockSpec((1, 1, d), lambda i: (grp(i) * 6 + which, 0, 0))

    const = lambda shape: pl.BlockSpec(shape, lambda i: (0,) * len(shape))
    row = lambda w: pl.BlockSpec((t, w), lambda i: (i, 0))
    lane = lambda: pl.BlockSpec((8, t), lambda i: (0, i))
    return pl.pallas_call(
        functools.partial(_outproj_kernel, len(x_parts), n_ctx // t),
        grid=(n // t,),
        in_specs=_token_specs(x_parts, t, n_ctx) + _token_specs(attn, t, n_ctx)
        + _token_specs(conv, t, n_ctx) + _token_specs(ret, t, n_ctx) + [
            mod_spec(MOD_G1), mod_spec(MOD_SC2), mod_spec(MOD_SH2),
            const((1, d)), const((d, d)), const((N_EXPERTS, d)), const((N_EXPERTS, 1))],
        out_specs=[row(d), row(d), lane(), lane()],
        out_shape=[jax.ShapeDtypeStruct((n, d), F32), jax.ShapeDtypeStruct((n, d), BF16),
                   jax.ShapeDtypeStruct((8, n), I32), jax.ShapeDtypeStruct((8, n), F32)],
        compiler_params=_cparams("arbitrary"),
    )(*x_parts, *attn, *conv, *ret, mod, mod, mod, norm_g, w_out_bf, rwt, rb)


def _round_up(x, m):
    return jnp.floor((x + (m - 1.0)) * (1.0 / m)) * m


def _route_kernel(te_ref, u_ref, ltri_ref, lpos_ref, run8_ref, loff_ref, goff_ref, seg_ref, run_ref, start_ref):
    ph = pl.program_id(0)
    i = pl.program_id(1)
    t = MOVE_TILE
    eidx = lax.broadcasted_iota(I32, (N_EXPERTS, t), 0)

    @pl.when(jnp.logical_and(ph == 0, i == 0))
    def _():
        run_ref[...] = jnp.zeros_like(run_ref)

    @pl.when(jnp.logical_and(ph == 1, i == 0))
    def _():
        seg = run_ref[...]
        seg_ref[...] = seg
        start_ref[...] = _dot(ltri_ref[...], _round_up(seg, MOE_BM), precision=lax.Precision.HIGHEST)
        run_ref[...] = jnp.zeros_like(run_ref)

    for s in range(ROUTE_TILES):
        te = te_ref[:, s * t:(s + 1) * t]
        hits = [eidx == te[k:k + 1, :] for k in range(TOP_K)]
        onehot = sum(h.astype(F32) for h in hits)
        run8 = _round_up(jnp.sum(onehot, axis=1, keepdims=True), SUBLANES)

        @pl.when(ph == 0)
        def _(run8=run8):
            run_ref[...] = run_ref[...] + run8

        @pl.when(ph == 1)
        def _(s=s, hits=hits, onehot=onehot, run8=run8):
            before = _dot(onehot.astype(BF16), u_ref[...])
            run8_b = jnp.broadcast_to(run8, (N_EXPERTS, LANES))
            loff = _dot(ltri_ref[...], run8_b, precision=lax.Precision.HIGHEST)
            base = before + loff[:, 0:1]
            rows = [jnp.sum(jnp.where(h, base, 0.0), axis=0, keepdims=True) for h in hits]
            rows.append(jnp.zeros((8 - TOP_K, t), F32))
            lpos_ref[:, s * t:(s + 1) * t] = jnp.concatenate(rows, axis=0).astype(I32)
            run8_ref[s] = run8_b.astype(I32)
            loff_ref[s] = loff.astype(I32)
            goff_ref[s] = (start_ref[...] + run_ref[...]).astype(I32)
            run_ref[...] = run_ref[...] + run8


def _route(top_e, upper, ltri):
    n = top_e.shape[1]
    t = MOVE_TILE
    nt = n // t
    g = ROUTE_TILES
    assert nt % g == 0
    table = pl.BlockSpec((g, N_EXPERTS, LANES), lambda ph, i: (i * ph, 0, 0))
    table_shape = jax.ShapeDtypeStruct((nt, N_EXPERTS, LANES), I32)
    return pl.pallas_call(
        _route_kernel,
        grid=(2, nt // g),
        in_specs=[pl.BlockSpec((8, g * t), lambda ph, i: (0, i)),
                  pl.BlockSpec((t, t), lambda ph, i: (0, 0)),
                  pl.BlockSpec((N_EXPERTS, N_EXPERTS), lambda ph, i: (0, 0))],
        out_specs=[pl.BlockSpec((8, g * t), lambda ph, i: (0, i * ph)), table, table, table,
                   pl.BlockSpec((N_EXPERTS, LANES), lambda ph, i: (0, 0))],
        out_shape=[jax.ShapeDtypeStruct((8, n), I32), table_shape, table_shape, table_shape,
                   jax.ShapeDtypeStruct((N_EXPERTS, LANES), F32)],
        scratch_shapes=[pltpu.VMEM((N_EXPERTS, LANES), F32), pltpu.VMEM((N_EXPERTS, LANES), F32)],
        compiler_params=_cparams("arbitrary", "arbitrary"),
    )(top_e, upper, ltri)


def _for_each_run_chunk(tile, run8_ref, loff_ref, goff_ref, move):
    def pieces(rows, lo, go, sizes, queue):
        for size in sizes:
            hit = (rows & size) != 0

            @pl.when(hit)
            def _(lo=lo, go=go, size=size):
                move(pl.multiple_of(lo, SUBLANES), pl.multiple_of(go, SUBLANES), size, queue)

            lo = lo + jnp.where(hit, size, 0)
            go = go + jnp.where(hit, size, 0)
        return lo, go

    def per_expert(e, queue):
        idx = tile * N_EXPERTS + e
        rows = run8_ref[idx]
        lo, go = pieces(rows, loff_ref[idx], goff_ref[idx], CHUNK_SIZES[RARE_SIZES:], queue)

        @pl.when(rows >= CHUNK_SIZES[RARE_SIZES - 1])
        def _():
            pieces(rows, lo, go, CHUNK_SIZES[:RARE_SIZES], queue)

        return rows

    def per_pair(j, total):
        return total + per_expert(2 * j, 0) + per_expert(2 * j + 1, 1)

    return lax.fori_loop(0, N_EXPERTS // 2, per_pair, jnp.int32(0))


def _wait_rows(total, copy_of_rows):
    size = 1 << (COMPACT_ROWS.bit_length() - 1)
    while size >= SUBLANES:
        @pl.when((total & size) != 0)
        def _(size=size):
            copy_of_rows(size).wait()

        size //= 2


def _make_dispatch_kernel(n_tiles, n_blocks):
    t = MOVE_TILE
    cb = COMPACT_ROWS

    def kern(run8_ref, loff_ref, goff_ref, seg_ref, start_ref, nv_ref, h_ref, lpos_ref, xs_ref,
             buf, zeros_ref, moved_ref, sem, zsem):
        step = pl.program_id(0)
        slot = lax.rem(step, 2)

        def wait_tile(s):
            _wait_rows(moved_ref[s], lambda rows: pltpu.make_async_copy(
                buf.at[s, pl.ds(0, rows)], xs_ref.at[pl.ds(0, rows)], sem.at[s]))

        def for_each_zero_chunk(action):
            def per_expert(e, carry):
                seg = seg_ref[e]
                padlen = (-seg) & (MOE_BM - 1)
                row = start_ref[e] + seg
                for size in CHUNK_SIZES[1:]:
                    hit = (padlen & size) != 0

                    @pl.when(hit)
                    def _(row=row, size=size):
                        dst = xs_ref.at[pl.ds(pl.multiple_of(row, SUBLANES), size)]
                        action(pltpu.make_async_copy(zeros_ref.at[pl.ds(0, size)], dst, zsem))

                    row = row + jnp.where(hit, size, 0)
                return carry

            lax.fori_loop(0, N_EXPERTS, per_expert, 0)

            def per_tail_block(b, carry):
                action(pltpu.make_async_copy(zeros_ref, xs_ref.at[pl.ds(b * MOE_BM, MOE_BM)], zsem))
                return carry

            lax.fori_loop(nv_ref[0], n_blocks, per_tail_block, 0)

        @pl.when(step == 0)
        def _():
            zeros_ref[...] = jnp.zeros_like(zeros_ref)
            for_each_zero_chunk(lambda cp: cp.start())
            for_each_zero_chunk(lambda cp: cp.wait())

        @pl.when(step >= 2)
        def _():
            wait_tile(slot)

        lpos = lpos_ref[...]
        rows = lax.broadcasted_iota(I32, (cb, t), 0)
        onehot = jnp.where(rows == lpos[0:1, :], 1.0, 0.0)
        for k in range(1, TOP_K):
            onehot = onehot + jnp.where(rows == lpos[k:k + 1, :], 1.0, 0.0)
        buf[slot] = _dot(onehot.astype(BF16), h_ref[...])

        def move(lo, go, size, queue):
            pltpu.make_async_copy(buf.at[slot, pl.ds(lo, size)], xs_ref.at[pl.ds(go, size)],
                                  sem.at[slot]).start(priority=queue)

        moved_ref[slot] = _for_each_run_chunk(step, run8_ref, loff_ref, goff_ref, move)

        @pl.when(step == n_tiles - 1)
        def _():
            wait_tile(slot)
            if n_tiles > 1:
                wait_tile(1 - slot)
    return kern


def _dispatch(tables, seg, starts, n_valid, h2, lpos, n_blocks):
    n, d = h2.shape
    t = MOVE_TILE
    return pl.pallas_call(
        _make_dispatch_kernel(n // t, n_blocks),
        grid_spec=pltpu.PrefetchScalarGridSpec(
            num_scalar_prefetch=6,
            grid=(n // t,),
            in_specs=[pl.BlockSpec((t, d), lambda i, *_: (i, 0)),
                      pl.BlockSpec((8, t), lambda i, *_: (0, i))],
            out_specs=pl.BlockSpec(memory_space=pl.ANY),
            scratch_shapes=[pltpu.VMEM((2, COMPACT_ROWS, d), F32), pltpu.VMEM((MOE_BM, d), F32),
                            pltpu.SMEM((2,), I32), pltpu.SemaphoreType.DMA((2,)), pltpu.SemaphoreType.DMA(())],
        ),
        out_shape=jax.ShapeDtypeStruct((n_blocks * MOE_BM, d), F32),
        compiler_params=_cparams("arbitrary"),
    )(*tables, seg, starts, n_valid, h2, lpos)


def _make_expert_kernel(layer):
    def kern(be_ref, bs_ref, nv_ref, ord_ref, nxt_ref, x_hbm, wgu_hbm, bgu_ref, wdn_hbm, bdn_ref, y_ref,
             wgu_f, wdn_f, wgu_s, wdn_s, sem, xbuf, xsem):
        i = pl.program_id(0)
        e = be_ref[i]
        prev = be_ref[jnp.maximum(i - 1, 0)]
        new_expert = jnp.logical_or(i == 0, e != prev)
        slot = lax.rem(ord_ref[i], 2)

        def weight_copies(expert, s):
            return (pltpu.make_async_copy(wgu_hbm.at[layer, expert], wgu_f.at[s], sem.at[0, s]),
                    pltpu.make_async_copy(wdn_hbm.at[layer, expert], wdn_f.at[s], sem.at[1, s]))

        @pl.when(i == 0)
        def _():
            for cp in weight_copies(e, slot):
                cp.start()

        @pl.when(new_expert)
        def _():
            for cp in weight_copies(e, slot):
                cp.wait()
            nxt = nxt_ref[i]

            @pl.when(nxt >= 0)
            def _():
                for cp in weight_copies(nxt, 1 - slot):
                    cp.start(priority=1)

            wgu_s[...] = wgu_f[slot].astype(BF16)
            wdn_s[...] = wdn_f[slot].astype(BF16)

        n_valid = nv_ref[0]

        def x_copy(blk):
            rows = x_hbm.at[pl.ds(pl.multiple_of(blk * MOE_BM, MOE_BM), MOE_BM)]
            s = lax.rem(blk, X_RING)
            return pltpu.make_async_copy(rows, xbuf.at[s], xsem.at[s])

        @pl.when(i == 0)
        def _():
            x_copy(0).start()

            @pl.when(n_valid > 1)
            def _():
                x_copy(1).start()

        @pl.when(i + 2 < n_valid)
        def _():
            x_copy(i + 2).start()

        @pl.when(i < n_valid)
        def _():
            x_copy(i).wait()
            x = xbuf[lax.rem(i, X_RING)].astype(BF16)
            gu = _dot(x, wgu_s[...]) + bgu_ref[0, 0]
            gate = jnp.minimum(gu[:, :D_FF], SWIGLU_LIMIT)
            up = jnp.clip(gu[:, D_FF:], -SWIGLU_LIMIT, SWIGLU_LIMIT)
            hdn = (up + 1.0) * (gate * jax.nn.sigmoid(SWIGLU_ALPHA * gate))
            y_ref[...] = _dot(hdn.astype(BF16), wdn_s[...]) + bdn_ref[0, 0]

        @pl.when(i >= n_valid)
        def _():
            y_ref[...] = jnp.zeros_like(y_ref)
    return kern


def _experts(blk_e, blk_src, n_valid, blk_ord, blk_next, xs, w_gu, b_gu, w_dn, b_dn, layer, n_blocks):
    d = xs.shape[1]
    depth = w_gu.shape[0]

    def bias(width):
        return pl.BlockSpec((1, 1, 1, width), lambda i, be, *_: (layer, be[i], 0, 0))

    rows = pl.BlockSpec((MOE_BM, d), lambda i, be, bs, *_: (bs[i], 0))
    return pl.pallas_call(
        _make_expert_kernel(layer),
        grid_spec=pltpu.PrefetchScalarGridSpec(
            num_scalar_prefetch=5,
            grid=(n_blocks,),
            in_specs=[pl.BlockSpec(memory_space=pl.ANY),
                      pl.BlockSpec(memory_space=pl.ANY), bias(2 * D_FF),
                      pl.BlockSpec(memory_space=pl.ANY), bias(d)],
            out_specs=pl.BlockSpec((MOE_BM, d), lambda i, *_: (i, 0)),
            scratch_shapes=[pltpu.VMEM((2, d, 2 * D_FF), F32), pltpu.VMEM((2, D_FF, d), F32),
                            pltpu.VMEM((d, 2 * D_FF), BF16), pltpu.VMEM((D_FF, d), BF16),
                            pltpu.SemaphoreType.DMA((2, 2)),
                            pltpu.VMEM((X_RING, MOE_BM, d), F32), pltpu.SemaphoreType.DMA((X_RING,))],
        ),
        out_shape=jax.ShapeDtypeStruct(xs.shape, xs.dtype),
        compiler_params=_cparams("arbitrary"),
    )(blk_e, blk_src, n_valid, blk_ord, blk_next, xs, w_gu, b_gu.reshape(depth, N_EXPERTS, 1, -1), w_dn,
      b_dn.reshape(depth, N_EXPERTS, 1, -1))


def _make_combine_kernel(n_tiles, ctx_tiles, final):
    t = MOVE_TILE
    cb = COMPACT_ROWS

    def kern(run8_ref, loff_ref, goff_ref, ys_ref, x1_ref, lpt_ref, gt_ref, g2_ref, fg_ref, *rest):
        o_refs = rest[:2] if final else rest[:1]
        ybuf, moved_ref, sem = rest[len(o_refs):]
        step = pl.program_id(0)
        slot = lax.rem(step, 2)

        def fetch(tile, s):
            def move(lo, go, size, queue):
                pltpu.make_async_copy(ys_ref.at[pl.ds(go, size)], ybuf.at[s, pl.ds(lo, size)],
                                      sem.at[s]).start(priority=queue)

            moved_ref[s] = _for_each_run_chunk(tile, run8_ref, loff_ref, goff_ref, move)

        @pl.when(step == 0)
        def _():
            ybuf[...] = jnp.zeros_like(ybuf)
            fetch(0, 0)

        @pl.when(step + 1 < n_tiles)
        def _():
            fetch(step + 1, 1 - slot)

        _wait_rows(moved_ref[slot], lambda rows: pltpu.make_async_copy(
            ys_ref.at[pl.ds(0, rows)], ybuf.at[slot, pl.ds(0, rows)], sem.at[slot]))

        lpos = lpt_ref[...]
        gates = gt_ref[...]
        rows = lax.broadcasted_iota(I32, (cb, t), 0)
        placed = jnp.where(rows == lpos[0:1, :], gates[0:1, :], 0.0)
        for k in range(1, TOP_K):
            placed = placed + jnp.where(rows == lpos[k:k + 1, :], gates[k:k + 1, :], 0.0)
        row_gate = jnp.sum(placed, axis=1, keepdims=True)
        yb = (ybuf[slot] * row_gate).astype(BF16)
        lpt = lpos.astype(F32).T.astype(I32)
        cols = lax.broadcasted_iota(I32, (t, cb), 1)
        unsort = jnp.where(cols == lpt[:, 0:1], 1.0, 0.0)
        for k in range(1, TOP_K):
            unsort = unsort + jnp.where(cols == lpt[:, k:k + 1], 1.0, 0.0)
        y = _dot(unsort.astype(BF16), yb)
        x2 = x1_ref[...] + g2_ref[0] * y
        if not final:
            o_refs[0][...] = x2
        else:
            x2 = x2 * lax.rsqrt(jnp.mean(x2 * x2, axis=-1, keepdims=True) + EPS) * fg_ref[...]

            @pl.when(step < ctx_tiles)
            def _():
                o_refs[0][...] = x2

            @pl.when(step >= ctx_tiles)
            def _():
                o_refs[1][...] = x2
    return kern


def _combine(tables, ys, x1, lpos_t, gates_t, mod, final_g, final, n_ctx, dec_seq):
    n, d = x1.shape
    t = MOVE_TILE
    ctx_tiles = n_ctx // t
    grp = functools.partial(_group_of_tile, tile=t, n_ctx=n_ctx, dec_seq=dec_seq)
    if final:
        out_specs = [pl.BlockSpec((t, d), lambda i, *_: (jnp.minimum(i, ctx_tiles - 1), 0)),
                     pl.BlockSpec((t, d), lambda i, *_: (jnp.maximum(i - ctx_tiles, 0), 0))]
        out_shape = [jax.ShapeDtypeStruct((n_ctx, d), F32), jax.ShapeDtypeStruct((n - n_ctx, d), F32)]
    else:
        out_specs = pl.BlockSpec((t, d), lambda i, *_: (i, 0))
        out_shape = jax.ShapeDtypeStruct((n, d), F32)
    return pl.pallas_call(
        _make_combine_kernel(n // t, ctx_tiles, final),
        grid_spec=pltpu.PrefetchScalarGridSpec(
            num_scalar_prefetch=3,
            grid=(n // t,),
            in_specs=[pl.BlockSpec(memory_space=pl.ANY),
                      pl.BlockSpec((t, d), lambda i, *_: (i, 0)),
                      pl.BlockSpec((8, t), lambda i, *_: (0, i)),
                      pl.BlockSpec((8, t), lambda i, *_: (0, i)),
                      pl.BlockSpec((1, 1, d), lambda i, *_: (grp(i) * 6 + MOD_G2, 0, 0)),
                      pl.BlockSpec((1, d), lambda i, *_: (0, 0))],
            out_specs=out_specs,
            scratch_shapes=[pltpu.VMEM((2, COMPACT_ROWS, d), F32), pltpu.SMEM((2,), I32),
                            pltpu.SemaphoreType.DMA((2,))],
        ),
        out_shape=out_shape,
        compiler_params=_cparams("arbitrary"),
    )(*tables, ys, x1, lpos_t, gates_t, mod, final_g)


def _blockdiag_pairs(s):
    z = jnp.zeros_like(s[..., 0, :, :])
    def pair(a, b):
        return jnp.concatenate([jnp.concatenate([a, z], axis=-1), jnp.concatenate([z, b], axis=-1)], axis=-2)
    return jnp.stack([pair(s[..., 0, :, :], s[..., 1, :, :]), pair(s[..., 2, :, :], s[..., 3, :, :])], axis=-3)


def _diag_blocks(st):
    h = HEAD_DIM
    blocks = [st[:, :, c, j * h:(j + 1) * h, j * h:(j + 1) * h] for c in range(2) for j in range(2)]
    return jnp.stack(blocks, axis=2)


def kernel(x_prompt, x_sample, cache_k, cache_v, state_ret, c, c_ctx, w_mod, b_mod, norm1_g, norm2_g, w_in,
           q_norm_g, k_norm_g, conv_w, conv_b, conv_ln_g, conv_ln_b, ret_decay_logit, ret_gn_g, w_out,
           router_w, router_b, moe_w_gu, moe_b_gu, moe_w_dn, moe_b_dn, final_g):
    batch, seq, d = x_prompt.shape
    dec_batch, dec_seq, _ = x_sample.shape
    depth = w_mod.shape[0]
    past = cache_k.shape[2]
    n_ctx = batch * seq
    n_lat = dec_batch * dec_seq
    n = n_ctx + n_lat
    assert d == D_MODEL and dec_batch == 2
    assert n_ctx % dec_seq == 0 and dec_seq % TOK_TILE == 0 and dec_seq % ATTN_QB == 0
    assert n % (MOVE_TILE * ROUTE_TILES) == 0

    x_parts = (x_prompt.reshape(n_ctx, d), x_sample.reshape(n_lat, d))
    mods = _modulation(jnp.concatenate([c_ctx[None, :], c], axis=0), w_mod, b_mod)
    rope = _rope_tables(dec_seq)

    head_of_col = np.arange(QK_W) // HEAD_DIM
    hsum_np = (head_of_col[:, None] == np.arange(LANES)[None, :]).astype(np.float32)
    hsum = jnp.asarray(hsum_np, BF16)
    hbc = jnp.asarray(hsum_np.T, BF16)
    lane_head = np.arange(LANES) // HEAD_DIM
    gmat = jnp.asarray((lane_head[:, None] == lane_head[None, :]).astype(np.float32) / HEAD_DIM, BF16)
    tt = np.arange(MOVE_TILE)
    upper = jnp.asarray((tt[:, None] < tt[None, :]).astype(np.float32), BF16)
    ee = jnp.arange(N_EXPERTS)
    ltri = jnp.asarray((np.arange(N_EXPERTS)[None, :] < np.arange(N_EXPERTS)[:, None]).astype(np.float32))

    n_tiles = n // MOVE_TILE
    max_rows = n * TOP_K + n_tiles * N_EXPERTS * (SUBLANES - 1) + N_EXPERTS * (MOE_BM - 1)
    n_blocks = -(-max_rows // MOE_BM)
    cache_k2 = cache_k.reshape(dec_batch, depth, past, KV_W)
    cache_v2 = cache_v.reshape(dec_batch, depth, past, KV_W)

    ks_out, vs_out, ss_out = [], [], []
    for l in range(depth):
        mod = mods[l]
        gqk = jnp.concatenate([jnp.tile(q_norm_g[l], N_Q_HEADS), jnp.tile(k_norm_g[l], N_KV_HEADS)])[None, :]
        p, kn, vv = _inproj(x_parts, mod, norm1_g[l][None, :], w_in[l].astype(BF16), gqk, hsum, hbc, rope,
                            n_ctx, dec_seq)
        ks_out.append(kn.reshape(batch, seq, N_KV_HEADS, HEAD_DIM))
        vs_out.append(vv.reshape(batch, seq, N_KV_HEADS, HEAD_DIM))

        attn = (_attention(p, batch, seq, 0, None, l),
                _attention(p, dec_batch, dec_seq, n_ctx, (cache_k2, cache_v2), l))

        cw, cb = conv_w[l], conv_b[l][None, :]
        clg, clb = conv_ln_g[l][None, :], conv_ln_b[l][None, :]
        conv = (_conv(p, batch, seq, 0, cw, cb, clg, clb),
                _conv(p, dec_batch, dec_seq, n_ctx, cw, cb, clg, clb))

        log_g = jax.nn.log_sigmoid(ret_decay_logit[l].astype(F32))
        gn = ret_gn_g[l][None, :]
        ret_ctx, st_ctx = _retention(p, batch, seq, 0, log_g, gn, gmat, None)
        ret_lat, _ = _retention(p, dec_batch, dec_seq, n_ctx, log_g, gn, gmat,
                                _blockdiag_pairs(state_ret[:, l].astype(F32)))
        ret = (ret_ctx, ret_lat)
        ss_out.append(_diag_blocks(st_ctx))

        x1, h2, top_e, top_g = _outproj(attn, conv, ret, x_parts, mod, norm2_g[l][None, :], w_out[l].astype(BF16),
                                        router_w[l].T, router_b[l][:, None], n_ctx, dec_seq)
        lpos, run8, loff, goff, seg = _route(top_e, upper, ltri)
        tables = [tb[:, :, 0].reshape(-1) for tb in (run8, loff, goff)]
        seg = seg[:, 0].astype(I32)
        padded = (seg + MOE_BM - 1) // MOE_BM * MOE_BM
        pad_end = jnp.cumsum(padded)
        n_valid = (pad_end[-1] // MOE_BM).reshape(1)
        blk_src = jnp.minimum(jnp.arange(n_blocks, dtype=I32), n_valid - 1)
        blk_e = jnp.minimum(jnp.sum((pad_end[None, :] <= (blk_src * MOE_BM)[:, None]).astype(I32), axis=1),
                            N_EXPERTS - 1)
        owns = padded > 0
        ord_e = jnp.cumsum(owns.astype(I32)) - 1
        next_e = jnp.sum(jnp.where(owns[None, :] & (ord_e[None, :] == ord_e[:, None] + 1),
                                   ee[None, :] + 1, 0), axis=1).astype(I32) - 1

        xs = _dispatch(tables, seg, pad_end - padded, n_valid, h2, lpos, n_blocks)
        of_blk = blk_e[:, None] == ee[None, :]
        per_blk = lambda table: jnp.sum(jnp.where(of_blk, table[None, :], 0), axis=1)
        ys = _experts(blk_e, blk_src, n_valid, per_blk(ord_e), per_blk(next_e), xs,
                      moe_w_gu, moe_b_gu, moe_w_dn, moe_b_dn, l, n_blocks)
        out = _combine(tables, ys, x1, lpos, top_g, mod, final_g[None, :], l == depth - 1, n_ctx, dec_seq)
        x_parts = (out,)

    y_prompt = out[0].reshape(batch, seq, d)
    y_sample = out[1].reshape(dec_batch, dec_seq, d)
    return (y_prompt, y_sample, jnp.stack(ks_out, axis=1), jnp.stack(vs_out, axis=1),
            jnp.stack(ss_out, axis=1))
```
